```python
import jax, jax.numpy as jnp
from jax import lax
import numpy as np

D_MODEL = 1024
BATCH = 8
SEQ = 2048
DEPTH = 1
DEC_BATCH = 128
DEC_SEQ = 1
PAST_LEN = 16384
PAGE_SIZE = 128

MIX_DIM = D_MODEL
CONV_DIM = MIX_DIM // 2
CONV_GROUPS = 8
CONV_WIDTH = 3
RET_DIM = MIX_DIM - CONV_DIM
RET_HEADS = 4
RET_DK = RET_DIM // RET_HEADS
RET_DV = RET_DIM // RET_HEADS
RET_CHUNK = 128
ROPE_BASE = 10000.0
IN_PROJ_DIM = 3 * CONV_DIM + 4 * RET_DIM
N_GROUPS = 4
EXPERTS_PER_GROUP = 8
N_EXPERTS = N_GROUPS * EXPERTS_PER_GROUP
TOP_K_INNER = 2
D_FF_EXPERT = D_MODEL // 2
PLE_DIM = 256
EPS = 1e-6

kernel_name = 'hymba_conv_retention_hmoe_step'


def rmsnorm(x, g):
    xf = x.astype(jnp.float32)
    y = xf * lax.rsqrt(jnp.mean(xf * xf, axis=-1, keepdims=True) + EPS)
    return (y * g.astype(jnp.float32)).astype(x.dtype)


def rope(x, pos):
    d = x.shape[-1]
    inv = 1.0 / (ROPE_BASE ** (jnp.arange(0, d, 2, dtype=jnp.float32) / d))
    ang = pos.astype(jnp.float32)[:, None] * inv[None, :]
    cos = jnp.cos(ang)[None, :, None, :]
    sin = jnp.sin(ang)[None, :, None, :]
    x1, x2 = x[..., : d // 2], x[..., d // 2:]
    return jnp.concatenate([x1 * cos - x2 * sin, x1 * sin + x2 * cos], axis=-1)


def short_conv(z, buf, w):
    T = z.shape[1]
    zc = jnp.concatenate([buf.astype(z.dtype), z], axis=1)
    wz = w.astype(z.dtype)
    y = wz[0] * zc[:, 0:T]
    for i in range(1, CONV_WIDTH):
        y = y + wz[i] * zc[:, i:i + T]
    return y, zc[:, T:]


def retention(q, k, v, s0, log_gamma):
    B, T, H, dk = q.shape
    dv = v.shape[-1]
    C = RET_CHUNK if T % RET_CHUNK == 0 else T
    n = T // C
    idx = jnp.arange(C, dtype=jnp.float32)
    diff = idx[:, None] - idx[None, :]
    lg = log_gamma[:, None, None]
    dmask = jnp.where((diff >= 0.0)[None], jnp.exp(lg * jnp.maximum(diff, 0.0)[None]), 0.0)
    q_dec = jnp.exp(log_gamma[None, :] * (idx[:, None] + 1.0))
    k_dec = jnp.exp(log_gamma[None, :] * (C - 1.0 - idx[:, None]))
    c_dec = jnp.exp(log_gamma * C)

    def to_chunks(a):
        return a.reshape(B, n, C, H, a.shape[-1]).transpose(1, 0, 2, 3, 4)

    def step(s, inp):
        qc, kc, vc = inp
        scores = jnp.einsum('bihd,bjhd->bhij', qc, kc) * dmask[None]
        o = (jnp.einsum('bhij,bjhe->bihe', scores, vc)
             + jnp.einsum('bihd,bhde->bihe', qc * q_dec[None, :, :, None], s))
        s_new = (s * c_dec[None, :, None, None]
                 + jnp.einsum('bjhd,bjhe->bhde', kc * k_dec[None, :, :, None], vc))
        return s_new, o

    s_fin, o = lax.scan(step, s0, (to_chunks(q), to_chunks(k), to_chunks(v)))
    o = o.transpose(1, 0, 2, 3, 4).reshape(B, T, H, dv)
    return o, s_fin


def mixer(u, conv_buf, ret_state, pos, log_gamma, w_in, conv_w, ret_gn, w_o):
    B, T, _ = u.shape
    f32 = jnp.float32
    proj = u @ w_in.astype(u.dtype)
    cuts = [CONV_DIM, 2 * CONV_DIM, 3 * CONV_DIM,
            3 * CONV_DIM + RET_DIM, 3 * CONV_DIM + 2 * RET_DIM, 3 * CONV_DIM + 3 * RET_DIM]
    gb, gc, hc, q, k, v, g = jnp.split(proj, cuts, axis=-1)
    yc, new_buf = short_conv(gc * hc, conv_buf, conv_w)
    yc = gb * yc
    qh = rope(q.astype(f32).reshape(B, T, RET_HEADS, RET_DK), pos)
    kh = rope(k.astype(f32).reshape(B, T, RET_HEADS, RET_DK), pos) * (RET_DK ** -0.5)
    vh = v.astype(f32).reshape(B, T, RET_HEADS, RET_DV)
    o, s_new = retention(qh, kh, vh, ret_state.astype(f32), log_gamma)
    mu = jnp.mean(o, axis=-1, keepdims=True)
    oc = o - mu
    o = oc * lax.rsqrt(jnp.mean(oc * oc, axis=-1, keepdims=True) + EPS)
    o = o.reshape(B, T, RET_DIM) * ret_gn.astype(f32) * jax.nn.silu(g.astype(f32))
    out = jnp.concatenate([yc, o.astype(u.dtype)], axis=-1) @ w_o.astype(u.dtype)
    return out, new_buf, s_new


def hier_moe(u, w_rg, b_rg, w_re, b_re, w_gate, w_up, w_down):
    f32 = jnp.float32
    B, T, D = u.shape
    x = u.reshape(B * T, D)
    g_prob = jax.nn.softmax((x @ w_rg.astype(x.dtype)).astype(f32) + b_rg.astype(f32), axis=-1)
    g_top, g_idx = lax.top_k(g_prob, 1)
    e_logits = (x @ w_re.astype(x.dtype)).astype(f32).reshape(-1, N_GROUPS, EXPERTS_PER_GROUP) + b_re.astype(f32)
    e_sel = jnp.einsum('nge,ng->ne', e_logits, jax.nn.one_hot(g_idx[:, 0], N_GROUPS, dtype=f32))
    e_top, e_idx = lax.top_k(jax.nn.softmax(e_sel, axis=-1), TOP_K_INNER)
    wk = g_top * e_top / jnp.sum(e_top, axis=-1, keepdims=True)
    eid = g_idx * EXPERTS_PER_GROUP + e_idx
    cw = jnp.sum(jax.nn.one_hot(eid, N_EXPERTS, dtype=f32) * wk[..., None], axis=1)
    y = jnp.zeros((B * T, D), f32)
    for e in range(N_EXPERTS):
        h = jax.nn.silu(x @ w_gate[e].astype(x.dtype)) * (x @ w_up[e].astype(x.dtype))
        y = y + cw[:, e:e + 1] * (h @ w_down[e].astype(x.dtype)).astype(f32)
    return y.astype(u.dtype).reshape(B, T, D)


def block(h, p, conv_buf, ret_state, pos, log_gamma, g_mix, w_in, conv_w, ret_gn, w_o, g_ffn,
          w_rg, b_rg, w_re, b_re, w_gate, w_up, w_down, g_ple, w_ple_proj, w_ple_gate):
    m, new_buf, new_state = mixer(rmsnorm(h, g_mix), conv_buf, ret_state, pos, log_gamma,
                                  w_in, conv_w, ret_gn, w_o)
    h = h + m
    h = h + hier_moe(rmsnorm(h, g_ffn), w_rg, b_rg, w_re, b_re, w_gate, w_up, w_down)
    gate = jax.nn.sigmoid(rmsnorm(h, g_ple) @ w_ple_gate.astype(h.dtype))
    h = h + (p.astype(h.dtype) @ w_ple_proj.astype(h.dtype)) * gate
    return h, new_buf, new_state


def setup_inputs(seed: int = 0) -> dict:
    key = jax.random.key(seed)
    ks = jax.random.split(key, 24)
    nrm = jax.random.normal
    f32 = jnp.float32
    D = D_MODEL
    return {
        'x_prompt': nrm(ks[0], (BATCH, SEQ, D), f32),
        'x_sample': nrm(ks[1], (DEC_BATCH, DEC_SEQ, D), f32),
        'state_conv': nrm(ks[2], (DEPTH, DEC_BATCH, CONV_WIDTH - 1, CONV_DIM), f32),
        'state_ret': 0.5 * nrm(ks[3], (DEPTH, DEC_BATCH, RET_HEADS, RET_DK, RET_DV), f32),
        'p_prompt': nrm(ks[4], (DEPTH, BATCH, SEQ, PLE_DIM), f32),
        'p_sample': nrm(ks[5], (DEPTH, DEC_BATCH, DEC_SEQ, PLE_DIM), f32),
        'g_mix': 1.0 + 0.01 * nrm(ks[6], (DEPTH, D), f32),
        'w_in': nrm(ks[7], (DEPTH, D, IN_PROJ_DIM), f32) * D ** -0.5,
        'conv_w': nrm(ks[8], (DEPTH, CONV_WIDTH, CONV_DIM), f32) * CONV_WIDTH ** -0.5,
        'ret_gn': 1.0 + 0.01 * nrm(ks[9], (DEPTH, RET_DIM), f32),
        'w_o': nrm(ks[10], (DEPTH, MIX_DIM, D), f32) * MIX_DIM ** -0.5,
        'g_ffn': 1.0 + 0.01 * nrm(ks[11], (DEPTH, D), f32),
        'w_router_group': nrm(ks[12], (DEPTH, D, N_GROUPS), f32) * D ** -0.5,
        'b_router_group': 0.01 * nrm(ks[13], (DEPTH, N_GROUPS), f32),
        'w_router_expert': nrm(ks[14], (DEPTH, D, N_EXPERTS), f32) * D ** -0.5,
        'b_router_expert': 0.01 * nrm(ks[15], (DEPTH, N_GROUPS, EXPERTS_PER_GROUP), f32),
        'w_gate': nrm(ks[16], (DEPTH, N_EXPERTS, D, D_FF_EXPERT), f32) * D ** -0.5,
        'w_up': nrm(ks[17], (DEPTH, N_EXPERTS, D, D_FF_EXPERT), f32) * D ** -0.5,
        'w_down': nrm(ks[18], (DEPTH, N_EXPERTS, D_FF_EXPERT, D), f32) * D_FF_EXPERT ** -0.5,
        'g_ple': 1.0 + 0.01 * nrm(ks[19], (DEPTH, D), f32),
        'w_ple_proj': nrm(ks[20], (DEPTH, PLE_DIM, D), f32) * PLE_DIM ** -0.5,
        'w_ple_gate': nrm(ks[21], (DEPTH, D, D), f32) * D ** -0.5,
        'g_final': 1.0 + 0.01 * nrm(ks[22], (D,), f32),
    }


def reference(x_prompt, x_sample, state_conv, state_ret, p_prompt, p_sample, g_mix, w_in, conv_w,
              ret_gn, w_o, g_ffn, w_router_group, b_router_group, w_router_expert, b_router_expert,
              w_gate, w_up, w_down, g_ple, w_ple_proj, w_ple_gate, g_final):
    log_gamma = jnp.log(1.0 - 2.0 ** (-5.0 - jnp.arange(RET_HEADS, dtype=jnp.float32)))
    bp, tp = x_prompt.shape[0], x_prompt.shape[1]
    bs, ts = x_sample.shape[0], x_sample.shape[1]
    pos_prompt = jnp.arange(tp, dtype=jnp.int32)
    pos_sample = PAST_LEN + jnp.arange(ts, dtype=jnp.int32)
    hp, hs = x_prompt, x_sample
    conv_p, ret_p, conv_s, ret_s = [], [], [], []
    for l in range(DEPTH):
        layer = (g_mix[l], w_in[l], conv_w[l], ret_gn[l], w_o[l], g_ffn[l],
                 w_router_group[l], b_router_group[l], w_router_expert[l], b_router_expert[l],
                 w_gate[l], w_up[l], w_down[l], g_ple[l], w_ple_proj[l], w_ple_gate[l])
        buf0 = jnp.zeros((bp, CONV_WIDTH - 1, CONV_DIM), x_prompt.dtype)
        st0 = jnp.zeros((bp, RET_HEADS, RET_DK, RET_DV), jnp.float32)
        hp, cb, rs = block(hp, p_prompt[l], buf0, st0, pos_prompt, log_gamma, *layer)
        hs, cbs, rss = block(hs, p_sample[l], state_conv[l], state_ret[l], pos_sample, log_gamma, *layer)
        conv_p.append(cb.astype(state_conv.dtype))
        ret_p.append(rs.astype(state_ret.dtype))
        conv_s.append(cbs.astype(state_conv.dtype))
        ret_s.append(rss.astype(state_ret.dtype))
    y_prompt = rmsnorm(hp, g_final)
    y_sample = rmsnorm(hs, g_final)
    new_conv_prompt = jnp.stack(conv_p, axis=0)
    new_ret_prompt = jnp.stack(ret_p, axis=0)
    new_conv_sample = jnp.stack(conv_s, axis=0)
    new_ret_sample = jnp.stack(ret_s, axis=0)
    return (y_prompt, y_sample, new_conv_prompt, new_ret_prompt, new_conv_sample, new_ret_sample)
```

```python
import functools

import jax
import jax.numpy as jnp
from jax import lax
from jax.experimental import pallas as pl
from jax.experimental.pallas import tpu as pltpu

F32, BF16, I32 = jnp.float32, jnp.bfloat16, jnp.int32

D_MODEL = 1024
CONV_DIM = 512
CONV_WIDTH = 3
RET_DIM = 512
RET_HEADS = 4
RET_DK = 128
RET_DV = 128
RET_CHUNK = 128
ROPE_BASE = 10000.0
IN_PROJ_DIM = 3 * CONV_DIM + 4 * RET_DIM
N_GROUPS = 4
EXPERTS_PER_GROUP = 8
N_EXPERTS = 32
D_FF = 512
PLE_DIM = 256
EPS = 1e-6
PAST_LEN = 16384

LANES = 128
SUBLANES = 8
ROUTER_ROWS = SUBLANES + N_EXPERTS
VMEM_LIMIT = 56 * 1024 * 1024
NEG_BIG = -1e30

MIX_TILE = 256
ROW_TILE = 128
FFN_TILE = 256
SAMPLE_BLOCK = 8


def _rms(x, g):
    return x * lax.rsqrt(jnp.mean(x * x, axis=-1, keepdims=True) + EPS) * g


def _dot(a, b):
    return jnp.dot(a, b, preferred_element_type=F32)


def _dot_nt(a, b):
    return lax.dot_general(a, b, (((1,), (1,)), ((), ())), preferred_element_type=F32)


def _rope(x, cos, sin_signed):
    return x * cos + pltpu.roll(x, RET_DK // 2, 1) * sin_signed


def _route(h, gffn, wr, br, tri, cnt):
    t = h.shape[0]
    xn = _rms(h, gffn).astype(BF16)
    lt = _dot_nt(wr, xn) + br[:, 0:1]
    row8 = lax.broadcasted_iota(I32, (SUBLANES, t), 0).astype(F32)
    gl = lt[0:SUBLANES]
    m = jnp.max(gl, axis=0, keepdims=True)
    g_top = 1.0 / jnp.sum(jnp.exp(gl - m), axis=0, keepdims=True)
    gidx = jnp.min(jnp.where(gl == m, row8, float(SUBLANES)), axis=0, keepdims=True)
    e_sel = jnp.where(gidx == 0.0, lt[8:16],
                      jnp.where(gidx == 1.0, lt[16:24], jnp.where(gidx == 2.0, lt[24:32], lt[32:40])))
    m1 = jnp.max(e_sel, axis=0, keepdims=True)
    i1 = jnp.min(jnp.where(e_sel == m1, row8, float(SUBLANES)), axis=0, keepdims=True)
    rest = jnp.where(row8 == i1, -jnp.inf, e_sel)
    m2 = jnp.max(rest, axis=0, keepdims=True)
    i2 = jnp.min(jnp.where(rest == m2, row8, float(SUBLANES)), axis=0, keepdims=True)
    d = jnp.exp(m2 - m1)
    w1 = g_top / (1.0 + d)
    w2 = g_top * d / (1.0 + d)
    e1 = gidx * float(EXPERTS_PER_GROUP) + i1
    e2 = gidx * float(EXPERTS_PER_GROUP) + i2
    row32 = lax.broadcasted_iota(I32, (N_EXPERTS, t), 0).astype(F32)
    a1 = jnp.where(row32 == e1, 1.0, 0.0)
    a2 = jnp.where(row32 == e2, 1.0, 0.0)
    a = a1 + a2
    base = _dot(a.astype(BF16), tri) + cnt[:, 0:1]
    r1 = jnp.sum(a1 * base, axis=0, keepdims=True)
    r2 = jnp.sum(a2 * base, axis=0, keepdims=True)
    cnt_new = cnt + jnp.sum(a, axis=1, keepdims=True)
    ri = jnp.where(row8 == 0.0, e1, jnp.where(row8 == 1.0, e2, jnp.where(row8 == 2.0, r1,
                                                                         jnp.where(row8 == 3.0, r2, 0.0))))
    rw = jnp.where(row8 == 0.0, w1, jnp.where(row8 == 1.0, w2, 0.0))
    return ri.astype(I32), rw, cnt_new


def _group_norm_gate(o, gate):
    mu = jnp.mean(o, axis=-1, keepdims=True)
    oc = o - mu
    return oc * lax.rsqrt(jnp.mean(oc * oc, axis=-1, keepdims=True) + EPS) * gate


def _mixer_prompt_kernel(x_ref, cos_ref, sin_ref, qdec_ref, kdec_ref, dmask_ref, cdec_ref, gmix_ref, win_ref,
                         convw_ref, retgn_ref, wo_ref, gffn_ref, wr_ref, br_ref, tri_ref,
                         h_ref, ri_ref, rw_ref, cnt_ref, conv_ref, ret_ref,
                         s_scr, z_scr, mix_scr, c_scr):
    b, j = pl.program_id(0), pl.program_id(1)
    n_j = pl.num_programs(1)
    tt = x_ref.shape[1]

    @pl.when(j == 0)
    def _():
        s_scr[...] = jnp.zeros_like(s_scr)
        z_scr[0:SUBLANES, :] = jnp.zeros((SUBLANES, CONV_DIM), F32)

    @pl.when((b == 0) & (j == 0))
    def _():
        c_scr[...] = jnp.zeros_like(c_scr)

    x = x_ref[0]
    u = _rms(x, gmix_ref[...]).astype(BF16)
    proj = _dot(u, win_ref[...])
    gb = proj[:, 0:CONV_DIM]
    z = proj[:, CONV_DIM:2 * CONV_DIM] * proj[:, 2 * CONV_DIM:3 * CONV_DIM]
    q0 = 3 * CONV_DIM

    z_scr[SUBLANES:SUBLANES + tt, :] = z
    cw = convw_ref[...]
    yc = (cw[0:1] * z_scr[SUBLANES - 2:SUBLANES - 2 + tt, :]
          + cw[1:2] * z_scr[SUBLANES - 1:SUBLANES - 1 + tt, :] + cw[2:3] * z)
    mix_scr[:, 0:CONV_DIM] = (gb * yc).astype(BF16)
    z_scr[0:SUBLANES, :] = z_scr[tt:tt + SUBLANES, :]

    retgn = retgn_ref[...]
    for c in range(tt // RET_CHUNK):
        r0 = c * RET_CHUNK
        cos = cos_ref[r0:r0 + RET_CHUNK, :]
        sin = sin_ref[r0:r0 + RET_CHUNK, :]
        for hh in range(RET_HEADS):
            l0 = hh * RET_DK
            qr = _rope(proj[r0:r0 + RET_CHUNK, q0 + l0:q0 + l0 + RET_DK], cos, sin)
            kr = _rope(proj[r0:r0 + RET_CHUNK, q0 + RET_DIM + l0:q0 + RET_DIM + l0 + RET_DK], cos, sin) * (RET_DK ** -0.5)
            v = proj[r0:r0 + RET_CHUNK, q0 + 2 * RET_DIM + l0:q0 + 2 * RET_DIM + l0 + RET_DV].astype(BF16)
            g = proj[r0:r0 + RET_CHUNK, q0 + 3 * RET_DIM + l0:q0 + 3 * RET_DIM + l0 + RET_DV]
            s_old = s_scr[hh]
            scores = _dot_nt(qr.astype(BF16), kr.astype(BF16)) * dmask_ref[hh]
            o = _dot(scores.astype(BF16), v) + _dot((qr * qdec_ref[:, l0:l0 + RET_DK]).astype(BF16), s_old.astype(BF16))
            kd = (kr * kdec_ref[:, l0:l0 + RET_DK]).T.astype(BF16)
            s_scr[hh] = s_old * cdec_ref[hh:hh + 1, :] + _dot(kd, v)
            gate = retgn[:, l0:l0 + RET_DV] * (g * jax.nn.sigmoid(g))
            mix_scr[r0:r0 + RET_CHUNK, CONV_DIM + l0:CONV_DIM + l0 + RET_DV] = _group_norm_gate(o, gate).astype(BF16)

    h = x + _dot(mix_scr[...], wo_ref[...])
    h_ref[0] = h

    ri, rw, cnt_new = _route(h, gffn_ref[...], wr_ref[...], br_ref[...], tri_ref[...], c_scr[...])
    ri_ref[...] = ri
    rw_ref[...] = rw
    c_scr[...] = cnt_new
    cnt_ref[...] = cnt_new

    @pl.when(j == n_j - 1)
    def _():
        conv_ref[0] = z[tt - 2:tt, :]
        ret_ref[0] = s_scr[...]


def _const_spec(shape):
    nd = len(shape)
    return pl.BlockSpec(shape, lambda *_: (0,) * nd)


def _mixer_prompt(x, tabs, wts):
    bsz, t, _ = x.shape
    tt = MIX_TILE
    n_j = t // tt
    cos, sin, qdec, kdec, dmask, cdec = tabs
    gmix, win, convw, retgn, wo, gffn, wr, br, tri = wts
    n_tok = bsz * t
    in_specs = [
        pl.BlockSpec((1, tt, D_MODEL), lambda b, j: (b, j, 0)),
        pl.BlockSpec((tt, LANES), lambda b, j: (j, 0)),
        pl.BlockSpec((tt, LANES), lambda b, j: (j, 0)),
        _const_spec(qdec.shape), _const_spec(kdec.shape), _const_spec(dmask.shape), _const_spec(cdec.shape),
        _const_spec(gmix.shape), _const_spec(win.shape), _const_spec(convw.shape), _const_spec(retgn.shape),
        _const_spec(wo.shape), _const_spec(gffn.shape), _const_spec(wr.shape), _const_spec(br.shape),
        _const_spec(tri.shape),
    ]
    out_shape = (
        jax.ShapeDtypeStruct((bsz, t, D_MODEL), F32),
        jax.ShapeDtypeStruct((SUBLANES, n_tok), I32),
        jax.ShapeDtypeStruct((SUBLANES, n_tok), F32),
        jax.ShapeDtypeStruct((N_EXPERTS, LANES), F32),
        jax.ShapeDtypeStruct((bsz, CONV_WIDTH - 1, CONV_DIM), F32),
        jax.ShapeDtypeStruct((bsz, RET_HEADS, RET_DK, RET_DV), F32),
    )
    out_specs = (
        pl.BlockSpec((1, tt, D_MODEL), lambda b, j: (b, j, 0)),
        pl.BlockSpec((SUBLANES, tt), lambda b, j: (0, b * n_j + j)),
        pl.BlockSpec((SUBLANES, tt), lambda b, j: (0, b * n_j + j)),
        pl.BlockSpec((N_EXPERTS, LANES), lambda b, j: (0, 0)),
        pl.BlockSpec((1, CONV_WIDTH - 1, CONV_DIM), lambda b, j: (b, 0, 0)),
        pl.BlockSpec((1, RET_HEADS, RET_DK, RET_DV), lambda b, j: (b, 0, 0, 0)),
    )
    return pl.pallas_call(
        _mixer_prompt_kernel,
        grid=(bsz, n_j),
        in_specs=in_specs,
        out_specs=out_specs,
        out_shape=out_shape,
        scratch_shapes=[
            pltpu.VMEM((RET_HEADS, RET_DK, RET_DV), F32),
            pltpu.VMEM((tt + SUBLANES, CONV_DIM), F32),
            pltpu.VMEM((tt, D_MODEL), BF16),
            pltpu.VMEM((N_EXPERTS, LANES), F32),
        ],
        compiler_params=pltpu.CompilerParams(
            dimension_semantics=("arbitrary", "arbitrary"), vmem_limit_bytes=VMEM_LIMIT),
        name="mixer_prompt",
    )(x, cos, sin, qdec, kdec, dmask, cdec, gmix, win, convw, retgn, wo, gffn, wr, br, tri)


def _sample_proj_kernel(x_ref, sc_ref, cos_ref, sin_ref, gmix_ref, win_ref, convw_ref, retgn_ref,
                        yc_ref, q_ref, k_ref, v_ref, gate_ref, conv_ref):
    x = x_ref[...]
    u = _rms(x, gmix_ref[...]).astype(BF16)
    proj = _dot(u, win_ref[...])
    gb = proj[:, 0:CONV_DIM]
    z = proj[:, CONV_DIM:2 * CONV_DIM] * proj[:, 2 * CONV_DIM:3 * CONV_DIM]
    cw = convw_ref[...]
    buf0, buf1 = sc_ref[:, 0:CONV_DIM], sc_ref[:, CONV_DIM:2 * CONV_DIM]
    yc_ref[...] = gb * (cw[0:1] * buf0 + cw[1:2] * buf1 + cw[2:3] * z)
    conv_ref[:, 0:CONV_DIM] = buf1
    conv_ref[:, CONV_DIM:2 * CONV_DIM] = z
    q0 = 3 * CONV_DIM
    cos, sin = cos_ref[0:1, :], sin_ref[0:1, :]
    retgn = retgn_ref[...]
    for hh in range(RET_HEADS):
        l0 = hh * RET_DK
        q_ref[:, l0:l0 + RET_DK] = _rope(proj[:, q0 + l0:q0 + l0 + RET_DK], cos, sin)
        k_ref[:, l0:l0 + RET_DK] = _rope(proj[:, q0 + RET_DIM + l0:q0 + RET_DIM + l0 + RET_DK], cos, sin) * (RET_DK ** -0.5)
    v_ref[...] = proj[:, q0 + 2 * RET_DIM:q0 + 3 * RET_DIM]
    g = proj[:, q0 + 3 * RET_DIM:q0 + 4 * RET_DIM]
    gate_ref[...] = retgn * (g * jax.nn.sigmoid(g))


def _sample_proj(x, sc, cos, sin, gmix, win, convw, retgn):
    n = x.shape[0]
    outs = (
        jax.ShapeDtypeStruct((n, CONV_DIM), F32), jax.ShapeDtypeStruct((n, RET_DIM), F32),
        jax.ShapeDtypeStruct((n, RET_DIM), F32), jax.ShapeDtypeStruct((n, RET_DIM), F32),
        jax.ShapeDtypeStruct((n, RET_DIM), F32), jax.ShapeDtypeStruct((n, 2 * CONV_DIM), F32),
    )
    args = (x, sc, cos, sin, gmix, win, convw, retgn)
    return pl.pallas_call(
        _sample_proj_kernel,
        grid=(1,),
        in_specs=[_const_spec(a.shape) for a in args],
        out_specs=tuple(_const_spec(o.shape) for o in outs),
        out_shape=outs,
        compiler_params=pltpu.CompilerParams(dimension_semantics=("arbitrary",), vmem_limit_bytes=VMEM_LIMIT),
        name="sample_proj",
    )(*args)


def _sample_state_kernel(s_ref, qt_ref, kt_ref, v_ref, dec_ref, snew_ref, o_ref):
    for r in range(s_ref.shape[0]):
        for hh in range(RET_HEADS):
            l0 = hh * RET_DV
            s = s_ref[r, hh]
            qc = qt_ref[0, hh, :, r:r + 1]
            kc = kt_ref[0, hh, :, r:r + 1]
            vr = v_ref[r:r + 1, l0:l0 + RET_DV]
            qdec = dec_ref[hh:hh + 1, :]
            cdec = dec_ref[RET_HEADS + hh:RET_HEADS + hh + 1, :]
            qk = jnp.sum(qc * kc, axis=0, keepdims=True)
            o_ref[r:r + 1, l0:l0 + RET_DV] = qk * vr + jnp.sum((qc * qdec[:, 0:1]) * s, axis=0, keepdims=True)
            snew_ref[r, hh] = s * cdec + kc * vr


def _sample_state(s, qt, kt, v, dec):
    n = s.shape[0]
    nb = SAMPLE_BLOCK
    return pl.pallas_call(
        _sample_state_kernel,
        grid=(n // nb,),
        in_specs=[
            pl.BlockSpec((nb, RET_HEADS, RET_DK, RET_DV), lambda i: (i, 0, 0, 0)),
            pl.BlockSpec((1, RET_HEADS, RET_DK, nb), lambda i: (i, 0, 0, 0)),
            pl.BlockSpec((1, RET_HEADS, RET_DK, nb), lambda i: (i, 0, 0, 0)),
            pl.BlockSpec((nb, RET_DIM), lambda i: (i, 0)),
            _const_spec(dec.shape),
        ],
        out_specs=(
            pl.BlockSpec((nb, RET_HEADS, RET_DK, RET_DV), lambda i: (i, 0, 0, 0)),
            pl.BlockSpec((nb, RET_DIM), lambda i: (i, 0)),
        ),
        out_shape=(jax.ShapeDtypeStruct(s.shape, F32), jax.ShapeDtypeStruct((n, RET_DIM), F32)),
        compiler_params=pltpu.CompilerParams(dimension_semantics=("arbitrary",), vmem_limit_bytes=VMEM_LIMIT),
        name="sample_state",
    )(s, qt, kt, v, dec)


def _sample_out_kernel(x_ref, yc_ref, o_ref, gate_ref, wo_ref, gffn_ref, wr_ref, br_ref, tri_ref, cnt0_ref,
                       h_ref, ri_ref, rw_ref, cnt_ref):
    parts = [yc_ref[...].astype(BF16)]
    for hh in range(RET_HEADS):
        l0 = hh * RET_DV
        parts.append(_group_norm_gate(o_ref[:, l0:l0 + RET_DV], gate_ref[:, l0:l0 + RET_DV]).astype(BF16))
    h = x_ref[...] + _dot(jnp.concatenate(parts, axis=-1), wo_ref[...])
    h_ref[...] = h
    ri, rw, cnt_new = _route(h, gffn_ref[...], wr_ref[...], br_ref[...], tri_ref[...], cnt0_ref[...])
    ri_ref[...] = ri
    rw_ref[...] = rw
    cnt_ref[...] = cnt_new


def _sample_out(x, yc, o, gate, wo, gffn, wr, br, tri, cnt0):
    n = x.shape[0]
    outs = (
        jax.ShapeDtypeStruct((n, D_MODEL), F32), jax.ShapeDtypeStruct((SUBLANES, n), I32),
        jax.ShapeDtypeStruct((SUBLANES, n), F32), jax.ShapeDtypeStruct((N_EXPERTS, LANES), F32),
    )
    args = (x, yc, o, gate, wo, gffn, wr, br, tri, cnt0)
    return pl.pallas_call(
        _sample_out_kernel,
        grid=(1,),
        in_specs=[_const_spec(a.shape) for a in args],
        out_specs=tuple(_const_spec(o_.shape) for o_ in outs),
        out_shape=outs,
        compiler_params=pltpu.CompilerParams(dimension_semantics=("arbitrary",), vmem_limit_bytes=VMEM_LIMIT),
        name="sample_out",
    )(*args)


def _mixer_sample(x, state_conv, state_ret, gmix, win, convw, retgn, wo, gffn, wr, br, cnt0):
    n = x.shape[0]
    lg = _log_gamma()
    cos, sin = _rope_tables(jnp.full((SUBLANES,), PAST_LEN, I32))
    yc, q, k, v, gate, conv_new = _sample_proj(x, state_conv.reshape(n, 2 * CONV_DIM), cos, sin, gmix, win, convw, retgn)
    nb = SAMPLE_BLOCK
    cols = lambda a: a.reshape(n // nb, nb, RET_HEADS, RET_DK).transpose(0, 2, 3, 1)
    one = jnp.ones((RET_HEADS, 1), F32)
    dec = jnp.concatenate([jnp.exp(lg[:, None] * 1.0) * one, jnp.exp(lg[:, None] * 1.0) * one], axis=0)
    dec = jnp.broadcast_to(dec, (2 * RET_HEADS, LANES))
    s_new, o = _sample_state(state_ret, cols(q), cols(k), v, dec)
    h, ri, rw, cnt = _sample_out(x, yc, o, gate, wo, gffn, wr, br, _strict_upper(n), cnt0)
    return h, ri, rw, cnt, conv_new.reshape(n, 2, CONV_DIM), s_new


def _wait_rows(hbm_ref, vmem_ref, sem, times):
    for _ in range(times):
        pltpu.make_async_copy(hbm_ref.at[pl.ds(0, vmem_ref.shape[0])], vmem_ref, sem).wait()


def _dispatch_kernel(n_p, pos_ref, hp_ref, hs_ref, gffn_ref, xs_ref, idx_smem, xn_scr, isem, rsem):
    i, n = pl.program_id(0), pl.num_programs(0)
    slot = i % 2

    def idx_copy(step, sl):
        return pltpu.make_async_copy(pos_ref.at[step], idx_smem.at[sl], isem.at[sl])

    @pl.when(i == 0)
    def _():
        idx_copy(0, 0).start()

    @pl.when(i + 1 < n)
    def _():
        idx_copy(i + 1, 1 - slot).start()

    @pl.when(i >= 2)
    def _():
        _wait_rows(xs_ref, xn_scr.at[slot], rsem.at[slot], 2)

    h = jnp.where(i < n_p, hp_ref[...], hs_ref[...])
    xn_scr[slot] = _rms(h, gffn_ref[...])
    idx_copy(i, slot).wait()

    def body(r, carry):
        for k in range(2):
            p = idx_smem[slot, k * ROW_TILE + r]
            pltpu.make_async_copy(xn_scr.at[slot, pl.ds(r, 1)], xs_ref.at[pl.ds(p, 1)], rsem.at[slot]).start()
        return carry

    lax.fori_loop(0, ROW_TILE, body, 0, unroll=8)

    @pl.when(i == n - 1)
    def _():
        _wait_rows(xs_ref, xn_scr.at[slot], rsem.at[slot], 2)

    @pl.when((i == n - 1) & (n >= 2))
    def _():
        _wait_rows(xs_ref, xn_scr.at[1 - slot], rsem.at[1 - slot], 2)


def _dispatch(pos_tab, h_p, h_s, gffn):
    n_p = h_p.shape[0] // ROW_TILE
    n = pos_tab.shape[0]
    rows = 2 * n * ROW_TILE
    return pl.pallas_call(
        functools.partial(_dispatch_kernel, n_p),
        grid=(n,),
        in_specs=[
            pl.BlockSpec(memory_space=pl.ANY),
            pl.BlockSpec((ROW_TILE, D_MODEL), lambda i: (jnp.minimum(i, n_p - 1), 0)),
            pl.BlockSpec((ROW_TILE, D_MODEL), lambda i: (jnp.maximum(i - n_p, 0), 0)),
            _const_spec(gffn.shape),
        ],
        out_specs=pl.BlockSpec(memory_space=pl.ANY),
        out_shape=jax.ShapeDtypeStruct((rows, D_MODEL), F32),
        scratch_shapes=[
            pltpu.SMEM((2, 2 * ROW_TILE), I32),
            pltpu.VMEM((2, ROW_TILE, D_MODEL), F32),
            pltpu.SemaphoreType.DMA((2,)),
            pltpu.SemaphoreType.DMA((2,)),
        ],
        compiler_params=pltpu.CompilerParams(dimension_semantics=("arbitrary",), vmem_limit_bytes=VMEM_LIMIT),
        name="moe_dispatch",
    )(pos_tab, h_p, h_s, gffn)


def _ffn_kernel(tile_ref, exp_ref, lo_ref, hi_ref, x_ref, wg_ref, wu_ref, wd_ref, o_ref, wgu_scr, wd_scr):
    w = pl.program_id(0)
    prev = jnp.maximum(w - 1, 0)
    lo, hi = lo_ref[w], hi_ref[w]

    @pl.when((w == 0) | (exp_ref[w] != exp_ref[prev]))
    def _():
        wgu_scr[:, 0:D_FF] = wg_ref[0].astype(BF16)
        wgu_scr[:, D_FF:2 * D_FF] = wu_ref[0].astype(BF16)
        wd_scr[...] = wd_ref[0].astype(BF16)

    def expert_rows():
        gu = _dot(x_ref[...].astype(BF16), wgu_scr[...])
        g, u = gu[:, 0:D_FF], gu[:, D_FF:2 * D_FF]
        return _dot((g * jax.nn.sigmoid(g) * u).astype(BF16), wd_scr[...])

    first = (w == 0) | (tile_ref[w] != tile_ref[prev])

    @pl.when((hi > lo) & first)
    def _():
        o_ref[...] = expert_rows()

    @pl.when((hi > lo) & jnp.logical_not(first))
    def _():
        row = lax.broadcasted_iota(I32, (FFN_TILE, 1), 0)
        o_ref[...] = jnp.where((row >= lo) & (row < hi), expert_rows(), o_ref[...])


def _expert_ffn(items, xs, w_gate, w_up, w_down):
    tile, exp, lo, hi = items
    n_items = tile.shape[0]
    grid_spec = pltpu.PrefetchScalarGridSpec(
        num_scalar_prefetch=4,
        grid=(n_items,),
        in_specs=[
            pl.BlockSpec((FFN_TILE, D_MODEL), lambda w, t, e, l, h: (t[w], 0)),
            pl.BlockSpec((1, D_MODEL, D_FF), lambda w, t, e, l, h: (e[w], 0, 0)),
            pl.BlockSpec((1, D_MODEL, D_FF), lambda w, t, e, l, h: (e[w], 0, 0)),
            pl.BlockSpec((1, D_FF, D_MODEL), lambda w, t, e, l, h: (e[w], 0, 0)),
        ],
        out_specs=pl.BlockSpec((FFN_TILE, D_MODEL), lambda w, t, e, l, h: (t[w], 0)),
        scratch_shapes=[pltpu.VMEM((D_MODEL, 2 * D_FF), BF16), pltpu.VMEM((D_FF, D_MODEL), BF16)],
    )
    return pl.pallas_call(
        _ffn_kernel,
        grid_spec=grid_spec,
        out_shape=jax.ShapeDtypeStruct(xs.shape, F32),
        compiler_params=pltpu.CompilerParams(dimension_semantics=("arbitrary",), vmem_limit_bytes=VMEM_LIMIT),
        name="moe_ffn",
    )(tile, exp, lo, hi, xs, w_gate, w_up, w_down)


def _work_items(counts, n_rows):
    n_tiles = n_rows // FFN_TILE
    n_items = n_tiles + N_EXPERTS - 1
    off = jnp.cumsum(counts) - counts
    first_tile = off // FFN_TILE
    last_tile = jnp.maximum(off + counts - 1, off) // FFN_TILE
    n_e = jnp.where(counts > 0, last_tile - first_tile + 1, 0)
    start = jnp.cumsum(n_e) - n_e
    total = jnp.sum(n_e)
    w = jnp.minimum(jnp.arange(n_items, dtype=I32), total - 1)
    e = jnp.max(jnp.where((start[None, :] <= w[:, None]) & (n_e[None, :] > 0), jnp.arange(N_EXPERTS)[None, :], 0), axis=1)
    tile = first_tile[e] + (w - start[e])
    lo = jnp.clip(off[e] - tile * FFN_TILE, 0, FFN_TILE)
    hi = jnp.clip(off[e] + counts[e] - tile * FFN_TILE, 0, FFN_TILE)
    hi = jnp.where(jnp.arange(n_items) < total, hi, lo)
    return tile.astype(I32), e.astype(I32), lo.astype(I32), hi.astype(I32)


def _combine_kernel(pos_ref, h_ref, p_ref, w_ref, ys_ref, gple_ref, wpg_ref, wpp_ref, gfin_ref, o_ref,
                    idx_smem, rows_scr, isem, rsem):
    i, n = pl.program_id(0), pl.num_programs(0)
    slot = i % 2
    tt = h_ref.shape[0]

    def idx_copy(step, sl):
        return pltpu.make_async_copy(pos_ref.at[step], idx_smem.at[sl], isem.at[sl])

    @pl.when(i == 0)
    def _():
        idx_copy(0, 0).start()

    @pl.when(i + 1 < n)
    def _():
        idx_copy(i + 1, 1 - slot).start()

    idx_copy(i, slot).wait()

    def body(r, carry):
        for k in range(2):
            p = idx_smem[slot, k * tt + r]
            pltpu.make_async_copy(ys_ref.at[pl.ds(p, 1)], rows_scr.at[k, pl.ds(r, 1)], rsem).start()
        return carry

    lax.fori_loop(0, tt, body, 0, unroll=8)
    _wait_rows(ys_ref, rows_scr.at[0], rsem, 2)

    wts = w_ref[...]
    h = h_ref[...] + wts[:, 0:1] * rows_scr[0] + wts[:, 1:2] * rows_scr[1]
    gate = jax.nn.sigmoid(_dot(_rms(h, gple_ref[...]).astype(BF16), wpg_ref[...]))
    h = h + _dot(p_ref[...].astype(BF16), wpp_ref[...]) * gate
    o_ref[...] = _rms(h, gfin_ref[...])


def _combine(pos_tab, h, p, wcols, ys, gple, wpg, wpp, gfin):
    n_tok = h.shape[0]
    n = pos_tab.shape[0]
    tt = n_tok // n
    return pl.pallas_call(
        _combine_kernel,
        grid=(n,),
        in_specs=[
            pl.BlockSpec(memory_space=pl.ANY),
            pl.BlockSpec((tt, D_MODEL), lambda i: (i, 0)),
            pl.BlockSpec((tt, PLE_DIM), lambda i: (i, 0)),
            pl.BlockSpec((tt, SUBLANES), lambda i: (i, 0)),
            pl.BlockSpec(memory_space=pl.ANY),
            _const_spec(gple.shape), _const_spec(wpg.shape), _const_spec(wpp.shape), _const_spec(gfin.shape),
        ],
        out_specs=pl.BlockSpec((tt, D_MODEL), lambda i: (i, 0)),
        out_shape=jax.ShapeDtypeStruct((n_tok, D_MODEL), F32),
        scratch_shapes=[
            pltpu.SMEM((2, 2 * tt), I32),
            pltpu.VMEM((2, tt, D_MODEL), F32),
            pltpu.SemaphoreType.DMA((2,)),
            pltpu.SemaphoreType.DMA,
        ],
        compiler_params=pltpu.CompilerParams(dimension_semantics=("arbitrary",), vmem_limit_bytes=VMEM_LIMIT),
        name="moe_combine",
    )(pos_tab, h, p, wcols, ys, gple, wpg, wpp, gfin)


def _log_gamma():
    return jnp.log(1.0 - 2.0 ** (-5.0 - jnp.arange(RET_HEADS, dtype=F32)))


def _rope_tables(pos):
    inv = 1.0 / (ROPE_BASE ** (jnp.arange(0, RET_DK, 2, dtype=F32) / RET_DK))
    ang = pos.astype(F32)[:, None] * inv[None, :]
    cos, sin = jnp.cos(ang), jnp.sin(ang)
    return jnp.concatenate([cos, cos], axis=-1), jnp.concatenate([-sin, sin], axis=-1)


def _decay_tables(c):
    lg = _log_gamma()
    idx = jnp.arange(c, dtype=F32)
    diff = idx[:, None] - idx[None, :]
    dmask = jnp.where((diff >= 0.0)[None], jnp.exp(lg[:, None, None] * jnp.maximum(diff, 0.0)[None]), 0.0)
    q_dec = jnp.exp(lg[None, :] * (idx[:, None] + 1.0))
    k_dec = jnp.exp(lg[None, :] * (c - 1.0 - idx[:, None]))
    c_dec = jnp.exp(lg * c)
    lanes = lambda a: jnp.repeat(a, RET_DK, axis=1)
    cdec = jnp.zeros((SUBLANES, LANES), F32).at[:RET_HEADS].set(jnp.broadcast_to(c_dec[:, None], (RET_HEADS, LANES)))
    return lanes(q_dec), lanes(k_dec), dmask, cdec


def _router_params(w_rg, b_rg, w_re, b_re):
    wr = jnp.zeros((ROUTER_ROWS, D_MODEL), F32).at[:N_GROUPS].set(w_rg.T).at[SUBLANES:].set(w_re.T)
    br = jnp.full((ROUTER_ROWS,), NEG_BIG, F32).at[:N_GROUPS].set(b_rg).at[SUBLANES:].set(b_re.reshape(-1))
    return wr.astype(BF16), jnp.broadcast_to(br[:, None], (ROUTER_ROWS, LANES))


def _strict_upper(t):
    i = jnp.arange(t)
    return (i[:, None] < i[None, :]).astype(BF16)


def _pad_rows(a, rows):
    return jnp.zeros((rows,) + a.shape[1:], a.dtype).at[:a.shape[0]].set(a)


def kernel(x_prompt, x_sample, state_conv, state_ret, p_prompt, p_sample, g_mix, w_in, conv_w, ret_gn, w_o, g_ffn,
           w_router_group, b_router_group, w_router_expert, b_router_expert, w_gate, w_up, w_down, g_ple,
           w_ple_proj, w_ple_gate, g_final):
    bp, tp, _ = x_prompt.shape
    bs = x_sample.shape[0]
    assert x_sample.shape[1] == 1 and g_mix.shape[0] == 1, "one layer, one new token per sample request"
    n_p = bp * tp

    tabs = _rope_tables(jnp.arange(tp, dtype=I32)) + _decay_tables(RET_CHUNK)
    wr, br = _router_params(w_router_group[0], b_router_group[0], w_router_expert[0], b_router_expert[0])
    win, wo, convw = w_in[0].astype(BF16), w_o[0].astype(BF16), _pad_rows(conv_w[0], SUBLANES)

    h_p, ri_p, rw_p, cnt_p, conv_p, ret_p = _mixer_prompt(
        x_prompt, tabs, (g_mix, win, convw, ret_gn, wo, g_ffn, wr, br, _strict_upper(MIX_TILE)))
    h_s, ri_s, rw_s, cnt, conv_s, ret_s = _mixer_sample(
        x_sample[:, 0], state_conv[0], state_ret[0], g_mix, win, convw, ret_gn, wo, g_ffn, wr, br, cnt_p)
    h_p = h_p.reshape(n_p, D_MODEL)

    counts = cnt[:, 0].astype(I32)
    off = jnp.cumsum(counts) - counts
    ri = jnp.concatenate([ri_p, ri_s], axis=1)
    pos1, pos2 = off[ri[0]] + ri[2], off[ri[1]] + ri[3]
    n_all = n_p + bs
    pos_tab = jnp.concatenate([pos1.reshape(-1, ROW_TILE), pos2.reshape(-1, ROW_TILE)], axis=1)

    xs = _dispatch(pos_tab, h_p, h_s, g_ffn)
    ys = _expert_ffn(_work_items(counts, 2 * n_all), xs, w_gate[0], w_up[0], w_down[0])

    wpg, wpp = w_ple_gate[0].astype(BF16), w_ple_proj[0].astype(BF16)
    gfin = g_final[None, :]

    def finish(h, p, pa, pb, rw, tile):
        tab = jnp.concatenate([pa.reshape(-1, tile), pb.reshape(-1, tile)], axis=1)
        return _combine(tab, h, p, rw.T, ys, g_ple, wpg, wpp, gfin)

    y_p = finish(h_p, p_prompt[0].reshape(n_p, PLE_DIM), pos1[:n_p], pos2[:n_p], rw_p, 2 * ROW_TILE)
    y_s = finish(h_s, p_sample[0].reshape(bs, PLE_DIM), pos1[n_p:], pos2[n_p:], rw_s, bs)
    return (y_p.reshape(bp, tp, D_MODEL), y_s.reshape(bs, 1, D_MODEL), conv_p[None], ret_p[None],
            conv_s[None], ret_s[None])
```

```python
import functools

import jax
import jax.numpy as jnp
from jax import lax
from jax.experimental import pallas as pl
from jax.experimental.pallas import tpu as pltpu

F32, BF16, I32 = jnp.float32, jnp.bfloat16, jnp.int32

D_MODEL = 1024
CONV_DIM = 512
CONV_WIDTH = 3
RET_DIM = 512
RET_HEADS = 4
RET_DK = 128
RET_DV = 128
RET_CHUNK = 128
ROPE_BASE = 10000.0
IN_PROJ_DIM = 3 * CONV_DIM + 4 * RET_DIM
N_GROUPS = 4
EXPERTS_PER_GROUP = 8
N_EXPERTS = 32
D_FF = 512
PLE_DIM = 256
EPS = 1e-6
PAST_LEN = 16384

LANES = 128
SUBLANES = 8
ROW_CHUNKS = D_MODEL // LANES
ROUTER_ROWS = SUBLANES + N_EXPERTS
VMEM_LIMIT = 56 * 1024 * 1024
NEG_BIG = -1e30

TOK_TILE = 256
PAIR_ROWS = 2 * TOK_TILE
FFN_TILE = 256
SAMPLE_BLOCK = 8
RUN_BITS = 9
SRC_BITS = 10
TAB_COLS = 384


def _rms(x, g):
    return x * lax.rsqrt(jnp.mean(x * x, axis=-1, keepdims=True) + EPS) * g


def _dot(a, b):
    return jnp.dot(a, b, preferred_element_type=F32)


def _dot_nt(a, b):
    return lax.dot_general(a, b, (((1,), (1,)), ((), ())), preferred_element_type=F32)


def _dot_tn(a, b):
    return lax.dot_general(a, b, (((0,), (0,)), ((), ())), preferred_element_type=F32)


def _rope(x, cos, sin_signed):
    return x * cos + pltpu.roll(x, RET_DK // 2, 1) * sin_signed


def _const_spec(shape):
    nd = len(shape)
    return pl.BlockSpec(shape, lambda *_: (0,) * nd)


def _params(*sem):
    return pltpu.CompilerParams(dimension_semantics=sem, vmem_limit_bytes=VMEM_LIMIT)


def _route(h, gffn, wr, br, tri, ltri):
    t = h.shape[0]
    xn = _rms(h, gffn).astype(BF16)
    lt = _dot_nt(wr, xn) + br[:, 0:1]
    row8 = lax.broadcasted_iota(I32, (SUBLANES, t), 0).astype(F32)
    gl = lt[0:SUBLANES]
    m = jnp.max(gl, axis=0, keepdims=True)
    g_top = 1.0 / jnp.sum(jnp.exp(gl - m), axis=0, keepdims=True)
    gidx = jnp.min(jnp.where(gl == m, row8, float(SUBLANES)), axis=0, keepdims=True)
    e_sel = jnp.where(gidx == 0.0, lt[8:16],
                      jnp.where(gidx == 1.0, lt[16:24], jnp.where(gidx == 2.0, lt[24:32], lt[32:40])))
    m1 = jnp.max(e_sel, axis=0, keepdims=True)
    i1 = jnp.min(jnp.where(e_sel == m1, row8, float(SUBLANES)), axis=0, keepdims=True)
    rest = jnp.where(row8 == i1, -jnp.inf, e_sel)
    m2 = jnp.max(rest, axis=0, keepdims=True)
    i2 = jnp.min(jnp.where(rest == m2, row8, float(SUBLANES)), axis=0, keepdims=True)
    d = jnp.exp(m2 - m1)
    w1 = g_top / (1.0 + d)
    w2 = g_top * d / (1.0 + d)
    e1 = gidx * float(EXPERTS_PER_GROUP) + i1
    e2 = gidx * float(EXPERTS_PER_GROUP) + i2
    row32 = lax.broadcasted_iota(I32, (N_EXPERTS, t), 0).astype(F32)
    a1 = jnp.where(row32 == e1, 1.0, 0.0)
    a2 = jnp.where(row32 == e2, 1.0, 0.0)
    a = a1 + a2
    n = jnp.broadcast_to(jnp.sum(a, axis=1, keepdims=True), (N_EXPERTS, LANES))
    start = _dot(ltri, n.astype(BF16))
    base = _dot(a.astype(BF16), tri) + start[:, 0:1]
    r1 = jnp.sum(a1 * base, axis=0, keepdims=True)
    r2 = jnp.sum(a2 * base, axis=0, keepdims=True)
    ri = jnp.where(row8 == 0.0, e1, jnp.where(row8 == 1.0, e2, jnp.where(row8 == 2.0, r1,
                                                                         jnp.where(row8 == 3.0, r2, 0.0))))
    rw = jnp.where(row8 == 0.0, w1, jnp.where(row8 == 1.0, w2, 0.0))
    return ri.astype(I32), rw, n


def _group_norm_gate(o, gate):
    mu = jnp.mean(o, axis=-1, keepdims=True)
    oc = o - mu
    return oc * lax.rsqrt(jnp.mean(oc * oc, axis=-1, keepdims=True) + EPS) * gate


def _mixer_prompt_kernel(x_ref, cos_ref, sin_ref, qdec_ref, kdec_ref, dmask_ref, cdec_ref, gmix_ref, win_ref,
                         convw_ref, retgn_ref, wo_ref, gffn_ref, wr_ref, br_ref, tri_ref, ltri_ref,
                         h_ref, ri_ref, rw_ref, n_ref, conv_ref, ret_ref,
                         s_scr, z_scr, mix_scr):
    j = pl.program_id(1)
    n_j = pl.num_programs(1)
    tt = x_ref.shape[1]

    @pl.when(j == 0)
    def _():
        s_scr[...] = jnp.zeros_like(s_scr)
        z_scr[0:SUBLANES, :] = jnp.zeros((SUBLANES, CONV_DIM), F32)

    x = x_ref[0]
    u = _rms(x, gmix_ref[...]).astype(BF16)
    proj = _dot(u, win_ref[...])
    gb = proj[:, 0:CONV_DIM]
    z = proj[:, CONV_DIM:2 * CONV_DIM] * proj[:, 2 * CONV_DIM:3 * CONV_DIM]
    q0 = 3 * CONV_DIM

    z_scr[SUBLANES:SUBLANES + tt, :] = z
    cw = convw_ref[...]
    yc = (cw[0:1] * z_scr[SUBLANES - 2:SUBLANES - 2 + tt, :]
          + cw[1:2] * z_scr[SUBLANES - 1:SUBLANES - 1 + tt, :] + cw[2:3] * z)
    mix_scr[:, 0:CONV_DIM] = (gb * yc).astype(BF16)
    z_scr[0:SUBLANES, :] = z_scr[tt:tt + SUBLANES, :]

    retgn = retgn_ref[...]
    for c in range(tt // RET_CHUNK):
        r0 = c * RET_CHUNK
        cos = cos_ref[r0:r0 + RET_CHUNK, :]
        sin = sin_ref[r0:r0 + RET_CHUNK, :]
        for hh in range(RET_HEADS):
            l0 = hh * RET_DK
            qr = _rope(proj[r0:r0 + RET_CHUNK, q0 + l0:q0 + l0 + RET_DK], cos, sin)
            kr = _rope(proj[r0:r0 + RET_CHUNK, q0 + RET_DIM + l0:q0 + RET_DIM + l0 + RET_DK], cos, sin) * (RET_DK ** -0.5)
            v = proj[r0:r0 + RET_CHUNK, q0 + 2 * RET_DIM + l0:q0 + 2 * RET_DIM + l0 + RET_DV].astype(BF16)
            g = proj[r0:r0 + RET_CHUNK, q0 + 3 * RET_DIM + l0:q0 + 3 * RET_DIM + l0 + RET_DV]
            s_old = s_scr[hh]
            scores = _dot_nt(qr.astype(BF16), kr.astype(BF16)) * dmask_ref[hh]
            o = _dot(scores.astype(BF16), v) + _dot((qr * qdec_ref[:, l0:l0 + RET_DK]).astype(BF16), s_old.astype(BF16))
            kd = (kr * kdec_ref[:, l0:l0 + RET_DK]).T.astype(BF16)
            s_scr[hh] = s_old * cdec_ref[hh:hh + 1, :] + _dot(kd, v)
            gate = retgn[:, l0:l0 + RET_DV] * (g * jax.nn.sigmoid(g))
            mix_scr[r0:r0 + RET_CHUNK, CONV_DIM + l0:CONV_DIM + l0 + RET_DV] = _group_norm_gate(o, gate).astype(BF16)

    h = x + _dot(mix_scr[...], wo_ref[...])
    h_ref[0] = h

    ri, rw, n = _route(h, gffn_ref[...], wr_ref[...], br_ref[...], tri_ref[...], ltri_ref[...])
    ri_ref[...] = ri
    rw_ref[...] = rw
    n_ref[0] = n

    @pl.when(j == n_j - 1)
    def _():
        conv_ref[0] = z[tt - 2:tt, :]
        ret_ref[0] = s_scr[...]


def _mixer_prompt(x, tabs, wts):
    bsz, t, _ = x.shape
    tt = TOK_TILE
    n_j = t // tt
    n_tok = bsz * t
    args = (x,) + tuple(tabs) + tuple(wts)
    in_specs = [
        pl.BlockSpec((1, tt, D_MODEL), lambda b, j: (b, j, 0)),
        pl.BlockSpec((tt, LANES), lambda b, j: (j, 0)),
        pl.BlockSpec((tt, LANES), lambda b, j: (j, 0)),
    ] + [_const_spec(a.shape) for a in args[3:]]
    out_shape = (
        jax.ShapeDtypeStruct((bsz, t, D_MODEL), F32),
        jax.ShapeDtypeStruct((SUBLANES, n_tok), I32),
        jax.ShapeDtypeStruct((SUBLANES, n_tok), F32),
        jax.ShapeDtypeStruct((bsz * n_j, N_EXPERTS, LANES), F32),
        jax.ShapeDtypeStruct((bsz, CONV_WIDTH - 1, CONV_DIM), F32),
        jax.ShapeDtypeStruct((bsz, RET_HEADS, RET_DK, RET_DV), F32),
    )
    out_specs = (
        pl.BlockSpec((1, tt, D_MODEL), lambda b, j: (b, j, 0)),
        pl.BlockSpec((SUBLANES, tt), lambda b, j: (0, b * n_j + j)),
        pl.BlockSpec((SUBLANES, tt), lambda b, j: (0, b * n_j + j)),
        pl.BlockSpec((1, N_EXPERTS, LANES), lambda b, j: (b * n_j + j, 0, 0)),
        pl.BlockSpec((1, CONV_WIDTH - 1, CONV_DIM), lambda b, j: (b, 0, 0)),
        pl.BlockSpec((1, RET_HEADS, RET_DK, RET_DV), lambda b, j: (b, 0, 0, 0)),
    )
    return pl.pallas_call(
        _mixer_prompt_kernel,
        grid=(bsz, n_j),
        in_specs=in_specs,
        out_specs=out_specs,
        out_shape=out_shape,
        scratch_shapes=[
            pltpu.VMEM((RET_HEADS, RET_DK, RET_DV), F32),
            pltpu.VMEM((tt + SUBLANES, CONV_DIM), F32),
            pltpu.VMEM((tt, D_MODEL), BF16),
        ],
        compiler_params=_params("arbitrary", "arbitrary"),
        name="mixer_prompt",
    )(*args)


def _sample_proj_kernel(x_ref, sc_ref, cos_ref, sin_ref, gmix_ref, win_ref, convw_ref, retgn_ref,
                        yc_ref, q_ref, k_ref, v_ref, gate_ref, conv_ref):
    x = x_ref[...]
    u = _rms(x, gmix_ref[...]).astype(BF16)
    proj = _dot(u, win_ref[...])
    gb = proj[:, 0:CONV_DIM]
    z = proj[:, CONV_DIM:2 * CONV_DIM] * proj[:, 2 * CONV_DIM:3 * CONV_DIM]
    cw = convw_ref[...]
    buf0, buf1 = sc_ref[:, 0:CONV_DIM], sc_ref[:, CONV_DIM:2 * CONV_DIM]
    yc_ref[...] = gb * (cw[0:1] * buf0 + cw[1:2] * buf1 + cw[2:3] * z)
    conv_ref[:, 0:CONV_DIM] = buf1
    conv_ref[:, CONV_DIM:2 * CONV_DIM] = z
    q0 = 3 * CONV_DIM
    cos, sin = cos_ref[0:1, :], sin_ref[0:1, :]
    retgn = retgn_ref[...]
    for hh in range(RET_HEADS):
        l0 = hh * RET_DK
        q_ref[:, l0:l0 + RET_DK] = _rope(proj[:, q0 + l0:q0 + l0 + RET_DK], cos, sin)
        k_ref[:, l0:l0 + RET_DK] = _rope(proj[:, q0 + RET_DIM + l0:q0 + RET_DIM + l0 + RET_DK], cos, sin) * (RET_DK ** -0.5)
    v_ref[...] = proj[:, q0 + 2 * RET_DIM:q0 + 3 * RET_DIM]
    g = proj[:, q0 + 3 * RET_DIM:q0 + 4 * RET_DIM]
    gate_ref[...] = retgn * (g * jax.nn.sigmoid(g))


def _sample_proj(x, sc, cos, sin, gmix, win, convw, retgn):
    n = x.shape[0]
    outs = (
        jax.ShapeDtypeStruct((n, CONV_DIM), F32), jax.ShapeDtypeStruct((n, RET_DIM), F32),
        jax.ShapeDtypeStruct((n, RET_DIM), F32), jax.ShapeDtypeStruct((n, RET_DIM), F32),
        jax.ShapeDtypeStruct((n, RET_DIM), F32), jax.ShapeDtypeStruct((n, 2 * CONV_DIM), F32),
    )
    args = (x, sc, cos, sin, gmix, win, convw, retgn)
    return pl.pallas_call(
        _sample_proj_kernel,
        grid=(1,),
        in_specs=[_const_spec(a.shape) for a in args],
        out_specs=tuple(_const_spec(o.shape) for o in outs),
        out_shape=outs,
        compiler_params=_params("arbitrary"),
        name="sample_proj",
    )(*args)


def _sample_state_kernel(s_ref, qt_ref, kt_ref, v_ref, dec_ref, snew_ref, o_ref):
    for r in range(s_ref.shape[0]):
        for hh in range(RET_HEADS):
            l0 = hh * RET_DV
            s = s_ref[r, hh]
            qc = qt_ref[0, hh, :, r:r + 1]
            kc = kt_ref[0, hh, :, r:r + 1]
            vr = v_ref[r:r + 1, l0:l0 + RET_DV]
            qdec = dec_ref[hh:hh + 1, :]
            cdec = dec_ref[RET_HEADS + hh:RET_HEADS + hh + 1, :]
            qk = jnp.sum(qc * kc, axis=0, keepdims=True)
            o_ref[r:r + 1, l0:l0 + RET_DV] = qk * vr + jnp.sum((qc * qdec[:, 0:1]) * s, axis=0, keepdims=True)
            snew_ref[r, hh] = s * cdec + kc * vr


def _sample_state(s, qt, kt, v, dec):
    n = s.shape[0]
    nb = SAMPLE_BLOCK
    return pl.pallas_call(
        _sample_state_kernel,
        grid=(n // nb,),
        in_specs=[
            pl.BlockSpec((nb, RET_HEADS, RET_DK, RET_DV), lambda i: (i, 0, 0, 0)),
            pl.BlockSpec((1, RET_HEADS, RET_DK, nb), lambda i: (i, 0, 0, 0)),
            pl.BlockSpec((1, RET_HEADS, RET_DK, nb), lambda i: (i, 0, 0, 0)),
            pl.BlockSpec((nb, RET_DIM), lambda i: (i, 0)),
            _const_spec(dec.shape),
        ],
        out_specs=(
            pl.BlockSpec((nb, RET_HEADS, RET_DK, RET_DV), lambda i: (i, 0, 0, 0)),
            pl.BlockSpec((nb, RET_DIM), lambda i: (i, 0)),
        ),
        out_shape=(jax.ShapeDtypeStruct(s.shape, F32), jax.ShapeDtypeStruct((n, RET_DIM), F32)),
        compiler_params=_params("arbitrary"),
        name="sample_state",
    )(s, qt, kt, v, dec)


def _sample_out_kernel(x_ref, yc_ref, o_ref, gate_ref, wo_ref, gffn_ref, wr_ref, br_ref, tri_ref, ltri_ref,
                       h_ref, ri_ref, rw_ref, n_ref):
    n = x_ref.shape[0]
    parts = [yc_ref[...].astype(BF16)]
    for hh in range(RET_HEADS):
        l0 = hh * RET_DV
        parts.append(_group_norm_gate(o_ref[:, l0:l0 + RET_DV], gate_ref[:, l0:l0 + RET_DV]).astype(BF16))
    h_ref[0:n, :] = x_ref[...] + _dot(jnp.concatenate(parts, axis=-1), wo_ref[...])
    if n < TOK_TILE:
        h_ref[n:TOK_TILE, :] = jnp.zeros((TOK_TILE - n, D_MODEL), F32)
    ri, rw, cnt = _route(h_ref[...], gffn_ref[...], wr_ref[...], br_ref[...], tri_ref[...], ltri_ref[...])
    ri_ref[...] = ri
    rw_ref[...] = rw
    n_ref[0] = cnt


def _sample_out(x, yc, o, gate, wo, gffn, wr, br, tri, ltri):
    outs = (
        jax.ShapeDtypeStruct((TOK_TILE, D_MODEL), F32), jax.ShapeDtypeStruct((SUBLANES, TOK_TILE), I32),
        jax.ShapeDtypeStruct((SUBLANES, TOK_TILE), F32), jax.ShapeDtypeStruct((1, N_EXPERTS, LANES), F32),
    )
    args = (x, yc, o, gate, wo, gffn, wr, br, tri, ltri)
    return pl.pallas_call(
        _sample_out_kernel,
        grid=(1,),
        in_specs=[_const_spec(a.shape) for a in args],
        out_specs=tuple(_const_spec(o_.shape) for o_ in outs),
        out_shape=outs,
        compiler_params=_params("arbitrary"),
        name="sample_out",
    )(*args)


def _mixer_sample(x, state_conv, state_ret, gmix, win, convw, retgn, wo, gffn, wr, br, tri, ltri):
    n = x.shape[0]
    lg = _log_gamma()
    cos, sin = _rope_tables(jnp.full((SUBLANES,), PAST_LEN, I32))
    yc, q, k, v, gate, conv_new = _sample_proj(x, state_conv.reshape(n, 2 * CONV_DIM), cos, sin, gmix, win, convw, retgn)
    nb = SAMPLE_BLOCK
    cols = lambda a: a.reshape(n // nb, nb, RET_HEADS, RET_DK).transpose(0, 2, 3, 1)
    step = jnp.exp(lg[:, None] * 1.0)
    dec = jnp.broadcast_to(jnp.concatenate([step, step], axis=0), (2 * RET_HEADS, LANES))
    s_new, o = _sample_state(state_ret, cols(q), cols(k), v, dec)
    h, ri, rw, cnt = _sample_out(x, yc, o, gate, wo, gffn, wr, br, tri, ltri)
    return h, ri, rw, cnt, conv_new.reshape(n, 2, CONV_DIM), s_new


def _run_copies(idx_smem, slot, make_copy):
    for b in range(RUN_BITS):
        def body(j, carry, b=b):
            entry = idx_smem[slot, N_EXPERTS + N_EXPERTS * b + j]
            make_copy(entry & ((1 << SRC_BITS) - 1), entry >> SRC_BITS, 1 << b).start()
            return carry

        lax.fori_loop(0, idx_smem[slot, b], body, 0)


def _wait_tile_rows(hbm_ref, vmem_ref, sem):
    pltpu.make_async_copy(hbm_ref.at[pl.ds(0, PAIR_ROWS)], vmem_ref, sem).wait()


def _pair_rows_iota():
    return lax.broadcasted_iota(I32, (PAIR_ROWS, TOK_TILE), 0)


def _dispatch_kernel(n_p, tab_ref, hp_ref, hs_ref, ri_ref, gffn_ref, xs_ref, idx_smem, y_scr, isem, rsem):
    i, n = pl.program_id(0), pl.num_programs(0)
    slot = i % 2

    def idx_copy(step, sl):
        return pltpu.make_async_copy(tab_ref.at[step], idx_smem.at[sl], isem.at[sl])

    @pl.when(i == 0)
    def _():
        idx_copy(0, 0).start()

    @pl.when(i + 1 < n)
    def _():
        idx_copy(i + 1, 1 - slot).start()

    @pl.when(i >= 2)
    def _():
        _wait_tile_rows(xs_ref, y_scr.at[slot], rsem.at[slot])

    h = jnp.where(i < n_p, hp_ref[...], hs_ref[...])
    xn = _rms(h, gffn_ref[...]).astype(BF16)
    ri = ri_ref[...]
    rows = _pair_rows_iota()
    perm = jnp.where((rows == ri[2:3]) | (rows == ri[3:4]), 1.0, 0.0).astype(BF16)
    y = _dot(perm, xn)
    for c in range(ROW_CHUNKS):
        y_scr[slot, :, c, :] = y[:, c * LANES:(c + 1) * LANES]

    idx_copy(i, slot).wait()
    _run_copies(idx_smem, slot, lambda src, dst, rows_: pltpu.make_async_copy(
        y_scr.at[slot, pl.ds(src, rows_)], xs_ref.at[pl.ds(dst, rows_)], rsem.at[slot]))

    @pl.when(i == n - 1)
    def _():
        _wait_tile_rows(xs_ref, y_scr.at[slot], rsem.at[slot])

    @pl.when((i == n - 1) & (n >= 2))
    def _():
        _wait_tile_rows(xs_ref, y_scr.at[1 - slot], rsem.at[1 - slot])


def _dispatch(tab, h_p, h_s, ri, gffn):
    n_p = h_p.shape[0] // TOK_TILE
    n = tab.shape[0]
    return pl.pallas_call(
        functools.partial(_dispatch_kernel, n_p),
        grid=(n,),
        in_specs=[
            pl.BlockSpec(memory_space=pl.ANY),
            pl.BlockSpec((TOK_TILE, D_MODEL), lambda i: (jnp.minimum(i, n_p - 1), 0)),
            pl.BlockSpec((TOK_TILE, D_MODEL), lambda i: (jnp.maximum(i - n_p, 0), 0)),
            pl.BlockSpec((SUBLANES, TOK_TILE), lambda i: (0, i)),
            _const_spec(gffn.shape),
        ],
        out_specs=pl.BlockSpec(memory_space=pl.ANY),
        out_shape=jax.ShapeDtypeStruct((n * PAIR_ROWS, ROW_CHUNKS, LANES), F32),
        scratch_shapes=[
            pltpu.SMEM((2, TAB_COLS), I32),
            pltpu.VMEM((2, PAIR_ROWS, ROW_CHUNKS, LANES), F32),
            pltpu.SemaphoreType.DMA((2,)),
            pltpu.SemaphoreType.DMA((2,)),
        ],
        compiler_params=_params("arbitrary"),
        name="moe_dispatch",
    )(tab, h_p, h_s, ri, gffn)


def _ffn_kernel(tile_ref, exp_ref, lo_ref, hi_ref, x_ref, wg_ref, wu_ref, wd_ref, o_ref, wgu_scr, wd_scr):
    w = pl.program_id(0)
    prev = jnp.maximum(w - 1, 0)
    lo, hi = lo_ref[w], hi_ref[w]

    @pl.when((w == 0) | (exp_ref[w] != exp_ref[prev]))
    def _():
        wgu_scr[:, 0:D_FF] = wg_ref[0].astype(BF16)
        wgu_scr[:, D_FF:2 * D_FF] = wu_ref[0].astype(BF16)
        wd_scr[...] = wd_ref[0].astype(BF16)

    def expert_rows():
        x = jnp.concatenate([x_ref[:, c, :] for c in range(ROW_CHUNKS)], axis=-1).astype(BF16)
        gu = _dot(x, wgu_scr[...])
        g, u = gu[:, 0:D_FF], gu[:, D_FF:2 * D_FF]
        return _dot((g * jax.nn.sigmoid(g) * u).astype(BF16), wd_scr[...])

    first = (w == 0) | (tile_ref[w] != tile_ref[prev])

    @pl.when((hi > lo) & first)
    def _():
        y = expert_rows()
        for c in range(ROW_CHUNKS):
            o_ref[:, c, :] = y[:, c * LANES:(c + 1) * LANES]

    @pl.when((hi > lo) & jnp.logical_not(first))
    def _():
        y = expert_rows()
        row = lax.broadcasted_iota(I32, (FFN_TILE, 1), 0)
        mine = (row >= lo) & (row < hi)
        for c in range(ROW_CHUNKS):
            o_ref[:, c, :] = jnp.where(mine, y[:, c * LANES:(c + 1) * LANES], o_ref[:, c, :])


def _expert_ffn(items, xs, w_gate, w_up, w_down):
    tile, exp, lo, hi = items
    n_items = tile.shape[0]
    row_spec = pl.BlockSpec((FFN_TILE, ROW_CHUNKS, LANES), lambda w, t, e, l, h: (t[w], 0, 0))
    grid_spec = pltpu.PrefetchScalarGridSpec(
        num_scalar_prefetch=4,
        grid=(n_items,),
        in_specs=[
            row_spec,
            pl.BlockSpec((1, D_MODEL, D_FF), lambda w, t, e, l, h: (e[w], 0, 0)),
            pl.BlockSpec((1, D_MODEL, D_FF), lambda w, t, e, l, h: (e[w], 0, 0)),
            pl.BlockSpec((1, D_FF, D_MODEL), lambda w, t, e, l, h: (e[w], 0, 0)),
        ],
        out_specs=row_spec,
        scratch_shapes=[pltpu.VMEM((D_MODEL, 2 * D_FF), BF16), pltpu.VMEM((D_FF, D_MODEL), BF16)],
    )
    return pl.pallas_call(
        _ffn_kernel,
        grid_spec=grid_spec,
        out_shape=jax.ShapeDtypeStruct(xs.shape, F32),
        compiler_params=_params("arbitrary"),
        name="moe_ffn",
    )(tile, exp, lo, hi, xs, w_gate, w_up, w_down)


def _work_items(counts, n_rows):
    n_tiles = n_rows // FFN_TILE
    n_items = n_tiles + N_EXPERTS - 1
    off = jnp.cumsum(counts) - counts
    first_tile = off // FFN_TILE
    last_tile = jnp.maximum(off + counts - 1, off) // FFN_TILE
    n_e = jnp.where(counts > 0, last_tile - first_tile + 1, 0)
    start = jnp.cumsum(n_e) - n_e
    total = jnp.sum(n_e)
    w = jnp.minimum(jnp.arange(n_items, dtype=I32), total - 1)
    e = jnp.max(jnp.where((start[None, :] <= w[:, None]) & (n_e[None, :] > 0), jnp.arange(N_EXPERTS)[None, :], 0), axis=1)
    tile = first_tile[e] + (w - start[e])
    lo = jnp.clip(off[e] - tile * FFN_TILE, 0, FFN_TILE)
    hi = jnp.clip(off[e] + counts[e] - tile * FFN_TILE, 0, FFN_TILE)
    hi = jnp.where(jnp.arange(n_items) < total, hi, lo)
    return tile.astype(I32), e.astype(I32), lo.astype(I32), hi.astype(I32)


def _run_table(n):
    tiles = n.shape[0]
    counts = jnp.sum(n, axis=0)
    goff = jnp.cumsum(counts) - counts
    dst0 = goff[None, :] + jnp.cumsum(n, axis=0) - n
    src0 = jnp.cumsum(n, axis=1) - n
    bits = jnp.arange(RUN_BITS, dtype=I32)[:, None, None]
    flag = (n[None] >> bits) & 1
    done = (n[None] >> (bits + 1)) << (bits + 1)
    entry = ((dst0[None] + done) << SRC_BITS) | (src0[None] + done)
    rank = jnp.cumsum(flag, axis=2) - flag
    slot = jnp.arange(N_EXPERTS, dtype=I32)
    sel = (flag[..., None] == 1) & (rank[..., None] == slot)
    packed = jnp.sum(jnp.where(sel, entry[..., None], 0), axis=2)
    m = jnp.sum(flag, axis=2).T
    tab = jnp.zeros((tiles, TAB_COLS), I32)
    tab = tab.at[:, :RUN_BITS].set(m)
    tab = tab.at[:, N_EXPERTS:N_EXPERTS * (RUN_BITS + 1)].set(packed.transpose(1, 0, 2).reshape(tiles, -1))
    return counts, tab


def _combine_kernel(tab_ref, h_ref, p_ref, ri_ref, rw_ref, ys_ref, gple_ref, wpg_ref, wpp_ref, gfin_ref, o_ref,
                    idx_smem, rows_scr, isem, rsem):
    i, n = pl.program_id(0), pl.num_programs(0)
    slot = i % 2

    def idx_copy(step, sl):
        return pltpu.make_async_copy(tab_ref.at[step], idx_smem.at[sl], isem.at[sl])

    def gather(sl):
        _run_copies(idx_smem, sl, lambda src, dst, rows_: pltpu.make_async_copy(
            ys_ref.at[pl.ds(dst, rows_)], rows_scr.at[sl, pl.ds(src, rows_)], rsem.at[sl]))

    @pl.when(i == 0)
    def _():
        idx_copy(0, 0).start()
        idx_copy(0, 0).wait()
        gather(0)

    @pl.when((i == 0) & (n >= 2))
    def _():
        idx_copy(1, 1).start()

    @pl.when(i + 1 < n)
    def _():
        idx_copy(i + 1, 1 - slot).wait()
        gather(1 - slot)

    @pl.when(i + 2 < n)
    def _():
        idx_copy(i + 2, slot).start()

    _wait_tile_rows(ys_ref, rows_scr.at[slot], rsem.at[slot])

    ri, rw = ri_ref[...], rw_ref[...]
    rows = _pair_rows_iota()
    wperm = jnp.where(rows == ri[2:3], rw[0:1], 0.0) + jnp.where(rows == ri[3:4], rw[1:2], 0.0)
    w_hi = wperm.astype(BF16)
    w_lo = (wperm - w_hi.astype(F32)).astype(BF16)
    s = jnp.concatenate([rows_scr[slot, :, c, :] for c in range(ROW_CHUNKS)], axis=-1).astype(BF16)
    h = h_ref[...] + _dot_tn(w_hi, s) + _dot_tn(w_lo, s)
    gate = jax.nn.sigmoid(_dot(_rms(h, gple_ref[...]).astype(BF16), wpg_ref[...]))
    h = h + _dot(p_ref[...].astype(BF16), wpp_ref[...]) * gate
    o_ref[...] = _rms(h, gfin_ref[...])


def _combine(tab, h, p, ri, rw, ys, gple, wpg, wpp, gfin):
    n = tab.shape[0]
    n_tok = n * TOK_TILE
    return pl.pallas_call(
        _combine_kernel,
        grid=(n,),
        in_specs=[
            pl.BlockSpec(memory_space=pl.ANY),
            pl.BlockSpec((TOK_TILE, D_MODEL), lambda i: (i, 0)),
            pl.BlockSpec((TOK_TILE, PLE_DIM), lambda i: (i, 0)),
            pl.BlockSpec((SUBLANES, TOK_TILE), lambda i: (0, i)),
            pl.BlockSpec((SUBLANES, TOK_TILE), lambda i: (0, i)),
            pl.BlockSpec(memory_space=pl.ANY),
            _const_spec(gple.shape), _const_spec(wpg.shape), _const_spec(wpp.shape), _const_spec(gfin.shape),
        ],
        out_specs=pl.BlockSpec((TOK_TILE, D_MODEL), lambda i: (i, 0)),
        out_shape=jax.ShapeDtypeStruct((n_tok, D_MODEL), F32),
        scratch_shapes=[
            pltpu.SMEM((2, TAB_COLS), I32),
            pltpu.VMEM((2, PAIR_ROWS, ROW_CHUNKS, LANES), F32),
            pltpu.SemaphoreType.DMA((2,)),
            pltpu.SemaphoreType.DMA((2,)),
        ],
        compiler_params=_params("arbitrary"),
        name="moe_combine",
    )(tab, h, p, ri, rw, ys, gple, wpg, wpp, gfin)


def _log_gamma():
    return jnp.log(1.0 - 2.0 ** (-5.0 - jnp.arange(RET_HEADS, dtype=F32)))


def _rope_tables(pos):
    inv = 1.0 / (ROPE_BASE ** (jnp.arange(0, RET_DK, 2, dtype=F32) / RET_DK))
    ang = pos.astype(F32)[:, None] * inv[None, :]
    cos, sin = jnp.cos(ang), jnp.sin(ang)
    return jnp.concatenate([cos, cos], axis=-1), jnp.concatenate([-sin, sin], axis=-1)


def _decay_tables(c):
    lg = _log_gamma()
    idx = jnp.arange(c, dtype=F32)
    diff = idx[:, None] - idx[None, :]
    dmask = jnp.where((diff >= 0.0)[None], jnp.exp(lg[:, None, None] * jnp.maximum(diff, 0.0)[None]), 0.0)
    q_dec = jnp.exp(lg[None, :] * (idx[:, None] + 1.0))
    k_dec = jnp.exp(lg[None, :] * (c - 1.0 - idx[:, None]))
    c_dec = jnp.exp(lg * c)
    lanes = lambda a: jnp.repeat(a, RET_DK, axis=1)
    cdec = jnp.zeros((SUBLANES, LANES), F32).at[:RET_HEADS].set(jnp.broadcast_to(c_dec[:, None], (RET_HEADS, LANES)))
    return lanes(q_dec), lanes(k_dec), dmask, cdec


def _router_params(w_rg, b_rg, w_re, b_re):
    wr = jnp.zeros((ROUTER_ROWS, D_MODEL), F32).at[:N_GROUPS].set(w_rg.T).at[SUBLANES:].set(w_re.T)
    br = jnp.full((ROUTER_ROWS,), NEG_BIG, F32).at[:N_GROUPS].set(b_rg).at[SUBLANES:].set(b_re.reshape(-1))
    return wr.astype(BF16), jnp.broadcast_to(br[:, None], (ROUTER_ROWS, LANES))


def _strict_upper(t):
    i = jnp.arange(t)
    return (i[:, None] < i[None, :]).astype(BF16)


def _pad_rows(a, rows):
    return jnp.zeros((rows,) + a.shape[1:], a.dtype).at[:a.shape[0]].set(a)


def kernel(x_prompt, x_sample, state_conv, state_ret, p_prompt, p_sample, g_mix, w_in, conv_w, ret_gn, w_o, g_ffn,
           w_router_group, b_router_group, w_router_expert, b_router_expert, w_gate, w_up, w_down, g_ple,
           w_ple_proj, w_ple_gate, g_final):
    bp, tp, _ = x_prompt.shape
    bs = x_sample.shape[0]
    assert x_sample.shape[1] == 1 and g_mix.shape[0] == 1, "one layer, one new token per sample request"
    assert bs <= TOK_TILE and tp % TOK_TILE == 0
    n_p = bp * tp

    tabs = _rope_tables(jnp.arange(tp, dtype=I32)) + _decay_tables(RET_CHUNK)
    wr, br = _router_params(w_router_group[0], b_router_group[0], w_router_expert[0], b_router_expert[0])
    win, wo, convw = w_in[0].astype(BF16), w_o[0].astype(BF16), _pad_rows(conv_w[0], SUBLANES)
    tri, ltri = _strict_upper(TOK_TILE), _strict_upper(N_EXPERTS).T

    h_p, ri_p, rw_p, n_p_tab, conv_p, ret_p = _mixer_prompt(
        x_prompt, tabs, (g_mix, win, convw, ret_gn, wo, g_ffn, wr, br, tri, ltri))
    h_s, ri_s, rw_s, n_s_tab, conv_s, ret_s = _mixer_sample(
        x_sample[:, 0], state_conv[0], state_ret[0], g_mix, win, convw, ret_gn, wo, g_ffn, wr, br, tri, ltri)
    h_p = h_p.reshape(n_p, D_MODEL)

    n_tab = jnp.concatenate([n_p_tab, n_s_tab], axis=0)[:, :, 0].astype(I32)
    counts, tab = _run_table(n_tab)
    ri = jnp.concatenate([ri_p, ri_s], axis=1)
    n_tiles = n_tab.shape[0]

    xs = _dispatch(tab, h_p, h_s, ri, g_ffn)
    ys = _expert_ffn(_work_items(counts, n_tiles * PAIR_ROWS), xs, w_gate[0], w_up[0], w_down[0])

    wpg, wpp = w_ple_gate[0].astype(BF16), w_ple_proj[0].astype(BF16)
    gfin = g_final[None, :]
    p_s = _pad_rows(p_sample[0].reshape(bs, PLE_DIM), TOK_TILE)
    y_p = _combine(tab[:n_tiles - 1], h_p, p_prompt[0].reshape(n_p, PLE_DIM), ri_p, rw_p, ys, g_ple, wpg, wpp, gfin)
    y_s = _combine(tab[n_tiles - 1:], h_s, p_s, ri_s, rw_s, ys, g_ple, wpg, wpp, gfin)
    return (y_p.reshape(bp, tp, D_MODEL), y_s[:bs].reshape(bs, 1, D_MODEL), conv_p[None], ret_p[None],
            conv_s[None], ret_s[None])
```

```python
import functools

import jax
import jax.numpy as jnp
from jax import lax
from jax.experimental import pallas as pl
from jax.experimental.pallas import tpu as pltpu

F32, BF16, I32 = jnp.float32, jnp.bfloat16, jnp.int32

D_MODEL = 1024
CONV_DIM = 512
CONV_WIDTH = 3
RET_DIM = 512
RET_HEADS = 4
RET_DK = 128
RET_DV = 128
RET_CHUNK = 128
ROPE_BASE = 10000.0
IN_PROJ_DIM = 3 * CONV_DIM + 4 * RET_DIM
N_GROUPS = 4
EXPERTS_PER_GROUP = 8
N_EXPERTS = 32
D_FF = 512
PLE_DIM = 256
EPS = 1e-6
PAST_LEN = 16384

LANES = 128
SUBLANES = 8
ROW_CHUNKS = D_MODEL // LANES
ROUTER_ROWS = SUBLANES + N_EXPERTS
VMEM_LIMIT = 56 * 1024 * 1024
NEG_BIG = -1e30

TOK_TILE = 256
PAIR_ROWS = 2 * TOK_TILE
FFN_TILE = 256
SAMPLE_BLOCK = 8
RUN_BITS = 9
SRC_BITS = 10
TAB_COLS = 384


def _rms(x, g):
    return x * lax.rsqrt(jnp.mean(x * x, axis=-1, keepdims=True) + EPS) * g


def _dot(a, b):
    return jnp.dot(a, b, preferred_element_type=F32)


def _dot_nt(a, b):
    return lax.dot_general(a, b, (((1,), (1,)), ((), ())), preferred_element_type=F32)


def _dot_tn(a, b):
    return lax.dot_general(a, b, (((0,), (0,)), ((), ())), preferred_element_type=F32)


def _rope(x, cos, sin_signed):
    return x * cos + pltpu.roll(x, RET_DK // 2, 1) * sin_signed


def _const_spec(shape):
    nd = len(shape)
    return pl.BlockSpec(shape, lambda *_: (0,) * nd)


def _params(*sem):
    return pltpu.CompilerParams(dimension_semantics=sem, vmem_limit_bytes=VMEM_LIMIT)


def _route(h, gffn, wr, br, tri, ltri):
    t = h.shape[0]
    xn = _rms(h, gffn).astype(BF16)
    lt = _dot_nt(wr, xn) + br[:, 0:1]
    row8 = lax.broadcasted_iota(I32, (SUBLANES, t), 0).astype(F32)
    gl = lt[0:SUBLANES]
    m = jnp.max(gl, axis=0, keepdims=True)
    g_top = 1.0 / jnp.sum(jnp.exp(gl - m), axis=0, keepdims=True)
    gidx = jnp.min(jnp.where(gl == m, row8, float(SUBLANES)), axis=0, keepdims=True)
    e_sel = jnp.where(gidx == 0.0, lt[8:16],
                      jnp.where(gidx == 1.0, lt[16:24], jnp.where(gidx == 2.0, lt[24:32], lt[32:40])))
    m1 = jnp.max(e_sel, axis=0, keepdims=True)
    i1 = jnp.min(jnp.where(e_sel == m1, row8, float(SUBLANES)), axis=0, keepdims=True)
    rest = jnp.where(row8 == i1, -jnp.inf, e_sel)
    m2 = jnp.max(rest, axis=0, keepdims=True)
    i2 = jnp.min(jnp.where(rest == m2, row8, float(SUBLANES)), axis=0, keepdims=True)
    d = jnp.exp(m2 - m1)
    w1 = g_top / (1.0 + d)
    w2 = g_top * d / (1.0 + d)
    e1 = gidx * float(EXPERTS_PER_GROUP) + i1
    e2 = gidx * float(EXPERTS_PER_GROUP) + i2
    row32 = lax.broadcasted_iota(I32, (N_EXPERTS, t), 0).astype(F32)
    a1 = jnp.where(row32 == e1, 1.0, 0.0)
    a2 = jnp.where(row32 == e2, 1.0, 0.0)
    a = a1 + a2
    n = jnp.broadcast_to(jnp.sum(a, axis=1, keepdims=True), (N_EXPERTS, LANES))
    start = _dot(ltri, n.astype(BF16))
    base = _dot(a.astype(BF16), tri) + start[:, 0:1]
    r1 = jnp.sum(a1 * base, axis=0, keepdims=True)
    r2 = jnp.sum(a2 * base, axis=0, keepdims=True)
    ri = jnp.where(row8 == 0.0, e1, jnp.where(row8 == 1.0, e2, jnp.where(row8 == 2.0, r1,
                                                                         jnp.where(row8 == 3.0, r2, 0.0))))
    rw = jnp.where(row8 == 0.0, w1, jnp.where(row8 == 1.0, w2, 0.0))
    return ri.astype(I32), rw, n


def _group_norm_gate(o, gate):
    mu = jnp.mean(o, axis=-1, keepdims=True)
    oc = o - mu
    return oc * lax.rsqrt(jnp.mean(oc * oc, axis=-1, keepdims=True) + EPS) * gate


def _mixer_prompt_kernel(x_ref, cos_ref, sin_ref, qdec_ref, kdec_ref, dmask_ref, cdec_ref, gmix_ref, win_ref,
                         convw_ref, retgn_ref, wo_ref, gffn_ref, wr_ref, br_ref, tri_ref, ltri_ref,
                         h_ref, ri_ref, rw_ref, n_ref, conv_ref, ret_ref,
                         s_scr, z_scr, mix_scr):
    j = pl.program_id(1)
    n_j = pl.num_programs(1)
    tt = x_ref.shape[1]

    @pl.when(j == 0)
    def _():
        s_scr[...] = jnp.zeros_like(s_scr)
        z_scr[0:SUBLANES, :] = jnp.zeros((SUBLANES, CONV_DIM), F32)

    x = x_ref[0]
    u = _rms(x, gmix_ref[...]).astype(BF16)
    proj = _dot(u, win_ref[...])
    gb = proj[:, 0:CONV_DIM]
    z = proj[:, CONV_DIM:2 * CONV_DIM] * proj[:, 2 * CONV_DIM:3 * CONV_DIM]
    q0 = 3 * CONV_DIM

    z_scr[SUBLANES:SUBLANES + tt, :] = z
    cw = convw_ref[...]
    yc = (cw[0:1] * z_scr[SUBLANES - 2:SUBLANES - 2 + tt, :]
          + cw[1:2] * z_scr[SUBLANES - 1:SUBLANES - 1 + tt, :] + cw[2:3] * z)
    mix_scr[:, 0:CONV_DIM] = (gb * yc).astype(BF16)
    z_scr[0:SUBLANES, :] = z_scr[tt:tt + SUBLANES, :]

    retgn = retgn_ref[...]
    for c in range(tt // RET_CHUNK):
        r0 = c * RET_CHUNK
        cos = cos_ref[r0:r0 + RET_CHUNK, :]
        sin = sin_ref[r0:r0 + RET_CHUNK, :]
        for hh in range(RET_HEADS):
            l0 = hh * RET_DK
            qr = _rope(proj[r0:r0 + RET_CHUNK, q0 + l0:q0 + l0 + RET_DK], cos, sin)
            kr = _rope(proj[r0:r0 + RET_CHUNK, q0 + RET_DIM + l0:q0 + RET_DIM + l0 + RET_DK], cos, sin) * (RET_DK ** -0.5)
            v = proj[r0:r0 + RET_CHUNK, q0 + 2 * RET_DIM + l0:q0 + 2 * RET_DIM + l0 + RET_DV].astype(BF16)
            g = proj[r0:r0 + RET_CHUNK, q0 + 3 * RET_DIM + l0:q0 + 3 * RET_DIM + l0 + RET_DV]
            s_old = s_scr[hh]
            scores = _dot_nt(qr.astype(BF16), kr.astype(BF16)) * dmask_ref[hh]
            o = _dot(scores.astype(BF16), v) + _dot((qr * qdec_ref[:, l0:l0 + RET_DK]).astype(BF16), s_old.astype(BF16))
            kd = (kr * kdec_ref[:, l0:l0 + RET_DK]).T.astype(BF16)
            s_scr[hh] = s_old * cdec_ref[hh:hh + 1, :] + _dot(kd, v)
            gate = retgn[:, l0:l0 + RET_DV] * (g * jax.nn.sigmoid(g))
            mix_scr[r0:r0 + RET_CHUNK, CONV_DIM + l0:CONV_DIM + l0 + RET_DV] = _group_norm_gate(o, gate).astype(BF16)

    h = x + _dot(mix_scr[...], wo_ref[...])
    h_ref[0] = h

    ri, rw, n = _route(h, gffn_ref[...], wr_ref[...], br_ref[...], tri_ref[...], ltri_ref[...])
    ri_ref[...] = ri
    rw_ref[...] = rw
    n_ref[0] = n

    @pl.when(j == n_j - 1)
    def _():
        conv_ref[0] = z[tt - 2:tt, :]
        ret_ref[0] = s_scr[...]


def _mixer_prompt(x, tabs, wts):
    bsz, t, _ = x.shape
    tt = TOK_TILE
    n_j = t // tt
    n_tok = bsz * t
    args = (x,) + tuple(tabs) + tuple(wts)
    in_specs = [
        pl.BlockSpec((1, tt, D_MODEL), lambda b, j: (b, j, 0)),
        pl.BlockSpec((tt, LANES), lambda b, j: (j, 0)),
        pl.BlockSpec((tt, LANES), lambda b, j: (j, 0)),
    ] + [_const_spec(a.shape) for a in args[3:]]
    out_shape = (
        jax.ShapeDtypeStruct((bsz, t, D_MODEL), F32),
        jax.ShapeDtypeStruct((SUBLANES, n_tok), I32),
        jax.ShapeDtypeStruct((SUBLANES, n_tok), F32),
        jax.ShapeDtypeStruct((bsz * n_j, N_EXPERTS, LANES), F32),
        jax.ShapeDtypeStruct((bsz, CONV_WIDTH - 1, CONV_DIM), F32),
        jax.ShapeDtypeStruct((bsz, RET_HEADS, RET_DK, RET_DV), F32),
    )
    out_specs = (
        pl.BlockSpec((1, tt, D_MODEL), lambda b, j: (b, j, 0)),
        pl.BlockSpec((SUBLANES, tt), lambda b, j: (0, b * n_j + j)),
        pl.BlockSpec((SUBLANES, tt), lambda b, j: (0, b * n_j + j)),
        pl.BlockSpec((1, N_EXPERTS, LANES), lambda b, j: (b * n_j + j, 0, 0)),
        pl.BlockSpec((1, CONV_WIDTH - 1, CONV_DIM), lambda b, j: (b, 0, 0)),
        pl.BlockSpec((1, RET_HEADS, RET_DK, RET_DV), lambda b, j: (b, 0, 0, 0)),
    )
    return pl.pallas_call(
        _mixer_prompt_kernel,
        grid=(bsz, n_j),
        in_specs=in_specs,
        out_specs=out_specs,
        out_shape=out_shape,
        scratch_shapes=[
            pltpu.VMEM((RET_HEADS, RET_DK, RET_DV), F32),
            pltpu.VMEM((tt + SUBLANES, CONV_DIM), F32),
            pltpu.VMEM((tt, D_MODEL), BF16),
        ],
        compiler_params=_params("arbitrary", "arbitrary"),
        name="mixer_prompt",
    )(*args)


def _sample_proj_kernel(x_ref, sc_ref, cos_ref, sin_ref, gmix_ref, win_ref, convw_ref, retgn_ref,
                        yc_ref, q_ref, k_ref, v_ref, gate_ref, conv_ref):
    x = x_ref[...]
    u = _rms(x, gmix_ref[...]).astype(BF16)
    proj = _dot(u, win_ref[...])
    gb = proj[:, 0:CONV_DIM]
    z = proj[:, CONV_DIM:2 * CONV_DIM] * proj[:, 2 * CONV_DIM:3 * CONV_DIM]
    cw = convw_ref[...]
    buf0, buf1 = sc_ref[:, 0:CONV_DIM], sc_ref[:, CONV_DIM:2 * CONV_DIM]
    yc_ref[...] = gb * (cw[0:1] * buf0 + cw[1:2] * buf1 + cw[2:3] * z)
    conv_ref[:, 0:CONV_DIM] = buf1
    conv_ref[:, CONV_DIM:2 * CONV_DIM] = z
    q0 = 3 * CONV_DIM
    cos, sin = cos_ref[0:1, :], sin_ref[0:1, :]
    retgn = retgn_ref[...]
    for hh in range(RET_HEADS):
        l0 = hh * RET_DK
        q_ref[:, l0:l0 + RET_DK] = _rope(proj[:, q0 + l0:q0 + l0 + RET_DK], cos, sin)
        k_ref[:, l0:l0 + RET_DK] = _rope(proj[:, q0 + RET_DIM + l0:q0 + RET_DIM + l0 + RET_DK], cos, sin) * (RET_DK ** -0.5)
    v_ref[...] = proj[:, q0 + 2 * RET_DIM:q0 + 3 * RET_DIM]
    g = proj[:, q0 + 3 * RET_DIM:q0 + 4 * RET_DIM]
    gate_ref[...] = retgn * (g * jax.nn.sigmoid(g))


def _sample_proj(x, sc, cos, sin, gmix, win, convw, retgn):
    n = x.shape[0]
    outs = (
        jax.ShapeDtypeStruct((n, CONV_DIM), F32), jax.ShapeDtypeStruct((n, RET_DIM), F32),
        jax.ShapeDtypeStruct((n, RET_DIM), F32), jax.ShapeDtypeStruct((n, RET_DIM), F32),
        jax.ShapeDtypeStruct((n, RET_DIM), F32), jax.ShapeDtypeStruct((n, 2 * CONV_DIM), F32),
    )
    args = (x, sc, cos, sin, gmix, win, convw, retgn)
    return pl.pallas_call(
        _sample_proj_kernel,
        grid=(1,),
        in_specs=[_const_spec(a.shape) for a in args],
        out_specs=tuple(_const_spec(o.shape) for o in outs),
        out_shape=outs,
        compiler_params=_params("arbitrary"),
        name="sample_proj",
    )(*args)


def _sample_state_kernel(s_ref, qt_ref, kt_ref, v_ref, dec_ref, snew_ref, o_ref):
    for r in range(s_ref.shape[0]):
        for hh in range(RET_HEADS):
            l0 = hh * RET_DV
            s = s_ref[r, hh]
            qc = qt_ref[0, hh, :, r:r + 1]
            kc = kt_ref[0, hh, :, r:r + 1]
            vr = v_ref[r:r + 1, l0:l0 + RET_DV]
            qdec = dec_ref[hh:hh + 1, :]
            cdec = dec_ref[RET_HEADS + hh:RET_HEADS + hh + 1, :]
            qk = jnp.sum(qc * kc, axis=0, keepdims=True)
            o_ref[r:r + 1, l0:l0 + RET_DV] = qk * vr + jnp.sum((qc * qdec[:, 0:1]) * s, axis=0, keepdims=True)
            snew_ref[r, hh] = s * cdec + kc * vr


def _sample_state(s, qt, kt, v, dec):
    n = s.shape[0]
    nb = SAMPLE_BLOCK
    return pl.pallas_call(
        _sample_state_kernel,
        grid=(n // nb,),
        in_specs=[
            pl.BlockSpec((nb, RET_HEADS, RET_DK, RET_DV), lambda i: (i, 0, 0, 0)),
            pl.BlockSpec((1, RET_HEADS, RET_DK, nb), lambda i: (i, 0, 0, 0)),
            pl.BlockSpec((1, RET_HEADS, RET_DK, nb), lambda i: (i, 0, 0, 0)),
            pl.BlockSpec((nb, RET_DIM), lambda i: (i, 0)),
            _const_spec(dec.shape),
        ],
        out_specs=(
            pl.BlockSpec((nb, RET_HEADS, RET_DK, RET_DV), lambda i: (i, 0, 0, 0)),
            pl.BlockSpec((nb, RET_DIM), lambda i: (i, 0)),
        ),
        out_shape=(jax.ShapeDtypeStruct(s.shape, F32), jax.ShapeDtypeStruct((n, RET_DIM), F32)),
        compiler_params=_params("arbitrary"),
        name="sample_state",
    )(s, qt, kt, v, dec)


def _sample_out_kernel(x_ref, yc_ref, o_ref, gate_ref, wo_ref, gffn_ref, wr_ref, br_ref, tri_ref, ltri_ref,
                       h_ref, ri_ref, rw_ref, n_ref):
    n = x_ref.shape[0]
    parts = [yc_ref[...].astype(BF16)]
    for hh in range(RET_HEADS):
        l0 = hh * RET_DV
        parts.append(_group_norm_gate(o_ref[:, l0:l0 + RET_DV], gate_ref[:, l0:l0 + RET_DV]).astype(BF16))
    h_ref[0:n, :] = x_ref[...] + _dot(jnp.concatenate(parts, axis=-1), wo_ref[...])
    if n < TOK_TILE:
        h_ref[n:TOK_TILE, :] = jnp.zeros((TOK_TILE - n, D_MODEL), F32)
    ri, rw, cnt = _route(h_ref[...], gffn_ref[...], wr_ref[...], br_ref[...], tri_ref[...], ltri_ref[...])
    ri_ref[...] = ri
    rw_ref[...] = rw
    n_ref[0] = cnt


def _sample_out(x, yc, o, gate, wo, gffn, wr, br, tri, ltri):
    outs = (
        jax.ShapeDtypeStruct((TOK_TILE, D_MODEL), F32), jax.ShapeDtypeStruct((SUBLANES, TOK_TILE), I32),
        jax.ShapeDtypeStruct((SUBLANES, TOK_TILE), F32), jax.ShapeDtypeStruct((1, N_EXPERTS, LANES), F32),
    )
    args = (x, yc, o, gate, wo, gffn, wr, br, tri, ltri)
    return pl.pallas_call(
        _sample_out_kernel,
        grid=(1,),
        in_specs=[_const_spec(a.shape) for a in args],
        out_specs=tuple(_const_spec(o_.shape) for o_ in outs),
        out_shape=outs,
        compiler_params=_params("arbitrary"),
        name="sample_out",
    )(*args)


def _mixer_sample(x, state_conv, state_ret, gmix, win, convw, retgn, wo, gffn, wr, br, tri, ltri):
    n = x.shape[0]
    lg = _log_gamma()
    cos, sin = _rope_tables(jnp.full((SUBLANES,), PAST_LEN, I32))
    yc, q, k, v, gate, conv_new = _sample_proj(x, state_conv.reshape(n, 2 * CONV_DIM), cos, sin, gmix, win, convw, retgn)
    nb = SAMPLE_BLOCK
    cols = lambda a: a.reshape(n // nb, nb, RET_HEADS, RET_DK).transpose(0, 2, 3, 1)
    step = jnp.exp(lg[:, None] * 1.0)
    dec = jnp.broadcast_to(jnp.concatenate([step, step], axis=0), (2 * RET_HEADS, LANES))
    s_new, o = _sample_state(state_ret, cols(q), cols(k), v, dec)
    h, ri, rw, cnt = _sample_out(x, yc, o, gate, wo, gffn, wr, br, tri, ltri)
    return h, ri, rw, cnt, conv_new.reshape(n, 2, CONV_DIM), s_new


def _load_rows(ref, n_rows):
    return jnp.concatenate([ref[pl.ds(c, n_rows, stride=ROW_CHUNKS), :] for c in range(ROW_CHUNKS)], axis=-1)


def _store_rows(ref, val):
    for c in range(ROW_CHUNKS):
        ref[pl.ds(c, val.shape[0], stride=ROW_CHUNKS), :] = val[:, c * LANES:(c + 1) * LANES]


def _row_slice(ref, row, rows):
    return ref.at[pl.ds(pl.multiple_of(row * ROW_CHUNKS, ROW_CHUNKS), rows * ROW_CHUNKS)]


def _run_copies(idx_smem, slot, make_copy):
    for b in range(RUN_BITS):
        def body(j, carry, b=b):
            entry = idx_smem[slot, N_EXPERTS + N_EXPERTS * b + j]
            make_copy(entry & ((1 << SRC_BITS) - 1), entry >> SRC_BITS, 1 << b).start()
            return carry

        lax.fori_loop(0, idx_smem[slot, b], body, 0)


def _wait_tile_rows(hbm_ref, vmem_ref, sem):
    pltpu.make_async_copy(_row_slice(hbm_ref, 0, PAIR_ROWS), vmem_ref, sem).wait()


def _pair_rows_iota():
    return lax.broadcasted_iota(I32, (PAIR_ROWS, TOK_TILE), 0)


def _dispatch_kernel(n_p, tab_ref, hp_ref, hs_ref, ri_ref, gffn_ref, xs_ref, idx_smem, y_scr, isem, rsem):
    i, n = pl.program_id(0), pl.num_programs(0)
    slot = i % 2

    def idx_copy(step, sl):
        return pltpu.make_async_copy(tab_ref.at[step], idx_smem.at[sl], isem.at[sl])

    @pl.when(i == 0)
    def _():
        idx_copy(0, 0).start()

    @pl.when(i + 1 < n)
    def _():
        idx_copy(i + 1, 1 - slot).start()

    @pl.when(i >= 2)
    def _():
        _wait_tile_rows(xs_ref, y_scr.at[slot], rsem.at[slot])

    h = jnp.where(i < n_p, hp_ref[...], hs_ref[...])
    xn = _rms(h, gffn_ref[...]).astype(BF16)
    ri = ri_ref[...]
    rows = _pair_rows_iota()
    perm = jnp.where((rows == ri[2:3]) | (rows == ri[3:4]), 1.0, 0.0).astype(BF16)
    y = _dot(perm, xn)
    _store_rows(y_scr.at[slot], y)

    idx_copy(i, slot).wait()
    _run_copies(idx_smem, slot, lambda src, dst, rows_: pltpu.make_async_copy(
        _row_slice(y_scr.at[slot], src, rows_), _row_slice(xs_ref, dst, rows_), rsem.at[slot]))

    @pl.when(i == n - 1)
    def _():
        _wait_tile_rows(xs_ref, y_scr.at[slot], rsem.at[slot])

    @pl.when((i == n - 1) & (n >= 2))
    def _():
        _wait_tile_rows(xs_ref, y_scr.at[1 - slot], rsem.at[1 - slot])


def _dispatch(tab, h_p, h_s, ri, gffn):
    n_p = h_p.shape[0] // TOK_TILE
    n = tab.shape[0]
    return pl.pallas_call(
        functools.partial(_dispatch_kernel, n_p),
        grid=(n,),
        in_specs=[
            pl.BlockSpec(memory_space=pl.ANY),
            pl.BlockSpec((TOK_TILE, D_MODEL), lambda i: (jnp.minimum(i, n_p - 1), 0)),
            pl.BlockSpec((TOK_TILE, D_MODEL), lambda i: (jnp.maximum(i - n_p, 0), 0)),
            pl.BlockSpec((SUBLANES, TOK_TILE), lambda i: (0, i)),
            _const_spec(gffn.shape),
        ],
        out_specs=pl.BlockSpec(memory_space=pl.ANY),
        out_shape=jax.ShapeDtypeStruct((n * PAIR_ROWS * ROW_CHUNKS, LANES), F32),
        scratch_shapes=[
            pltpu.SMEM((2, TAB_COLS), I32),
            pltpu.VMEM((2, PAIR_ROWS * ROW_CHUNKS, LANES), F32),
            pltpu.SemaphoreType.DMA((2,)),
            pltpu.SemaphoreType.DMA((2,)),
        ],
        compiler_params=_params("arbitrary"),
        name="moe_dispatch",
    )(tab, h_p, h_s, ri, gffn)


def _ffn_kernel(tile_ref, exp_ref, lo_ref, hi_ref, x_ref, wg_ref, wu_ref, wd_ref, o_ref, wgu_scr, wd_scr, y_scr):
    w, n_w = pl.program_id(0), pl.num_programs(0)
    prev, nxt = jnp.maximum(w - 1, 0), jnp.minimum(w + 1, n_w - 1)
    lo, hi = lo_ref[w], hi_ref[w]
    first = (w == 0) | (tile_ref[w] != tile_ref[prev])
    last = (w == n_w - 1) | (tile_ref[w] != tile_ref[nxt])

    @pl.when((w == 0) | (exp_ref[w] != exp_ref[prev]))
    def _():
        wgu_scr[:, 0:D_FF] = wg_ref[0].astype(BF16)
        wgu_scr[:, D_FF:2 * D_FF] = wu_ref[0].astype(BF16)
        wd_scr[...] = wd_ref[0].astype(BF16)

    @pl.when(hi > lo)
    def _():
        gu = _dot(_load_rows(x_ref, FFN_TILE).astype(BF16), wgu_scr[...])
        g, u = gu[:, 0:D_FF], gu[:, D_FF:2 * D_FF]
        y = _dot((g * jax.nn.sigmoid(g) * u).astype(BF16), wd_scr[...])

        @pl.when(first)
        def _():
            y_scr[...] = y

        @pl.when(jnp.logical_not(first))
        def _():
            row = lax.broadcasted_iota(I32, (FFN_TILE, 1), 0)
            y_scr[...] = jnp.where((row >= lo) & (row < hi), y, y_scr[...])

    @pl.when(last)
    def _():
        _store_rows(o_ref, y_scr[...])


def _expert_ffn(items, xs, w_gate, w_up, w_down):
    tile, exp, lo, hi = items
    n_items = tile.shape[0]
    row_spec = pl.BlockSpec((FFN_TILE * ROW_CHUNKS, LANES), lambda w, t, e, l, h: (t[w], 0))
    grid_spec = pltpu.PrefetchScalarGridSpec(
        num_scalar_prefetch=4,
        grid=(n_items,),
        in_specs=[
            row_spec,
            pl.BlockSpec((1, D_MODEL, D_FF), lambda w, t, e, l, h: (e[w], 0, 0)),
            pl.BlockSpec((1, D_MODEL, D_FF), lambda w, t, e, l, h: (e[w], 0, 0)),
            pl.BlockSpec((1, D_FF, D_MODEL), lambda w, t, e, l, h: (e[w], 0, 0)),
        ],
        out_specs=row_spec,
        scratch_shapes=[pltpu.VMEM((D_MODEL, 2 * D_FF), BF16), pltpu.VMEM((D_FF, D_MODEL), BF16),
                        pltpu.VMEM((FFN_TILE, D_MODEL), F32)],
    )
    return pl.pallas_call(
        _ffn_kernel,
        grid_spec=grid_spec,
        out_shape=jax.ShapeDtypeStruct(xs.shape, F32),
        compiler_params=_params("arbitrary"),
        name="moe_ffn",
    )(tile, exp, lo, hi, xs, w_gate, w_up, w_down)


def _work_items(counts, n_rows):
    n_tiles = n_rows // FFN_TILE
    n_items = n_tiles + N_EXPERTS - 1
    off = jnp.cumsum(counts) - counts
    first_tile = off // FFN_TILE
    last_tile = jnp.maximum(off + counts - 1, off) // FFN_TILE
    n_e = jnp.where(counts > 0, last_tile - first_tile + 1, 0)
    start = jnp.cumsum(n_e) - n_e
    total = jnp.sum(n_e)
    w = jnp.minimum(jnp.arange(n_items, dtype=I32), total - 1)
    ids = jnp.arange(N_EXPERTS, dtype=I32)[None, :]
    e = jnp.max(jnp.where((start[None, :] <= w[:, None]) & (n_e[None, :] > 0), ids, 0), axis=1)
    pick = lambda a: jnp.sum(jnp.where(ids == e[:, None], a[None, :], 0), axis=1)
    tile = pick(first_tile) + (w - pick(start))
    lo = jnp.clip(pick(off) - tile * FFN_TILE, 0, FFN_TILE)
    hi = jnp.clip(pick(off + counts) - tile * FFN_TILE, 0, FFN_TILE)
    hi = jnp.where(jnp.arange(n_items) < total, hi, lo)
    return tile.astype(I32), e.astype(I32), lo.astype(I32), hi.astype(I32)


def _run_table(n):
    tiles = n.shape[0]
    counts = jnp.sum(n, axis=0)
    goff = jnp.cumsum(counts) - counts
    dst0 = goff[None, :] + jnp.cumsum(n, axis=0) - n
    src0 = jnp.cumsum(n, axis=1) - n
    bits = jnp.arange(RUN_BITS, dtype=I32)[:, None, None]
    flag = (n[None] >> bits) & 1
    done = (n[None] >> (bits + 1)) << (bits + 1)
    entry = ((dst0[None] + done) << SRC_BITS) | (src0[None] + done)
    rank = jnp.cumsum(flag, axis=2) - flag
    slot = jnp.arange(N_EXPERTS, dtype=I32)
    sel = (flag[..., None] == 1) & (rank[..., None] == slot)
    packed = jnp.sum(jnp.where(sel, entry[..., None], 0), axis=2)
    m = jnp.sum(flag, axis=2).T
    tab = jnp.zeros((tiles, TAB_COLS), I32)
    tab = tab.at[:, :RUN_BITS].set(m)
    tab = tab.at[:, N_EXPERTS:N_EXPERTS * (RUN_BITS + 1)].set(packed.transpose(1, 0, 2).reshape(tiles, -1))
    return counts, tab


def _combine_kernel(tab_ref, h_ref, p_ref, ri_ref, rw_ref, ys_ref, gple_ref, wpg_ref, wpp_ref, gfin_ref, o_ref,
                    idx_smem, rows_scr, isem, rsem):
    i, n = pl.program_id(0), pl.num_programs(0)
    slot = i % 2

    def idx_copy(step, sl):
        return pltpu.make_async_copy(tab_ref.at[step], idx_smem.at[sl], isem.at[sl])

    def gather(sl):
        _run_copies(idx_smem, sl, lambda src, dst, rows_: pltpu.make_async_copy(
            _row_slice(ys_ref, dst, rows_), _row_slice(rows_scr.at[sl], src, rows_), rsem.at[sl]))

    @pl.when(i == 0)
    def _():
        idx_copy(0, 0).start()
        idx_copy(0, 0).wait()
        gather(0)

    @pl.when((i == 0) & (n >= 2))
    def _():
        idx_copy(1, 1).start()

    @pl.when(i + 1 < n)
    def _():
        idx_copy(i + 1, 1 - slot).wait()
        gather(1 - slot)

    @pl.when(i + 2 < n)
    def _():
        idx_copy(i + 2, slot).start()

    _wait_tile_rows(ys_ref, rows_scr.at[slot], rsem.at[slot])

    ri, rw = ri_ref[...], rw_ref[...]
    rows = _pair_rows_iota()
    wperm = jnp.where(rows == ri[2:3], rw[0:1], 0.0) + jnp.where(rows == ri[3:4], rw[1:2], 0.0)
    w_hi = wperm.astype(BF16)
    w_lo = (wperm - w_hi.astype(F32)).astype(BF16)
    s = _load_rows(rows_scr.at[slot], PAIR_ROWS).astype(BF16)
    h = h_ref[...] + _dot_tn(w_hi, s) + _dot_tn(w_lo, s)
    gate = jax.nn.sigmoid(_dot(_rms(h, gple_ref[...]).astype(BF16), wpg_ref[...]))
    h = h + _dot(p_ref[...].astype(BF16), wpp_ref[...]) * gate
    o_ref[...] = _rms(h, gfin_ref[...])


def _combine(tab, h, p, ri, rw, ys, gple, wpg, wpp, gfin):
    n = tab.shape[0]
    n_tok = n * TOK_TILE
    return pl.pallas_call(
        _combine_kernel,
        grid=(n,),
        in_specs=[
            pl.BlockSpec(memory_space=pl.ANY),
            pl.BlockSpec((TOK_TILE, D_MODEL), lambda i: (i, 0)),
            pl.BlockSpec((TOK_TILE, PLE_DIM), lambda i: (i, 0)),
            pl.BlockSpec((SUBLANES, TOK_TILE), lambda i: (0, i)),
            pl.BlockSpec((SUBLANES, TOK_TILE), lambda i: (0, i)),
            pl.BlockSpec(memory_space=pl.ANY),
            _const_spec(gple.shape), _const_spec(wpg.shape), _const_spec(wpp.shape), _const_spec(gfin.shape),
        ],
        out_specs=pl.BlockSpec((TOK_TILE, D_MODEL), lambda i: (i, 0)),
        out_shape=jax.ShapeDtypeStruct((n_tok, D_MODEL), F32),
        scratch_shapes=[
            pltpu.SMEM((2, TAB_COLS), I32),
            pltpu.VMEM((2, PAIR_ROWS * ROW_CHUNKS, LANES), F32),
            pltpu.SemaphoreType.DMA((2,)),
            pltpu.SemaphoreType.DMA((2,)),
        ],
        compiler_params=_params("arbitrary"),
        name="moe_combine",
    )(tab, h, p, ri, rw, ys, gple, wpg, wpp, gfin)


def _log_gamma():
    return jnp.log(1.0 - 2.0 ** (-5.0 - jnp.arange(RET_HEADS, dtype=F32)))


def _rope_tables(pos):
    inv = 1.0 / (ROPE_BASE ** (jnp.arange(0, RET_DK, 2, dtype=F32) / RET_DK))
    ang = pos.astype(F32)[:, None] * inv[None, :]
    cos, sin = jnp.cos(ang), jnp.sin(ang)
    return jnp.concatenate([cos, cos], axis=-1), jnp.concatenate([-sin, sin], axis=-1)


def _decay_tables(c):
    lg = _log_gamma()
    idx = jnp.arange(c, dtype=F32)
    diff = idx[:, None] - idx[None, :]
    dmask = jnp.where((diff >= 0.0)[None], jnp.exp(lg[:, None, None] * jnp.maximum(diff, 0.0)[None]), 0.0)
    q_dec = jnp.exp(lg[None, :] * (idx[:, None] + 1.0))
    k_dec = jnp.exp(lg[None, :] * (c - 1.0 - idx[:, None]))
    c_dec = jnp.exp(lg * c)
    lanes = lambda a: jnp.repeat(a, RET_DK, axis=1)
    cdec = jnp.zeros((SUBLANES, LANES), F32).at[:RET_HEADS].set(jnp.broadcast_to(c_dec[:, None], (RET_HEADS, LANES)))
    return lanes(q_dec), lanes(k_dec), dmask, cdec


def _router_params(w_rg, b_rg, w_re, b_re):
    wr = jnp.zeros((ROUTER_ROWS, D_MODEL), F32).at[:N_GROUPS].set(w_rg.T).at[SUBLANES:].set(w_re.T)
    br = jnp.full((ROUTER_ROWS,), NEG_BIG, F32).at[:N_GROUPS].set(b_rg).at[SUBLANES:].set(b_re.reshape(-1))
    return wr.astype(BF16), jnp.broadcast_to(br[:, None], (ROUTER_ROWS, LANES))


def _strict_upper(t):
    i = jnp.arange(t)
    return (i[:, None] < i[None, :]).astype(BF16)


def _pad_rows(a, rows):
    return jnp.zeros((rows,) + a.shape[1:], a.dtype).at[:a.shape[0]].set(a)


def kernel(x_prompt, x_sample, state_conv, state_ret, p_prompt, p_sample, g_mix, w_in, conv_w, ret_gn, w_o, g_ffn,
           w_router_group, b_router_group, w_router_expert, b_router_expert, w_gate, w_up, w_down, g_ple,
           w_ple_proj, w_ple_gate, g_final):
    bp, tp, _ = x_prompt.shape
    bs = x_sample.shape[0]
    assert x_sample.shape[1] == 1 and g_mix.shape[0] == 1, "one layer, one new token per sample request"
    assert bs <= TOK_TILE and tp % TOK_TILE == 0
    n_p = bp * tp

    tabs = _rope_tables(jnp.arange(tp, dtype=I32)) + _decay_tables(RET_CHUNK)
    wr, br = _router_params(w_router_group[0], b_router_group[0], w_router_expert[0], b_router_expert[0])
    win, wo, convw = w_in[0].astype(BF16), w_o[0].astype(BF16), _pad_rows(conv_w[0], SUBLANES)
    tri, ltri = _strict_upper(TOK_TILE), _strict_upper(N_EXPERTS).T

    h_p, ri_p, rw_p, n_p_tab, conv_p, ret_p = _mixer_prompt(
        x_prompt, tabs, (g_mix, win, convw, ret_gn, wo, g_ffn, wr, br, tri, ltri))
    h_s, ri_s, rw_s, n_s_tab, conv_s, ret_s = _mixer_sample(
        x_sample[:, 0], state_conv[0], state_ret[0], g_mix, win, convw, ret_gn, wo, g_ffn, wr, br, tri, ltri)
    h_p = h_p.reshape(n_p, D_MODEL)

    n_tab = jnp.concatenate([n_p_tab, n_s_tab], axis=0)[:, :, 0].astype(I32)
    counts, tab = _run_table(n_tab)
    ri = jnp.concatenate([ri_p, ri_s], axis=1)
    n_tiles = n_tab.shape[0]

    xs = _dispatch(tab, h_p, h_s, ri, g_ffn)
    ys = _expert_ffn(_work_items(counts, n_tiles * PAIR_ROWS), xs, w_gate[0], w_up[0], w_down[0])

    wpg, wpp = w_ple_gate[0].astype(BF16), w_ple_proj[0].astype(BF16)
    gfin = g_final[None, :]
    p_s = _pad_rows(p_sample[0].reshape(bs, PLE_DIM), TOK_TILE)
    y_p = _combine(tab[:n_tiles - 1], h_p, p_prompt[0].reshape(n_p, PLE_DIM), ri_p, rw_p, ys, g_ple, wpg, wpp, gfin)
    y_s = _combine(tab[n_tiles - 1:], h_s, p_s, ri_s, rw_s, ys, g_ple, wpg, wpp, gfin)
    return (y_p.reshape(bp, tp, D_MODEL), y_s[:bs].reshape(bs, 1, D_MODEL), conv_p[None], ret_p[None],
            conv_s[None], ret_s[None])
```

```python
import functools

import jax
import jax.numpy as jnp
from jax import lax
from jax.experimental import pallas as pl
from jax.experimental.pallas import tpu as pltpu

F32, BF16, I32 = jnp.float32, jnp.bfloat16, jnp.int32

D_MODEL = 1024
CONV_DIM = 512
CONV_WIDTH = 3
RET_DIM = 512
RET_HEADS = 4
RET_DK = 128
RET_DV = 128
RET_CHUNK = 128
ROPE_BASE = 10000.0
IN_PROJ_DIM = 3 * CONV_DIM + 4 * RET_DIM
N_GROUPS = 4
EXPERTS_PER_GROUP = 8
N_EXPERTS = 32
D_FF = 512
PLE_DIM = 256
EPS = 1e-6
PAST_LEN = 16384

LANES = 128
SUBLANES = 8
ROW_CHUNKS = D_MODEL // LANES
ROUTER_ROWS = SUBLANES + N_EXPERTS
VMEM_LIMIT = 56 * 1024 * 1024
NEG_BIG = -1e30

TOK_TILE = 256
PAIR_ROWS = 2 * TOK_TILE
FFN_TILE = 256
SAMPLE_BLOCK = 8
RUN_BITS = 9
SRC_BITS = 10
TAB_COLS = 384


def _rms(x, g):
    return x * lax.rsqrt(jnp.mean(x * x, axis=-1, keepdims=True) + EPS) * g


def _dot(a, b):
    return jnp.dot(a, b, preferred_element_type=F32)


def _dot_nt(a, b):
    return lax.dot_general(a, b, (((1,), (1,)), ((), ())), preferred_element_type=F32)


def _dot_tn(a, b):
    return lax.dot_general(a, b, (((0,), (0,)), ((), ())), preferred_element_type=F32)


def _rope(x, cos, sin_signed):
    return x * cos + pltpu.roll(x, RET_DK // 2, 1) * sin_signed


def _const_spec(shape):
    nd = len(shape)
    return pl.BlockSpec(shape, lambda *_: (0,) * nd)


def _params(*sem):
    return pltpu.CompilerParams(dimension_semantics=sem, vmem_limit_bytes=VMEM_LIMIT)


def _route(h, gffn, wr, br, tri, ltri):
    t = h.shape[0]
    xn = _rms(h, gffn).astype(BF16)
    lt = _dot_nt(wr, xn) + br[:, 0:1]
    row8 = lax.broadcasted_iota(I32, (SUBLANES, t), 0).astype(F32)
    gl = lt[0:SUBLANES]
    m = jnp.max(gl, axis=0, keepdims=True)
    g_top = 1.0 / jnp.sum(jnp.exp(gl - m), axis=0, keepdims=True)
    gidx = jnp.min(jnp.where(gl == m, row8, float(SUBLANES)), axis=0, keepdims=True)
    e_sel = jnp.where(gidx == 0.0, lt[8:16],
                      jnp.where(gidx == 1.0, lt[16:24], jnp.where(gidx == 2.0, lt[24:32], lt[32:40])))
    m1 = jnp.max(e_sel, axis=0, keepdims=True)
    i1 = jnp.min(jnp.where(e_sel == m1, row8, float(SUBLANES)), axis=0, keepdims=True)
    rest = jnp.where(row8 == i1, -jnp.inf, e_sel)
    m2 = jnp.max(rest, axis=0, keepdims=True)
    i2 = jnp.min(jnp.where(rest == m2, row8, float(SUBLANES)), axis=0, keepdims=True)
    d = jnp.exp(m2 - m1)
    w1 = g_top / (1.0 + d)
    w2 = g_top * d / (1.0 + d)
    e1 = gidx * float(EXPERTS_PER_GROUP) + i1
    e2 = gidx * float(EXPERTS_PER_GROUP) + i2
    row32 = lax.broadcasted_iota(I32, (N_EXPERTS, t), 0).astype(F32)
    a1 = jnp.where(row32 == e1, 1.0, 0.0)
    a2 = jnp.where(row32 == e2, 1.0, 0.0)
    a = a1 + a2
    n = jnp.broadcast_to(jnp.sum(a, axis=1, keepdims=True), (N_EXPERTS, LANES))
    start = _dot(ltri, n.astype(BF16))
    base = _dot(a.astype(BF16), tri) + start[:, 0:1]
    r1 = jnp.sum(a1 * base, axis=0, keepdims=True)
    r2 = jnp.sum(a2 * base, axis=0, keepdims=True)
    ri = jnp.where(row8 == 0.0, e1, jnp.where(row8 == 1.0, e2, jnp.where(row8 == 2.0, r1,
                                                                         jnp.where(row8 == 3.0, r2, 0.0))))
    rw = jnp.where(row8 == 0.0, w1, jnp.where(row8 == 1.0, w2, 0.0))
    return ri.astype(I32), rw, n


def _group_norm_gate(o, gate):
    mu = jnp.mean(o, axis=-1, keepdims=True)
    oc = o - mu
    return oc * lax.rsqrt(jnp.mean(oc * oc, axis=-1, keepdims=True) + EPS) * gate


def _mixer_prompt_kernel(x_ref, cos_ref, sin_ref, qdec_ref, kdec_ref, dmask_ref, cdec_ref, gmix_ref, win_ref,
                         convw_ref, retgn_ref, wo_ref, gffn_ref, wr_ref, br_ref, tri_ref, ltri_ref,
                         h_ref, ri_ref, rw_ref, n_ref, conv_ref, ret_ref,
                         s_scr, z_scr, mix_scr):
    j = pl.program_id(1)
    n_j = pl.num_programs(1)
    tt = x_ref.shape[1]

    @pl.when(j == 0)
    def _():
        s_scr[...] = jnp.zeros_like(s_scr)
        z_scr[0:SUBLANES, :] = jnp.zeros((SUBLANES, CONV_DIM), F32)

    x = x_ref[0]
    u = _rms(x, gmix_ref[...]).astype(BF16)
    proj = _dot(u, win_ref[...])
    gb = proj[:, 0:CONV_DIM]
    z = proj[:, CONV_DIM:2 * CONV_DIM] * proj[:, 2 * CONV_DIM:3 * CONV_DIM]
    q0 = 3 * CONV_DIM

    z_scr[SUBLANES:SUBLANES + tt, :] = z
    cw = convw_ref[...]
    yc = (cw[0:1] * z_scr[SUBLANES - 2:SUBLANES - 2 + tt, :]
          + cw[1:2] * z_scr[SUBLANES - 1:SUBLANES - 1 + tt, :] + cw[2:3] * z)
    mix_scr[:, 0:CONV_DIM] = (gb * yc).astype(BF16)
    z_scr[0:SUBLANES, :] = z_scr[tt:tt + SUBLANES, :]

    retgn = retgn_ref[...]
    for c in range(tt // RET_CHUNK):
        r0 = c * RET_CHUNK
        cos = cos_ref[r0:r0 + RET_CHUNK, :]
        sin = sin_ref[r0:r0 + RET_CHUNK, :]
        for hh in range(RET_HEADS):
            l0 = hh * RET_DK
            qr = _rope(proj[r0:r0 + RET_CHUNK, q0 + l0:q0 + l0 + RET_DK], cos, sin)
            kr = _rope(proj[r0:r0 + RET_CHUNK, q0 + RET_DIM + l0:q0 + RET_DIM + l0 + RET_DK], cos, sin) * (RET_DK ** -0.5)
            v = proj[r0:r0 + RET_CHUNK, q0 + 2 * RET_DIM + l0:q0 + 2 * RET_DIM + l0 + RET_DV].astype(BF16)
            g = proj[r0:r0 + RET_CHUNK, q0 + 3 * RET_DIM + l0:q0 + 3 * RET_DIM + l0 + RET_DV]
            s_old = s_scr[hh]
            scores = _dot_nt(qr.astype(BF16), kr.astype(BF16)) * dmask_ref[hh]
            o = _dot(scores.astype(BF16), v) + _dot((qr * qdec_ref[:, l0:l0 + RET_DK]).astype(BF16), s_old.astype(BF16))
            kd = (kr * kdec_ref[:, l0:l0 + RET_DK]).T.astype(BF16)
            s_scr[hh] = s_old * cdec_ref[hh:hh + 1, :] + _dot(kd, v)
            gate = retgn[:, l0:l0 + RET_DV] * (g * jax.nn.sigmoid(g))
            mix_scr[r0:r0 + RET_CHUNK, CONV_DIM + l0:CONV_DIM + l0 + RET_DV] = _group_norm_gate(o, gate).astype(BF16)

    h = x + _dot(mix_scr[...], wo_ref[...])
    h_ref[0] = h

    ri, rw, n = _route(h, gffn_ref[...], wr_ref[...], br_ref[...], tri_ref[...], ltri_ref[...])
    ri_ref[...] = ri
    rw_ref[...] = rw
    n_ref[0] = n

    @pl.when(j == n_j - 1)
    def _():
        conv_ref[0] = z[tt - 2:tt, :]
        ret_ref[0] = s_scr[...]


def _mixer_prompt(x, tabs, wts):
    bsz, t, _ = x.shape
    tt = TOK_TILE
    n_j = t // tt
    n_tok = bsz * t
    args = (x,) + tuple(tabs) + tuple(wts)
    in_specs = [
        pl.BlockSpec((1, tt, D_MODEL), lambda b, j: (b, j, 0)),
        pl.BlockSpec((tt, LANES), lambda b, j: (j, 0)),
        pl.BlockSpec((tt, LANES), lambda b, j: (j, 0)),
    ] + [_const_spec(a.shape) for a in args[3:]]
    out_shape = (
        jax.ShapeDtypeStruct((bsz, t, D_MODEL), F32),
        jax.ShapeDtypeStruct((SUBLANES, n_tok), I32),
        jax.ShapeDtypeStruct((SUBLANES, n_tok), F32),
        jax.ShapeDtypeStruct((bsz * n_j, N_EXPERTS, LANES), F32),
        jax.ShapeDtypeStruct((bsz, CONV_WIDTH - 1, CONV_DIM), F32),
        jax.ShapeDtypeStruct((bsz, RET_HEADS, RET_DK, RET_DV), F32),
    )
    out_specs = (
        pl.BlockSpec((1, tt, D_MODEL), lambda b, j: (b, j, 0)),
        pl.BlockSpec((SUBLANES, tt), lambda b, j: (0, b * n_j + j)),
        pl.BlockSpec((SUBLANES, tt), lambda b, j: (0, b * n_j + j)),
        pl.BlockSpec((1, N_EXPERTS, LANES), lambda b, j: (b * n_j + j, 0, 0)),
        pl.BlockSpec((1, CONV_WIDTH - 1, CONV_DIM), lambda b, j: (b, 0, 0)),
        pl.BlockSpec((1, RET_HEADS, RET_DK, RET_DV), lambda b, j: (b, 0, 0, 0)),
    )
    return pl.pallas_call(
        _mixer_prompt_kernel,
        grid=(bsz, n_j),
        in_specs=in_specs,
        out_specs=out_specs,
        out_shape=out_shape,
        scratch_shapes=[
            pltpu.VMEM((RET_HEADS, RET_DK, RET_DV), F32),
            pltpu.VMEM((tt + SUBLANES, CONV_DIM), F32),
            pltpu.VMEM((tt, D_MODEL), BF16),
        ],
        compiler_params=_params("arbitrary", "arbitrary"),
        name="mixer_prompt",
    )(*args)


def _sample_proj_kernel(x_ref, sc_ref, cos_ref, sin_ref, gmix_ref, win_ref, convw_ref, retgn_ref,
                        yc_ref, q_ref, k_ref, v_ref, gate_ref, conv_ref):
    x = x_ref[...]
    u = _rms(x, gmix_ref[...]).astype(BF16)
    proj = _dot(u, win_ref[...])
    gb = proj[:, 0:CONV_DIM]
    z = proj[:, CONV_DIM:2 * CONV_DIM] * proj[:, 2 * CONV_DIM:3 * CONV_DIM]
    cw = convw_ref[...]
    buf0, buf1 = sc_ref[:, 0:CONV_DIM], sc_ref[:, CONV_DIM:2 * CONV_DIM]
    yc_ref[...] = gb * (cw[0:1] * buf0 + cw[1:2] * buf1 + cw[2:3] * z)
    conv_ref[:, 0:CONV_DIM] = buf1
    conv_ref[:, CONV_DIM:2 * CONV_DIM] = z
    q0 = 3 * CONV_DIM
    cos, sin = cos_ref[0:1, :], sin_ref[0:1, :]
    retgn = retgn_ref[...]
    for hh in range(RET_HEADS):
        l0 = hh * RET_DK
        q_ref[:, l0:l0 + RET_DK] = _rope(proj[:, q0 + l0:q0 + l0 + RET_DK], cos, sin)
        k_ref[:, l0:l0 + RET_DK] = _rope(proj[:, q0 + RET_DIM + l0:q0 + RET_DIM + l0 + RET_DK], cos, sin) * (RET_DK ** -0.5)
    v_ref[...] = proj[:, q0 + 2 * RET_DIM:q0 + 3 * RET_DIM]
    g = proj[:, q0 + 3 * RET_DIM:q0 + 4 * RET_DIM]
    gate_ref[...] = retgn * (g * jax.nn.sigmoid(g))


def _sample_proj(x, sc, cos, sin, gmix, win, convw, retgn):
    n = x.shape[0]
    outs = (
        jax.ShapeDtypeStruct((n, CONV_DIM), F32), jax.ShapeDtypeStruct((n, RET_DIM), F32),
        jax.ShapeDtypeStruct((n, RET_DIM), F32), jax.ShapeDtypeStruct((n, RET_DIM), F32),
        jax.ShapeDtypeStruct((n, RET_DIM), F32), jax.ShapeDtypeStruct((n, 2 * CONV_DIM), F32),
    )
    args = (x, sc, cos, sin, gmix, win, convw, retgn)
    return pl.pallas_call(
        _sample_proj_kernel,
        grid=(1,),
        in_specs=[_const_spec(a.shape) for a in args],
        out_specs=tuple(_const_spec(o.shape) for o in outs),
        out_shape=outs,
        compiler_params=_params("arbitrary"),
        name="sample_proj",
    )(*args)


def _sample_state_kernel(s_ref, qt_ref, kt_ref, v_ref, dec_ref, snew_ref, o_ref):
    for r in range(s_ref.shape[0]):
        for hh in range(RET_HEADS):
            l0 = hh * RET_DV
            s = s_ref[r, hh]
            qc = qt_ref[0, hh, :, r:r + 1]
            kc = kt_ref[0, hh, :, r:r + 1]
            vr = v_ref[r:r + 1, l0:l0 + RET_DV]
            qdec = dec_ref[hh:hh + 1, :]
            cdec = dec_ref[RET_HEADS + hh:RET_HEADS + hh + 1, :]
            qk = jnp.sum(qc * kc, axis=0, keepdims=True)
            o_ref[r:r + 1, l0:l0 + RET_DV] = qk * vr + jnp.sum((qc * qdec[:, 0:1]) * s, axis=0, keepdims=True)
            snew_ref[r, hh] = s * cdec + kc * vr


def _sample_state(s, qt, kt, v, dec):
    n = s.shape[0]
    nb = SAMPLE_BLOCK
    return pl.pallas_call(
        _sample_state_kernel,
        grid=(n // nb,),
        in_specs=[
            pl.BlockSpec((nb, RET_HEADS, RET_DK, RET_DV), lambda i: (i, 0, 0, 0)),
            pl.BlockSpec((1, RET_HEADS, RET_DK, nb), lambda i: (i, 0, 0, 0)),
            pl.BlockSpec((1, RET_HEADS, RET_DK, nb), lambda i: (i, 0, 0, 0)),
            pl.BlockSpec((nb, RET_DIM), lambda i: (i, 0)),
            _const_spec(dec.shape),
        ],
        out_specs=(
            pl.BlockSpec((nb, RET_HEADS, RET_DK, RET_DV), lambda i: (i, 0, 0, 0)),
            pl.BlockSpec((nb, RET_DIM), lambda i: (i, 0)),
        ),
        out_shape=(jax.ShapeDtypeStruct(s.shape, F32), jax.ShapeDtypeStruct((n, RET_DIM), F32)),
        compiler_params=_params("arbitrary"),
        name="sample_state",
    )(s, qt, kt, v, dec)


def _sample_out_kernel(x_ref, yc_ref, o_ref, gate_ref, wo_ref, gffn_ref, wr_ref, br_ref, tri_ref, ltri_ref,
                       h_ref, ri_ref, rw_ref, n_ref):
    n = x_ref.shape[0]
    parts = [yc_ref[...].astype(BF16)]
    for hh in range(RET_HEADS):
        l0 = hh * RET_DV
        parts.append(_group_norm_gate(o_ref[:, l0:l0 + RET_DV], gate_ref[:, l0:l0 + RET_DV]).astype(BF16))
    h_ref[0:n, :] = x_ref[...] + _dot(jnp.concatenate(parts, axis=-1), wo_ref[...])
    if n < TOK_TILE:
        h_ref[n:TOK_TILE, :] = jnp.zeros((TOK_TILE - n, D_MODEL), F32)
    ri, rw, cnt = _route(h_ref[...], gffn_ref[...], wr_ref[...], br_ref[...], tri_ref[...], ltri_ref[...])
    ri_ref[...] = ri
    rw_ref[...] = rw
    n_ref[0] = cnt


def _sample_out(x, yc, o, gate, wo, gffn, wr, br, tri, ltri):
    outs = (
        jax.ShapeDtypeStruct((TOK_TILE, D_MODEL), F32), jax.ShapeDtypeStruct((SUBLANES, TOK_TILE), I32),
        jax.ShapeDtypeStruct((SUBLANES, TOK_TILE), F32), jax.ShapeDtypeStruct((1, N_EXPERTS, LANES), F32),
    )
    args = (x, yc, o, gate, wo, gffn, wr, br, tri, ltri)
    return pl.pallas_call(
        _sample_out_kernel,
        grid=(1,),
        in_specs=[_const_spec(a.shape) for a in args],
        out_specs=tuple(_const_spec(o_.shape) for o_ in outs),
        out_shape=outs,
        compiler_params=_params("arbitrary"),
        name="sample_out",
    )(*args)


def _mixer_sample(x, state_conv, state_ret, gmix, win, convw, retgn, wo, gffn, wr, br, tri, ltri):
    n = x.shape[0]
    lg = _log_gamma()
    cos, sin = _rope_tables(jnp.full((SUBLANES,), PAST_LEN, I32))
    yc, q, k, v, gate, conv_new = _sample_proj(x, state_conv.reshape(n, 2 * CONV_DIM), cos, sin, gmix, win, convw, retgn)
    nb = SAMPLE_BLOCK
    cols = lambda a: a.reshape(n // nb, nb, RET_HEADS, RET_DK).transpose(0, 2, 3, 1)
    step = jnp.exp(lg[:, None] * 1.0)
    dec = jnp.broadcast_to(jnp.concatenate([step, step], axis=0), (2 * RET_HEADS, LANES))
    s_new, o = _sample_state(state_ret, cols(q), cols(k), v, dec)
    h, ri, rw, cnt = _sample_out(x, yc, o, gate, wo, gffn, wr, br, tri, ltri)
    return h, ri, rw, cnt, conv_new.reshape(n, 2, CONV_DIM), s_new


def _load_rows(ref, n_rows):
    return jnp.concatenate([ref[pl.ds(c, n_rows, stride=ROW_CHUNKS), :] for c in range(ROW_CHUNKS)], axis=-1)


def _store_rows(ref, val):
    for c in range(ROW_CHUNKS):
        ref[pl.ds(c, val.shape[0], stride=ROW_CHUNKS), :] = val[:, c * LANES:(c + 1) * LANES]


def _row_slice(ref, row, rows):
    return ref.at[pl.ds(pl.multiple_of(row * ROW_CHUNKS, ROW_CHUNKS), rows * ROW_CHUNKS)]


def _run_copies(idx_smem, slot, make_copy):
    for b in range(RUN_BITS):
        def body(j, carry, b=b):
            entry = idx_smem[slot, N_EXPERTS + N_EXPERTS * b + j]
            make_copy(entry & ((1 << SRC_BITS) - 1), entry >> SRC_BITS, 1 << b).start()
            return carry

        lax.fori_loop(0, idx_smem[slot, b], body, 0)


def _wait_tile_rows(hbm_ref, vmem_ref, sem):
    pltpu.make_async_copy(_row_slice(hbm_ref, 0, PAIR_ROWS), vmem_ref, sem).wait()


def _pair_rows_iota():
    return lax.broadcasted_iota(I32, (PAIR_ROWS, TOK_TILE), 0)


def _dispatch_kernel(n_p, tab_ref, hp_ref, hs_ref, ri_ref, gffn_ref, xs_ref, idx_smem, y_scr, isem, rsem):
    i, n = pl.program_id(0), pl.num_programs(0)
    slot = i % 2

    def idx_copy(step, sl):
        return pltpu.make_async_copy(tab_ref.at[step], idx_smem.at[sl], isem.at[sl])

    @pl.when(i == 0)
    def _():
        idx_copy(0, 0).start()

    @pl.when(i + 1 < n)
    def _():
        idx_copy(i + 1, 1 - slot).start()

    @pl.when(i >= 2)
    def _():
        _wait_tile_rows(xs_ref, y_scr.at[slot], rsem.at[slot])

    h = jnp.where(i < n_p, hp_ref[...], hs_ref[...])
    xn = _rms(h, gffn_ref[...]).astype(BF16)
    ri = ri_ref[...]
    rows = _pair_rows_iota()
    perm = jnp.where((rows == ri[2:3]) | (rows == ri[3:4]), 1.0, 0.0).astype(BF16)
    y = _dot(perm, xn)
    _store_rows(y_scr.at[slot], y)

    idx_copy(i, slot).wait()
    _run_copies(idx_smem, slot, lambda src, dst, rows_: pltpu.make_async_copy(
        _row_slice(y_scr.at[slot], src, rows_), _row_slice(xs_ref, dst, rows_), rsem.at[slot]))

    @pl.when(i == n - 1)
    def _():
        _wait_tile_rows(xs_ref, y_scr.at[slot], rsem.at[slot])

    @pl.when((i == n - 1) & (n >= 2))
    def _():
        _wait_tile_rows(xs_ref, y_scr.at[1 - slot], rsem.at[1 - slot])


def _dispatch(tab, h_p, h_s, ri, gffn):
    n_p = h_p.shape[0] // TOK_TILE
    n = tab.shape[0]
    return pl.pallas_call(
        functools.partial(_dispatch_kernel, n_p),
        grid=(n,),
        in_specs=[
            pl.BlockSpec(memory_space=pl.ANY),
            pl.BlockSpec((TOK_TILE, D_MODEL), lambda i: (jnp.minimum(i, n_p - 1), 0)),
            pl.BlockSpec((TOK_TILE, D_MODEL), lambda i: (jnp.maximum(i - n_p, 0), 0)),
            pl.BlockSpec((SUBLANES, TOK_TILE), lambda i: (0, i)),
            _const_spec(gffn.shape),
        ],
        out_specs=pl.BlockSpec(memory_space=pl.ANY),
        out_shape=jax.ShapeDtypeStruct((n * PAIR_ROWS * ROW_CHUNKS, LANES), F32),
        scratch_shapes=[
            pltpu.SMEM((2, TAB_COLS), I32),
            pltpu.VMEM((2, PAIR_ROWS * ROW_CHUNKS, LANES), F32),
            pltpu.SemaphoreType.DMA((2,)),
            pltpu.SemaphoreType.DMA((2,)),
        ],
        compiler_params=_params("arbitrary"),
        name="moe_dispatch",
    )(tab, h_p, h_s, ri, gffn)


def _ffn_kernel(tile_ref, exp_ref, lo_ref, hi_ref, nxt_ref, slot_ref, x_ref, wg_ref, wu_ref, wd_ref, o_ref,
                wg_buf, wu_buf, wd_buf, wgu_scr, wd_scr, y_scr, wsem):
    w, n_w = pl.program_id(0), pl.num_programs(0)
    prev, nxt = jnp.maximum(w - 1, 0), jnp.minimum(w + 1, n_w - 1)
    lo, hi = lo_ref[w], hi_ref[w]
    e, slot = exp_ref[w], slot_ref[w]
    first = (w == 0) | (tile_ref[w] != tile_ref[prev])
    last = (w == n_w - 1) | (tile_ref[w] != tile_ref[nxt])

    def weight_copies(expert, sl):
        return [pltpu.make_async_copy(src.at[expert], buf.at[sl], wsem.at[sl])
                for src, buf in ((wg_ref, wg_buf), (wu_ref, wu_buf), (wd_ref, wd_buf))]

    @pl.when(w == 0)
    def _():
        for cp in weight_copies(e, slot):
            cp.start()

    @pl.when((w == 0) | (e != exp_ref[prev]))
    def _():
        for cp in weight_copies(e, slot):
            cp.wait()

        @pl.when(nxt_ref[w] != e)
        def _():
            for cp in weight_copies(nxt_ref[w], 1 - slot):
                cp.start()

        wgu_scr[:, 0:D_FF] = wg_buf[slot].astype(BF16)
        wgu_scr[:, D_FF:2 * D_FF] = wu_buf[slot].astype(BF16)
        wd_scr[...] = wd_buf[slot].astype(BF16)

    @pl.when(hi > lo)
    def _():
        gu = _dot(_load_rows(x_ref, FFN_TILE).astype(BF16), wgu_scr[...])
        g, u = gu[:, 0:D_FF], gu[:, D_FF:2 * D_FF]
        y = _dot((g * jax.nn.sigmoid(g) * u).astype(BF16), wd_scr[...])

        @pl.when(first)
        def _():
            y_scr[...] = y

        @pl.when(jnp.logical_not(first))
        def _():
            row = lax.broadcasted_iota(I32, (FFN_TILE, 1), 0)
            y_scr[...] = jnp.where((row >= lo) & (row < hi), y, y_scr[...])

    @pl.when(last)
    def _():
        _store_rows(o_ref, y_scr[...])


def _expert_ffn(items, xs, w_gate, w_up, w_down):
    n_items = items[0].shape[0]
    row_spec = pl.BlockSpec((FFN_TILE * ROW_CHUNKS, LANES), lambda w, t, *_: (t[w], 0))
    grid_spec = pltpu.PrefetchScalarGridSpec(
        num_scalar_prefetch=len(items),
        grid=(n_items,),
        in_specs=[row_spec] + [pl.BlockSpec(memory_space=pl.ANY)] * 3,
        out_specs=row_spec,
        scratch_shapes=[
            pltpu.VMEM((2, D_MODEL, D_FF), F32), pltpu.VMEM((2, D_MODEL, D_FF), F32),
            pltpu.VMEM((2, D_FF, D_MODEL), F32),
            pltpu.VMEM((D_MODEL, 2 * D_FF), BF16), pltpu.VMEM((D_FF, D_MODEL), BF16),
            pltpu.VMEM((FFN_TILE, D_MODEL), F32),
            pltpu.SemaphoreType.DMA((2,)),
        ],
    )
    return pl.pallas_call(
        _ffn_kernel,
        grid_spec=grid_spec,
        out_shape=jax.ShapeDtypeStruct(xs.shape, F32),
        compiler_params=_params("arbitrary"),
        name="moe_ffn",
    )(*items, xs, w_gate, w_up, w_down)


def _work_items(counts, n_rows):
    n_tiles = n_rows // FFN_TILE
    n_items = n_tiles + N_EXPERTS - 1
    off = jnp.cumsum(counts) - counts
    first_tile = off // FFN_TILE
    last_tile = jnp.maximum(off + counts - 1, off) // FFN_TILE
    n_e = jnp.where(counts > 0, last_tile - first_tile + 1, 0)
    start = jnp.cumsum(n_e) - n_e
    total = jnp.sum(n_e)
    w = jnp.minimum(jnp.arange(n_items, dtype=I32), total - 1)
    ids = jnp.arange(N_EXPERTS, dtype=I32)[None, :]
    e = jnp.max(jnp.where((start[None, :] <= w[:, None]) & (n_e[None, :] > 0), ids, 0), axis=1)
    pick = lambda a: jnp.sum(jnp.where(ids == e[:, None], a[None, :], 0), axis=1)
    tile = pick(first_tile) + (w - pick(start))
    lo = jnp.clip(pick(off) - tile * FFN_TILE, 0, FFN_TILE)
    hi = jnp.clip(pick(off + counts) - tile * FFN_TILE, 0, FFN_TILE)
    hi = jnp.where(jnp.arange(n_items) < total, hi, lo)
    used = n_e[None, :] > 0
    nxt = jnp.min(jnp.where(used & (ids > e[:, None]), ids, N_EXPERTS), axis=1)
    nxt = jnp.where(nxt == N_EXPERTS, e, nxt)
    slot = jnp.sum(jnp.where(used & (ids < e[:, None]), 1, 0), axis=1) % 2
    return tuple(a.astype(I32) for a in (tile, e, lo, hi, nxt, slot))


def _run_table(n):
    tiles = n.shape[0]
    counts = jnp.sum(n, axis=0)
    goff = jnp.cumsum(counts) - counts
    dst0 = goff[None, :] + jnp.cumsum(n, axis=0) - n
    src0 = jnp.cumsum(n, axis=1) - n
    bits = jnp.arange(RUN_BITS, dtype=I32)[:, None, None]
    flag = (n[None] >> bits) & 1
    done = (n[None] >> (bits + 1)) << (bits + 1)
    entry = ((dst0[None] + done) << SRC_BITS) | (src0[None] + done)
    rank = jnp.cumsum(flag, axis=2) - flag
    slot = jnp.arange(N_EXPERTS, dtype=I32)
    sel = (flag[..., None] == 1) & (rank[..., None] == slot)
    packed = jnp.sum(jnp.where(sel, entry[..., None], 0), axis=2)
    m = jnp.sum(flag, axis=2).T
    tab = jnp.zeros((tiles, TAB_COLS), I32)
    tab = tab.at[:, :RUN_BITS].set(m)
    tab = tab.at[:, N_EXPERTS:N_EXPERTS * (RUN_BITS + 1)].set(packed.transpose(1, 0, 2).reshape(tiles, -1))
    return counts, tab


def _combine_kernel(tab_ref, h_ref, p_ref, ri_ref, rw_ref, ys_ref, gple_ref, wpg_ref, wpp_ref, gfin_ref, o_ref,
                    idx_smem, rows_scr, isem, rsem):
    i, n = pl.program_id(0), pl.num_programs(0)
    slot = i % 2

    def idx_copy(step, sl):
        return pltpu.make_async_copy(tab_ref.at[step], idx_smem.at[sl], isem.at[sl])

    def gather(sl):
        _run_copies(idx_smem, sl, lambda src, dst, rows_: pltpu.make_async_copy(
            _row_slice(ys_ref, dst, rows_), _row_slice(rows_scr.at[sl], src, rows_), rsem.at[sl]))

    @pl.when(i == 0)
    def _():
        idx_copy(0, 0).start()
        idx_copy(0, 0).wait()
        gather(0)

    @pl.when((i == 0) & (n >= 2))
    def _():
        idx_copy(1, 1).start()

    @pl.when(i + 1 < n)
    def _():
        idx_copy(i + 1, 1 - slot).wait()
        gather(1 - slot)

    @pl.when(i + 2 < n)
    def _():
        idx_copy(i + 2, slot).start()

    _wait_tile_rows(ys_ref, rows_scr.at[slot], rsem.at[slot])

    ri, rw = ri_ref[...], rw_ref[...]
    rows = _pair_rows_iota()
    wperm = jnp.where(rows == ri[2:3], rw[0:1], 0.0) + jnp.where(rows == ri[3:4], rw[1:2], 0.0)
    w_hi = wperm.astype(BF16)
    w_lo = (wperm - w_hi.astype(F32)).astype(BF16)
    s = _load_rows(rows_scr.at[slot], PAIR_ROWS).astype(BF16)
    h = h_ref[...] + _dot_tn(w_hi, s) + _dot_tn(w_lo, s)
    gate = jax.nn.sigmoid(_dot(_rms(h, gple_ref[...]).astype(BF16), wpg_ref[...]))
    h = h + _dot(p_ref[...].astype(BF16), wpp_ref[...]) * gate
    o_ref[...] = _rms(h, gfin_ref[...])


def _combine(tab, h, p, ri, rw, ys, gple, wpg, wpp, gfin):
    n = tab.shape[0]
    n_tok = n * TOK_TILE
    return pl.pallas_call(
        _combine_kernel,
        grid=(n,),
        in_specs=[
            pl.BlockSpec(memory_space=pl.ANY),
            pl.BlockSpec((TOK_TILE, D_MODEL), lambda i: (i, 0)),
            pl.BlockSpec((TOK_TILE, PLE_DIM), lambda i: (i, 0)),
            pl.BlockSpec((SUBLANES, TOK_TILE), lambda i: (0, i)),
            pl.BlockSpec((SUBLANES, TOK_TILE), lambda i: (0, i)),
            pl.BlockSpec(memory_space=pl.ANY),
            _const_spec(gple.shape), _const_spec(wpg.shape), _const_spec(wpp.shape), _const_spec(gfin.shape),
        ],
        out_specs=pl.BlockSpec((TOK_TILE, D_MODEL), lambda i: (i, 0)),
        out_shape=jax.ShapeDtypeStruct((n_tok, D_MODEL), F32),
        scratch_shapes=[
            pltpu.SMEM((2, TAB_COLS), I32),
            pltpu.VMEM((2, PAIR_ROWS * ROW_CHUNKS, LANES), F32),
            pltpu.SemaphoreType.DMA((2,)),
            pltpu.SemaphoreType.DMA((2,)),
        ],
        compiler_params=_params("arbitrary"),
        name="moe_combine",
    )(tab, h, p, ri, rw, ys, gple, wpg, wpp, gfin)


def _log_gamma():
    return jnp.log(1.0 - 2.0 ** (-5.0 - jnp.arange(RET_HEADS, dtype=F32)))


def _rope_tables(pos):
    inv = 1.0 / (ROPE_BASE ** (jnp.arange(0, RET_DK, 2, dtype=F32) / RET_DK))
    ang = pos.astype(F32)[:, None] * inv[None, :]
    cos, sin = jnp.cos(ang), jnp.sin(ang)
    return jnp.concatenate([cos, cos], axis=-1), jnp.concatenate([-sin, sin], axis=-1)


def _decay_tables(c):
    lg = _log_gamma()
    idx = jnp.arange(c, dtype=F32)
    diff = idx[:, None] - idx[None, :]
    dmask = jnp.where((diff >= 0.0)[None], jnp.exp(lg[:, None, None] * jnp.maximum(diff, 0.0)[None]), 0.0)
    q_dec = jnp.exp(lg[None, :] * (idx[:, None] + 1.0))
    k_dec = jnp.exp(lg[None, :] * (c - 1.0 - idx[:, None]))
    c_dec = jnp.exp(lg * c)
    lanes = lambda a: jnp.repeat(a, RET_DK, axis=1)
    cdec = jnp.zeros((SUBLANES, LANES), F32).at[:RET_HEADS].set(jnp.broadcast_to(c_dec[:, None], (RET_HEADS, LANES)))
    return lanes(q_dec), lanes(k_dec), dmask, cdec


def _router_params(w_rg, b_rg, w_re, b_re):
    wr = jnp.zeros((ROUTER_ROWS, D_MODEL), F32).at[:N_GROUPS].set(w_rg.T).at[SUBLANES:].set(w_re.T)
    br = jnp.full((ROUTER_ROWS,), NEG_BIG, F32).at[:N_GROUPS].set(b_rg).at[SUBLANES:].set(b_re.reshape(-1))
    return wr.astype(BF16), jnp.broadcast_to(br[:, None], (ROUTER_ROWS, LANES))


def _strict_upper(t):
    i = jnp.arange(t)
    return (i[:, None] < i[None, :]).astype(BF16)


def _pad_rows(a, rows):
    return jnp.zeros((rows,) + a.shape[1:], a.dtype).at[:a.shape[0]].set(a)


def kernel(x_prompt, x_sample, state_conv, state_ret, p_prompt, p_sample, g_mix, w_in, conv_w, ret_gn, w_o, g_ffn,
           w_router_group, b_router_group, w_router_expert, b_router_expert, w_gate, w_up, w_down, g_ple,
           w_ple_proj, w_ple_gate, g_final):
    bp, tp, _ = x_prompt.shape
    bs = x_sample.shape[0]
    assert x_sample.shape[1] == 1 and g_mix.shape[0] == 1, "one layer, one new token per sample request"
    assert bs <= TOK_TILE and tp % TOK_TILE == 0
    n_p = bp * tp

    tabs = _rope_tables(jnp.arange(tp, dtype=I32)) + _decay_tables(RET_CHUNK)
    wr, br = _router_params(w_router_group[0], b_router_group[0], w_router_expert[0], b_router_expert[0])
    win, wo, convw = w_in[0].astype(BF16), w_o[0].astype(BF16), _pad_rows(conv_w[0], SUBLANES)
    tri, ltri = _strict_upper(TOK_TILE), _strict_upper(N_EXPERTS).T

    h_p, ri_p, rw_p, n_p_tab, conv_p, ret_p = _mixer_prompt(
        x_prompt, tabs, (g_mix, win, convw, ret_gn, wo, g_ffn, wr, br, tri, ltri))
    h_s, ri_s, rw_s, n_s_tab, conv_s, ret_s = _mixer_sample(
        x_sample[:, 0], state_conv[0], state_ret[0], g_mix, win, convw, ret_gn, wo, g_ffn, wr, br, tri, ltri)
    h_p = h_p.reshape(n_p, D_MODEL)

    n_tab = jnp.concatenate([n_p_tab, n_s_tab], axis=0)[:, :, 0].astype(I32)
    counts, tab = _run_table(n_tab)
    ri = jnp.concatenate([ri_p, ri_s], axis=1)
    n_tiles = n_tab.shape[0]

    xs = _dispatch(tab, h_p, h_s, ri, g_ffn)
    ys = _expert_ffn(_work_items(counts, n_tiles * PAIR_ROWS), xs, w_gate[0], w_up[0], w_down[0])

    wpg, wpp = w_ple_gate[0].astype(BF16), w_ple_proj[0].astype(BF16)
    gfin = g_final[None, :]
    p_s = _pad_rows(p_sample[0].reshape(bs, PLE_DIM), TOK_TILE)
    y_p = _combine(tab[:n_tiles - 1], h_p, p_prompt[0].reshape(n_p, PLE_DIM), ri_p, rw_p, ys, g_ple, wpg, wpp, gfin)
    y_s = _combine(tab[n_tiles - 1:], h_s, p_s, ri_s, rw_s, ys, g_ple, wpg, wpp, gfin)
    return (y_p.reshape(bp, tp, D_MODEL), y_s[:bs].reshape(bs, 1, D_MODEL), conv_p[None], ret_p[None],
            conv_s[None], ret_s[None])
```

```python
import functools

import jax
import jax.numpy as jnp
from jax import lax
from jax.experimental import pallas as pl
from jax.experimental.pallas import tpu as pltpu

F32, BF16, I32, U32 = jnp.float32, jnp.bfloat16, jnp.int32, jnp.uint32

D_MODEL = 1024
CONV_DIM = 512
CONV_WIDTH = 3
RET_DIM = 512
RET_HEADS = 4
RET_DK = 128
RET_DV = 128
RET_CHUNK = 128
ROPE_BASE = 10000.0
IN_PROJ_DIM = 3 * CONV_DIM + 4 * RET_DIM
N_GROUPS = 4
EXPERTS_PER_GROUP = 8
N_EXPERTS = 32
D_FF = 512
PLE_DIM = 256
EPS = 1e-6
PAST_LEN = 16384

LANES = 128
SUBLANES = 8
ROW_CHUNKS = D_MODEL // (2 * LANES)
ROUTER_ROWS = SUBLANES + N_EXPERTS
VMEM_LIMIT = 56 * 1024 * 1024
NEG_BIG = -1e30

TOK_TILE = 256
MIX_TILE = 512
PAIR_ROWS = 2 * TOK_TILE
FFN_TILE = 256
SAMPLE_BLOCK = 8
RUN_BITS = 9
SRC_BITS = 10
TAB_COLS = 384


def _rms(x, g):
    return x * lax.rsqrt(jnp.mean(x * x, axis=-1, keepdims=True) + EPS) * g


def _dot(a, b):
    return jnp.dot(a, b, preferred_element_type=F32)


def _dot_nt(a, b):
    return lax.dot_general(a, b, (((1,), (1,)), ((), ())), preferred_element_type=F32)


def _dot_tn(a, b):
    return lax.dot_general(a, b, (((0,), (0,)), ((), ())), preferred_element_type=F32)


def _rope(x, cos, sin_signed):
    return x * cos + pltpu.roll(x, RET_DK // 2, 1) * sin_signed


def _const_spec(shape):
    nd = len(shape)
    return pl.BlockSpec(shape, lambda *_: (0,) * nd)


def _params(*sem):
    return pltpu.CompilerParams(dimension_semantics=sem, vmem_limit_bytes=VMEM_LIMIT)


def _route(h, gffn, wr, br, tri, ltri):
    t = h.shape[0]
    xn = _rms(h, gffn).astype(BF16)
    lt = _dot_nt(wr, xn) + br[:, 0:1]
    row8 = lax.broadcasted_iota(I32, (SUBLANES, t), 0).astype(F32)
    gl = lt[0:SUBLANES]
    m = jnp.max(gl, axis=0, keepdims=True)
    g_top = 1.0 / jnp.sum(jnp.exp(gl - m), axis=0, keepdims=True)
    gidx = jnp.min(jnp.where(gl == m, row8, float(SUBLANES)), axis=0, keepdims=True)
    e_sel = jnp.where(gidx == 0.0, lt[8:16],
                      jnp.where(gidx == 1.0, lt[16:24], jnp.where(gidx == 2.0, lt[24:32], lt[32:40])))
    m1 = jnp.max(e_sel, axis=0, keepdims=True)
    i1 = jnp.min(jnp.where(e_sel == m1, row8, float(SUBLANES)), axis=0, keepdims=True)
    rest = jnp.where(row8 == i1, -jnp.inf, e_sel)
    m2 = jnp.max(rest, axis=0, keepdims=True)
    i2 = jnp.min(jnp.where(rest == m2, row8, float(SUBLANES)), axis=0, keepdims=True)
    d = jnp.exp(m2 - m1)
    w1 = g_top / (1.0 + d)
    w2 = g_top * d / (1.0 + d)
    e1 = gidx * float(EXPERTS_PER_GROUP) + i1
    e2 = gidx * float(EXPERTS_PER_GROUP) + i2
    row32 = lax.broadcasted_iota(I32, (N_EXPERTS, t), 0).astype(F32)
    a1 = jnp.where(row32 == e1, 1.0, 0.0)
    a2 = jnp.where(row32 == e2, 1.0, 0.0)
    a = a1 + a2
    n = jnp.broadcast_to(jnp.sum(a, axis=1, keepdims=True), (N_EXPERTS, LANES))
    start = _dot(ltri, n.astype(BF16))
    base = _dot(a.astype(BF16), tri) + start[:, 0:1]
    r1 = jnp.sum(a1 * base, axis=0, keepdims=True)
    r2 = jnp.sum(a2 * base, axis=0, keepdims=True)
    ri = jnp.where(row8 == 0.0, e1, jnp.where(row8 == 1.0, e2, jnp.where(row8 == 2.0, r1,
                                                                         jnp.where(row8 == 3.0, r2, 0.0))))
    rw = jnp.where(row8 == 0.0, w1, jnp.where(row8 == 1.0, w2, 0.0))
    return ri.astype(I32), rw, n


def _group_norm_gate(o, gate):
    mu = jnp.mean(o, axis=-1, keepdims=True)
    oc = o - mu
    return oc * lax.rsqrt(jnp.mean(oc * oc, axis=-1, keepdims=True) + EPS) * gate


def _mixer_prompt_kernel(x_ref, cos_ref, sin_ref, qdec_ref, kdec_ref, dmask_ref, cdec_ref, gmix_ref, win_ref,
                         convw_ref, retgn_ref, wo_ref, gffn_ref, wr_ref, br_ref, tri_ref, ltri_ref,
                         h_ref, ri_ref, rw_ref, n_ref, conv_ref, ret_ref,
                         s_scr, z_scr, mix_scr):
    j = pl.program_id(1)
    n_j = pl.num_programs(1)
    tt = x_ref.shape[1]

    @pl.when(j == 0)
    def _():
        s_scr[...] = jnp.zeros_like(s_scr)
        z_scr[0:SUBLANES, :] = jnp.zeros((SUBLANES, CONV_DIM), F32)

    cw = convw_ref[...]
    retgn = retgn_ref[...]
    q0 = 3 * CONV_DIM
    for part in range(tt // TOK_TILE):
        p0 = part * TOK_TILE
        x = x_ref[0, p0:p0 + TOK_TILE, :]
        u = _rms(x, gmix_ref[...]).astype(BF16)
        proj = _dot(u, win_ref[...])
        gb = proj[:, 0:CONV_DIM]
        z = proj[:, CONV_DIM:2 * CONV_DIM] * proj[:, 2 * CONV_DIM:3 * CONV_DIM]

        z0 = SUBLANES + p0
        z_scr[z0:z0 + TOK_TILE, :] = z
        yc = cw[0:1] * z_scr[z0 - 2:z0 - 2 + TOK_TILE, :] + cw[1:2] * z_scr[z0 - 1:z0 - 1 + TOK_TILE, :] + cw[2:3] * z
        mix_scr[p0:p0 + TOK_TILE, 0:CONV_DIM] = (gb * yc).astype(BF16)

        for c in range(TOK_TILE // RET_CHUNK):
            r0 = c * RET_CHUNK
            cos = cos_ref[p0 + r0:p0 + r0 + RET_CHUNK, :]
            sin = sin_ref[p0 + r0:p0 + r0 + RET_CHUNK, :]
            for hh in range(RET_HEADS):
                l0 = hh * RET_DK
                qr = _rope(proj[r0:r0 + RET_CHUNK, q0 + l0:q0 + l0 + RET_DK], cos, sin)
                kr = _rope(proj[r0:r0 + RET_CHUNK, q0 + RET_DIM + l0:q0 + RET_DIM + l0 + RET_DK], cos, sin) * (RET_DK ** -0.5)
                v = proj[r0:r0 + RET_CHUNK, q0 + 2 * RET_DIM + l0:q0 + 2 * RET_DIM + l0 + RET_DV].astype(BF16)
                g = proj[r0:r0 + RET_CHUNK, q0 + 3 * RET_DIM + l0:q0 + 3 * RET_DIM + l0 + RET_DV]
                s_old = s_scr[hh]
                scores = _dot_nt(qr.astype(BF16), kr.astype(BF16)) * dmask_ref[hh]
                o = (_dot(scores.astype(BF16), v)
                     + _dot((qr * qdec_ref[:, l0:l0 + RET_DK]).astype(BF16), s_old.astype(BF16)))
                kd = (kr * kdec_ref[:, l0:l0 + RET_DK]).T.astype(BF16)
                s_scr[hh] = s_old * cdec_ref[hh:hh + 1, :] + _dot(kd, v)
                gate = retgn[:, l0:l0 + RET_DV] * (g * jax.nn.sigmoid(g))
                mix_scr[p0 + r0:p0 + r0 + RET_CHUNK, CONV_DIM + l0:CONV_DIM + l0 + RET_DV] = (
                    _group_norm_gate(o, gate).astype(BF16))

        h = x + _dot(mix_scr[p0:p0 + TOK_TILE, :], wo_ref[...])
        h_ref[0, p0:p0 + TOK_TILE, :] = h
        ri, rw, n = _route(h, gffn_ref[...], wr_ref[...], br_ref[...], tri_ref[...], ltri_ref[...])
        ri_ref[:, p0:p0 + TOK_TILE] = ri
        rw_ref[:, p0:p0 + TOK_TILE] = rw
        n_ref[part] = n

    @pl.when(j == n_j - 1)
    def _():
        conv_ref[0] = z_scr[SUBLANES + tt - 2:SUBLANES + tt, :]
        ret_ref[0] = s_scr[...]

    z_scr[0:SUBLANES, :] = z_scr[tt:tt + SUBLANES, :]


def _mixer_prompt(x, tabs, wts):
    bsz, t, _ = x.shape
    tt = MIX_TILE
    n_j = t // tt
    parts = tt // TOK_TILE
    n_tok = bsz * t
    args = (x,) + tuple(tabs) + tuple(wts)
    in_specs = [
        pl.BlockSpec((1, tt, D_MODEL), lambda b, j: (b, j, 0)),
        pl.BlockSpec((tt, LANES), lambda b, j: (j, 0)),
        pl.BlockSpec((tt, LANES), lambda b, j: (j, 0)),
    ] + [_const_spec(a.shape) for a in args[3:]]
    out_shape = (
        jax.ShapeDtypeStruct((bsz, t, D_MODEL), F32),
        jax.ShapeDtypeStruct((SUBLANES, n_tok), I32),
        jax.ShapeDtypeStruct((SUBLANES, n_tok), F32),
        jax.ShapeDtypeStruct((bsz * n_j * parts, N_EXPERTS, LANES), F32),
        jax.ShapeDtypeStruct((bsz, CONV_WIDTH - 1, CONV_DIM), F32),
        jax.ShapeDtypeStruct((bsz, RET_HEADS, RET_DK, RET_DV), F32),
    )
    out_specs = (
        pl.BlockSpec((1, tt, D_MODEL), lambda b, j: (b, j, 0)),
        pl.BlockSpec((SUBLANES, tt), lambda b, j: (0, b * n_j + j)),
        pl.BlockSpec((SUBLANES, tt), lambda b, j: (0, b * n_j + j)),
        pl.BlockSpec((parts, N_EXPERTS, LANES), lambda b, j: (b * n_j + j, 0, 0)),
        pl.BlockSpec((1, CONV_WIDTH - 1, CONV_DIM), lambda b, j: (b, 0, 0)),
        pl.BlockSpec((1, RET_HEADS, RET_DK, RET_DV), lambda b, j: (b, 0, 0, 0)),
    )
    return pl.pallas_call(
        _mixer_prompt_kernel,
        grid=(bsz, n_j),
        in_specs=in_specs,
        out_specs=out_specs,
        out_shape=out_shape,
        scratch_shapes=[
            pltpu.VMEM((RET_HEADS, RET_DK, RET_DV), F32),
            pltpu.VMEM((tt + SUBLANES, CONV_DIM), F32),
            pltpu.VMEM((tt, D_MODEL), BF16),
        ],
        compiler_params=_params("arbitrary", "arbitrary"),
        name="mixer_prompt",
    )(*args)


def _sample_proj_kernel(x_ref, sc_ref, cos_ref, sin_ref, gmix_ref, win_ref, convw_ref, retgn_ref,
                        yc_ref, q_ref, k_ref, v_ref, gate_ref, conv_ref):
    x = x_ref[...]
    u = _rms(x, gmix_ref[...]).astype(BF16)
    proj = _dot(u, win_ref[...])
    gb = proj[:, 0:CONV_DIM]
    z = proj[:, CONV_DIM:2 * CONV_DIM] * proj[:, 2 * CONV_DIM:3 * CONV_DIM]
    cw = convw_ref[...]
    buf0, buf1 = sc_ref[:, 0:CONV_DIM], sc_ref[:, CONV_DIM:2 * CONV_DIM]
    yc_ref[...] = gb * (cw[0:1] * buf0 + cw[1:2] * buf1 + cw[2:3] * z)
    conv_ref[:, 0:CONV_DIM] = buf1
    conv_ref[:, CONV_DIM:2 * CONV_DIM] = z
    q0 = 3 * CONV_DIM
    cos, sin = cos_ref[0:1, :], sin_ref[0:1, :]
    retgn = retgn_ref[...]
    for hh in range(RET_HEADS):
        l0 = hh * RET_DK
        q_ref[:, l0:l0 + RET_DK] = _rope(proj[:, q0 + l0:q0 + l0 + RET_DK], cos, sin)
        k_ref[:, l0:l0 + RET_DK] = _rope(proj[:, q0 + RET_DIM + l0:q0 + RET_DIM + l0 + RET_DK], cos, sin) * (RET_DK ** -0.5)
    v_ref[...] = proj[:, q0 + 2 * RET_DIM:q0 + 3 * RET_DIM]
    g = proj[:, q0 + 3 * RET_DIM:q0 + 4 * RET_DIM]
    gate_ref[...] = retgn * (g * jax.nn.sigmoid(g))


def _sample_proj(x, sc, cos, sin, gmix, win, convw, retgn):
    n = x.shape[0]
    outs = (
        jax.ShapeDtypeStruct((n, CONV_DIM), F32), jax.ShapeDtypeStruct((n, RET_DIM), F32),
        jax.ShapeDtypeStruct((n, RET_DIM), F32), jax.ShapeDtypeStruct((n, RET_DIM), F32),
        jax.ShapeDtypeStruct((n, RET_DIM), F32), jax.ShapeDtypeStruct((n, 2 * CONV_DIM), F32),
    )
    args = (x, sc, cos, sin, gmix, win, convw, retgn)
    return pl.pallas_call(
        _sample_proj_kernel,
        grid=(1,),
        in_specs=[_const_spec(a.shape) for a in args],
        out_specs=tuple(_const_spec(o.shape) for o in outs),
        out_shape=outs,
        compiler_params=_params("arbitrary"),
        name="sample_proj",
    )(*args)


def _sample_state_kernel(s_ref, qt_ref, kt_ref, v_ref, dec_ref, snew_ref, o_ref):
    for r in range(s_ref.shape[0]):
        for hh in range(RET_HEADS):
            l0 = hh * RET_DV
            s = s_ref[r, hh]
            qc = qt_ref[0, hh, :, r:r + 1]
            kc = kt_ref[0, hh, :, r:r + 1]
            vr = v_ref[r:r + 1, l0:l0 + RET_DV]
            qdec = dec_ref[hh:hh + 1, :]
            cdec = dec_ref[RET_HEADS + hh:RET_HEADS + hh + 1, :]
            qk = jnp.sum(qc * kc, axis=0, keepdims=True)
            o_ref[r:r + 1, l0:l0 + RET_DV] = qk * vr + jnp.sum((qc * qdec[:, 0:1]) * s, axis=0, keepdims=True)
            snew_ref[r, hh] = s * cdec + kc * vr


def _sample_state(s, qt, kt, v, dec):
    n = s.shape[0]
    nb = SAMPLE_BLOCK
    return pl.pallas_call(
        _sample_state_kernel,
        grid=(n // nb,),
        in_specs=[
            pl.BlockSpec((nb, RET_HEADS, RET_DK, RET_DV), lambda i: (i, 0, 0, 0)),
            pl.BlockSpec((1, RET_HEADS, RET_DK, nb), lambda i: (i, 0, 0, 0)),
            pl.BlockSpec((1, RET_HEADS, RET_DK, nb), lambda i: (i, 0, 0, 0)),
            pl.BlockSpec((nb, RET_DIM), lambda i: (i, 0)),
            _const_spec(dec.shape),
        ],
        out_specs=(
            pl.BlockSpec((nb, RET_HEADS, RET_DK, RET_DV), lambda i: (i, 0, 0, 0)),
            pl.BlockSpec((nb, RET_DIM), lambda i: (i, 0)),
        ),
        out_shape=(jax.ShapeDtypeStruct(s.shape, F32), jax.ShapeDtypeStruct((n, RET_DIM), F32)),
        compiler_params=_params("arbitrary"),
        name="sample_state",
    )(s, qt, kt, v, dec)


def _sample_out_kernel(x_ref, yc_ref, o_ref, gate_ref, wo_ref, gffn_ref, wr_ref, br_ref, tri_ref, ltri_ref,
                       h_ref, ri_ref, rw_ref, n_ref):
    n = x_ref.shape[0]
    parts = [yc_ref[...].astype(BF16)]
    for hh in range(RET_HEADS):
        l0 = hh * RET_DV
        parts.append(_group_norm_gate(o_ref[:, l0:l0 + RET_DV], gate_ref[:, l0:l0 + RET_DV]).astype(BF16))
    h_ref[0:n, :] = x_ref[...] + _dot(jnp.concatenate(parts, axis=-1), wo_ref[...])
    if n < TOK_TILE:
        h_ref[n:TOK_TILE, :] = jnp.zeros((TOK_TILE - n, D_MODEL), F32)
    ri, rw, cnt = _route(h_ref[...], gffn_ref[...], wr_ref[...], br_ref[...], tri_ref[...], ltri_ref[...])
    ri_ref[...] = ri
    rw_ref[...] = rw
    n_ref[0] = cnt


def _sample_out(x, yc, o, gate, wo, gffn, wr, br, tri, ltri):
    outs = (
        jax.ShapeDtypeStruct((TOK_TILE, D_MODEL), F32), jax.ShapeDtypeStruct((SUBLANES, TOK_TILE), I32),
        jax.ShapeDtypeStruct((SUBLANES, TOK_TILE), F32), jax.ShapeDtypeStruct((1, N_EXPERTS, LANES), F32),
    )
    args = (x, yc, o, gate, wo, gffn, wr, br, tri, ltri)
    return pl.pallas_call(
        _sample_out_kernel,
        grid=(1,),
        in_specs=[_const_spec(a.shape) for a in args],
        out_specs=tuple(_const_spec(o_.shape) for o_ in outs),
        out_shape=outs,
        compiler_params=_params("arbitrary"),
        name="sample_out",
    )(*args)


def _mixer_sample(x, state_conv, state_ret, gmix, win, convw, retgn, wo, gffn, wr, br, tri, ltri):
    n = x.shape[0]
    lg = _log_gamma()
    cos, sin = _rope_tables(jnp.full((SUBLANES,), PAST_LEN, I32))
    yc, q, k, v, gate, conv_new = _sample_proj(x, state_conv.reshape(n, 2 * CONV_DIM), cos, sin, gmix, win, convw, retgn)
    nb = SAMPLE_BLOCK
    cols = lambda a: a.reshape(n // nb, nb, RET_HEADS, RET_DK).transpose(0, 2, 3, 1)
    step = jnp.exp(lg[:, None] * 1.0)
    dec = jnp.broadcast_to(jnp.concatenate([step, step], axis=0), (2 * RET_HEADS, LANES))
    s_new, o = _sample_state(state_ret, cols(q), cols(k), v, dec)
    h, ri, rw, cnt = _sample_out(x, yc, o, gate, wo, gffn, wr, br, tri, ltri)
    return h, ri, rw, cnt, conv_new.reshape(n, 2, CONV_DIM), s_new


def _load_rows(ref, n_rows):
    words = [ref[pl.ds(c, n_rows, stride=ROW_CHUNKS), :] for c in range(ROW_CHUNKS)]
    half = lambda i: [pltpu.unpack_elementwise(w, index=i, packed_dtype=BF16, unpacked_dtype=F32) for w in words]
    return jnp.concatenate(half(0) + half(1), axis=-1)


def _store_rows(ref, val):
    half = D_MODEL // 2
    for c in range(ROW_CHUNKS):
        pair = [val[:, c * LANES:(c + 1) * LANES], val[:, half + c * LANES:half + (c + 1) * LANES]]
        ref[pl.ds(c, val.shape[0], stride=ROW_CHUNKS), :] = pltpu.pack_elementwise(pair, packed_dtype=BF16)


def _row_slice(ref, row, rows):
    return ref.at[pl.ds(pl.multiple_of(row * ROW_CHUNKS, ROW_CHUNKS), rows * ROW_CHUNKS)]


def _run_copies(idx_smem, slot, make_copy):
    for b in range(RUN_BITS):
        def body(j, carry, b=b):
            entry = idx_smem[slot, N_EXPERTS + N_EXPERTS * b + j]
            make_copy(entry & ((1 << SRC_BITS) - 1), entry >> SRC_BITS, 1 << b).start()
            return carry

        lax.fori_loop(0, idx_smem[slot, b], body, 0)


def _wait_tile_rows(hbm_ref, vmem_ref, sem):
    pltpu.make_async_copy(_row_slice(hbm_ref, 0, PAIR_ROWS), vmem_ref, sem).wait()


def _pair_rows_iota():
    return lax.broadcasted_iota(I32, (PAIR_ROWS, TOK_TILE), 0)


def _dispatch_kernel(n_p, tab_ref, hp_ref, hs_ref, ri_ref, gffn_ref, xs_ref, idx_smem, y_scr, isem, rsem):
    i, n = pl.program_id(0), pl.num_programs(0)
    slot = i % 2

    def idx_copy(step, sl):
        return pltpu.make_async_copy(tab_ref.at[step], idx_smem.at[sl], isem.at[sl])

    @pl.when(i == 0)
    def _():
        idx_copy(0, 0).start()

    @pl.when(i + 1 < n)
    def _():
        idx_copy(i + 1, 1 - slot).start()

    @pl.when(i >= 2)
    def _():
        _wait_tile_rows(xs_ref, y_scr.at[slot], rsem.at[slot])

    h = jnp.where(i < n_p, hp_ref[...], hs_ref[...])
    xn = _rms(h, gffn_ref[...]).astype(BF16)
    ri = ri_ref[...]
    rows = _pair_rows_iota()
    perm = jnp.where((rows == ri[2:3]) | (rows == ri[3:4]), 1.0, 0.0).astype(BF16)
    y = _dot(perm, xn)
    _store_rows(y_scr.at[slot], y)

    idx_copy(i, slot).wait()
    _run_copies(idx_smem, slot, lambda src, dst, rows_: pltpu.make_async_copy(
        _row_slice(y_scr.at[slot], src, rows_), _row_slice(xs_ref, dst, rows_), rsem.at[slot]))

    @pl.when(i == n - 1)
    def _():
        _wait_tile_rows(xs_ref, y_scr.at[slot], rsem.at[slot])

    @pl.when((i == n - 1) & (n >= 2))
    def _():
        _wait_tile_rows(xs_ref, y_scr.at[1 - slot], rsem.at[1 - slot])


def _dispatch(tab, h_p, h_s, ri, gffn):
    n_p = h_p.shape[0] // TOK_TILE
    n = tab.shape[0]
    return pl.pallas_call(
        functools.partial(_dispatch_kernel, n_p),
        grid=(n,),
        in_specs=[
            pl.BlockSpec(memory_space=pl.ANY),
            pl.BlockSpec((TOK_TILE, D_MODEL), lambda i: (jnp.minimum(i, n_p - 1), 0)),
            pl.BlockSpec((TOK_TILE, D_MODEL), lambda i: (jnp.maximum(i - n_p, 0), 0)),
            pl.BlockSpec((SUBLANES, TOK_TILE), lambda i: (0, i)),
            _const_spec(gffn.shape),
        ],
        out_specs=pl.BlockSpec(memory_space=pl.ANY),
        out_shape=jax.ShapeDtypeStruct((n * PAIR_ROWS * ROW_CHUNKS, LANES), U32),
        scratch_shapes=[
            pltpu.SMEM((2, TAB_COLS), I32),
            pltpu.VMEM((2, PAIR_ROWS * ROW_CHUNKS, LANES), U32),
            pltpu.SemaphoreType.DMA((2,)),
            pltpu.SemaphoreType.DMA((2,)),
        ],
        compiler_params=_params("arbitrary"),
        name="moe_dispatch",
    )(tab, h_p, h_s, ri, gffn)


def _ffn_kernel(tile_ref, exp_ref, lo_ref, hi_ref, nxt_ref, slot_ref, x_ref, wg_ref, wu_ref, wd_ref, o_ref,
                wg_buf, wu_buf, wd_buf, wgu_scr, wd_scr, y_scr, wsem):
    w, n_w = pl.program_id(0), pl.num_programs(0)
    prev, nxt = jnp.maximum(w - 1, 0), jnp.minimum(w + 1, n_w - 1)
    lo, hi = lo_ref[w], hi_ref[w]
    e, slot = exp_ref[w], slot_ref[w]
    first = (w == 0) | (tile_ref[w] != tile_ref[prev])
    last = (w == n_w - 1) | (tile_ref[w] != tile_ref[nxt])

    def weight_copies(expert, sl):
        return [pltpu.make_async_copy(src.at[expert], buf.at[sl], wsem.at[sl])
                for src, buf in ((wg_ref, wg_buf), (wu_ref, wu_buf), (wd_ref, wd_buf))]

    @pl.when(w == 0)
    def _():
        for cp in weight_copies(e, slot):
            cp.start()

    @pl.when((w == 0) | (e != exp_ref[prev]))
    def _():
        for cp in weight_copies(e, slot):
            cp.wait()

        @pl.when(nxt_ref[w] != e)
        def _():
            for cp in weight_copies(nxt_ref[w], 1 - slot):
                cp.start()

        wgu_scr[:, 0:D_FF] = wg_buf[slot].astype(BF16)
        wgu_scr[:, D_FF:2 * D_FF] = wu_buf[slot].astype(BF16)
        wd_scr[...] = wd_buf[slot].astype(BF16)

    @pl.when(hi > lo)
    def _():
        gu = _dot(_load_rows(x_ref, FFN_TILE).astype(BF16), wgu_scr[...])
        g, u = gu[:, 0:D_FF], gu[:, D_FF:2 * D_FF]
        y = _dot((g * jax.nn.sigmoid(g) * u).astype(BF16), wd_scr[...])

        @pl.when(first)
        def _():
            y_scr[...] = y

        @pl.when(jnp.logical_not(first))
        def _():
            row = lax.broadcasted_iota(I32, (FFN_TILE, 1), 0)
            y_scr[...] = jnp.where((row >= lo) & (row < hi), y, y_scr[...])

    @pl.when(last)
    def _():
        _store_rows(o_ref, y_scr[...])


def _expert_ffn(items, xs, w_gate, w_up, w_down):
    n_items = items[0].shape[0]
    row_spec = pl.BlockSpec((FFN_TILE * ROW_CHUNKS, LANES), lambda w, t, *_: (t[w], 0))
    grid_spec = pltpu.PrefetchScalarGridSpec(
        num_scalar_prefetch=len(items),
        grid=(n_items,),
        in_specs=[row_spec] + [pl.BlockSpec(memory_space=pl.ANY)] * 3,
        out_specs=row_spec,
        scratch_shapes=[
            pltpu.VMEM((2, D_MODEL, D_FF), F32), pltpu.VMEM((2, D_MODEL, D_FF), F32),
            pltpu.VMEM((2, D_FF, D_MODEL), F32),
            pltpu.VMEM((D_MODEL, 2 * D_FF), BF16), pltpu.VMEM((D_FF, D_MODEL), BF16),
            pltpu.VMEM((FFN_TILE, D_MODEL), F32),
            pltpu.SemaphoreType.DMA((2,)),
        ],
    )
    return pl.pallas_call(
        _ffn_kernel,
        grid_spec=grid_spec,
        out_shape=jax.ShapeDtypeStruct(xs.shape, U32),
        compiler_params=_params("arbitrary"),
        name="moe_ffn",
    )(*items, xs, w_gate, w_up, w_down)


def _work_items(counts, n_rows):
    n_tiles = n_rows // FFN_TILE
    n_items = n_tiles + N_EXPERTS - 1
    off = jnp.cumsum(counts) - counts
    first_tile = off // FFN_TILE
    last_tile = jnp.maximum(off + counts - 1, off) // FFN_TILE
    n_e = jnp.where(counts > 0, last_tile - first_tile + 1, 0)
    start = jnp.cumsum(n_e) - n_e
    total = jnp.sum(n_e)
    w = jnp.minimum(jnp.arange(n_items, dtype=I32), total - 1)
    ids = jnp.arange(N_EXPERTS, dtype=I32)[None, :]
    e = jnp.max(jnp.where((start[None, :] <= w[:, None]) & (n_e[None, :] > 0), ids, 0), axis=1)
    pick = lambda a: jnp.sum(jnp.where(ids == e[:, None], a[None, :], 0), axis=1)
    tile = pick(first_tile) + (w - pick(start))
    lo = jnp.clip(pick(off) - tile * FFN_TILE, 0, FFN_TILE)
    hi = jnp.clip(pick(off + counts) - tile * FFN_TILE, 0, FFN_TILE)
    hi = jnp.where(jnp.arange(n_items) < total, hi, lo)
    used = n_e[None, :] > 0
    nxt = jnp.min(jnp.where(used & (ids > e[:, None]), ids, N_EXPERTS), axis=1)
    nxt = jnp.where(nxt == N_EXPERTS, e, nxt)
    slot = jnp.sum(jnp.where(used & (ids < e[:, None]), 1, 0), axis=1) % 2
    return tuple(a.astype(I32) for a in (tile, e, lo, hi, nxt, slot))


def _run_table(n):
    tiles = n.shape[0]
    counts = jnp.sum(n, axis=0)
    goff = jnp.cumsum(counts) - counts
    dst0 = goff[None, :] + jnp.cumsum(n, axis=0) - n
    src0 = jnp.cumsum(n, axis=1) - n
    bits = jnp.arange(RUN_BITS, dtype=I32)[:, None, None]
    flag = (n[None] >> bits) & 1
    done = (n[None] >> (bits + 1)) << (bits + 1)
    entry = ((dst0[None] + done) << SRC_BITS) | (src0[None] + done)
    rank = jnp.cumsum(flag, axis=2) - flag
    slot = jnp.arange(N_EXPERTS, dtype=I32)
    sel = (flag[..., None] == 1) & (rank[..., None] == slot)
    packed = jnp.sum(jnp.where(sel, entry[..., None], 0), axis=2)
    m = jnp.sum(flag, axis=2).T
    tab = jnp.zeros((tiles, TAB_COLS), I32)
    tab = tab.at[:, :RUN_BITS].set(m)
    tab = tab.at[:, N_EXPERTS:N_EXPERTS * (RUN_BITS + 1)].set(packed.transpose(1, 0, 2).reshape(tiles, -1))
    return counts, tab


def _combine_kernel(tab_ref, h_ref, p_ref, ri_ref, rw_ref, ys_ref, gple_ref, wpg_ref, wpp_ref, gfin_ref, o_ref,
                    idx_smem, rows_scr, isem, rsem):
    i, n = pl.program_id(0), pl.num_programs(0)
    slot = i % 2

    def idx_copy(step, sl):
        return pltpu.make_async_copy(tab_ref.at[step], idx_smem.at[sl], isem.at[sl])

    def gather(sl):
        _run_copies(idx_smem, sl, lambda src, dst, rows_: pltpu.make_async_copy(
            _row_slice(ys_ref, dst, rows_), _row_slice(rows_scr.at[sl], src, rows_), rsem.at[sl]))

    @pl.when(i == 0)
    def _():
        idx_copy(0, 0).start()
        idx_copy(0, 0).wait()
        gather(0)

    @pl.when((i == 0) & (n >= 2))
    def _():
        idx_copy(1, 1).start()

    @pl.when(i + 1 < n)
    def _():
        idx_copy(i + 1, 1 - slot).wait()
        gather(1 - slot)

    @pl.when(i + 2 < n)
    def _():
        idx_copy(i + 2, slot).start()

    _wait_tile_rows(ys_ref, rows_scr.at[slot], rsem.at[slot])

    ri, rw = ri_ref[...], rw_ref[...]
    rows = _pair_rows_iota()
    wperm = jnp.where(rows == ri[2:3], rw[0:1], 0.0) + jnp.where(rows == ri[3:4], rw[1:2], 0.0)
    w_hi = wperm.astype(BF16)
    w_lo = (wperm - w_hi.astype(F32)).astype(BF16)
    s = _load_rows(rows_scr.at[slot], PAIR_ROWS).astype(BF16)
    h = h_ref[...] + _dot_tn(w_hi, s) + _dot_tn(w_lo, s)
    gate = jax.nn.sigmoid(_dot(_rms(h, gple_ref[...]).astype(BF16), wpg_ref[...]))
    h = h + _dot(p_ref[...].astype(BF16), wpp_ref[...]) * gate
    o_ref[...] = _rms(h, gfin_ref[...])


def _combine(tab, h, p, ri, rw, ys, gple, wpg, wpp, gfin):
    n = tab.shape[0]
    n_tok = n * TOK_TILE
    return pl.pallas_call(
        _combine_kernel,
        grid=(n,),
        in_specs=[
            pl.BlockSpec(memory_space=pl.ANY),
            pl.BlockSpec((TOK_TILE, D_MODEL), lambda i: (i, 0)),
            pl.BlockSpec((TOK_TILE, PLE_DIM), lambda i: (i, 0)),
            pl.BlockSpec((SUBLANES, TOK_TILE), lambda i: (0, i)),
            pl.BlockSpec((SUBLANES, TOK_TILE), lambda i: (0, i)),
            pl.BlockSpec(memory_space=pl.ANY),
            _const_spec(gple.shape), _const_spec(wpg.shape), _const_spec(wpp.shape), _const_spec(gfin.shape),
        ],
        out_specs=pl.BlockSpec((TOK_TILE, D_MODEL), lambda i: (i, 0)),
        out_shape=jax.ShapeDtypeStruct((n_tok, D_MODEL), F32),
        scratch_shapes=[
            pltpu.SMEM((2, TAB_COLS), I32),
            pltpu.VMEM((2, PAIR_ROWS * ROW_CHUNKS, LANES), U32),
            pltpu.SemaphoreType.DMA((2,)),
            pltpu.SemaphoreType.DMA((2,)),
        ],
        compiler_params=_params("arbitrary"),
        name="moe_combine",
    )(tab, h, p, ri, rw, ys, gple, wpg, wpp, gfin)


def _log_gamma():
    return jnp.log(1.0 - 2.0 ** (-5.0 - jnp.arange(RET_HEADS, dtype=F32)))


def _rope_tables(pos):
    inv = 1.0 / (ROPE_BASE ** (jnp.arange(0, RET_DK, 2, dtype=F32) / RET_DK))
    ang = pos.astype(F32)[:, None] * inv[None, :]
    cos, sin = jnp.cos(ang), jnp.sin(ang)
    return jnp.concatenate([cos, cos], axis=-1), jnp.concatenate([-sin, sin], axis=-1)


def _decay_tables(c):
    lg = _log_gamma()
    idx = jnp.arange(c, dtype=F32)
    diff = idx[:, None] - idx[None, :]
    dmask = jnp.where((diff >= 0.0)[None], jnp.exp(lg[:, None, None] * jnp.maximum(diff, 0.0)[None]), 0.0)
    q_dec = jnp.exp(lg[None, :] * (idx[:, None] + 1.0))
    k_dec = jnp.exp(lg[None, :] * (c - 1.0 - idx[:, None]))
    c_dec = jnp.exp(lg * c)
    lanes = lambda a: jnp.repeat(a, RET_DK, axis=1)
    cdec = jnp.zeros((SUBLANES, LANES), F32).at[:RET_HEADS].set(jnp.broadcast_to(c_dec[:, None], (RET_HEADS, LANES)))
    return lanes(q_dec), lanes(k_dec), dmask, cdec


def _router_params(w_rg, b_rg, w_re, b_re):
    wr = jnp.zeros((ROUTER_ROWS, D_MODEL), F32).at[:N_GROUPS].set(w_rg.T).at[SUBLANES:].set(w_re.T)
    br = jnp.full((ROUTER_ROWS,), NEG_BIG, F32).at[:N_GROUPS].set(b_rg).at[SUBLANES:].set(b_re.reshape(-1))
    return wr.astype(BF16), jnp.broadcast_to(br[:, None], (ROUTER_ROWS, LANES))


def _strict_upper(t):
    i = jnp.arange(t)
    return (i[:, None] < i[None, :]).astype(BF16)


def _pad_rows(a, rows):
    return jnp.zeros((rows,) + a.shape[1:], a.dtype).at[:a.shape[0]].set(a)


def kernel(x_prompt, x_sample, state_conv, state_ret, p_prompt, p_sample, g_mix, w_in, conv_w, ret_gn, w_o, g_ffn,
           w_router_group, b_router_group, w_router_expert, b_router_expert, w_gate, w_up, w_down, g_ple,
           w_ple_proj, w_ple_gate, g_final):
    bp, tp, _ = x_prompt.shape
    bs = x_sample.shape[0]
    assert x_sample.shape[1] == 1 and g_mix.shape[0] == 1, "one layer, one new token per sample request"
    assert bs <= TOK_TILE and tp % MIX_TILE == 0
    n_p = bp * tp

    tabs = _rope_tables(jnp.arange(tp, dtype=I32)) + _decay_tables(RET_CHUNK)
    wr, br = _router_params(w_router_group[0], b_router_group[0], w_router_expert[0], b_router_expert[0])
    win, wo, convw = w_in[0].astype(BF16), w_o[0].astype(BF16), _pad_rows(conv_w[0], SUBLANES)
    tri, ltri = _strict_upper(TOK_TILE), _strict_upper(N_EXPERTS).T

    h_p, ri_p, rw_p, n_p_tab, conv_p, ret_p = _mixer_prompt(
        x_prompt, tabs, (g_mix, win, convw, ret_gn, wo, g_ffn, wr, br, tri, ltri))
    h_s, ri_s, rw_s, n_s_tab, conv_s, ret_s = _mixer_sample(
        x_sample[:, 0], state_conv[0], state_ret[0], g_mix, win, convw, ret_gn, wo, g_ffn, wr, br, tri, ltri)
    h_p = h_p.reshape(n_p, D_MODEL)

    n_tab = jnp.concatenate([n_p_tab, n_s_tab], axis=0)[:, :, 0].astype(I32)
    counts, tab = _run_table(n_tab)
    ri = jnp.concatenate([ri_p, ri_s], axis=1)
    n_tiles = n_tab.shape[0]

    xs = _dispatch(tab, h_p, h_s, ri, g_ffn)
    ys = _expert_ffn(_work_items(counts, n_tiles * PAIR_ROWS), xs, w_gate[0], w_up[0], w_down[0])

    wpg, wpp = w_ple_gate[0].astype(BF16), w_ple_proj[0].astype(BF16)
    gfin = g_final[None, :]
    p_s = _pad_rows(p_sample[0].reshape(bs, PLE_DIM), TOK_TILE)
    y_p = _combine(tab[:n_tiles - 1], h_p, p_prompt[0].reshape(n_p, PLE_DIM), ri_p, rw_p, ys, g_ple, wpg, wpp, gfin)
    y_s = _combine(tab[n_tiles - 1:], h_s, p_s, ri_s, rw_s, ys, g_ple, wpg, wpp, gfin)
    return (y_p.reshape(bp, tp, D_MODEL), y_s[:bs].reshape(bs, 1, D_MODEL), conv_p[None], ret_p[None],
            conv_s[None], ret_s[None])
```

```python
import functools

import jax
import jax.numpy as jnp
from jax import lax
from jax.experimental import pallas as pl
from jax.experimental.pallas import tpu as pltpu

F32, BF16, I32, U32 = jnp.float32, jnp.bfloat16, jnp.int32, jnp.uint32

D_MODEL = 1024
CONV_DIM = 512
CONV_WIDTH = 3
RET_DIM = 512
RET_HEADS = 4
RET_DK = 128
RET_DV = 128
RET_CHUNK = 128
ROPE_BASE = 10000.0
IN_PROJ_DIM = 3 * CONV_DIM + 4 * RET_DIM
N_GROUPS = 4
EXPERTS_PER_GROUP = 8
N_EXPERTS = 32
D_FF = 512
PLE_DIM = 256
EPS = 1e-6
PAST_LEN = 16384

LANES = 128
SUBLANES = 8
ROW_CHUNKS = D_MODEL // (2 * LANES)
ROUTER_ROWS = SUBLANES + N_EXPERTS
VMEM_LIMIT = 56 * 1024 * 1024
NEG_BIG = -1e30

TOK_TILE = 256
MIX_TILE = 1024
PAIR_ROWS = 2 * TOK_TILE
FFN_TILE = 256
SAMPLE_BLOCK = 8
RUN_BITS = 9
SRC_BITS = 10
TAB_COLS = 384


def _rms(x, g):
    return x * lax.rsqrt(jnp.mean(x * x, axis=-1, keepdims=True) + EPS) * g


def _dot(a, b):
    return jnp.dot(a, b, preferred_element_type=F32)


def _dot_nt(a, b):
    return lax.dot_general(a, b, (((1,), (1,)), ((), ())), preferred_element_type=F32)


def _dot_tn(a, b):
    return lax.dot_general(a, b, (((0,), (0,)), ((), ())), preferred_element_type=F32)


def _rope(x, cos, sin_signed):
    return x * cos + pltpu.roll(x, RET_DK // 2, 1) * sin_signed


def _const_spec(shape):
    nd = len(shape)
    return pl.BlockSpec(shape, lambda *_: (0,) * nd)


def _params(*sem):
    return pltpu.CompilerParams(dimension_semantics=sem, vmem_limit_bytes=VMEM_LIMIT)


def _route(h, gffn, wr, br, tri, ltri):
    t = h.shape[0]
    xn = _rms(h, gffn).astype(BF16)
    lt = _dot_nt(wr, xn) + br[:, 0:1]
    row8 = lax.broadcasted_iota(I32, (SUBLANES, t), 0).astype(F32)
    gl = lt[0:SUBLANES]
    m = jnp.max(gl, axis=0, keepdims=True)
    g_top = 1.0 / jnp.sum(jnp.exp(gl - m), axis=0, keepdims=True)
    gidx = jnp.min(jnp.where(gl == m, row8, float(SUBLANES)), axis=0, keepdims=True)
    e_sel = jnp.where(gidx == 0.0, lt[8:16],
                      jnp.where(gidx == 1.0, lt[16:24], jnp.where(gidx == 2.0, lt[24:32], lt[32:40])))
    m1 = jnp.max(e_sel, axis=0, keepdims=True)
    i1 = jnp.min(jnp.where(e_sel == m1, row8, float(SUBLANES)), axis=0, keepdims=True)
    rest = jnp.where(row8 == i1, -jnp.inf, e_sel)
    m2 = jnp.max(rest, axis=0, keepdims=True)
    i2 = jnp.min(jnp.where(rest == m2, row8, float(SUBLANES)), axis=0, keepdims=True)
    d = jnp.exp(m2 - m1)
    w1 = g_top / (1.0 + d)
    w2 = g_top * d / (1.0 + d)
    e1 = gidx * float(EXPERTS_PER_GROUP) + i1
    e2 = gidx * float(EXPERTS_PER_GROUP) + i2
    row32 = lax.broadcasted_iota(I32, (N_EXPERTS, t), 0).astype(F32)
    a1 = jnp.where(row32 == e1, 1.0, 0.0)
    a2 = jnp.where(row32 == e2, 1.0, 0.0)
    a = a1 + a2
    n = jnp.broadcast_to(jnp.sum(a, axis=1, keepdims=True), (N_EXPERTS, LANES))
    start = _dot(ltri, n.astype(BF16))
    base = _dot(a.astype(BF16), tri) + start[:, 0:1]
    r1 = jnp.sum(a1 * base, axis=0, keepdims=True)
    r2 = jnp.sum(a2 * base, axis=0, keepdims=True)
    ri = jnp.where(row8 == 0.0, e1, jnp.where(row8 == 1.0, e2, jnp.where(row8 == 2.0, r1,
                                                                         jnp.where(row8 == 3.0, r2, 0.0))))
    rw = jnp.where(row8 == 0.0, w1, jnp.where(row8 == 1.0, w2, 0.0))
    return ri.astype(I32), rw, n


def _group_norm_gate(o, gate):
    mu = jnp.mean(o, axis=-1, keepdims=True)
    oc = o - mu
    return oc * lax.rsqrt(jnp.mean(oc * oc, axis=-1, keepdims=True) + EPS) * gate


def _mixer_prompt_kernel(x_ref, cos_ref, sin_ref, qdec_ref, kdec_ref, dmask_ref, cdec_ref, gmix_ref, win_ref,
                         convw_ref, retgn_ref, wo_ref, gffn_ref, wr_ref, br_ref, tri_ref, ltri_ref,
                         h_ref, ri_ref, rw_ref, n_ref, conv_ref, ret_ref,
                         s_scr, z_scr, mix_scr):
    j = pl.program_id(1)
    n_j = pl.num_programs(1)
    tt = x_ref.shape[1]

    @pl.when(j == 0)
    def _():
        s_scr[...] = jnp.zeros_like(s_scr)
        z_scr[0:SUBLANES, :] = jnp.zeros((SUBLANES, CONV_DIM), F32)

    cw = convw_ref[...]
    retgn = retgn_ref[...]
    q0 = 3 * CONV_DIM

    def project(p0):
        x = x_ref[0, p0:p0 + TOK_TILE, :]
        proj = _dot(_rms(x, gmix_ref[...]).astype(BF16), win_ref[...])
        gb = proj[:, 0:CONV_DIM]
        z = proj[:, CONV_DIM:2 * CONV_DIM] * proj[:, 2 * CONV_DIM:3 * CONV_DIM]
        z0 = SUBLANES + p0
        z_scr[z0:z0 + TOK_TILE, :] = z
        yc = cw[0:1] * z_scr[z0 - 2:z0 - 2 + TOK_TILE, :] + cw[1:2] * z_scr[z0 - 1:z0 - 1 + TOK_TILE, :] + cw[2:3] * z
        mix_scr[p0:p0 + TOK_TILE, 0:CONV_DIM] = (gb * yc).astype(BF16)
        return x, proj

    def mix(p0, x, proj):
        cos = cos_ref[p0:p0 + TOK_TILE, :]
        sin = sin_ref[p0:p0 + TOK_TILE, :]
        for hh in range(RET_HEADS):
            l0 = hh * RET_DK
            qr = _rope(proj[:, q0 + l0:q0 + l0 + RET_DK], cos, sin)
            kr = _rope(proj[:, q0 + RET_DIM + l0:q0 + RET_DIM + l0 + RET_DK], cos, sin) * (RET_DK ** -0.5)
            v = proj[:, q0 + 2 * RET_DIM + l0:q0 + 2 * RET_DIM + l0 + RET_DV].astype(BF16)
            g = proj[:, q0 + 3 * RET_DIM + l0:q0 + 3 * RET_DIM + l0 + RET_DV]
            s_old = s_scr[hh]
            scores = _dot_nt(qr.astype(BF16), kr.astype(BF16)) * dmask_ref[hh]
            lhs = jnp.concatenate([scores.astype(BF16), (qr * qdec_ref[:, l0:l0 + RET_DK]).astype(BF16)], axis=1)
            o = _dot(lhs, jnp.concatenate([v, s_old.astype(BF16)], axis=0))
            kd = (kr * kdec_ref[:, l0:l0 + RET_DK]).T.astype(BF16)
            s_scr[hh] = s_old * cdec_ref[hh:hh + 1, :] + _dot(kd, v)
            gate = retgn[:, l0:l0 + RET_DV] * (g * jax.nn.sigmoid(g))
            mix_scr[p0:p0 + TOK_TILE, CONV_DIM + l0:CONV_DIM + l0 + RET_DV] = _group_norm_gate(o, gate).astype(BF16)
        h = x + _dot(mix_scr[p0:p0 + TOK_TILE, :], wo_ref[...])
        h_ref[0, p0:p0 + TOK_TILE, :] = h
        return h

    def route(part, h):
        p0 = part * TOK_TILE
        ri, rw, n = _route(h, gffn_ref[...], wr_ref[...], br_ref[...], tri_ref[...], ltri_ref[...])
        ri_ref[:, p0:p0 + TOK_TILE] = ri
        rw_ref[:, p0:p0 + TOK_TILE] = rw
        n_ref[part] = n

    parts = tt // TOK_TILE
    nxt = project(0)
    for part in range(parts):
        h = mix(part * TOK_TILE, *nxt)
        if part + 1 < parts:
            nxt = project((part + 1) * TOK_TILE)
        route(part, h)

    @pl.when(j == n_j - 1)
    def _():
        conv_ref[0] = z_scr[SUBLANES + tt - 2:SUBLANES + tt, :]
        ret_ref[0] = s_scr[...]

    z_scr[0:SUBLANES, :] = z_scr[tt:tt + SUBLANES, :]


def _mixer_prompt(x, tabs, wts):
    bsz, t, _ = x.shape
    tt = MIX_TILE
    n_j = t // tt
    parts = tt // TOK_TILE
    n_tok = bsz * t
    args = (x,) + tuple(tabs) + tuple(wts)
    in_specs = [
        pl.BlockSpec((1, tt, D_MODEL), lambda b, j: (b, j, 0)),
        pl.BlockSpec((tt, LANES), lambda b, j: (j, 0)),
        pl.BlockSpec((tt, LANES), lambda b, j: (j, 0)),
    ] + [_const_spec(a.shape) for a in args[3:]]
    out_shape = (
        jax.ShapeDtypeStruct((bsz, t, D_MODEL), F32),
        jax.ShapeDtypeStruct((SUBLANES, n_tok), I32),
        jax.ShapeDtypeStruct((SUBLANES, n_tok), F32),
        jax.ShapeDtypeStruct((bsz * n_j * parts, N_EXPERTS, LANES), F32),
        jax.ShapeDtypeStruct((bsz, CONV_WIDTH - 1, CONV_DIM), F32),
        jax.ShapeDtypeStruct((bsz, RET_HEADS, RET_DK, RET_DV), F32),
    )
    out_specs = (
        pl.BlockSpec((1, tt, D_MODEL), lambda b, j: (b, j, 0)),
        pl.BlockSpec((SUBLANES, tt), lambda b, j: (0, b * n_j + j)),
        pl.BlockSpec((SUBLANES, tt), lambda b, j: (0, b * n_j + j)),
        pl.BlockSpec((parts, N_EXPERTS, LANES), lambda b, j: (b * n_j + j, 0, 0)),
        pl.BlockSpec((1, CONV_WIDTH - 1, CONV_DIM), lambda b, j: (b, 0, 0)),
        pl.BlockSpec((1, RET_HEADS, RET_DK, RET_DV), lambda b, j: (b, 0, 0, 0)),
    )
    return pl.pallas_call(
        _mixer_prompt_kernel,
        grid=(bsz, n_j),
        in_specs=in_specs,
        out_specs=out_specs,
        out_shape=out_shape,
        scratch_shapes=[
            pltpu.VMEM((RET_HEADS, RET_DK, RET_DV), F32),
            pltpu.VMEM((tt + SUBLANES, CONV_DIM), F32),
            pltpu.VMEM((tt, D_MODEL), BF16),
        ],
        compiler_params=_params("arbitrary", "arbitrary"),
        name="mixer_prompt",
    )(*args)


def _sample_proj_kernel(x_ref, sc_ref, cos_ref, sin_ref, gmix_ref, win_ref, convw_ref, retgn_ref,
                        yc_ref, q_ref, k_ref, v_ref, gate_ref, conv_ref):
    x = x_ref[...]
    u = _rms(x, gmix_ref[...]).astype(BF16)
    proj = _dot(u, win_ref[...])
    gb = proj[:, 0:CONV_DIM]
    z = proj[:, CONV_DIM:2 * CONV_DIM] * proj[:, 2 * CONV_DIM:3 * CONV_DIM]
    cw = convw_ref[...]
    buf0, buf1 = sc_ref[:, 0:CONV_DIM], sc_ref[:, CONV_DIM:2 * CONV_DIM]
    yc_ref[...] = gb * (cw[0:1] * buf0 + cw[1:2] * buf1 + cw[2:3] * z)
    conv_ref[:, 0:CONV_DIM] = buf1
    conv_ref[:, CONV_DIM:2 * CONV_DIM] = z
    q0 = 3 * CONV_DIM
    cos, sin = cos_ref[0:1, :], sin_ref[0:1, :]
    retgn = retgn_ref[...]
    for hh in range(RET_HEADS):
        l0 = hh * RET_DK
        q_ref[:, l0:l0 + RET_DK] = _rope(proj[:, q0 + l0:q0 + l0 + RET_DK], cos, sin)
        k_ref[:, l0:l0 + RET_DK] = _rope(proj[:, q0 + RET_DIM + l0:q0 + RET_DIM + l0 + RET_DK], cos, sin) * (RET_DK ** -0.5)
    v_ref[...] = proj[:, q0 + 2 * RET_DIM:q0 + 3 * RET_DIM]
    g = proj[:, q0 + 3 * RET_DIM:q0 + 4 * RET_DIM]
    gate_ref[...] = retgn * (g * jax.nn.sigmoid(g))


def _sample_proj(x, sc, cos, sin, gmix, win, convw, retgn):
    n = x.shape[0]
    outs = (
        jax.ShapeDtypeStruct((n, CONV_DIM), F32), jax.ShapeDtypeStruct((n, RET_DIM), F32),
        jax.ShapeDtypeStruct((n, RET_DIM), F32), jax.ShapeDtypeStruct((n, RET_DIM), F32),
        jax.ShapeDtypeStruct((n, RET_DIM), F32), jax.ShapeDtypeStruct((n, 2 * CONV_DIM), F32),
    )
    args = (x, sc, cos, sin, gmix, win, convw, retgn)
    return pl.pallas_call(
        _sample_proj_kernel,
        grid=(1,),
        in_specs=[_const_spec(a.shape) for a in args],
        out_specs=tuple(_const_spec(o.shape) for o in outs),
        out_shape=outs,
        compiler_params=_params("arbitrary"),
        name="sample_proj",
    )(*args)


def _sample_state_kernel(s_ref, qt_ref, kt_ref, v_ref, dec_ref, snew_ref, o_ref):
    for r in range(s_ref.shape[0]):
        for hh in range(RET_HEADS):
            l0 = hh * RET_DV
            s = s_ref[r, hh]
            qc = qt_ref[0, hh, :, r:r + 1]
            kc = kt_ref[0, hh, :, r:r + 1]
            vr = v_ref[r:r + 1, l0:l0 + RET_DV]
            qdec = dec_ref[hh:hh + 1, :]
            cdec = dec_ref[RET_HEADS + hh:RET_HEADS + hh + 1, :]
            qk = jnp.sum(qc * kc, axis=0, keepdims=True)
            o_ref[r:r + 1, l0:l0 + RET_DV] = qk * vr + jnp.sum((qc * qdec[:, 0:1]) * s, axis=0, keepdims=True)
            snew_ref[r, hh] = s * cdec + kc * vr


def _sample_state(s, qt, kt, v, dec):
    n = s.shape[0]
    nb = SAMPLE_BLOCK
    return pl.pallas_call(
        _sample_state_kernel,
        grid=(n // nb,),
        in_specs=[
            pl.BlockSpec((nb, RET_HEADS, RET_DK, RET_DV), lambda i: (i, 0, 0, 0)),
            pl.BlockSpec((1, RET_HEADS, RET_DK, nb), lambda i: (i, 0, 0, 0)),
            pl.BlockSpec((1, RET_HEADS, RET_DK, nb), lambda i: (i, 0, 0, 0)),
            pl.BlockSpec((nb, RET_DIM), lambda i: (i, 0)),
            _const_spec(dec.shape),
        ],
        out_specs=(
            pl.BlockSpec((nb, RET_HEADS, RET_DK, RET_DV), lambda i: (i, 0, 0, 0)),
            pl.BlockSpec((nb, RET_DIM), lambda i: (i, 0)),
        ),
        out_shape=(jax.ShapeDtypeStruct(s.shape, F32), jax.ShapeDtypeStruct((n, RET_DIM), F32)),
        compiler_params=_params("arbitrary"),
        name="sample_state",
    )(s, qt, kt, v, dec)


def _sample_out_kernel(x_ref, yc_ref, o_ref, gate_ref, wo_ref, gffn_ref, wr_ref, br_ref, tri_ref, ltri_ref,
                       h_ref, ri_ref, rw_ref, n_ref):
    n = x_ref.shape[0]
    parts = [yc_ref[...].astype(BF16)]
    for hh in range(RET_HEADS):
        l0 = hh * RET_DV
        parts.append(_group_norm_gate(o_ref[:, l0:l0 + RET_DV], gate_ref[:, l0:l0 + RET_DV]).astype(BF16))
    h_ref[0:n, :] = x_ref[...] + _dot(jnp.concatenate(parts, axis=-1), wo_ref[...])
    if n < TOK_TILE:
        h_ref[n:TOK_TILE, :] = jnp.zeros((TOK_TILE - n, D_MODEL), F32)
    ri, rw, cnt = _route(h_ref[...], gffn_ref[...], wr_ref[...], br_ref[...], tri_ref[...], ltri_ref[...])
    ri_ref[...] = ri
    rw_ref[...] = rw
    n_ref[0] = cnt


def _sample_out(x, yc, o, gate, wo, gffn, wr, br, tri, ltri):
    outs = (
        jax.ShapeDtypeStruct((TOK_TILE, D_MODEL), F32), jax.ShapeDtypeStruct((SUBLANES, TOK_TILE), I32),
        jax.ShapeDtypeStruct((SUBLANES, TOK_TILE), F32), jax.ShapeDtypeStruct((1, N_EXPERTS, LANES), F32),
    )
    args = (x, yc, o, gate, wo, gffn, wr, br, tri, ltri)
    return pl.pallas_call(
        _sample_out_kernel,
        grid=(1,),
        in_specs=[_const_spec(a.shape) for a in args],
        out_specs=tuple(_const_spec(o_.shape) for o_ in outs),
        out_shape=outs,
        compiler_params=_params("arbitrary"),
        name="sample_out",
    )(*args)


def _mixer_sample(x, state_conv, state_ret, gmix, win, convw, retgn, wo, gffn, wr, br, tri, ltri):
    n = x.shape[0]
    lg = _log_gamma()
    cos, sin = _rope_tables(jnp.full((SUBLANES,), PAST_LEN, I32))
    yc, q, k, v, gate, conv_new = _sample_proj(x, state_conv.reshape(n, 2 * CONV_DIM), cos, sin, gmix, win, convw, retgn)
    nb = SAMPLE_BLOCK
    cols = lambda a: a.reshape(n // nb, nb, RET_HEADS, RET_DK).transpose(0, 2, 3, 1)
    step = jnp.exp(lg[:, None] * 1.0)
    dec = jnp.broadcast_to(jnp.concatenate([step, step], axis=0), (2 * RET_HEADS, LANES))
    s_new, o = _sample_state(state_ret, cols(q), cols(k), v, dec)
    h, ri, rw, cnt = _sample_out(x, yc, o, gate, wo, gffn, wr, br, tri, ltri)
    return h, ri, rw, cnt, conv_new.reshape(n, 2, CONV_DIM), s_new


def _load_rows(ref, n_rows):
    words = [ref[pl.ds(c, n_rows, stride=ROW_CHUNKS), :] for c in range(ROW_CHUNKS)]
    half = lambda i: [pltpu.unpack_elementwise(w, index=i, packed_dtype=BF16, unpacked_dtype=F32) for w in words]
    return jnp.concatenate(half(0) + half(1), axis=-1)


def _store_rows(ref, val):
    half = D_MODEL // 2
    for c in range(ROW_CHUNKS):
        pair = [val[:, c * LANES:(c + 1) * LANES], val[:, half + c * LANES:half + (c + 1) * LANES]]
        ref[pl.ds(c, val.shape[0], stride=ROW_CHUNKS), :] = pltpu.pack_elementwise(pair, packed_dtype=BF16)


def _row_slice(ref, row, rows):
    return ref.at[pl.ds(pl.multiple_of(row * ROW_CHUNKS, ROW_CHUNKS), rows * ROW_CHUNKS)]


def _run_copies(idx_smem, slot, make_copy):
    for b in range(RUN_BITS):
        def body(j, carry, b=b):
            entry = idx_smem[slot, N_EXPERTS + N_EXPERTS * b + j]
            make_copy(entry & ((1 << SRC_BITS) - 1), entry >> SRC_BITS, 1 << b).start()
            return carry

        lax.fori_loop(0, idx_smem[slot, b], body, 0)


def _wait_tile_rows(hbm_ref, vmem_ref, sem):
    pltpu.make_async_copy(_row_slice(hbm_ref, 0, PAIR_ROWS), vmem_ref, sem).wait()


def _pair_rows_iota():
    return lax.broadcasted_iota(I32, (PAIR_ROWS, TOK_TILE), 0)


def _dispatch_kernel(n_p, tab_ref, hp_ref, hs_ref, ri_ref, gffn_ref, xs_ref, idx_smem, y_scr, isem, rsem):
    i, n = pl.program_id(0), pl.num_programs(0)
    slot = i % 2

    def idx_copy(step, sl):
        return pltpu.make_async_copy(tab_ref.at[step], idx_smem.at[sl], isem.at[sl])

    @pl.when(i == 0)
    def _():
        idx_copy(0, 0).start()

    @pl.when(i + 1 < n)
    def _():
        idx_copy(i + 1, 1 - slot).start()

    @pl.when(i >= 2)
    def _():
        _wait_tile_rows(xs_ref, y_scr.at[slot], rsem.at[slot])

    h = jnp.where(i < n_p, hp_ref[...], hs_ref[...])
    xn = _rms(h, gffn_ref[...]).astype(BF16)
    ri = ri_ref[...]
    rows = _pair_rows_iota()
    perm = jnp.where((rows == ri[2:3]) | (rows == ri[3:4]), 1.0, 0.0).astype(BF16)
    y = _dot(perm, xn)
    _store_rows(y_scr.at[slot], y)

    idx_copy(i, slot).wait()
    _run_copies(idx_smem, slot, lambda src, dst, rows_: pltpu.make_async_copy(
        _row_slice(y_scr.at[slot], src, rows_), _row_slice(xs_ref, dst, rows_), rsem.at[slot]))

    @pl.when(i == n - 1)
    def _():
        _wait_tile_rows(xs_ref, y_scr.at[slot], rsem.at[slot])

    @pl.when((i == n - 1) & (n >= 2))
    def _():
        _wait_tile_rows(xs_ref, y_scr.at[1 - slot], rsem.at[1 - slot])


def _dispatch(tab, h_p, h_s, ri, gffn):
    n_p = h_p.shape[0] // TOK_TILE
    n = tab.shape[0]
    return pl.pallas_call(
        functools.partial(_dispatch_kernel, n_p),
        grid=(n,),
        in_specs=[
            pl.BlockSpec(memory_space=pl.ANY),
            pl.BlockSpec((TOK_TILE, D_MODEL), lambda i: (jnp.minimum(i, n_p - 1), 0)),
            pl.BlockSpec((TOK_TILE, D_MODEL), lambda i: (jnp.maximum(i - n_p, 0), 0)),
            pl.BlockSpec((SUBLANES, TOK_TILE), lambda i: (0, i)),
            _const_spec(gffn.shape),
        ],
        out_specs=pl.BlockSpec(memory_space=pl.ANY),
        out_shape=jax.ShapeDtypeStruct((n * PAIR_ROWS * ROW_CHUNKS, LANES), U32),
        scratch_shapes=[
            pltpu.SMEM((2, TAB_COLS), I32),
            pltpu.VMEM((2, PAIR_ROWS * ROW_CHUNKS, LANES), U32),
            pltpu.SemaphoreType.DMA((2,)),
            pltpu.SemaphoreType.DMA((2,)),
        ],
        compiler_params=_params("arbitrary"),
        name="moe_dispatch",
    )(tab, h_p, h_s, ri, gffn)


def _ffn_kernel(tile_ref, exp_ref, lo_ref, hi_ref, nxt_ref, slot_ref, x_ref, wg_ref, wu_ref, wd_ref, o_ref,
                wg_buf, wu_buf, wd_buf, wgu_scr, wd_scr, y_scr, wsem):
    w = pl.program_id(0)
    prev = jnp.maximum(w - 1, 0)
    lo, hi = lo_ref[w], hi_ref[w]
    e, slot = exp_ref[w], slot_ref[w]
    first = (w == 0) | (tile_ref[w] != tile_ref[prev])

    def weight_copies(expert, sl):
        return [pltpu.make_async_copy(src.at[expert], buf.at[sl], wsem.at[sl])
                for src, buf in ((wg_ref, wg_buf), (wu_ref, wu_buf), (wd_ref, wd_buf))]

    @pl.when(w == 0)
    def _():
        for cp in weight_copies(e, slot):
            cp.start()

    @pl.when((w == 0) | (e != exp_ref[prev]))
    def _():
        for cp in weight_copies(e, slot):
            cp.wait()

        @pl.when(nxt_ref[w] != e)
        def _():
            for cp in weight_copies(nxt_ref[w], 1 - slot):
                cp.start()

        wgu_scr[:, 0:D_FF] = wg_buf[slot].astype(BF16)
        wgu_scr[:, D_FF:2 * D_FF] = wu_buf[slot].astype(BF16)
        wd_scr[...] = wd_buf[slot].astype(BF16)

    @pl.when(w == 0)
    def _():
        y_scr[...] = jnp.zeros_like(y_scr)

    @pl.when(hi > lo)
    def _():
        gu = _dot(_load_rows(x_ref, FFN_TILE).astype(BF16), wgu_scr[...])
        g, u = gu[:, 0:D_FF], gu[:, D_FF:2 * D_FF]
        y = _dot((g * jax.nn.sigmoid(g) * u).astype(BF16), wd_scr[...])
        row = lax.broadcasted_iota(I32, (FFN_TILE, 1), 0)
        y = jnp.where(((row >= lo) & (row < hi)) | first, y, y_scr[...])
        y_scr[...] = y
        _store_rows(o_ref, y)


def _expert_ffn(items, xs, w_gate, w_up, w_down):
    n_items = items[0].shape[0]
    row_spec = pl.BlockSpec((FFN_TILE * ROW_CHUNKS, LANES), lambda w, t, *_: (t[w], 0))
    grid_spec = pltpu.PrefetchScalarGridSpec(
        num_scalar_prefetch=len(items),
        grid=(n_items,),
        in_specs=[row_spec] + [pl.BlockSpec(memory_space=pl.ANY)] * 3,
        out_specs=row_spec,
        scratch_shapes=[
            pltpu.VMEM((2, D_MODEL, D_FF), F32), pltpu.VMEM((2, D_MODEL, D_FF), F32),
            pltpu.VMEM((2, D_FF, D_MODEL), F32),
            pltpu.VMEM((D_MODEL, 2 * D_FF), BF16), pltpu.VMEM((D_FF, D_MODEL), BF16),
            pltpu.VMEM((FFN_TILE, D_MODEL), F32),
            pltpu.SemaphoreType.DMA((2,)),
        ],
    )
    return pl.pallas_call(
        _ffn_kernel,
        grid_spec=grid_spec,
        out_shape=jax.ShapeDtypeStruct(xs.shape, U32),
        compiler_params=_params("arbitrary"),
        name="moe_ffn",
    )(*items, xs, w_gate, w_up, w_down)


def _work_items(counts, n_rows):
    n_tiles = n_rows // FFN_TILE
    n_items = n_tiles + N_EXPERTS - 1
    off = jnp.cumsum(counts) - counts
    first_tile = off // FFN_TILE
    last_tile = jnp.maximum(off + counts - 1, off) // FFN_TILE
    n_e = jnp.where(counts > 0, last_tile - first_tile + 1, 0)
    start = jnp.cumsum(n_e) - n_e
    total = jnp.sum(n_e)
    w = jnp.minimum(jnp.arange(n_items, dtype=I32), total - 1)
    ids = jnp.arange(N_EXPERTS, dtype=I32)[None, :]
    e = jnp.max(jnp.where((start[None, :] <= w[:, None]) & (n_e[None, :] > 0), ids, 0), axis=1)
    pick = lambda a: jnp.sum(jnp.where(ids == e[:, None], a[None, :], 0), axis=1)
    tile = pick(first_tile) + (w - pick(start))
    lo = jnp.clip(pick(off) - tile * FFN_TILE, 0, FFN_TILE)
    hi = jnp.clip(pick(off + counts) - tile * FFN_TILE, 0, FFN_TILE)
    hi = jnp.where(jnp.arange(n_items) < total, hi, lo)
    used = n_e[None, :] > 0
    nxt = jnp.min(jnp.where(used & (ids > e[:, None]), ids, N_EXPERTS), axis=1)
    nxt = jnp.where(nxt == N_EXPERTS, e, nxt)
    slot = jnp.sum(jnp.where(used & (ids < e[:, None]), 1, 0), axis=1) % 2
    return tuple(a.astype(I32) for a in (tile, e, lo, hi, nxt, slot))


def _run_table(n):
    tiles = n.shape[0]
    counts = jnp.sum(n, axis=0)
    goff = jnp.cumsum(counts) - counts
    dst0 = goff[None, :] + jnp.cumsum(n, axis=0) - n
    src0 = jnp.cumsum(n, axis=1) - n
    bits = jnp.arange(RUN_BITS, dtype=I32)[:, None, None]
    flag = (n[None] >> bits) & 1
    done = (n[None] >> (bits + 1)) << (bits + 1)
    entry = ((dst0[None] + done) << SRC_BITS) | (src0[None] + done)
    rank = jnp.cumsum(flag, axis=2) - flag
    slot = jnp.arange(N_EXPERTS, dtype=I32)
    sel = (flag[..., None] == 1) & (rank[..., None] == slot)
    packed = jnp.sum(jnp.where(sel, entry[..., None], 0), axis=2)
    m = jnp.sum(flag, axis=2).T
    tab = jnp.zeros((tiles, TAB_COLS), I32)
    tab = tab.at[:, :RUN_BITS].set(m)
    tab = tab.at[:, N_EXPERTS:N_EXPERTS * (RUN_BITS + 1)].set(packed.transpose(1, 0, 2).reshape(tiles, -1))
    return counts, tab


def _combine_kernel(tab_ref, h_ref, p_ref, ri_ref, rw_ref, ys_ref, gple_ref, wpg_ref, wpp_ref, gfin_ref, o_ref,
                    idx_smem, rows_scr, isem, rsem):
    i, n = pl.program_id(0), pl.num_programs(0)
    slot = i % 2

    def idx_copy(step, sl):
        return pltpu.make_async_copy(tab_ref.at[step], idx_smem.at[sl], isem.at[sl])

    def gather(sl):
        _run_copies(idx_smem, sl, lambda src, dst, rows_: pltpu.make_async_copy(
            _row_slice(ys_ref, dst, rows_), _row_slice(rows_scr.at[sl], src, rows_), rsem.at[sl]))

    @pl.when(i == 0)
    def _():
        idx_copy(0, 0).start()
        idx_copy(0, 0).wait()
        gather(0)

    @pl.when((i == 0) & (n >= 2))
    def _():
        idx_copy(1, 1).start()

    @pl.when(i + 1 < n)
    def _():
        idx_copy(i + 1, 1 - slot).wait()
        gather(1 - slot)

    @pl.when(i + 2 < n)
    def _():
        idx_copy(i + 2, slot).start()

    _wait_tile_rows(ys_ref, rows_scr.at[slot], rsem.at[slot])

    ri, rw = ri_ref[...], rw_ref[...]
    rows = _pair_rows_iota()
    wperm = jnp.where(rows == ri[2:3], rw[0:1], 0.0) + jnp.where(rows == ri[3:4], rw[1:2], 0.0)
    w_hi = wperm.astype(BF16)
    w_lo = (wperm - w_hi.astype(F32)).astype(BF16)
    s = _load_rows(rows_scr.at[slot], PAIR_ROWS).astype(BF16)
    h = h_ref[...] + _dot_tn(w_hi, s) + _dot_tn(w_lo, s)
    gate = jax.nn.sigmoid(_dot(_rms(h, gple_ref[...]).astype(BF16), wpg_ref[...]))
    h = h + _dot(p_ref[...].astype(BF16), wpp_ref[...]) * gate
    o_ref[...] = _rms(h, gfin_ref[...])


def _combine(tab, h, p, ri, rw, ys, gple, wpg, wpp, gfin):
    n = tab.shape[0]
    n_tok = n * TOK_TILE
    return pl.pallas_call(
        _combine_kernel,
        grid=(n,),
        in_specs=[
            pl.BlockSpec(memory_space=pl.ANY),
            pl.BlockSpec((TOK_TILE, D_MODEL), lambda i: (i, 0)),
            pl.BlockSpec((TOK_TILE, PLE_DIM), lambda i: (i, 0)),
            pl.BlockSpec((SUBLANES, TOK_TILE), lambda i: (0, i)),
            pl.BlockSpec((SUBLANES, TOK_TILE), lambda i: (0, i)),
            pl.BlockSpec(memory_space=pl.ANY),
            _const_spec(gple.shape), _const_spec(wpg.shape), _const_spec(wpp.shape), _const_spec(gfin.shape),
        ],
        out_specs=pl.BlockSpec((TOK_TILE, D_MODEL), lambda i: (i, 0)),
        out_shape=jax.ShapeDtypeStruct((n_tok, D_MODEL), F32),
        scratch_shapes=[
            pltpu.SMEM((2, TAB_COLS), I32),
            pltpu.VMEM((2, PAIR_ROWS * ROW_CHUNKS, LANES), U32),
            pltpu.SemaphoreType.DMA((2,)),
            pltpu.SemaphoreType.DMA((2,)),
        ],
        compiler_params=_params("arbitrary"),
        name="moe_combine",
    )(tab, h, p, ri, rw, ys, gple, wpg, wpp, gfin)


def _log_gamma():
    return jnp.log(1.0 - 2.0 ** (-5.0 - jnp.arange(RET_HEADS, dtype=F32)))


def _rope_tables(pos):
    inv = 1.0 / (ROPE_BASE ** (jnp.arange(0, RET_DK, 2, dtype=F32) / RET_DK))
    ang = pos.astype(F32)[:, None] * inv[None, :]
    cos, sin = jnp.cos(ang), jnp.sin(ang)
    return jnp.concatenate([cos, cos], axis=-1), jnp.concatenate([-sin, sin], axis=-1)


def _decay_tables(c):
    lg = _log_gamma()
    idx = jnp.arange(c, dtype=F32)
    diff = idx[:, None] - idx[None, :]
    dmask = jnp.where((diff >= 0.0)[None], jnp.exp(lg[:, None, None] * jnp.maximum(diff, 0.0)[None]), 0.0)
    q_dec = jnp.exp(lg[None, :] * (idx[:, None] + 1.0))
    k_dec = jnp.exp(lg[None, :] * (c - 1.0 - idx[:, None]))
    c_dec = jnp.exp(lg * c)
    lanes = lambda a: jnp.repeat(a, RET_DK, axis=1)
    cdec = jnp.zeros((SUBLANES, LANES), F32).at[:RET_HEADS].set(jnp.broadcast_to(c_dec[:, None], (RET_HEADS, LANES)))
    return lanes(q_dec), lanes(k_dec), dmask, cdec


def _router_params(w_rg, b_rg, w_re, b_re):
    wr = jnp.zeros((ROUTER_ROWS, D_MODEL), F32).at[:N_GROUPS].set(w_rg.T).at[SUBLANES:].set(w_re.T)
    br = jnp.full((ROUTER_ROWS,), NEG_BIG, F32).at[:N_GROUPS].set(b_rg).at[SUBLANES:].set(b_re.reshape(-1))
    return wr.astype(BF16), jnp.broadcast_to(br[:, None], (ROUTER_ROWS, LANES))


def _strict_upper(t):
    i = jnp.arange(t)
    return (i[:, None] < i[None, :]).astype(BF16)


def _pad_rows(a, rows):
    return jnp.zeros((rows,) + a.shape[1:], a.dtype).at[:a.shape[0]].set(a)


def kernel(x_prompt, x_sample, state_conv, state_ret, p_prompt, p_sample, g_mix, w_in, conv_w, ret_gn, w_o, g_ffn,
           w_router_group, b_router_group, w_router_expert, b_router_expert, w_gate, w_up, w_down, g_ple,
           w_ple_proj, w_ple_gate, g_final):
    bp, tp, _ = x_prompt.shape
    bs = x_sample.shape[0]
    assert x_sample.shape[1] == 1 and g_mix.shape[0] == 1, "one layer, one new token per sample request"
    assert bs <= TOK_TILE and tp % MIX_TILE == 0
    n_p = bp * tp

    tabs = _rope_tables(jnp.arange(tp, dtype=I32)) + _decay_tables(TOK_TILE)
    wr, br = _router_params(w_router_group[0], b_router_group[0], w_router_expert[0], b_router_expert[0])
    win, wo, convw = w_in[0].astype(BF16), w_o[0].astype(BF16), _pad_rows(conv_w[0], SUBLANES)
    tri, ltri = _strict_upper(TOK_TILE), _strict_upper(N_EXPERTS).T

    h_p, ri_p, rw_p, n_p_tab, conv_p, ret_p = _mixer_prompt(
        x_prompt, tabs, (g_mix, win, convw, ret_gn, wo, g_ffn, wr, br, tri, ltri))
    h_s, ri_s, rw_s, n_s_tab, conv_s, ret_s = _mixer_sample(
        x_sample[:, 0], state_conv[0], state_ret[0], g_mix, win, convw, ret_gn, wo, g_ffn, wr, br, tri, ltri)
    h_p = h_p.reshape(n_p, D_MODEL)

    n_tab = jnp.concatenate([n_p_tab, n_s_tab], axis=0)[:, :, 0].astype(I32)
    counts, tab = _run_table(n_tab)
    ri = jnp.concatenate([ri_p, ri_s], axis=1)
    n_tiles = n_tab.shape[0]

    xs = _dispatch(tab, h_p, h_s, ri, g_ffn)
    ys = _expert_ffn(_work_items(counts, n_tiles * PAIR_ROWS), xs, w_gate[0], w_up[0], w_down[0])

    wpg, wpp = w_ple_gate[0].astype(BF16), w_ple_proj[0].astype(BF16)
    gfin = g_final[None, :]
    p_s = _pad_rows(p_sample[0].reshape(bs, PLE_DIM), TOK_TILE)
    y_p = _combine(tab[:n_tiles - 1], h_p, p_prompt[0].reshape(n_p, PLE_DIM), ri_p, rw_p, ys, g_ple, wpg, wpp, gfin)
    y_s = _combine(tab[n_tiles - 1:], h_s, p_s, ri_s, rw_s, ys, g_ple, wpg, wpp, gfin)
    return (y_p.reshape(bp, tp, D_MODEL), y_s[:bs].reshape(bs, 1, D_MODEL), conv_p[None], ret_p[None],
            conv_s[None], ret_s[None])
```

```python
import functools

import jax
import jax.numpy as jnp
from jax import lax
from jax.experimental import pallas as pl
from jax.experimental.pallas import tpu as pltpu

F32, BF16, I32, U32 = jnp.float32, jnp.bfloat16, jnp.int32, jnp.uint32

D_MODEL = 1024
CONV_DIM = 512
CONV_WIDTH = 3
RET_DIM = 512
RET_HEADS = 4
RET_DK = 128
RET_DV = 128
RET_CHUNK = 128
ROPE_BASE = 10000.0
IN_PROJ_DIM = 3 * CONV_DIM + 4 * RET_DIM
N_GROUPS = 4
EXPERTS_PER_GROUP = 8
N_EXPERTS = 32
D_FF = 512
PLE_DIM = 256
EPS = 1e-6
PAST_LEN = 16384

LANES = 128
SUBLANES = 8
ROW_CHUNKS = D_MODEL // (2 * LANES)
ROUTER_ROWS = SUBLANES + N_EXPERTS
VMEM_LIMIT = 56 * 1024 * 1024
NEG_BIG = -1e30

TOK_TILE = 256
MIX_TILE = 1024
PAIR_ROWS = 2 * TOK_TILE
FFN_TILE = 256
SAMPLE_BLOCK = 8
RUN_BITS = 9
SRC_BITS = 10
TAB_COLS = 384


def _rms(x, g):
    return x * lax.rsqrt(jnp.mean(x * x, axis=-1, keepdims=True) + EPS) * g


def _dot(a, b):
    return jnp.dot(a, b, preferred_element_type=F32)


def _dot_nt(a, b):
    return lax.dot_general(a, b, (((1,), (1,)), ((), ())), preferred_element_type=F32)


def _dot_tn(a, b):
    return lax.dot_general(a, b, (((0,), (0,)), ((), ())), preferred_element_type=F32)


def _rope(x, cos, sin_signed):
    return x * cos + pltpu.roll(x, RET_DK // 2, 1) * sin_signed


def _const_spec(shape):
    nd = len(shape)
    return pl.BlockSpec(shape, lambda *_: (0,) * nd)


def _params(*sem):
    return pltpu.CompilerParams(dimension_semantics=sem, vmem_limit_bytes=VMEM_LIMIT)


def _route(h, gffn, wr, br, tri, ltri):
    t = h.shape[0]
    xn = _rms(h, gffn).astype(BF16)
    lt = _dot_nt(wr, xn) + br[:, 0:1]
    row8 = lax.broadcasted_iota(I32, (SUBLANES, t), 0).astype(F32)
    gl = lt[0:SUBLANES]
    m = jnp.max(gl, axis=0, keepdims=True)
    g_top = 1.0 / jnp.sum(jnp.exp(gl - m), axis=0, keepdims=True)
    gidx = jnp.min(jnp.where(gl == m, row8, float(SUBLANES)), axis=0, keepdims=True)
    e_sel = jnp.where(gidx == 0.0, lt[8:16],
                      jnp.where(gidx == 1.0, lt[16:24], jnp.where(gidx == 2.0, lt[24:32], lt[32:40])))
    m1 = jnp.max(e_sel, axis=0, keepdims=True)
    i1 = jnp.min(jnp.where(e_sel == m1, row8, float(SUBLANES)), axis=0, keepdims=True)
    rest = jnp.where(row8 == i1, -jnp.inf, e_sel)
    m2 = jnp.max(rest, axis=0, keepdims=True)
    i2 = jnp.min(jnp.where(rest == m2, row8, float(SUBLANES)), axis=0, keepdims=True)
    d = jnp.exp(m2 - m1)
    w1 = g_top / (1.0 + d)
    w2 = g_top * d / (1.0 + d)
    e1 = gidx * float(EXPERTS_PER_GROUP) + i1
    e2 = gidx * float(EXPERTS_PER_GROUP) + i2
    row32 = lax.broadcasted_iota(I32, (N_EXPERTS, t), 0).astype(F32)
    a1 = jnp.where(row32 == e1, 1.0, 0.0)
    a2 = jnp.where(row32 == e2, 1.0, 0.0)
    a = a1 + a2
    n = jnp.broadcast_to(jnp.sum(a, axis=1, keepdims=True), (N_EXPERTS, LANES))
    start = _dot(ltri, n.astype(BF16))
    base = _dot(a.astype(BF16), tri) + start[:, 0:1]
    r1 = jnp.sum(a1 * base, axis=0, keepdims=True)
    r2 = jnp.sum(a2 * base, axis=0, keepdims=True)
    ri = jnp.where(row8 == 0.0, e1, jnp.where(row8 == 1.0, e2, jnp.where(row8 == 2.0, r1,
                                                                         jnp.where(row8 == 3.0, r2, 0.0))))
    rw = jnp.where(row8 == 0.0, w1, jnp.where(row8 == 1.0, w2, 0.0))
    return ri.astype(I32), rw, n


def _group_norm_gate(o, gate):
    mu = jnp.mean(o, axis=-1, keepdims=True)
    oc = o - mu
    return oc * lax.rsqrt(jnp.mean(oc * oc, axis=-1, keepdims=True) + EPS) * gate


def _mixer_prompt_kernel(x_ref, cos_ref, sin_ref, qdec_ref, kdec_ref, dmask_ref, cdec_ref, gmix_ref, win_ref,
                         convw_ref, retgn_ref, wo_ref, gffn_ref, wr_ref, br_ref, tri_ref, ltri_ref,
                         h_ref, ri_ref, rw_ref, n_ref, conv_ref, ret_ref,
                         s_scr, z_scr, mix_scr):
    j = pl.program_id(1)
    n_j = pl.num_programs(1)
    tt = x_ref.shape[1]

    @pl.when(j == 0)
    def _():
        s_scr[...] = jnp.zeros_like(s_scr)
        z_scr[0:SUBLANES, :] = jnp.zeros((SUBLANES, CONV_DIM), F32)

    cw = convw_ref[...]
    retgn = retgn_ref[...]
    q0 = 3 * CONV_DIM

    def project(p0):
        x = x_ref[0, p0:p0 + TOK_TILE, :]
        proj = _dot(_rms(x, gmix_ref[...]).astype(BF16), win_ref[...])
        gb = proj[:, 0:CONV_DIM]
        z = proj[:, CONV_DIM:2 * CONV_DIM] * proj[:, 2 * CONV_DIM:3 * CONV_DIM]
        z0 = SUBLANES + p0
        z_scr[z0:z0 + TOK_TILE, :] = z
        yc = cw[0:1] * z_scr[z0 - 2:z0 - 2 + TOK_TILE, :] + cw[1:2] * z_scr[z0 - 1:z0 - 1 + TOK_TILE, :] + cw[2:3] * z
        mix_scr[p0:p0 + TOK_TILE, 0:CONV_DIM] = (gb * yc).astype(BF16)
        return x, proj

    def mix(p0, x, proj):
        cos = cos_ref[p0:p0 + TOK_TILE, :]
        sin = sin_ref[p0:p0 + TOK_TILE, :]
        for hh in range(RET_HEADS):
            l0 = hh * RET_DK
            qr = _rope(proj[:, q0 + l0:q0 + l0 + RET_DK], cos, sin)
            kr = _rope(proj[:, q0 + RET_DIM + l0:q0 + RET_DIM + l0 + RET_DK], cos, sin) * (RET_DK ** -0.5)
            v = proj[:, q0 + 2 * RET_DIM + l0:q0 + 2 * RET_DIM + l0 + RET_DV].astype(BF16)
            g = proj[:, q0 + 3 * RET_DIM + l0:q0 + 3 * RET_DIM + l0 + RET_DV]
            s_old = s_scr[hh]
            scores = _dot_nt(qr.astype(BF16), kr.astype(BF16)) * dmask_ref[hh]
            lhs = jnp.concatenate([scores.astype(BF16), (qr * qdec_ref[:, l0:l0 + RET_DK]).astype(BF16)], axis=1)
            o = _dot(lhs, jnp.concatenate([v, s_old.astype(BF16)], axis=0))
            kd = (kr * kdec_ref[:, l0:l0 + RET_DK]).T.astype(BF16)
            s_scr[hh] = s_old * cdec_ref[hh:hh + 1, :] + _dot(kd, v)
            gate = retgn[:, l0:l0 + RET_DV] * (g * jax.nn.sigmoid(g))
            mix_scr[p0:p0 + TOK_TILE, CONV_DIM + l0:CONV_DIM + l0 + RET_DV] = _group_norm_gate(o, gate).astype(BF16)
        h = x + _dot(mix_scr[p0:p0 + TOK_TILE, :], wo_ref[...])
        h_ref[0, p0:p0 + TOK_TILE, :] = h
        return h

    def route(part, h):
        p0 = part * TOK_TILE
        ri, rw, n = _route(h, gffn_ref[...], wr_ref[...], br_ref[...], tri_ref[...], ltri_ref[...])
        ri_ref[:, p0:p0 + TOK_TILE] = ri
        rw_ref[:, p0:p0 + TOK_TILE] = rw
        n_ref[part] = n

    parts = tt // TOK_TILE
    nxt = project(0)
    for part in range(parts):
        h = mix(part * TOK_TILE, *nxt)
        if part + 1 < parts:
            nxt = project((part + 1) * TOK_TILE)
        route(part, h)

    @pl.when(j == n_j - 1)
    def _():
        conv_ref[0] = z_scr[SUBLANES + tt - 2:SUBLANES + tt, :]
        ret_ref[0] = s_scr[...]

    z_scr[0:SUBLANES, :] = z_scr[tt:tt + SUBLANES, :]


def _mixer_prompt(x, tabs, wts):
    bsz, t, _ = x.shape
    tt = MIX_TILE
    n_j = t // tt
    parts = tt // TOK_TILE
    n_tok = bsz * t
    args = (x,) + tuple(tabs) + tuple(wts)
    in_specs = [
        pl.BlockSpec((1, tt, D_MODEL), lambda b, j: (b, j, 0)),
        pl.BlockSpec((tt, LANES), lambda b, j: (j, 0)),
        pl.BlockSpec((tt, LANES), lambda b, j: (j, 0)),
    ] + [_const_spec(a.shape) for a in args[3:]]
    out_shape = (
        jax.ShapeDtypeStruct((bsz, t, D_MODEL), F32),
        jax.ShapeDtypeStruct((SUBLANES, n_tok), I32),
        jax.ShapeDtypeStruct((SUBLANES, n_tok), F32),
        jax.ShapeDtypeStruct((bsz * n_j * parts, N_EXPERTS, LANES), F32),
        jax.ShapeDtypeStruct((bsz, CONV_WIDTH - 1, CONV_DIM), F32),
        jax.ShapeDtypeStruct((bsz, RET_HEADS, RET_DK, RET_DV), F32),
    )
    out_specs = (
        pl.BlockSpec((1, tt, D_MODEL), lambda b, j: (b, j, 0)),
        pl.BlockSpec((SUBLANES, tt), lambda b, j: (0, b * n_j + j)),
        pl.BlockSpec((SUBLANES, tt), lambda b, j: (0, b * n_j + j)),
        pl.BlockSpec((parts, N_EXPERTS, LANES), lambda b, j: (b * n_j + j, 0, 0)),
        pl.BlockSpec((1, CONV_WIDTH - 1, CONV_DIM), lambda b, j: (b, 0, 0)),
        pl.BlockSpec((1, RET_HEADS, RET_DK, RET_DV), lambda b, j: (b, 0, 0, 0)),
    )
    return pl.pallas_call(
        _mixer_prompt_kernel,
        grid=(bsz, n_j),
        in_specs=in_specs,
        out_specs=out_specs,
        out_shape=out_shape,
        scratch_shapes=[
            pltpu.VMEM((RET_HEADS, RET_DK, RET_DV), F32),
            pltpu.VMEM((tt + SUBLANES, CONV_DIM), F32),
            pltpu.VMEM((tt, D_MODEL), BF16),
        ],
        compiler_params=_params("arbitrary", "arbitrary"),
        name="mixer_prompt",
    )(*args)


def _sample_proj_kernel(x_ref, sc_ref, cos_ref, sin_ref, gmix_ref, win_ref, convw_ref, retgn_ref,
                        yc_ref, q_ref, k_ref, v_ref, gate_ref, conv_ref):
    x = x_ref[...]
    u = _rms(x, gmix_ref[...]).astype(BF16)
    proj = _dot(u, win_ref[...])
    gb = proj[:, 0:CONV_DIM]
    z = proj[:, CONV_DIM:2 * CONV_DIM] * proj[:, 2 * CONV_DIM:3 * CONV_DIM]
    cw = convw_ref[...]
    buf0, buf1 = sc_ref[:, 0:CONV_DIM], sc_ref[:, CONV_DIM:2 * CONV_DIM]
    yc_ref[...] = gb * (cw[0:1] * buf0 + cw[1:2] * buf1 + cw[2:3] * z)
    conv_ref[:, 0:CONV_DIM] = buf1
    conv_ref[:, CONV_DIM:2 * CONV_DIM] = z
    q0 = 3 * CONV_DIM
    cos, sin = cos_ref[0:1, :], sin_ref[0:1, :]
    retgn = retgn_ref[...]
    for hh in range(RET_HEADS):
        l0 = hh * RET_DK
        q_ref[:, l0:l0 + RET_DK] = _rope(proj[:, q0 + l0:q0 + l0 + RET_DK], cos, sin)
        k_ref[:, l0:l0 + RET_DK] = _rope(proj[:, q0 + RET_DIM + l0:q0 + RET_DIM + l0 + RET_DK], cos, sin) * (RET_DK ** -0.5)
    v_ref[...] = proj[:, q0 + 2 * RET_DIM:q0 + 3 * RET_DIM]
    g = proj[:, q0 + 3 * RET_DIM:q0 + 4 * RET_DIM]
    gate_ref[...] = retgn * (g * jax.nn.sigmoid(g))


def _sample_proj(x, sc, cos, sin, gmix, win, convw, retgn):
    n = x.shape[0]
    outs = (
        jax.ShapeDtypeStruct((n, CONV_DIM), F32), jax.ShapeDtypeStruct((n, RET_DIM), F32),
        jax.ShapeDtypeStruct((n, RET_DIM), F32), jax.ShapeDtypeStruct((n, RET_DIM), F32),
        jax.ShapeDtypeStruct((n, RET_DIM), F32), jax.ShapeDtypeStruct((n, 2 * CONV_DIM), F32),
    )
    args = (x, sc, cos, sin, gmix, win, convw, retgn)
    return pl.pallas_call(
        _sample_proj_kernel,
        grid=(1,),
        in_specs=[_const_spec(a.shape) for a in args],
        out_specs=tuple(_const_spec(o.shape) for o in outs),
        out_shape=outs,
        compiler_params=_params("arbitrary"),
        name="sample_proj",
    )(*args)


def _sample_state_kernel(s_ref, q_ref, k_ref, kt_ref, v_ref, dec_ref, snew_ref, o_ref):
    for r in range(s_ref.shape[0]):
        for hh in range(RET_HEADS):
            l0 = hh * RET_DV
            s = s_ref[r, hh]
            q = q_ref[r:r + 1, l0:l0 + RET_DK]
            k = k_ref[r:r + 1, l0:l0 + RET_DK]
            kc = kt_ref[0, hh, :, r:r + 1]
            vr = v_ref[r:r + 1, l0:l0 + RET_DV]
            qdec = dec_ref[hh:hh + 1, :]
            cdec = dec_ref[RET_HEADS + hh:RET_HEADS + hh + 1, :]
            qk = jnp.sum(q * k, axis=1, keepdims=True)
            o_ref[r:r + 1, l0:l0 + RET_DV] = qk * vr + _dot((q * qdec).astype(BF16), s.astype(BF16))
            snew_ref[r, hh] = s * cdec + kc * vr


def _sample_state(s, q, k, kt, v, dec):
    n = s.shape[0]
    nb = SAMPLE_BLOCK
    return pl.pallas_call(
        _sample_state_kernel,
        grid=(n // nb,),
        in_specs=[
            pl.BlockSpec((nb, RET_HEADS, RET_DK, RET_DV), lambda i: (i, 0, 0, 0)),
            pl.BlockSpec((nb, RET_DIM), lambda i: (i, 0)),
            pl.BlockSpec((nb, RET_DIM), lambda i: (i, 0)),
            pl.BlockSpec((1, RET_HEADS, RET_DK, nb), lambda i: (i, 0, 0, 0)),
            pl.BlockSpec((nb, RET_DIM), lambda i: (i, 0)),
            _const_spec(dec.shape),
        ],
        out_specs=(
            pl.BlockSpec((nb, RET_HEADS, RET_DK, RET_DV), lambda i: (i, 0, 0, 0)),
            pl.BlockSpec((nb, RET_DIM), lambda i: (i, 0)),
        ),
        out_shape=(jax.ShapeDtypeStruct(s.shape, F32), jax.ShapeDtypeStruct((n, RET_DIM), F32)),
        compiler_params=_params("arbitrary"),
        name="sample_state",
    )(s, q, k, kt, v, dec)


def _sample_out_kernel(x_ref, yc_ref, o_ref, gate_ref, wo_ref, gffn_ref, wr_ref, br_ref, tri_ref, ltri_ref,
                       h_ref, ri_ref, rw_ref, n_ref):
    n = x_ref.shape[0]
    parts = [yc_ref[...].astype(BF16)]
    for hh in range(RET_HEADS):
        l0 = hh * RET_DV
        parts.append(_group_norm_gate(o_ref[:, l0:l0 + RET_DV], gate_ref[:, l0:l0 + RET_DV]).astype(BF16))
    h_ref[0:n, :] = x_ref[...] + _dot(jnp.concatenate(parts, axis=-1), wo_ref[...])
    if n < TOK_TILE:
        h_ref[n:TOK_TILE, :] = jnp.zeros((TOK_TILE - n, D_MODEL), F32)
    ri, rw, cnt = _route(h_ref[...], gffn_ref[...], wr_ref[...], br_ref[...], tri_ref[...], ltri_ref[...])
    ri_ref[...] = ri
    rw_ref[...] = rw
    n_ref[0] = cnt


def _sample_out(x, yc, o, gate, wo, gffn, wr, br, tri, ltri):
    outs = (
        jax.ShapeDtypeStruct((TOK_TILE, D_MODEL), F32), jax.ShapeDtypeStruct((SUBLANES, TOK_TILE), I32),
        jax.ShapeDtypeStruct((SUBLANES, TOK_TILE), F32), jax.ShapeDtypeStruct((1, N_EXPERTS, LANES), F32),
    )
    args = (x, yc, o, gate, wo, gffn, wr, br, tri, ltri)
    return pl.pallas_call(
        _sample_out_kernel,
        grid=(1,),
        in_specs=[_const_spec(a.shape) for a in args],
        out_specs=tuple(_const_spec(o_.shape) for o_ in outs),
        out_shape=outs,
        compiler_params=_params("arbitrary"),
        name="sample_out",
    )(*args)


def _mixer_sample(x, state_conv, state_ret, gmix, win, convw, retgn, wo, gffn, wr, br, tri, ltri):
    n = x.shape[0]
    lg = _log_gamma()
    cos, sin = _rope_tables(jnp.full((SUBLANES,), PAST_LEN, I32))
    yc, q, k, v, gate, conv_new = _sample_proj(x, state_conv.reshape(n, 2 * CONV_DIM), cos, sin, gmix, win, convw, retgn)
    nb = SAMPLE_BLOCK
    cols = lambda a: a.reshape(n // nb, nb, RET_HEADS, RET_DK).transpose(0, 2, 3, 1)
    step = jnp.exp(lg[:, None] * 1.0)
    dec = jnp.broadcast_to(jnp.concatenate([step, step], axis=0), (2 * RET_HEADS, LANES))
    s_new, o = _sample_state(state_ret, q, k, cols(k), v, dec)
    h, ri, rw, cnt = _sample_out(x, yc, o, gate, wo, gffn, wr, br, tri, ltri)
    return h, ri, rw, cnt, conv_new.reshape(n, 2, CONV_DIM), s_new


def _load_rows(ref, n_rows):
    words = [ref[pl.ds(c, n_rows, stride=ROW_CHUNKS), :] for c in range(ROW_CHUNKS)]
    half = lambda i: [pltpu.unpack_elementwise(w, index=i, packed_dtype=BF16, unpacked_dtype=F32) for w in words]
    return jnp.concatenate(half(0) + half(1), axis=-1)


def _store_rows(ref, val):
    half = D_MODEL // 2
    for c in range(ROW_CHUNKS):
        pair = [val[:, c * LANES:(c + 1) * LANES], val[:, half + c * LANES:half + (c + 1) * LANES]]
        ref[pl.ds(c, val.shape[0], stride=ROW_CHUNKS), :] = pltpu.pack_elementwise(pair, packed_dtype=BF16)


def _row_slice(ref, row, rows):
    return ref.at[pl.ds(pl.multiple_of(row * ROW_CHUNKS, ROW_CHUNKS), rows * ROW_CHUNKS)]


def _run_copies(idx_smem, slot, make_copy):
    for b in range(RUN_BITS):
        def body(j, carry, b=b):
            entry = idx_smem[slot, N_EXPERTS + N_EXPERTS * b + j]
            make_copy(entry & ((1 << SRC_BITS) - 1), entry >> SRC_BITS, 1 << b).start()
            return carry

        lax.fori_loop(0, idx_smem[slot, b], body, 0)


def _wait_tile_rows(hbm_ref, vmem_ref, sem):
    pltpu.make_async_copy(_row_slice(hbm_ref, 0, PAIR_ROWS), vmem_ref, sem).wait()


def _pair_rows_iota():
    return lax.broadcasted_iota(I32, (PAIR_ROWS, TOK_TILE), 0)


def _dispatch_kernel(n_p, tab_ref, hp_ref, hs_ref, ri_ref, gffn_ref, xs_ref, idx_smem, y_scr, isem, rsem):
    i, n = pl.program_id(0), pl.num_programs(0)
    slot = i % 2

    def idx_copy(step, sl):
        return pltpu.make_async_copy(tab_ref.at[step], idx_smem.at[sl], isem.at[sl])

    @pl.when(i == 0)
    def _():
        idx_copy(0, 0).start()

    @pl.when(i + 1 < n)
    def _():
        idx_copy(i + 1, 1 - slot).start()

    @pl.when(i >= 2)
    def _():
        _wait_tile_rows(xs_ref, y_scr.at[slot], rsem.at[slot])

    h = jnp.where(i < n_p, hp_ref[...], hs_ref[...])
    xn = _rms(h, gffn_ref[...]).astype(BF16)
    ri = ri_ref[...]
    rows = _pair_rows_iota()
    perm = jnp.where((rows == ri[2:3]) | (rows == ri[3:4]), 1.0, 0.0).astype(BF16)
    y = _dot(perm, xn)
    _store_rows(y_scr.at[slot], y)

    idx_copy(i, slot).wait()
    _run_copies(idx_smem, slot, lambda src, dst, rows_: pltpu.make_async_copy(
        _row_slice(y_scr.at[slot], src, rows_), _row_slice(xs_ref, dst, rows_), rsem.at[slot]))

    @pl.when(i == n - 1)
    def _():
        _wait_tile_rows(xs_ref, y_scr.at[slot], rsem.at[slot])

    @pl.when((i == n - 1) & (n >= 2))
    def _():
        _wait_tile_rows(xs_ref, y_scr.at[1 - slot], rsem.at[1 - slot])


def _dispatch(tab, h_p, h_s, ri, gffn):
    n_p = h_p.shape[0] // TOK_TILE
    n = tab.shape[0]
    return pl.pallas_call(
        functools.partial(_dispatch_kernel, n_p),
        grid=(n,),
        in_specs=[
            pl.BlockSpec(memory_space=pl.ANY),
            pl.BlockSpec((TOK_TILE, D_MODEL), lambda i: (jnp.minimum(i, n_p - 1), 0)),
            pl.BlockSpec((TOK_TILE, D_MODEL), lambda i: (jnp.maximum(i - n_p, 0), 0)),
            pl.BlockSpec((SUBLANES, TOK_TILE), lambda i: (0, i)),
            _const_spec(gffn.shape),
        ],
        out_specs=pl.BlockSpec(memory_space=pl.ANY),
        out_shape=jax.ShapeDtypeStruct((n * PAIR_ROWS * ROW_CHUNKS, LANES), U32),
        scratch_shapes=[
            pltpu.SMEM((2, TAB_COLS), I32),
            pltpu.VMEM((2, PAIR_ROWS * ROW_CHUNKS, LANES), U32),
            pltpu.SemaphoreType.DMA((2,)),
            pltpu.SemaphoreType.DMA((2,)),
        ],
        compiler_params=_params("arbitrary"),
        name="moe_dispatch",
    )(tab, h_p, h_s, ri, gffn)


def _ffn_kernel(tile_ref, exp_ref, lo_ref, hi_ref, nxt_ref, slot_ref, x_ref, wg_ref, wu_ref, wd_ref, o_ref,
                wg_buf, wu_buf, wd_buf, wgu_scr, wd_scr, y_scr, wsem):
    w = pl.program_id(0)
    prev = jnp.maximum(w - 1, 0)
    lo, hi = lo_ref[w], hi_ref[w]
    e, slot = exp_ref[w], slot_ref[w]
    first = (w == 0) | (tile_ref[w] != tile_ref[prev])

    def weight_copies(expert, sl):
        return [pltpu.make_async_copy(src.at[expert], buf.at[sl], wsem.at[sl])
                for src, buf in ((wg_ref, wg_buf), (wu_ref, wu_buf), (wd_ref, wd_buf))]

    @pl.when(w == 0)
    def _():
        for cp in weight_copies(e, slot):
            cp.start(priority=1)

    @pl.when((w == 0) | (e != exp_ref[prev]))
    def _():
        for cp in weight_copies(e, slot):
            cp.wait()

        @pl.when(nxt_ref[w] != e)
        def _():
            for cp in weight_copies(nxt_ref[w], 1 - slot):
                cp.start(priority=1)

        wgu_scr[:, 0:D_FF] = wg_buf[slot].astype(BF16)
        wgu_scr[:, D_FF:2 * D_FF] = wu_buf[slot].astype(BF16)
        wd_scr[...] = wd_buf[slot].astype(BF16)

    @pl.when(w == 0)
    def _():
        y_scr[...] = jnp.zeros_like(y_scr)

    @pl.when(hi > lo)
    def _():
        gu = _dot(_load_rows(x_ref, FFN_TILE).astype(BF16), wgu_scr[...])
        g, u = gu[:, 0:D_FF], gu[:, D_FF:2 * D_FF]
        y = _dot((g * jax.nn.sigmoid(g) * u).astype(BF16), wd_scr[...])
        row = lax.broadcasted_iota(I32, (FFN_TILE, 1), 0)
        y = jnp.where(((row >= lo) & (row < hi)) | first, y, y_scr[...])
        y_scr[...] = y
        _store_rows(o_ref, y)


def _expert_ffn(items, xs, w_gate, w_up, w_down):
    n_items = items[0].shape[0]
    row_spec = pl.BlockSpec((FFN_TILE * ROW_CHUNKS, LANES), lambda w, t, *_: (t[w], 0))
    grid_spec = pltpu.PrefetchScalarGridSpec(
        num_scalar_prefetch=len(items),
        grid=(n_items,),
        in_specs=[row_spec] + [pl.BlockSpec(memory_space=pl.ANY)] * 3,
        out_specs=row_spec,
        scratch_shapes=[
            pltpu.VMEM((2, D_MODEL, D_FF), F32), pltpu.VMEM((2, D_MODEL, D_FF), F32),
            pltpu.VMEM((2, D_FF, D_MODEL), F32),
            pltpu.VMEM((D_MODEL, 2 * D_FF), BF16), pltpu.VMEM((D_FF, D_MODEL), BF16),
            pltpu.VMEM((FFN_TILE, D_MODEL), F32),
            pltpu.SemaphoreType.DMA((2,)),
        ],
    )
    return pl.pallas_call(
        _ffn_kernel,
        grid_spec=grid_spec,
        out_shape=jax.ShapeDtypeStruct(xs.shape, U32),
        compiler_params=_params("arbitrary"),
        name="moe_ffn",
    )(*items, xs, w_gate, w_up, w_down)


def _work_items(counts, n_rows):
    n_tiles = n_rows // FFN_TILE
    n_items = n_tiles + N_EXPERTS - 1
    off = jnp.cumsum(counts) - counts
    first_tile = off // FFN_TILE
    last_tile = jnp.maximum(off + counts - 1, off) // FFN_TILE
    n_e = jnp.where(counts > 0, last_tile - first_tile + 1, 0)
    start = jnp.cumsum(n_e) - n_e
    total = jnp.sum(n_e)
    w = jnp.minimum(jnp.arange(n_items, dtype=I32), total - 1)
    ids = jnp.arange(N_EXPERTS, dtype=I32)[None, :]
    e = jnp.max(jnp.where((start[None, :] <= w[:, None]) & (n_e[None, :] > 0), ids, 0), axis=1)
    pick = lambda a: jnp.sum(jnp.where(ids == e[:, None], a[None, :], 0), axis=1)
    tile = pick(first_tile) + (w - pick(start))
    lo = jnp.clip(pick(off) - tile * FFN_TILE, 0, FFN_TILE)
    hi = jnp.clip(pick(off + counts) - tile * FFN_TILE, 0, FFN_TILE)
    hi = jnp.where(jnp.arange(n_items) < total, hi, lo)
    used = n_e[None, :] > 0
    nxt = jnp.min(jnp.where(used & (ids > e[:, None]), ids, N_EXPERTS), axis=1)
    nxt = jnp.where(nxt == N_EXPERTS, e, nxt)
    slot = jnp.sum(jnp.where(used & (ids < e[:, None]), 1, 0), axis=1) % 2
    return tuple(a.astype(I32) for a in (tile, e, lo, hi, nxt, slot))


def _run_table(n):
    tiles = n.shape[0]
    counts = jnp.sum(n, axis=0)
    goff = jnp.cumsum(counts) - counts
    dst0 = goff[None, :] + jnp.cumsum(n, axis=0) - n
    src0 = jnp.cumsum(n, axis=1) - n
    bits = jnp.arange(RUN_BITS, dtype=I32)[:, None, None]
    flag = (n[None] >> bits) & 1
    done = (n[None] >> (bits + 1)) << (bits + 1)
    entry = ((dst0[None] + done) << SRC_BITS) | (src0[None] + done)
    rank = jnp.cumsum(flag, axis=2) - flag
    slot = jnp.arange(N_EXPERTS, dtype=I32)
    sel = (flag[..., None] == 1) & (rank[..., None] == slot)
    packed = jnp.sum(jnp.where(sel, entry[..., None], 0), axis=2)
    m = jnp.sum(flag, axis=2).T
    tab = jnp.zeros((tiles, TAB_COLS), I32)
    tab = tab.at[:, :RUN_BITS].set(m)
    tab = tab.at[:, N_EXPERTS:N_EXPERTS * (RUN_BITS + 1)].set(packed.transpose(1, 0, 2).reshape(tiles, -1))
    return counts, tab


def _combine_kernel(tab_ref, h_ref, p_ref, ri_ref, rw_ref, ys_ref, gple_ref, wpg_ref, wpp_ref, gfin_ref, o_ref,
                    idx_smem, rows_scr, isem, rsem):
    i, n = pl.program_id(0), pl.num_programs(0)
    slot = i % 2

    def idx_copy(step, sl):
        return pltpu.make_async_copy(tab_ref.at[step], idx_smem.at[sl], isem.at[sl])

    def gather(sl):
        _run_copies(idx_smem, sl, lambda src, dst, rows_: pltpu.make_async_copy(
            _row_slice(ys_ref, dst, rows_), _row_slice(rows_scr.at[sl], src, rows_), rsem.at[sl]))

    @pl.when(i == 0)
    def _():
        idx_copy(0, 0).start()
        idx_copy(0, 0).wait()
        gather(0)

    @pl.when((i == 0) & (n >= 2))
    def _():
        idx_copy(1, 1).start()

    @pl.when(i + 1 < n)
    def _():
        idx_copy(i + 1, 1 - slot).wait()
        gather(1 - slot)

    @pl.when(i + 2 < n)
    def _():
        idx_copy(i + 2, slot).start()

    _wait_tile_rows(ys_ref, rows_scr.at[slot], rsem.at[slot])

    ri, rw = ri_ref[...], rw_ref[...]
    rows = _pair_rows_iota()
    wperm = jnp.where(rows == ri[2:3], rw[0:1], 0.0) + jnp.where(rows == ri[3:4], rw[1:2], 0.0)
    s = _load_rows(rows_scr.at[slot], PAIR_ROWS).astype(BF16)
    h = h_ref[...] + _dot_tn(wperm.astype(BF16), s)
    gate = jax.nn.sigmoid(_dot(_rms(h, gple_ref[...]).astype(BF16), wpg_ref[...]))
    h = h + _dot(p_ref[...].astype(BF16), wpp_ref[...]) * gate
    o_ref[...] = _rms(h, gfin_ref[...])


def _combine(tab, h, p, ri, rw, ys, gple, wpg, wpp, gfin):
    n = tab.shape[0]
    n_tok = n * TOK_TILE
    return pl.pallas_call(
        _combine_kernel,
        grid=(n,),
        in_specs=[
            pl.BlockSpec(memory_space=pl.ANY),
            pl.BlockSpec((TOK_TILE, D_MODEL), lambda i: (i, 0)),
            pl.BlockSpec((TOK_TILE, PLE_DIM), lambda i: (i, 0)),
            pl.BlockSpec((SUBLANES, TOK_TILE), lambda i: (0, i)),
            pl.BlockSpec((SUBLANES, TOK_TILE), lambda i: (0, i)),
            pl.BlockSpec(memory_space=pl.ANY),
            _const_spec(gple.shape), _const_spec(wpg.shape), _const_spec(wpp.shape), _const_spec(gfin.shape),
        ],
        out_specs=pl.BlockSpec((TOK_TILE, D_MODEL), lambda i: (i, 0)),
        out_shape=jax.ShapeDtypeStruct((n_tok, D_MODEL), F32),
        scratch_shapes=[
            pltpu.SMEM((2, TAB_COLS), I32),
            pltpu.VMEM((2, PAIR_ROWS * ROW_CHUNKS, LANES), U32),
            pltpu.SemaphoreType.DMA((2,)),
            pltpu.SemaphoreType.DMA((2,)),
        ],
        compiler_params=_params("arbitrary"),
        name="moe_combine",
    )(tab, h, p, ri, rw, ys, gple, wpg, wpp, gfin)


def _log_gamma():
    return jnp.log(1.0 - 2.0 ** (-5.0 - jnp.arange(RET_HEADS, dtype=F32)))


def _rope_tables(pos):
    inv = 1.0 / (ROPE_BASE ** (jnp.arange(0, RET_DK, 2, dtype=F32) / RET_DK))
    ang = pos.astype(F32)[:, None] * inv[None, :]
    cos, sin = jnp.cos(ang), jnp.sin(ang)
    return jnp.concatenate([cos, cos], axis=-1), jnp.concatenate([-sin, sin], axis=-1)


def _decay_tables(c):
    lg = _log_gamma()
    idx = jnp.arange(c, dtype=F32)
    diff = idx[:, None] - idx[None, :]
    dmask = jnp.where((diff >= 0.0)[None], jnp.exp(lg[:, None, None] * jnp.maximum(diff, 0.0)[None]), 0.0)
    q_dec = jnp.exp(lg[None, :] * (idx[:, None] + 1.0))
    k_dec = jnp.exp(lg[None, :] * (c - 1.0 - idx[:, None]))
    c_dec = jnp.exp(lg * c)
    lanes = lambda a: jnp.repeat(a, RET_DK, axis=1)
    cdec = jnp.zeros((SUBLANES, LANES), F32).at[:RET_HEADS].set(jnp.broadcast_to(c_dec[:, None], (RET_HEADS, LANES)))
    return lanes(q_dec), lanes(k_dec), dmask, cdec


def _router_params(w_rg, b_rg, w_re, b_re):
    wr = jnp.zeros((ROUTER_ROWS, D_MODEL), F32).at[:N_GROUPS].set(w_rg.T).at[SUBLANES:].set(w_re.T)
    br = jnp.full((ROUTER_ROWS,), NEG_BIG, F32).at[:N_GROUPS].set(b_rg).at[SUBLANES:].set(b_re.reshape(-1))
    return wr.astype(BF16), jnp.broadcast_to(br[:, None], (ROUTER_ROWS, LANES))


def _strict_upper(t):
    i = jnp.arange(t)
    return (i[:, None] < i[None, :]).astype(BF16)


def _pad_rows(a, rows):
    return jnp.zeros((rows,) + a.shape[1:], a.dtype).at[:a.shape[0]].set(a)


def kernel(x_prompt, x_sample, state_conv, state_ret, p_prompt, p_sample, g_mix, w_in, conv_w, ret_gn, w_o, g_ffn,
           w_router_group, b_router_group, w_router_expert, b_router_expert, w_gate, w_up, w_down, g_ple,
           w_ple_proj, w_ple_gate, g_final):
    bp, tp, _ = x_prompt.shape
    bs = x_sample.shape[0]
    assert x_sample.shape[1] == 1 and g_mix.shape[0] == 1, "one layer, one new token per sample request"
    assert bs <= TOK_TILE and tp % MIX_TILE == 0
    n_p = bp * tp

    tabs = _rope_tables(jnp.arange(tp, dtype=I32)) + _decay_tables(TOK_TILE)
    wr, br = _router_params(w_router_group[0], b_router_group[0], w_router_expert[0], b_router_expert[0])
    win, wo, convw = w_in[0].astype(BF16), w_o[0].astype(BF16), _pad_rows(conv_w[0], SUBLANES)
    tri, ltri = _strict_upper(TOK_TILE), _strict_upper(N_EXPERTS).T

    h_p, ri_p, rw_p, n_p_tab, conv_p, ret_p = _mixer_prompt(
        x_prompt, tabs, (g_mix, win, convw, ret_gn, wo, g_ffn, wr, br, tri, ltri))
    h_s, ri_s, rw_s, n_s_tab, conv_s, ret_s = _mixer_sample(
        x_sample[:, 0], state_conv[0], state_ret[0], g_mix, win, convw, ret_gn, wo, g_ffn, wr, br, tri, ltri)
    h_p = h_p.reshape(n_p, D_MODEL)

    n_tab = jnp.concatenate([n_p_tab, n_s_tab], axis=0)[:, :, 0].astype(I32)
    counts, tab = _run_table(n_tab)
    ri = jnp.concatenate([ri_p, ri_s], axis=1)
    n_tiles = n_tab.shape[0]

    xs = _dispatch(tab, h_p, h_s, ri, g_ffn)
    ys = _expert_ffn(_work_items(counts, n_tiles * PAIR_ROWS), xs, w_gate[0], w_up[0], w_down[0])

    wpg, wpp = w_ple_gate[0].astype(BF16), w_ple_proj[0].astype(BF16)
    gfin = g_final[None, :]
    p_s = _pad_rows(p_sample[0].reshape(bs, PLE_DIM), TOK_TILE)
    y_p = _combine(tab[:n_tiles - 1], h_p, p_prompt[0].reshape(n_p, PLE_DIM), ri_p, rw_p, ys, g_ple, wpg, wpp, gfin)
    y_s = _combine(tab[n_tiles - 1:], h_s, p_s, ri_s, rw_s, ys, g_ple, wpg, wpp, gfin)
    return (y_p.reshape(bp, tp, D_MODEL), y_s[:bs].reshape(bs, 1, D_MODEL), conv_p[None], ret_p[None],
            conv_s[None], ret_s[None])
```

```python
import functools

import jax
import jax.numpy as jnp
from jax import lax
from jax.experimental import pallas as pl
from jax.experimental.pallas import tpu as pltpu

F32, BF16, I32, U32 = jnp.float32, jnp.bfloat16, jnp.int32, jnp.uint32

D_MODEL = 1024
CONV_DIM = 512
CONV_WIDTH = 3
RET_DIM = 512
RET_HEADS = 4
RET_DK = 128
RET_DV = 128
RET_CHUNK = 128
ROPE_BASE = 10000.0
IN_PROJ_DIM = 3 * CONV_DIM + 4 * RET_DIM
N_GROUPS = 4
EXPERTS_PER_GROUP = 8
N_EXPERTS = 32
D_FF = 512
PLE_DIM = 256
EPS = 1e-6
PAST_LEN = 16384

LANES = 128
SUBLANES = 8
ROW_CHUNKS = D_MODEL // (2 * LANES)
ROUTER_ROWS = SUBLANES + N_EXPERTS
VMEM_LIMIT = 56 * 1024 * 1024
NEG_BIG = -1e30

TOK_TILE = 256
MIX_TILE = 1024
PAIR_ROWS = 2 * TOK_TILE
FFN_TILE = 256
SAMPLE_BLOCK = 8
RUN_BITS = 9
SRC_BITS = 10
TAB_COLS = 384


def _rms(x, g):
    return x * lax.rsqrt(jnp.mean(x * x, axis=-1, keepdims=True) + EPS) * g


def _dot(a, b):
    return jnp.dot(a, b, preferred_element_type=F32)


def _dot_nt(a, b):
    return lax.dot_general(a, b, (((1,), (1,)), ((), ())), preferred_element_type=F32)


def _dot_tn(a, b):
    return lax.dot_general(a, b, (((0,), (0,)), ((), ())), preferred_element_type=F32)


def _rope(x, cos, sin_signed):
    return x * cos + pltpu.roll(x, RET_DK // 2, 1) * sin_signed


def _const_spec(shape):
    nd = len(shape)
    return pl.BlockSpec(shape, lambda *_: (0,) * nd)


def _params(*sem):
    return pltpu.CompilerParams(dimension_semantics=sem, vmem_limit_bytes=VMEM_LIMIT)


def _route(h, gffn, wr, br, tri, ltri):
    t = h.shape[0]
    xn = _rms(h, gffn).astype(BF16)
    lt = _dot_nt(wr, xn) + br[:, 0:1]
    row8 = lax.broadcasted_iota(I32, (SUBLANES, t), 0).astype(F32)
    gl = lt[0:SUBLANES]
    m = jnp.max(gl, axis=0, keepdims=True)
    g_top = 1.0 / jnp.sum(jnp.exp(gl - m), axis=0, keepdims=True)
    gidx = jnp.min(jnp.where(gl == m, row8, float(SUBLANES)), axis=0, keepdims=True)
    e_sel = jnp.where(gidx == 0.0, lt[8:16],
                      jnp.where(gidx == 1.0, lt[16:24], jnp.where(gidx == 2.0, lt[24:32], lt[32:40])))
    m1 = jnp.max(e_sel, axis=0, keepdims=True)
    i1 = jnp.min(jnp.where(e_sel == m1, row8, float(SUBLANES)), axis=0, keepdims=True)
    rest = jnp.where(row8 == i1, -jnp.inf, e_sel)
    m2 = jnp.max(rest, axis=0, keepdims=True)
    i2 = jnp.min(jnp.where(rest == m2, row8, float(SUBLANES)), axis=0, keepdims=True)
    d = jnp.exp(m2 - m1)
    w1 = g_top / (1.0 + d)
    w2 = g_top * d / (1.0 + d)
    e1 = gidx * float(EXPERTS_PER_GROUP) + i1
    e2 = gidx * float(EXPERTS_PER_GROUP) + i2
    row32 = lax.broadcasted_iota(I32, (N_EXPERTS, t), 0).astype(F32)
    a1 = jnp.where(row32 == e1, 1.0, 0.0)
    a2 = jnp.where(row32 == e2, 1.0, 0.0)
    a = a1 + a2
    n = jnp.broadcast_to(jnp.sum(a, axis=1, keepdims=True), (N_EXPERTS, LANES))
    start = _dot(ltri, n.astype(BF16))
    base = _dot(a.astype(BF16), tri) + start[:, 0:1]
    r1 = jnp.sum(a1 * base, axis=0, keepdims=True)
    r2 = jnp.sum(a2 * base, axis=0, keepdims=True)
    ri = jnp.where(row8 == 0.0, e1, jnp.where(row8 == 1.0, e2, jnp.where(row8 == 2.0, r1,
                                                                         jnp.where(row8 == 3.0, r2, 0.0))))
    rw = jnp.where(row8 == 0.0, w1, jnp.where(row8 == 1.0, w2, 0.0))
    rows = lax.broadcasted_iota(I32, (2 * t, t), 0).astype(F32)
    perm = jnp.where((rows == r1) | (rows == r2), 1.0, 0.0).astype(BF16)
    return ri.astype(I32), rw, n, _dot(perm, xn)


def _group_norm_gate(o, gate):
    mu = jnp.mean(o, axis=-1, keepdims=True)
    oc = o - mu
    return oc * lax.rsqrt(jnp.mean(oc * oc, axis=-1, keepdims=True) + EPS) * gate


def _mixer_prompt_kernel(x_ref, cos_ref, sin_ref, qdec_ref, kdec_ref, dmask_ref, cdec_ref, gmix_ref, win_ref,
                         convw_ref, retgn_ref, wo_ref, gffn_ref, wr_ref, br_ref, tri_ref, ltri_ref,
                         h_ref, ri_ref, rw_ref, n_ref, yl_ref, conv_ref, ret_ref,
                         s_scr, z_scr, mix_scr):
    j = pl.program_id(1)
    n_j = pl.num_programs(1)
    tt = x_ref.shape[1]

    @pl.when(j == 0)
    def _():
        s_scr[...] = jnp.zeros_like(s_scr)
        z_scr[0:SUBLANES, :] = jnp.zeros((SUBLANES, CONV_DIM), F32)

    cw = convw_ref[...]
    retgn = retgn_ref[...]
    q0 = 3 * CONV_DIM

    def project(p0):
        x = x_ref[0, p0:p0 + TOK_TILE, :]
        proj = _dot(_rms(x, gmix_ref[...]).astype(BF16), win_ref[...])
        gb = proj[:, 0:CONV_DIM]
        z = proj[:, CONV_DIM:2 * CONV_DIM] * proj[:, 2 * CONV_DIM:3 * CONV_DIM]
        z0 = SUBLANES + p0
        z_scr[z0:z0 + TOK_TILE, :] = z
        yc = cw[0:1] * z_scr[z0 - 2:z0 - 2 + TOK_TILE, :] + cw[1:2] * z_scr[z0 - 1:z0 - 1 + TOK_TILE, :] + cw[2:3] * z
        mix_scr[p0:p0 + TOK_TILE, 0:CONV_DIM] = (gb * yc).astype(BF16)
        return x, proj

    def mix(p0, x, proj):
        cos = cos_ref[p0:p0 + TOK_TILE, :]
        sin = sin_ref[p0:p0 + TOK_TILE, :]
        for hh in range(RET_HEADS):
            l0 = hh * RET_DK
            qr = _rope(proj[:, q0 + l0:q0 + l0 + RET_DK], cos, sin)
            kr = _rope(proj[:, q0 + RET_DIM + l0:q0 + RET_DIM + l0 + RET_DK], cos, sin) * (RET_DK ** -0.5)
            v = proj[:, q0 + 2 * RET_DIM + l0:q0 + 2 * RET_DIM + l0 + RET_DV].astype(BF16)
            g = proj[:, q0 + 3 * RET_DIM + l0:q0 + 3 * RET_DIM + l0 + RET_DV]
            s_old = s_scr[hh]
            scores = _dot_nt(qr.astype(BF16), kr.astype(BF16)) * dmask_ref[hh]
            lhs = jnp.concatenate([scores.astype(BF16), (qr * qdec_ref[:, l0:l0 + RET_DK]).astype(BF16)], axis=1)
            o = _dot(lhs, jnp.concatenate([v, s_old.astype(BF16)], axis=0))
            kd = (kr * kdec_ref[:, l0:l0 + RET_DK]).T.astype(BF16)
            s_scr[hh] = s_old * cdec_ref[hh:hh + 1, :] + _dot(kd, v)
            gate = retgn[:, l0:l0 + RET_DV] * (g * jax.nn.sigmoid(g))
            mix_scr[p0:p0 + TOK_TILE, CONV_DIM + l0:CONV_DIM + l0 + RET_DV] = _group_norm_gate(o, gate).astype(BF16)
        h = x + _dot(mix_scr[p0:p0 + TOK_TILE, :], wo_ref[...])
        h_ref[0, p0:p0 + TOK_TILE, :] = h
        return h

    def route(part, h):
        p0 = part * TOK_TILE
        ri, rw, n, y = _route(h, gffn_ref[...], wr_ref[...], br_ref[...], tri_ref[...], ltri_ref[...])
        ri_ref[:, p0:p0 + TOK_TILE] = ri
        rw_ref[:, p0:p0 + TOK_TILE] = rw
        n_ref[part] = n
        _store_rows(yl_ref.at[pl.ds(part * PAIR_ROWS * ROW_CHUNKS, PAIR_ROWS * ROW_CHUNKS)], y)

    parts = tt // TOK_TILE
    nxt = project(0)
    for part in range(parts):
        h = mix(part * TOK_TILE, *nxt)
        if part + 1 < parts:
            nxt = project((part + 1) * TOK_TILE)
        route(part, h)

    @pl.when(j == n_j - 1)
    def _():
        conv_ref[0] = z_scr[SUBLANES + tt - 2:SUBLANES + tt, :]
        ret_ref[0] = s_scr[...]

    z_scr[0:SUBLANES, :] = z_scr[tt:tt + SUBLANES, :]


def _mixer_prompt(x, tabs, wts):
    bsz, t, _ = x.shape
    tt = MIX_TILE
    n_j = t // tt
    parts = tt // TOK_TILE
    n_tok = bsz * t
    args = (x,) + tuple(tabs) + tuple(wts)
    in_specs = [
        pl.BlockSpec((1, tt, D_MODEL), lambda b, j: (b, j, 0)),
        pl.BlockSpec((tt, LANES), lambda b, j: (j, 0)),
        pl.BlockSpec((tt, LANES), lambda b, j: (j, 0)),
    ] + [_const_spec(a.shape) for a in args[3:]]
    out_shape = (
        jax.ShapeDtypeStruct((bsz, t, D_MODEL), F32),
        jax.ShapeDtypeStruct((SUBLANES, n_tok), I32),
        jax.ShapeDtypeStruct((SUBLANES, n_tok), F32),
        jax.ShapeDtypeStruct((bsz * n_j * parts, N_EXPERTS, LANES), F32),
        jax.ShapeDtypeStruct((bsz * n_j * parts * PAIR_ROWS * ROW_CHUNKS, LANES), U32),
        jax.ShapeDtypeStruct((bsz, CONV_WIDTH - 1, CONV_DIM), F32),
        jax.ShapeDtypeStruct((bsz, RET_HEADS, RET_DK, RET_DV), F32),
    )
    out_specs = (
        pl.BlockSpec((1, tt, D_MODEL), lambda b, j: (b, j, 0)),
        pl.BlockSpec((SUBLANES, tt), lambda b, j: (0, b * n_j + j)),
        pl.BlockSpec((SUBLANES, tt), lambda b, j: (0, b * n_j + j)),
        pl.BlockSpec((parts, N_EXPERTS, LANES), lambda b, j: (b * n_j + j, 0, 0)),
        pl.BlockSpec((parts * PAIR_ROWS * ROW_CHUNKS, LANES), lambda b, j: (b * n_j + j, 0)),
        pl.BlockSpec((1, CONV_WIDTH - 1, CONV_DIM), lambda b, j: (b, 0, 0)),
        pl.BlockSpec((1, RET_HEADS, RET_DK, RET_DV), lambda b, j: (b, 0, 0, 0)),
    )
    return pl.pallas_call(
        _mixer_prompt_kernel,
        grid=(bsz, n_j),
        in_specs=in_specs,
        out_specs=out_specs,
        out_shape=out_shape,
        scratch_shapes=[
            pltpu.VMEM((RET_HEADS, RET_DK, RET_DV), F32),
            pltpu.VMEM((tt + SUBLANES, CONV_DIM), F32),
            pltpu.VMEM((tt, D_MODEL), BF16),
        ],
        compiler_params=_params("arbitrary", "arbitrary"),
        name="mixer_prompt",
    )(*args)


def _sample_proj_kernel(x_ref, sc_ref, cos_ref, sin_ref, gmix_ref, win_ref, convw_ref, retgn_ref,
                        yc_ref, q_ref, k_ref, v_ref, gate_ref, conv_ref):
    x = x_ref[...]
    u = _rms(x, gmix_ref[...]).astype(BF16)
    proj = _dot(u, win_ref[...])
    gb = proj[:, 0:CONV_DIM]
    z = proj[:, CONV_DIM:2 * CONV_DIM] * proj[:, 2 * CONV_DIM:3 * CONV_DIM]
    cw = convw_ref[...]
    buf0, buf1 = sc_ref[:, 0:CONV_DIM], sc_ref[:, CONV_DIM:2 * CONV_DIM]
    yc_ref[...] = gb * (cw[0:1] * buf0 + cw[1:2] * buf1 + cw[2:3] * z)
    conv_ref[:, 0:CONV_DIM] = buf1
    conv_ref[:, CONV_DIM:2 * CONV_DIM] = z
    q0 = 3 * CONV_DIM
    cos, sin = cos_ref[0:1, :], sin_ref[0:1, :]
    retgn = retgn_ref[...]
    for hh in range(RET_HEADS):
        l0 = hh * RET_DK
        q_ref[:, l0:l0 + RET_DK] = _rope(proj[:, q0 + l0:q0 + l0 + RET_DK], cos, sin)
        k_ref[:, l0:l0 + RET_DK] = _rope(proj[:, q0 + RET_DIM + l0:q0 + RET_DIM + l0 + RET_DK], cos, sin) * (RET_DK ** -0.5)
    v_ref[...] = proj[:, q0 + 2 * RET_DIM:q0 + 3 * RET_DIM]
    g = proj[:, q0 + 3 * RET_DIM:q0 + 4 * RET_DIM]
    gate_ref[...] = retgn * (g * jax.nn.sigmoid(g))


def _sample_proj(x, sc, cos, sin, gmix, win, convw, retgn):
    n = x.shape[0]
    outs = (
        jax.ShapeDtypeStruct((n, CONV_DIM), F32), jax.ShapeDtypeStruct((n, RET_DIM), F32),
        jax.ShapeDtypeStruct((n, RET_DIM), F32), jax.ShapeDtypeStruct((n, RET_DIM), F32),
        jax.ShapeDtypeStruct((n, RET_DIM), F32), jax.ShapeDtypeStruct((n, 2 * CONV_DIM), F32),
    )
    args = (x, sc, cos, sin, gmix, win, convw, retgn)
    return pl.pallas_call(
        _sample_proj_kernel,
        grid=(1,),
        in_specs=[_const_spec(a.shape) for a in args],
        out_specs=tuple(_const_spec(o.shape) for o in outs),
        out_shape=outs,
        compiler_params=_params("arbitrary"),
        name="sample_proj",
    )(*args)


def _sample_state_kernel(s_ref, q_ref, k_ref, kt_ref, v_ref, dec_ref, snew_ref, o_ref):
    for r in range(s_ref.shape[0]):
        for hh in range(RET_HEADS):
            l0 = hh * RET_DV
            s = s_ref[r, hh]
            q = q_ref[r:r + 1, l0:l0 + RET_DK]
            k = k_ref[r:r + 1, l0:l0 + RET_DK]
            kc = kt_ref[0, hh, :, r:r + 1]
            vr = v_ref[r:r + 1, l0:l0 + RET_DV]
            qdec = dec_ref[hh:hh + 1, :]
            cdec = dec_ref[RET_HEADS + hh:RET_HEADS + hh + 1, :]
            qk = jnp.sum(q * k, axis=1, keepdims=True)
            o_ref[r:r + 1, l0:l0 + RET_DV] = qk * vr + _dot((q * qdec).astype(BF16), s.astype(BF16))
            snew_ref[r, hh] = s * cdec + kc * vr


def _sample_state(s, q, k, kt, v, dec):
    n = s.shape[0]
    nb = SAMPLE_BLOCK
    return pl.pallas_call(
        _sample_state_kernel,
        grid=(n // nb,),
        in_specs=[
            pl.BlockSpec((nb, RET_HEADS, RET_DK, RET_DV), lambda i: (i, 0, 0, 0)),
            pl.BlockSpec((nb, RET_DIM), lambda i: (i, 0)),
            pl.BlockSpec((nb, RET_DIM), lambda i: (i, 0)),
            pl.BlockSpec((1, RET_HEADS, RET_DK, nb), lambda i: (i, 0, 0, 0)),
            pl.BlockSpec((nb, RET_DIM), lambda i: (i, 0)),
            _const_spec(dec.shape),
        ],
        out_specs=(
            pl.BlockSpec((nb, RET_HEADS, RET_DK, RET_DV), lambda i: (i, 0, 0, 0)),
            pl.BlockSpec((nb, RET_DIM), lambda i: (i, 0)),
        ),
        out_shape=(jax.ShapeDtypeStruct(s.shape, F32), jax.ShapeDtypeStruct((n, RET_DIM), F32)),
        compiler_params=_params("arbitrary"),
        name="sample_state",
    )(s, q, k, kt, v, dec)


def _sample_out_kernel(x_ref, yc_ref, o_ref, gate_ref, wo_ref, gffn_ref, wr_ref, br_ref, tri_ref, ltri_ref,
                       h_ref, ri_ref, rw_ref, n_ref, yl_ref):
    n = x_ref.shape[0]
    parts = [yc_ref[...].astype(BF16)]
    for hh in range(RET_HEADS):
        l0 = hh * RET_DV
        parts.append(_group_norm_gate(o_ref[:, l0:l0 + RET_DV], gate_ref[:, l0:l0 + RET_DV]).astype(BF16))
    h_ref[0:n, :] = x_ref[...] + _dot(jnp.concatenate(parts, axis=-1), wo_ref[...])
    if n < TOK_TILE:
        h_ref[n:TOK_TILE, :] = jnp.zeros((TOK_TILE - n, D_MODEL), F32)
    ri, rw, cnt, y = _route(h_ref[...], gffn_ref[...], wr_ref[...], br_ref[...], tri_ref[...], ltri_ref[...])
    ri_ref[...] = ri
    rw_ref[...] = rw
    n_ref[0] = cnt
    _store_rows(yl_ref, y)


def _sample_out(x, yc, o, gate, wo, gffn, wr, br, tri, ltri):
    outs = (
        jax.ShapeDtypeStruct((TOK_TILE, D_MODEL), F32), jax.ShapeDtypeStruct((SUBLANES, TOK_TILE), I32),
        jax.ShapeDtypeStruct((SUBLANES, TOK_TILE), F32), jax.ShapeDtypeStruct((1, N_EXPERTS, LANES), F32),
        jax.ShapeDtypeStruct((PAIR_ROWS * ROW_CHUNKS, LANES), U32),
    )
    args = (x, yc, o, gate, wo, gffn, wr, br, tri, ltri)
    return pl.pallas_call(
        _sample_out_kernel,
        grid=(1,),
        in_specs=[_const_spec(a.shape) for a in args],
        out_specs=tuple(_const_spec(o_.shape) for o_ in outs),
        out_shape=outs,
        compiler_params=_params("arbitrary"),
        name="sample_out",
    )(*args)


def _mixer_sample(x, state_conv, state_ret, gmix, win, convw, retgn, wo, gffn, wr, br, tri, ltri):
    n = x.shape[0]
    lg = _log_gamma()
    cos, sin = _rope_tables(jnp.full((SUBLANES,), PAST_LEN, I32))
    yc, q, k, v, gate, conv_new = _sample_proj(x, state_conv.reshape(n, 2 * CONV_DIM), cos, sin, gmix, win, convw, retgn)
    nb = SAMPLE_BLOCK
    cols = lambda a: a.reshape(n // nb, nb, RET_HEADS, RET_DK).transpose(0, 2, 3, 1)
    step = jnp.exp(lg[:, None] * 1.0)
    dec = jnp.broadcast_to(jnp.concatenate([step, step], axis=0), (2 * RET_HEADS, LANES))
    s_new, o = _sample_state(state_ret, q, k, cols(k), v, dec)
    h, ri, rw, cnt, yl = _sample_out(x, yc, o, gate, wo, gffn, wr, br, tri, ltri)
    return h, ri, rw, cnt, yl, conv_new.reshape(n, 2, CONV_DIM), s_new


def _load_rows(ref, n_rows):
    words = [ref[pl.ds(c, n_rows, stride=ROW_CHUNKS), :] for c in range(ROW_CHUNKS)]
    half = lambda i: [pltpu.unpack_elementwise(w, index=i, packed_dtype=BF16, unpacked_dtype=F32) for w in words]
    return jnp.concatenate(half(0) + half(1), axis=-1)


def _store_rows(ref, val):
    half = D_MODEL // 2
    for c in range(ROW_CHUNKS):
        pair = [val[:, c * LANES:(c + 1) * LANES], val[:, half + c * LANES:half + (c + 1) * LANES]]
        ref[pl.ds(c, val.shape[0], stride=ROW_CHUNKS), :] = pltpu.pack_elementwise(pair, packed_dtype=BF16)


def _row_slice(ref, row, rows):
    return ref.at[pl.ds(pl.multiple_of(row * ROW_CHUNKS, ROW_CHUNKS), rows * ROW_CHUNKS)]


def _run_copies(idx_smem, slot, make_copy):
    for b in range(RUN_BITS):
        def body(j, carry, b=b):
            entry = idx_smem[slot, N_EXPERTS + N_EXPERTS * b + j]
            make_copy(entry & ((1 << SRC_BITS) - 1), entry >> SRC_BITS, 1 << b).start()
            return carry

        lax.fori_loop(0, idx_smem[slot, b], body, 0)


def _wait_tile_rows(hbm_ref, vmem_ref, sem):
    pltpu.make_async_copy(_row_slice(hbm_ref, 0, PAIR_ROWS), vmem_ref, sem).wait()


def _pair_rows_iota():
    return lax.broadcasted_iota(I32, (PAIR_ROWS, TOK_TILE), 0)


def _dispatch_kernel(n_p, tab_ref, ylp_ref, yls_ref, xs_ref, idx_smem, isem, rsem):
    i, n = pl.program_id(0), pl.num_programs(0)
    slot = i % 2

    def idx_copy(step, sl):
        return pltpu.make_async_copy(tab_ref.at[step], idx_smem.at[sl], isem.at[sl])

    def wait_tile(sl):
        pltpu.make_async_copy(_row_slice(ylp_ref, 0, PAIR_ROWS), _row_slice(xs_ref, 0, PAIR_ROWS), rsem.at[sl]).wait()

    @pl.when(i == 0)
    def _():
        idx_copy(0, 0).start()

    @pl.when(i + 1 < n)
    def _():
        idx_copy(i + 1, 1 - slot).start()

    idx_copy(i, slot).wait()

    @pl.when(i < n_p)
    def _():
        _run_copies(idx_smem, slot, lambda src, dst, rows_: pltpu.make_async_copy(
            _row_slice(ylp_ref, i * PAIR_ROWS + src, rows_), _row_slice(xs_ref, dst, rows_), rsem.at[slot]))

    @pl.when(i >= n_p)
    def _():
        _run_copies(idx_smem, slot, lambda src, dst, rows_: pltpu.make_async_copy(
            _row_slice(yls_ref, (i - n_p) * PAIR_ROWS + src, rows_), _row_slice(xs_ref, dst, rows_), rsem.at[slot]))

    @pl.when(i >= 1)
    def _():
        wait_tile(1 - slot)

    @pl.when(i == n - 1)
    def _():
        wait_tile(slot)


def _dispatch(tab, yl_p, yl_s):
    n_p = yl_p.shape[0] // (PAIR_ROWS * ROW_CHUNKS)
    n = tab.shape[0]
    return pl.pallas_call(
        functools.partial(_dispatch_kernel, n_p),
        grid=(n,),
        in_specs=[pl.BlockSpec(memory_space=pl.ANY)] * 3,
        out_specs=pl.BlockSpec(memory_space=pl.ANY),
        out_shape=jax.ShapeDtypeStruct((n * PAIR_ROWS * ROW_CHUNKS, LANES), U32),
        scratch_shapes=[
            pltpu.SMEM((2, TAB_COLS), I32),
            pltpu.SemaphoreType.DMA((2,)),
            pltpu.SemaphoreType.DMA((2,)),
        ],
        compiler_params=_params("arbitrary"),
        name="moe_dispatch",
    )(tab, yl_p, yl_s)


def _ffn_kernel(tile_ref, exp_ref, lo_ref, hi_ref, nxt_ref, slot_ref, x_ref, wg_ref, wu_ref, wd_ref, o_ref,
                wg_buf, wu_buf, wd_buf, wgu_scr, wd_scr, y_scr, wsem):
    w = pl.program_id(0)
    prev = jnp.maximum(w - 1, 0)
    lo, hi = lo_ref[w], hi_ref[w]
    e, slot = exp_ref[w], slot_ref[w]
    first = (w == 0) | (tile_ref[w] != tile_ref[prev])

    def weight_copies(expert, sl):
        return [pltpu.make_async_copy(src.at[expert], buf.at[sl], wsem.at[sl])
                for src, buf in ((wg_ref, wg_buf), (wu_ref, wu_buf), (wd_ref, wd_buf))]

    @pl.when(w == 0)
    def _():
        for cp in weight_copies(e, slot):
            cp.start(priority=1)

    @pl.when((w == 0) | (e != exp_ref[prev]))
    def _():
        for cp in weight_copies(e, slot):
            cp.wait()

        @pl.when(nxt_ref[w] != e)
        def _():
            for cp in weight_copies(nxt_ref[w], 1 - slot):
                cp.start(priority=1)

        wgu_scr[:, 0:D_FF] = wg_buf[slot].astype(BF16)
        wgu_scr[:, D_FF:2 * D_FF] = wu_buf[slot].astype(BF16)
        wd_scr[...] = wd_buf[slot].astype(BF16)

    @pl.when(w == 0)
    def _():
        y_scr[...] = jnp.zeros_like(y_scr)

    @pl.when(hi > lo)
    def _():
        gu = _dot(_load_rows(x_ref, FFN_TILE).astype(BF16), wgu_scr[...])
        g, u = gu[:, 0:D_FF], gu[:, D_FF:2 * D_FF]
        y = _dot((g * jax.nn.sigmoid(g) * u).astype(BF16), wd_scr[...])
        row = lax.broadcasted_iota(I32, (FFN_TILE, 1), 0)
        y = jnp.where(((row >= lo) & (row < hi)) | first, y, y_scr[...])
        y_scr[...] = y
        _store_rows(o_ref, y)


def _expert_ffn(items, xs, w_gate, w_up, w_down):
    n_items = items[0].shape[0]
    row_spec = pl.BlockSpec((FFN_TILE * ROW_CHUNKS, LANES), lambda w, t, *_: (t[w], 0))
    grid_spec = pltpu.PrefetchScalarGridSpec(
        num_scalar_prefetch=len(items),
        grid=(n_items,),
        in_specs=[row_spec] + [pl.BlockSpec(memory_space=pl.ANY)] * 3,
        out_specs=row_spec,
        scratch_shapes=[
            pltpu.VMEM((2, D_MODEL, D_FF), F32), pltpu.VMEM((2, D_MODEL, D_FF), F32),
            pltpu.VMEM((2, D_FF, D_MODEL), F32),
            pltpu.VMEM((D_MODEL, 2 * D_FF), BF16), pltpu.VMEM((D_FF, D_MODEL), BF16),
            pltpu.VMEM((FFN_TILE, D_MODEL), F32),
            pltpu.SemaphoreType.DMA((2,)),
        ],
    )
    return pl.pallas_call(
        _ffn_kernel,
        grid_spec=grid_spec,
        out_shape=jax.ShapeDtypeStruct(xs.shape, U32),
        compiler_params=_params("arbitrary"),
        name="moe_ffn",
    )(*items, xs, w_gate, w_up, w_down)


def _work_items(counts, n_rows):
    n_tiles = n_rows // FFN_TILE
    n_items = n_tiles + N_EXPERTS - 1
    off = jnp.cumsum(counts) - counts
    first_tile = off // FFN_TILE
    last_tile = jnp.maximum(off + counts - 1, off) // FFN_TILE
    n_e = jnp.where(counts > 0, last_tile - first_tile + 1, 0)
    start = jnp.cumsum(n_e) - n_e
    total = jnp.sum(n_e)
    w = jnp.minimum(jnp.arange(n_items, dtype=I32), total - 1)
    ids = jnp.arange(N_EXPERTS, dtype=I32)[None, :]
    e = jnp.max(jnp.where((start[None, :] <= w[:, None]) & (n_e[None, :] > 0), ids, 0), axis=1)
    pick = lambda a: jnp.sum(jnp.where(ids == e[:, None], a[None, :], 0), axis=1)
    tile = pick(first_tile) + (w - pick(start))
    lo = jnp.clip(pick(off) - tile * FFN_TILE, 0, FFN_TILE)
    hi = jnp.clip(pick(off + counts) - tile * FFN_TILE, 0, FFN_TILE)
    hi = jnp.where(jnp.arange(n_items) < total, hi, lo)
    used = n_e[None, :] > 0
    nxt = jnp.min(jnp.where(used & (ids > e[:, None]), ids, N_EXPERTS), axis=1)
    nxt = jnp.where(nxt == N_EXPERTS, e, nxt)
    slot = jnp.sum(jnp.where(used & (ids < e[:, None]), 1, 0), axis=1) % 2
    return tuple(a.astype(I32) for a in (tile, e, lo, hi, nxt, slot))


def _run_table(n):
    tiles = n.shape[0]
    counts = jnp.sum(n, axis=0)
    goff = jnp.cumsum(counts) - counts
    dst0 = goff[None, :] + jnp.cumsum(n, axis=0) - n
    src0 = jnp.cumsum(n, axis=1) - n
    bits = jnp.arange(RUN_BITS, dtype=I32)[:, None, None]
    flag = (n[None] >> bits) & 1
    done = (n[None] >> (bits + 1)) << (bits + 1)
    entry = ((dst0[None] + done) << SRC_BITS) | (src0[None] + done)
    rank = jnp.cumsum(flag, axis=2) - flag
    slot = jnp.arange(N_EXPERTS, dtype=I32)
    sel = (flag[..., None] == 1) & (rank[..., None] == slot)
    packed = jnp.sum(jnp.where(sel, entry[..., None], 0), axis=2)
    m = jnp.sum(flag, axis=2).T
    tab = jnp.zeros((tiles, TAB_COLS), I32)
    tab = tab.at[:, :RUN_BITS].set(m)
    tab = tab.at[:, N_EXPERTS:N_EXPERTS * (RUN_BITS + 1)].set(packed.transpose(1, 0, 2).reshape(tiles, -1))
    return counts, tab


def _combine_kernel(tab_ref, h_ref, p_ref, ri_ref, rw_ref, ys_ref, gple_ref, wpg_ref, wpp_ref, gfin_ref, o_ref,
                    idx_smem, rows_scr, isem, rsem):
    i, n = pl.program_id(0), pl.num_programs(0)
    slot = i % 2

    def idx_copy(step, sl):
        return pltpu.make_async_copy(tab_ref.at[step], idx_smem.at[sl], isem.at[sl])

    def gather(sl):
        _run_copies(idx_smem, sl, lambda src, dst, rows_: pltpu.make_async_copy(
            _row_slice(ys_ref, dst, rows_), _row_slice(rows_scr.at[sl], src, rows_), rsem.at[sl]))

    @pl.when(i == 0)
    def _():
        idx_copy(0, 0).start()
        idx_copy(0, 0).wait()
        gather(0)

    @pl.when((i == 0) & (n >= 2))
    def _():
        idx_copy(1, 1).start()

    @pl.when(i + 1 < n)
    def _():
        idx_copy(i + 1, 1 - slot).wait()
        gather(1 - slot)

    @pl.when(i + 2 < n)
    def _():
        idx_copy(i + 2, slot).start()

    _wait_tile_rows(ys_ref, rows_scr.at[slot], rsem.at[slot])

    ri, rw = ri_ref[...], rw_ref[...]
    rows = _pair_rows_iota()
    wperm = jnp.where(rows == ri[2:3], rw[0:1], 0.0) + jnp.where(rows == ri[3:4], rw[1:2], 0.0)
    s = _load_rows(rows_scr.at[slot], PAIR_ROWS).astype(BF16)
    h = h_ref[...] + _dot_tn(wperm.astype(BF16), s)
    gate = jax.nn.sigmoid(_dot(_rms(h, gple_ref[...]).astype(BF16), wpg_ref[...]))
    h = h + _dot(p_ref[...].astype(BF16), wpp_ref[...]) * gate
    o_ref[...] = _rms(h, gfin_ref[...])


def _combine(tab, h, p, ri, rw, ys, gple, wpg, wpp, gfin):
    n = tab.shape[0]
    n_tok = n * TOK_TILE
    return pl.pallas_call(
        _combine_kernel,
        grid=(n,),
        in_specs=[
            pl.BlockSpec(memory_space=pl.ANY),
            pl.BlockSpec((TOK_TILE, D_MODEL), lambda i: (i, 0)),
            pl.BlockSpec((TOK_TILE, PLE_DIM), lambda i: (i, 0)),
            pl.BlockSpec((SUBLANES, TOK_TILE), lambda i: (0, i)),
            pl.BlockSpec((SUBLANES, TOK_TILE), lambda i: (0, i)),
            pl.BlockSpec(memory_space=pl.ANY),
            _const_spec(gple.shape), _const_spec(wpg.shape), _const_spec(wpp.shape), _const_spec(gfin.shape),
        ],
        out_specs=pl.BlockSpec((TOK_TILE, D_MODEL), lambda i: (i, 0)),
        out_shape=jax.ShapeDtypeStruct((n_tok, D_MODEL), F32),
        scratch_shapes=[
            pltpu.SMEM((2, TAB_COLS), I32),
            pltpu.VMEM((2, PAIR_ROWS * ROW_CHUNKS, LANES), U32),
            pltpu.SemaphoreType.DMA((2,)),
            pltpu.SemaphoreType.DMA((2,)),
        ],
        compiler_params=_params("arbitrary"),
        name="moe_combine",
    )(tab, h, p, ri, rw, ys, gple, wpg, wpp, gfin)


def _log_gamma():
    return jnp.log(1.0 - 2.0 ** (-5.0 - jnp.arange(RET_HEADS, dtype=F32)))


def _rope_tables(pos):
    inv = 1.0 / (ROPE_BASE ** (jnp.arange(0, RET_DK, 2, dtype=F32) / RET_DK))
    ang = pos.astype(F32)[:, None] * inv[None, :]
    cos, sin = jnp.cos(ang), jnp.sin(ang)
    return jnp.concatenate([cos, cos], axis=-1), jnp.concatenate([-sin, sin], axis=-1)


def _decay_tables(c):
    lg = _log_gamma()
    idx = jnp.arange(c, dtype=F32)
    diff = idx[:, None] - idx[None, :]
    dmask = jnp.where((diff >= 0.0)[None], jnp.exp(lg[:, None, None] * jnp.maximum(diff, 0.0)[None]), 0.0)
    q_dec = jnp.exp(lg[None, :] * (idx[:, None] + 1.0))
    k_dec = jnp.exp(lg[None, :] * (c - 1.0 - idx[:, None]))
    c_dec = jnp.exp(lg * c)
    lanes = lambda a: jnp.repeat(a, RET_DK, axis=1)
    cdec = jnp.zeros((SUBLANES, LANES), F32).at[:RET_HEADS].set(jnp.broadcast_to(c_dec[:, None], (RET_HEADS, LANES)))
    return lanes(q_dec), lanes(k_dec), dmask, cdec


def _router_params(w_rg, b_rg, w_re, b_re):
    wr = jnp.zeros((ROUTER_ROWS, D_MODEL), F32).at[:N_GROUPS].set(w_rg.T).at[SUBLANES:].set(w_re.T)
    br = jnp.full((ROUTER_ROWS,), NEG_BIG, F32).at[:N_GROUPS].set(b_rg).at[SUBLANES:].set(b_re.reshape(-1))
    return wr.astype(BF16), jnp.broadcast_to(br[:, None], (ROUTER_ROWS, LANES))


def _strict_upper(t):
    i = jnp.arange(t)
    return (i[:, None] < i[None, :]).astype(BF16)


def _pad_rows(a, rows):
    return jnp.zeros((rows,) + a.shape[1:], a.dtype).at[:a.shape[0]].set(a)


def kernel(x_prompt, x_sample, state_conv, state_ret, p_prompt, p_sample, g_mix, w_in, conv_w, ret_gn, w_o, g_ffn,
           w_router_group, b_router_group, w_router_expert, b_router_expert, w_gate, w_up, w_down, g_ple,
           w_ple_proj, w_ple_gate, g_final):
    bp, tp, _ = x_prompt.shape
    bs = x_sample.shape[0]
    assert x_sample.shape[1] == 1 and g_mix.shape[0] == 1, "one layer, one new token per sample request"
    assert bs <= TOK_TILE and tp % MIX_TILE == 0
    n_p = bp * tp

    tabs = _rope_tables(jnp.arange(tp, dtype=I32)) + _decay_tables(TOK_TILE)
    wr, br = _router_params(w_router_group[0], b_router_group[0], w_router_expert[0], b_router_expert[0])
    win, wo, convw = w_in[0].astype(BF16), w_o[0].astype(BF16), _pad_rows(conv_w[0], SUBLANES)
    tri, ltri = _strict_upper(TOK_TILE), _strict_upper(N_EXPERTS).T

    h_p, ri_p, rw_p, n_p_tab, yl_p, conv_p, ret_p = _mixer_prompt(
        x_prompt, tabs, (g_mix, win, convw, ret_gn, wo, g_ffn, wr, br, tri, ltri))
    h_s, ri_s, rw_s, n_s_tab, yl_s, conv_s, ret_s = _mixer_sample(
        x_sample[:, 0], state_conv[0], state_ret[0], g_mix, win, convw, ret_gn, wo, g_ffn, wr, br, tri, ltri)
    h_p = h_p.reshape(n_p, D_MODEL)

    n_tab = jnp.concatenate([n_p_tab, n_s_tab], axis=0)[:, :, 0].astype(I32)
    counts, tab = _run_table(n_tab)
    n_tiles = n_tab.shape[0]

    xs = _dispatch(tab, yl_p, yl_s)
    ys = _expert_ffn(_work_items(counts, n_tiles * PAIR_ROWS), xs, w_gate[0], w_up[0], w_down[0])

    wpg, wpp = w_ple_gate[0].astype(BF16), w_ple_proj[0].astype(BF16)
    gfin = g_final[None, :]
    p_s = _pad_rows(p_sample[0].reshape(bs, PLE_DIM), TOK_TILE)
    y_p = _combine(tab[:n_tiles - 1], h_p, p_prompt[0].reshape(n_p, PLE_DIM), ri_p, rw_p, ys, g_ple, wpg, wpp, gfin)
    y_s = _combine(tab[n_tiles - 1:], h_s, p_s, ri_s, rw_s, ys, g_ple, wpg, wpp, gfin)
    return (y_p.reshape(bp, tp, D_MODEL), y_s[:bs].reshape(bs, 1, D_MODEL), conv_p[None], ret_p[None],
            conv_s[None], ret_s[None])
```

```python
import functools

import jax
import jax.numpy as jnp
from jax import lax
from jax.experimental import pallas as pl
from jax.experimental.pallas import tpu as pltpu

F32, BF16, I32, U32 = jnp.float32, jnp.bfloat16, jnp.int32, jnp.uint32

D_MODEL = 1024
CONV_DIM = 512
CONV_WIDTH = 3
RET_DIM = 512
RET_HEADS = 4
RET_DK = 128
RET_DV = 128
RET_CHUNK = 128
ROPE_BASE = 10000.0
IN_PROJ_DIM = 3 * CONV_DIM + 4 * RET_DIM
N_GROUPS = 4
EXPERTS_PER_GROUP = 8
N_EXPERTS = 32
D_FF = 512
PLE_DIM = 256
EPS = 1e-6
PAST_LEN = 16384

LANES = 128
SUBLANES = 8
ROW_CHUNKS = D_MODEL // (2 * LANES)
ROUTER_ROWS = SUBLANES + N_EXPERTS
VMEM_LIMIT = 56 * 1024 * 1024
NEG_BIG = -1e30

TOK_TILE = 256
MIX_TILE = 1024
PAIR_ROWS = 2 * TOK_TILE
FFN_TILE = 256
SAMPLE_BLOCK = 8
RUN_BITS = 9
SRC_BITS = 10
TAB_COLS = 384


def _rms(x, g):
    return x * lax.rsqrt(jnp.mean(x * x, axis=-1, keepdims=True) + EPS) * g


def _dot(a, b):
    return jnp.dot(a, b, preferred_element_type=F32)


def _dot_nt(a, b):
    return lax.dot_general(a, b, (((1,), (1,)), ((), ())), preferred_element_type=F32)


def _dot_tn(a, b):
    return lax.dot_general(a, b, (((0,), (0,)), ((), ())), preferred_element_type=F32)


def _rope(x, cos, sin_signed):
    return x * cos + pltpu.roll(x, RET_DK // 2, 1) * sin_signed


def _const_spec(shape):
    nd = len(shape)
    return pl.BlockSpec(shape, lambda *_: (0,) * nd)


def _params(*sem):
    return pltpu.CompilerParams(dimension_semantics=sem, vmem_limit_bytes=VMEM_LIMIT)


def _route(h, gffn, wr, br, tri, ltri):
    t = h.shape[0]
    xn = _rms(h, gffn).astype(BF16)
    lt = _dot_nt(wr, xn) + br[:, 0:1]
    row8 = lax.broadcasted_iota(I32, (SUBLANES, t), 0).astype(F32)
    gl = lt[0:SUBLANES]
    m = jnp.max(gl, axis=0, keepdims=True)
    g_top = 1.0 / jnp.sum(jnp.exp(gl - m), axis=0, keepdims=True)
    gidx = jnp.min(jnp.where(gl == m, row8, float(SUBLANES)), axis=0, keepdims=True)
    e_sel = jnp.where(gidx == 0.0, lt[8:16],
                      jnp.where(gidx == 1.0, lt[16:24], jnp.where(gidx == 2.0, lt[24:32], lt[32:40])))
    m1 = jnp.max(e_sel, axis=0, keepdims=True)
    i1 = jnp.min(jnp.where(e_sel == m1, row8, float(SUBLANES)), axis=0, keepdims=True)
    rest = jnp.where(row8 == i1, -jnp.inf, e_sel)
    m2 = jnp.max(rest, axis=0, keepdims=True)
    i2 = jnp.min(jnp.where(rest == m2, row8, float(SUBLANES)), axis=0, keepdims=True)
    d = jnp.exp(m2 - m1)
    w1 = g_top / (1.0 + d)
    w2 = g_top * d / (1.0 + d)
    e1 = gidx * float(EXPERTS_PER_GROUP) + i1
    e2 = gidx * float(EXPERTS_PER_GROUP) + i2
    row32 = lax.broadcasted_iota(I32, (N_EXPERTS, t), 0).astype(F32)
    a1 = jnp.where(row32 == e1, 1.0, 0.0)
    a2 = jnp.where(row32 == e2, 1.0, 0.0)
    a = a1 + a2
    n = jnp.broadcast_to(jnp.sum(a, axis=1, keepdims=True), (N_EXPERTS, LANES))
    start = _dot(ltri, n.astype(BF16))
    base = _dot(a.astype(BF16), tri) + start[:, 0:1]
    r1 = jnp.sum(a1 * base, axis=0, keepdims=True)
    r2 = jnp.sum(a2 * base, axis=0, keepdims=True)
    ri = jnp.where(row8 == 0.0, e1, jnp.where(row8 == 1.0, e2, jnp.where(row8 == 2.0, r1,
                                                                         jnp.where(row8 == 3.0, r2, 0.0))))
    rw = jnp.where(row8 == 0.0, w1, jnp.where(row8 == 1.0, w2, 0.0))
    rows = lax.broadcasted_iota(I32, (2 * t, t), 0).astype(F32)
    perm = jnp.where((rows == r1) | (rows == r2), 1.0, 0.0).astype(BF16)
    return ri.astype(I32), rw, n, _dot(perm, xn)


def _group_norm_gate(o, gate):
    mu = jnp.mean(o, axis=-1, keepdims=True)
    oc = o - mu
    return oc * lax.rsqrt(jnp.mean(oc * oc, axis=-1, keepdims=True) + EPS) * gate


def _mixer_prompt_kernel(x_ref, cos_ref, sin_ref, qdec_ref, kdec_ref, dmask_ref, cdec_ref, gmix_ref, win_ref,
                         convw_ref, retgn_ref, wo_ref, gffn_ref, wr_ref, br_ref, tri_ref, ltri_ref,
                         h_ref, ri_ref, rw_ref, n_ref, yl_ref, conv_ref, ret_ref,
                         s_scr, z_scr, mix_scr):
    j = pl.program_id(1)
    n_j = pl.num_programs(1)
    tt = x_ref.shape[1]

    @pl.when(j == 0)
    def _():
        s_scr[...] = jnp.zeros_like(s_scr)
        z_scr[0:SUBLANES, :] = jnp.zeros((SUBLANES, CONV_DIM), F32)

    cw = convw_ref[...]
    retgn = retgn_ref[...]
    q0 = 3 * CONV_DIM

    def project(p0):
        x = x_ref[0, p0:p0 + TOK_TILE, :]
        proj = _dot(_rms(x, gmix_ref[...]).astype(BF16), win_ref[...])
        gb = proj[:, 0:CONV_DIM]
        z = proj[:, CONV_DIM:2 * CONV_DIM] * proj[:, 2 * CONV_DIM:3 * CONV_DIM]
        z0 = SUBLANES + p0
        z_scr[z0:z0 + TOK_TILE, :] = z
        yc = cw[0:1] * z_scr[z0 - 2:z0 - 2 + TOK_TILE, :] + cw[1:2] * z_scr[z0 - 1:z0 - 1 + TOK_TILE, :] + cw[2:3] * z
        mix_scr[p0:p0 + TOK_TILE, 0:CONV_DIM] = (gb * yc).astype(BF16)
        return x, proj

    def mix(p0, x, proj):
        cos = cos_ref[p0:p0 + TOK_TILE, :]
        sin = sin_ref[p0:p0 + TOK_TILE, :]
        for hh in range(RET_HEADS):
            l0 = hh * RET_DK
            qr = _rope(proj[:, q0 + l0:q0 + l0 + RET_DK], cos, sin)
            kr = _rope(proj[:, q0 + RET_DIM + l0:q0 + RET_DIM + l0 + RET_DK], cos, sin) * (RET_DK ** -0.5)
            v = proj[:, q0 + 2 * RET_DIM + l0:q0 + 2 * RET_DIM + l0 + RET_DV].astype(BF16)
            g = proj[:, q0 + 3 * RET_DIM + l0:q0 + 3 * RET_DIM + l0 + RET_DV]
            s_old = s_scr[hh]
            scores = _dot_nt(qr.astype(BF16), kr.astype(BF16)) * dmask_ref[hh]
            lhs = jnp.concatenate([scores.astype(BF16), (qr * qdec_ref[:, l0:l0 + RET_DK]).astype(BF16)], axis=1)
            o = _dot(lhs, jnp.concatenate([v, s_old.astype(BF16)], axis=0))
            kd = (kr * kdec_ref[:, l0:l0 + RET_DK]).T.astype(BF16)
            s_scr[hh] = s_old * cdec_ref[hh:hh + 1, :] + _dot(kd, v)
            gate = retgn[:, l0:l0 + RET_DV] * (g * jax.nn.sigmoid(g))
            mix_scr[p0:p0 + TOK_TILE, CONV_DIM + l0:CONV_DIM + l0 + RET_DV] = _group_norm_gate(o, gate).astype(BF16)
        h = x + _dot(mix_scr[p0:p0 + TOK_TILE, :], wo_ref[...])
        h_ref[0, p0:p0 + TOK_TILE, :] = h
        return h

    def route(part, h):
        p0 = part * TOK_TILE
        ri, rw, n, y = _route(h, gffn_ref[...], wr_ref[...], br_ref[...], tri_ref[...], ltri_ref[...])
        ri_ref[:, p0:p0 + TOK_TILE] = ri
        rw_ref[:, p0:p0 + TOK_TILE] = rw
        n_ref[part] = n
        _store_rows(yl_ref.at[pl.ds(part * PAIR_ROWS * ROW_CHUNKS, PAIR_ROWS * ROW_CHUNKS)], y)

    parts = tt // TOK_TILE
    nxt = project(0)
    for part in range(parts):
        h = mix(part * TOK_TILE, *nxt)
        if part + 1 < parts:
            nxt = project((part + 1) * TOK_TILE)
        route(part, h)

    @pl.when(j == n_j - 1)
    def _():
        conv_ref[0] = z_scr[SUBLANES + tt - 2:SUBLANES + tt, :]
        ret_ref[0] = s_scr[...]

    z_scr[0:SUBLANES, :] = z_scr[tt:tt + SUBLANES, :]


def _mixer_prompt(x, tabs, wts):
    bsz, t, _ = x.shape
    tt = MIX_TILE
    n_j = t // tt
    parts = tt // TOK_TILE
    n_tok = bsz * t
    args = (x,) + tuple(tabs) + tuple(wts)
    in_specs = [
        pl.BlockSpec((1, tt, D_MODEL), lambda b, j: (b, j, 0)),
        pl.BlockSpec((tt, LANES), lambda b, j: (j, 0)),
        pl.BlockSpec((tt, LANES), lambda b, j: (j, 0)),
    ] + [_const_spec(a.shape) for a in args[3:]]
    out_shape = (
        jax.ShapeDtypeStruct((bsz, t, D_MODEL), F32),
        jax.ShapeDtypeStruct((SUBLANES, n_tok), I32),
        jax.ShapeDtypeStruct((SUBLANES, n_tok), F32),
        jax.ShapeDtypeStruct((bsz * n_j * parts, N_EXPERTS, LANES), F32),
        jax.ShapeDtypeStruct((bsz * n_j * parts * PAIR_ROWS * ROW_CHUNKS, LANES), U32),
        jax.ShapeDtypeStruct((bsz, CONV_WIDTH - 1, CONV_DIM), F32),
        jax.ShapeDtypeStruct((bsz, RET_HEADS, RET_DK, RET_DV), F32),
    )
    out_specs = (
        pl.BlockSpec((1, tt, D_MODEL), lambda b, j: (b, j, 0)),
        pl.BlockSpec((SUBLANES, tt), lambda b, j: (0, b * n_j + j)),
        pl.BlockSpec((SUBLANES, tt), lambda b, j: (0, b * n_j + j)),
        pl.BlockSpec((parts, N_EXPERTS, LANES), lambda b, j: (b * n_j + j, 0, 0)),
        pl.BlockSpec((parts * PAIR_ROWS * ROW_CHUNKS, LANES), lambda b, j: (b * n_j + j, 0)),
        pl.BlockSpec((1, CONV_WIDTH - 1, CONV_DIM), lambda b, j: (b, 0, 0)),
        pl.BlockSpec((1, RET_HEADS, RET_DK, RET_DV), lambda b, j: (b, 0, 0, 0)),
    )
    return pl.pallas_call(
        _mixer_prompt_kernel,
        grid=(bsz, n_j),
        in_specs=in_specs,
        out_specs=out_specs,
        out_shape=out_shape,
        scratch_shapes=[
            pltpu.VMEM((RET_HEADS, RET_DK, RET_DV), F32),
            pltpu.VMEM((tt + SUBLANES, CONV_DIM), F32),
            pltpu.VMEM((tt, D_MODEL), BF16),
        ],
        compiler_params=_params("arbitrary", "arbitrary"),
        name="mixer_prompt",
    )(*args)


def _sample_proj_kernel(x_ref, sc_ref, cos_ref, sin_ref, gmix_ref, win_ref, convw_ref, retgn_ref,
                        yc_ref, q_ref, k_ref, v_ref, gate_ref, conv_ref):
    x = x_ref[...]
    u = _rms(x, gmix_ref[...]).astype(BF16)
    proj = _dot(u, win_ref[...])
    gb = proj[:, 0:CONV_DIM]
    z = proj[:, CONV_DIM:2 * CONV_DIM] * proj[:, 2 * CONV_DIM:3 * CONV_DIM]
    cw = convw_ref[...]
    buf0, buf1 = sc_ref[:, 0:CONV_DIM], sc_ref[:, CONV_DIM:2 * CONV_DIM]
    yc_ref[...] = gb * (cw[0:1] * buf0 + cw[1:2] * buf1 + cw[2:3] * z)
    conv_ref[:, 0:CONV_DIM] = buf1
    conv_ref[:, CONV_DIM:2 * CONV_DIM] = z
    q0 = 3 * CONV_DIM
    cos, sin = cos_ref[0:1, :], sin_ref[0:1, :]
    retgn = retgn_ref[...]
    for hh in range(RET_HEADS):
        l0 = hh * RET_DK
        q_ref[:, l0:l0 + RET_DK] = _rope(proj[:, q0 + l0:q0 + l0 + RET_DK], cos, sin)
        k_ref[:, l0:l0 + RET_DK] = _rope(proj[:, q0 + RET_DIM + l0:q0 + RET_DIM + l0 + RET_DK], cos, sin) * (RET_DK ** -0.5)
    v_ref[...] = proj[:, q0 + 2 * RET_DIM:q0 + 3 * RET_DIM]
    g = proj[:, q0 + 3 * RET_DIM:q0 + 4 * RET_DIM]
    gate_ref[...] = retgn * (g * jax.nn.sigmoid(g))


def _sample_proj(x, sc, cos, sin, gmix, win, convw, retgn):
    n = x.shape[0]
    outs = (
        jax.ShapeDtypeStruct((n, CONV_DIM), F32), jax.ShapeDtypeStruct((n, RET_DIM), F32),
        jax.ShapeDtypeStruct((n, RET_DIM), F32), jax.ShapeDtypeStruct((n, RET_DIM), F32),
        jax.ShapeDtypeStruct((n, RET_DIM), F32), jax.ShapeDtypeStruct((n, 2 * CONV_DIM), F32),
    )
    args = (x, sc, cos, sin, gmix, win, convw, retgn)
    return pl.pallas_call(
        _sample_proj_kernel,
        grid=(1,),
        in_specs=[_const_spec(a.shape) for a in args],
        out_specs=tuple(_const_spec(o.shape) for o in outs),
        out_shape=outs,
        compiler_params=_params("arbitrary"),
        name="sample_proj",
    )(*args)


def _sample_state_kernel(s_ref, q_ref, k_ref, kt_ref, v_ref, dec_ref, snew_ref, o_ref):
    for r in range(s_ref.shape[0]):
        for hh in range(RET_HEADS):
            l0 = hh * RET_DV
            s = s_ref[r, hh]
            q = q_ref[r:r + 1, l0:l0 + RET_DK]
            k = k_ref[r:r + 1, l0:l0 + RET_DK]
            kc = kt_ref[0, hh, :, r:r + 1]
            vr = v_ref[r:r + 1, l0:l0 + RET_DV]
            qdec = dec_ref[hh:hh + 1, :]
            cdec = dec_ref[RET_HEADS + hh:RET_HEADS + hh + 1, :]
            qk = jnp.sum(q * k, axis=1, keepdims=True)
            o_ref[r:r + 1, l0:l0 + RET_DV] = qk * vr + _dot((q * qdec).astype(BF16), s.astype(BF16))
            snew_ref[r, hh] = s * cdec + kc * vr


def _sample_state(s, q, k, kt, v, dec):
    n = s.shape[0]
    nb = SAMPLE_BLOCK
    return pl.pallas_call(
        _sample_state_kernel,
        grid=(n // nb,),
        in_specs=[
            pl.BlockSpec((nb, RET_HEADS, RET_DK, RET_DV), lambda i: (i, 0, 0, 0)),
            pl.BlockSpec((nb, RET_DIM), lambda i: (i, 0)),
            pl.BlockSpec((nb, RET_DIM), lambda i: (i, 0)),
            pl.BlockSpec((1, RET_HEADS, RET_DK, nb), lambda i: (i, 0, 0, 0)),
            pl.BlockSpec((nb, RET_DIM), lambda i: (i, 0)),
            _const_spec(dec.shape),
        ],
        out_specs=(
            pl.BlockSpec((nb, RET_HEADS, RET_DK, RET_DV), lambda i: (i, 0, 0, 0)),
            pl.BlockSpec((nb, RET_DIM), lambda i: (i, 0)),
        ),
        out_shape=(jax.ShapeDtypeStruct(s.shape, F32), jax.ShapeDtypeStruct((n, RET_DIM), F32)),
        compiler_params=_params("arbitrary"),
        name="sample_state",
    )(s, q, k, kt, v, dec)


def _sample_out_kernel(x_ref, yc_ref, o_ref, gate_ref, wo_ref, gffn_ref, wr_ref, br_ref, tri_ref, ltri_ref,
                       h_ref, ri_ref, rw_ref, n_ref, yl_ref):
    n = x_ref.shape[0]
    parts = [yc_ref[...].astype(BF16)]
    for hh in range(RET_HEADS):
        l0 = hh * RET_DV
        parts.append(_group_norm_gate(o_ref[:, l0:l0 + RET_DV], gate_ref[:, l0:l0 + RET_DV]).astype(BF16))
    h_ref[0:n, :] = x_ref[...] + _dot(jnp.concatenate(parts, axis=-1), wo_ref[...])
    if n < TOK_TILE:
        h_ref[n:TOK_TILE, :] = jnp.zeros((TOK_TILE - n, D_MODEL), F32)
    ri, rw, cnt, y = _route(h_ref[...], gffn_ref[...], wr_ref[...], br_ref[...], tri_ref[...], ltri_ref[...])
    ri_ref[...] = ri
    rw_ref[...] = rw
    n_ref[0] = cnt
    _store_rows(yl_ref, y)


def _sample_out(x, yc, o, gate, wo, gffn, wr, br, tri, ltri):
    outs = (
        jax.ShapeDtypeStruct((TOK_TILE, D_MODEL), F32), jax.ShapeDtypeStruct((SUBLANES, TOK_TILE), I32),
        jax.ShapeDtypeStruct((SUBLANES, TOK_TILE), F32), jax.ShapeDtypeStruct((1, N_EXPERTS, LANES), F32),
        jax.ShapeDtypeStruct((PAIR_ROWS * ROW_CHUNKS, LANES), U32),
    )
    args = (x, yc, o, gate, wo, gffn, wr, br, tri, ltri)
    return pl.pallas_call(
        _sample_out_kernel,
        grid=(1,),
        in_specs=[_const_spec(a.shape) for a in args],
        out_specs=tuple(_const_spec(o_.shape) for o_ in outs),
        out_shape=outs,
        compiler_params=_params("arbitrary"),
        name="sample_out",
    )(*args)


def _mixer_sample(x, state_conv, state_ret, gmix, win, convw, retgn, wo, gffn, wr, br, tri, ltri):
    n = x.shape[0]
    lg = _log_gamma()
    cos, sin = _rope_tables(jnp.full((SUBLANES,), PAST_LEN, I32))
    yc, q, k, v, gate, conv_new = _sample_proj(x, state_conv.reshape(n, 2 * CONV_DIM), cos, sin, gmix, win, convw, retgn)
    nb = SAMPLE_BLOCK
    cols = lambda a: a.reshape(n // nb, nb, RET_HEADS, RET_DK).transpose(0, 2, 3, 1)
    step = jnp.exp(lg[:, None] * 1.0)
    dec = jnp.broadcast_to(jnp.concatenate([step, step], axis=0), (2 * RET_HEADS, LANES))
    s_new, o = _sample_state(state_ret, q, k, cols(k), v, dec)
    h, ri, rw, cnt, yl = _sample_out(x, yc, o, gate, wo, gffn, wr, br, tri, ltri)
    return h, ri, rw, cnt, yl, conv_new.reshape(n, 2, CONV_DIM), s_new


def _load_rows(ref, n_rows):
    words = [ref[pl.ds(c, n_rows, stride=ROW_CHUNKS), :] for c in range(ROW_CHUNKS)]
    half = lambda i: [pltpu.unpack_elementwise(w, index=i, packed_dtype=BF16, unpacked_dtype=F32) for w in words]
    return jnp.concatenate(half(0) + half(1), axis=-1)


def _store_rows(ref, val):
    half = D_MODEL // 2
    for c in range(ROW_CHUNKS):
        pair = [val[:, c * LANES:(c + 1) * LANES], val[:, half + c * LANES:half + (c + 1) * LANES]]
        ref[pl.ds(c, val.shape[0], stride=ROW_CHUNKS), :] = pltpu.pack_elementwise(pair, packed_dtype=BF16)


def _row_slice(ref, row, rows):
    return ref.at[pl.ds(pl.multiple_of(row * ROW_CHUNKS, ROW_CHUNKS), rows * ROW_CHUNKS)]


def _run_copies(idx_smem, slot, make_copy):
    for b in range(RUN_BITS):
        def body(j, carry, b=b):
            entry = idx_smem[slot, N_EXPERTS + N_EXPERTS * b + j]
            make_copy(entry & ((1 << SRC_BITS) - 1), entry >> SRC_BITS, 1 << b).start()
            return carry

        lax.fori_loop(0, idx_smem[slot, b], body, 0)


def _wait_tile_rows(hbm_ref, vmem_ref, sem):
    pltpu.make_async_copy(_row_slice(hbm_ref, 0, PAIR_ROWS), vmem_ref, sem).wait()


def _pair_rows_iota():
    return lax.broadcasted_iota(I32, (PAIR_ROWS, TOK_TILE), 0)


def _dispatch_kernel(n_p, tab_ref, ylp_ref, yls_ref, xs_ref, idx_smem, y_scr, isem, lsem, rsem):
    i, n = pl.program_id(0), pl.num_programs(0)
    slot = i % 2
    tile_rows = PAIR_ROWS * ROW_CHUNKS

    def idx_copy(step, sl):
        return pltpu.make_async_copy(tab_ref.at[step], idx_smem.at[sl], isem.at[sl])

    def load(step):
        ring = step % 3
        p = pltpu.make_async_copy(ylp_ref.at[pl.ds(jnp.minimum(step, n_p - 1) * tile_rows, tile_rows)],
                                  y_scr.at[ring], lsem.at[ring])
        s = pltpu.make_async_copy(yls_ref.at[pl.ds(jnp.maximum(step - n_p, 0) * tile_rows, tile_rows)],
                                  y_scr.at[ring], lsem.at[ring])
        return p, s

    def start_load(step):
        p, s = load(step)

        @pl.when(step < n_p)
        def _():
            p.start()

        @pl.when(step >= n_p)
        def _():
            s.start()

    def wait_runs(ring):
        pltpu.make_async_copy(y_scr.at[ring], _row_slice(xs_ref, 0, PAIR_ROWS), rsem.at[ring]).wait()

    @pl.when(i == 0)
    def _():
        idx_copy(0, 0).start()
        start_load(0)

    @pl.when(i >= 2)
    def _():
        wait_runs((i + 1) % 3)

    @pl.when(i + 1 < n)
    def _():
        idx_copy(i + 1, 1 - slot).start()
        start_load(i + 1)

    load(i)[0].wait()
    idx_copy(i, slot).wait()
    ring = i % 3
    _run_copies(idx_smem, slot, lambda src, dst, rows_: pltpu.make_async_copy(
        _row_slice(y_scr.at[ring], src, rows_), _row_slice(xs_ref, dst, rows_), rsem.at[ring]))

    @pl.when(i == n - 1)
    def _():
        wait_runs(ring)

    @pl.when((i == n - 1) & (n >= 2))
    def _():
        wait_runs((i + 2) % 3)


def _dispatch(tab, yl_p, yl_s):
    n_p = yl_p.shape[0] // (PAIR_ROWS * ROW_CHUNKS)
    n = tab.shape[0]
    return pl.pallas_call(
        functools.partial(_dispatch_kernel, n_p),
        grid=(n,),
        in_specs=[pl.BlockSpec(memory_space=pl.ANY)] * 3,
        out_specs=pl.BlockSpec(memory_space=pl.ANY),
        out_shape=jax.ShapeDtypeStruct((n * PAIR_ROWS * ROW_CHUNKS, LANES), U32),
        scratch_shapes=[
            pltpu.SMEM((2, TAB_COLS), I32),
            pltpu.VMEM((3, PAIR_ROWS * ROW_CHUNKS, LANES), U32),
            pltpu.SemaphoreType.DMA((2,)),
            pltpu.SemaphoreType.DMA((3,)),
            pltpu.SemaphoreType.DMA((3,)),
        ],
        compiler_params=_params("arbitrary"),
        name="moe_dispatch",
    )(tab, yl_p, yl_s)


def _ffn_kernel(tile_ref, exp_ref, lo_ref, hi_ref, nxt_ref, slot_ref, x_ref, wg_ref, wu_ref, wd_ref, o_ref,
                wg_buf, wu_buf, wd_buf, wgu_scr, wd_scr, y_scr, wsem):
    w = pl.program_id(0)
    prev = jnp.maximum(w - 1, 0)
    lo, hi = lo_ref[w], hi_ref[w]
    e, slot = exp_ref[w], slot_ref[w]
    first = (w == 0) | (tile_ref[w] != tile_ref[prev])

    def weight_copies(expert, sl):
        return [pltpu.make_async_copy(src.at[expert], buf.at[sl], wsem.at[sl])
                for src, buf in ((wg_ref, wg_buf), (wu_ref, wu_buf), (wd_ref, wd_buf))]

    @pl.when(w == 0)
    def _():
        for cp in weight_copies(e, slot):
            cp.start(priority=1)

    @pl.when((w == 0) | (e != exp_ref[prev]))
    def _():
        for cp in weight_copies(e, slot):
            cp.wait()

        @pl.when(nxt_ref[w] != e)
        def _():
            for cp in weight_copies(nxt_ref[w], 1 - slot):
                cp.start(priority=1)

        wgu_scr[:, 0:D_FF] = wg_buf[slot].astype(BF16)
        wgu_scr[:, D_FF:2 * D_FF] = wu_buf[slot].astype(BF16)
        wd_scr[...] = wd_buf[slot].astype(BF16)

    @pl.when(w == 0)
    def _():
        y_scr[...] = jnp.zeros_like(y_scr)

    @pl.when(hi > lo)
    def _():
        gu = _dot(_load_rows(x_ref, FFN_TILE).astype(BF16), wgu_scr[...])
        g, u = gu[:, 0:D_FF], gu[:, D_FF:2 * D_FF]
        y = _dot((g * jax.nn.sigmoid(g) * u).astype(BF16), wd_scr[...])
        row = lax.broadcasted_iota(I32, (FFN_TILE, 1), 0)
        y = jnp.where(((row >= lo) & (row < hi)) | first, y, y_scr[...])
        y_scr[...] = y
        _store_rows(o_ref, y)


def _expert_ffn(items, xs, w_gate, w_up, w_down):
    n_items = items[0].shape[0]
    row_spec = pl.BlockSpec((FFN_TILE * ROW_CHUNKS, LANES), lambda w, t, *_: (t[w], 0))
    grid_spec = pltpu.PrefetchScalarGridSpec(
        num_scalar_prefetch=len(items),
        grid=(n_items,),
        in_specs=[row_spec] + [pl.BlockSpec(memory_space=pl.ANY)] * 3,
        out_specs=row_spec,
        scratch_shapes=[
            pltpu.VMEM((2, D_MODEL, D_FF), F32), pltpu.VMEM((2, D_MODEL, D_FF), F32),
            pltpu.VMEM((2, D_FF, D_MODEL), F32),
            pltpu.VMEM((D_MODEL, 2 * D_FF), BF16), pltpu.VMEM((D_FF, D_MODEL), BF16),
            pltpu.VMEM((FFN_TILE, D_MODEL), F32),
            pltpu.SemaphoreType.DMA((2,)),
        ],
    )
    return pl.pallas_call(
        _ffn_kernel,
        grid_spec=grid_spec,
        out_shape=jax.ShapeDtypeStruct(xs.shape, U32),
        compiler_params=_params("arbitrary"),
        name="moe_ffn",
    )(*items, xs, w_gate, w_up, w_down)


def _work_items(counts, n_rows):
    n_tiles = n_rows // FFN_TILE
    n_items = n_tiles + N_EXPERTS - 1
    off = jnp.cumsum(counts) - counts
    first_tile = off // FFN_TILE
    last_tile = jnp.maximum(off + counts - 1, off) // FFN_TILE
    n_e = jnp.where(counts > 0, last_tile - first_tile + 1, 0)
    start = jnp.cumsum(n_e) - n_e
    total = jnp.sum(n_e)
    w = jnp.minimum(jnp.arange(n_items, dtype=I32), total - 1)
    ids = jnp.arange(N_EXPERTS, dtype=I32)[None, :]
    e = jnp.max(jnp.where((start[None, :] <= w[:, None]) & (n_e[None, :] > 0), ids, 0), axis=1)
    pick = lambda a: jnp.sum(jnp.where(ids == e[:, None], a[None, :], 0), axis=1)
    tile = pick(first_tile) + (w - pick(start))
    lo = jnp.clip(pick(off) - tile * FFN_TILE, 0, FFN_TILE)
    hi = jnp.clip(pick(off + counts) - tile * FFN_TILE, 0, FFN_TILE)
    hi = jnp.where(jnp.arange(n_items) < total, hi, lo)
    used = n_e[None, :] > 0
    nxt = jnp.min(jnp.where(used & (ids > e[:, None]), ids, N_EXPERTS), axis=1)
    nxt = jnp.where(nxt == N_EXPERTS, e, nxt)
    slot = jnp.sum(jnp.where(used & (ids < e[:, None]), 1, 0), axis=1) % 2
    return tuple(a.astype(I32) for a in (tile, e, lo, hi, nxt, slot))


def _run_table(n):
    tiles = n.shape[0]
    counts = jnp.sum(n, axis=0)
    goff = jnp.cumsum(counts) - counts
    dst0 = goff[None, :] + jnp.cumsum(n, axis=0) - n
    src0 = jnp.cumsum(n, axis=1) - n
    bits = jnp.arange(RUN_BITS, dtype=I32)[:, None, None]
    flag = (n[None] >> bits) & 1
    done = (n[None] >> (bits + 1)) << (bits + 1)
    entry = ((dst0[None] + done) << SRC_BITS) | (src0[None] + done)
    rank = jnp.cumsum(flag, axis=2) - flag
    slot = jnp.arange(N_EXPERTS, dtype=I32)
    sel = (flag[..., None] == 1) & (rank[..., None] == slot)
    packed = jnp.sum(jnp.where(sel, entry[..., None], 0), axis=2)
    m = jnp.sum(flag, axis=2).T
    tab = jnp.zeros((tiles, TAB_COLS), I32)
    tab = tab.at[:, :RUN_BITS].set(m)
    tab = tab.at[:, N_EXPERTS:N_EXPERTS * (RUN_BITS + 1)].set(packed.transpose(1, 0, 2).reshape(tiles, -1))
    return counts, tab


def _combine_kernel(tab_ref, h_ref, p_ref, ri_ref, rw_ref, ys_ref, gple_ref, wpg_ref, wpp_ref, gfin_ref, o_ref,
                    idx_smem, rows_scr, isem, rsem):
    i, n = pl.program_id(0), pl.num_programs(0)
    slot = i % 2

    def idx_copy(step, sl):
        return pltpu.make_async_copy(tab_ref.at[step], idx_smem.at[sl], isem.at[sl])

    def gather(sl):
        _run_copies(idx_smem, sl, lambda src, dst, rows_: pltpu.make_async_copy(
            _row_slice(ys_ref, dst, rows_), _row_slice(rows_scr.at[sl], src, rows_), rsem.at[sl]))

    @pl.when(i == 0)
    def _():
        idx_copy(0, 0).start()
        idx_copy(0, 0).wait()
        gather(0)

    @pl.when((i == 0) & (n >= 2))
    def _():
        idx_copy(1, 1).start()

    @pl.when(i + 1 < n)
    def _():
        idx_copy(i + 1, 1 - slot).wait()
        gather(1 - slot)

    @pl.when(i + 2 < n)
    def _():
        idx_copy(i + 2, slot).start()

    _wait_tile_rows(ys_ref, rows_scr.at[slot], rsem.at[slot])

    ri, rw = ri_ref[...], rw_ref[...]
    rows = _pair_rows_iota()
    wperm = jnp.where(rows == ri[2:3], rw[0:1], 0.0) + jnp.where(rows == ri[3:4], rw[1:2], 0.0)
    s = _load_rows(rows_scr.at[slot], PAIR_ROWS).astype(BF16)
    h = h_ref[...] + _dot_tn(wperm.astype(BF16), s)
    gate = jax.nn.sigmoid(_dot(_rms(h, gple_ref[...]).astype(BF16), wpg_ref[...]))
    h = h + _dot(p_ref[...].astype(BF16), wpp_ref[...]) * gate
    o_ref[...] = _rms(h, gfin_ref[...])


def _combine(tab, h, p, ri, rw, ys, gple, wpg, wpp, gfin):
    n = tab.shape[0]
    n_tok = n * TOK_TILE
    return pl.pallas_call(
        _combine_kernel,
        grid=(n,),
        in_specs=[
            pl.BlockSpec(memory_space=pl.ANY),
            pl.BlockSpec((TOK_TILE, D_MODEL), lambda i: (i, 0)),
            pl.BlockSpec((TOK_TILE, PLE_DIM), lambda i: (i, 0)),
            pl.BlockSpec((SUBLANES, TOK_TILE), lambda i: (0, i)),
            pl.BlockSpec((SUBLANES, TOK_TILE), lambda i: (0, i)),
            pl.BlockSpec(memory_space=pl.ANY),
            _const_spec(gple.shape), _const_spec(wpg.shape), _const_spec(wpp.shape), _const_spec(gfin.shape),
        ],
        out_specs=pl.BlockSpec((TOK_TILE, D_MODEL), lambda i: (i, 0)),
        out_shape=jax.ShapeDtypeStruct((n_tok, D_MODEL), F32),
        scratch_shapes=[
            pltpu.SMEM((2, TAB_COLS), I32),
            pltpu.VMEM((2, PAIR_ROWS * ROW_CHUNKS, LANES), U32),
            pltpu.SemaphoreType.DMA((2,)),
            pltpu.SemaphoreType.DMA((2,)),
        ],
        compiler_params=_params("arbitrary"),
        name="moe_combine",
    )(tab, h, p, ri, rw, ys, gple, wpg, wpp, gfin)


def _log_gamma():
    return jnp.log(1.0 - 2.0 ** (-5.0 - jnp.arange(RET_HEADS, dtype=F32)))


def _rope_tables(pos):
    inv = 1.0 / (ROPE_BASE ** (jnp.arange(0, RET_DK, 2, dtype=F32) / RET_DK))
    ang = pos.astype(F32)[:, None] * inv[None, :]
    cos, sin = jnp.cos(ang), jnp.sin(ang)
    return jnp.concatenate([cos, cos], axis=-1), jnp.concatenate([-sin, sin], axis=-1)


def _decay_tables(c):
    lg = _log_gamma()
    idx = jnp.arange(c, dtype=F32)
    diff = idx[:, None] - idx[None, :]
    dmask = jnp.where((diff >= 0.0)[None], jnp.exp(lg[:, None, None] * jnp.maximum(diff, 0.0)[None]), 0.0)
    q_dec = jnp.exp(lg[None, :] * (idx[:, None] + 1.0))
    k_dec = jnp.exp(lg[None, :] * (c - 1.0 - idx[:, None]))
    c_dec = jnp.exp(lg * c)
    lanes = lambda a: jnp.repeat(a, RET_DK, axis=1)
    cdec = jnp.zeros((SUBLANES, LANES), F32).at[:RET_HEADS].set(jnp.broadcast_to(c_dec[:, None], (RET_HEADS, LANES)))
    return lanes(q_dec), lanes(k_dec), dmask, cdec


def _router_params(w_rg, b_rg, w_re, b_re):
    wr = jnp.zeros((ROUTER_ROWS, D_MODEL), F32).at[:N_GROUPS].set(w_rg.T).at[SUBLANES:].set(w_re.T)
    br = jnp.full((ROUTER_ROWS,), NEG_BIG, F32).at[:N_GROUPS].set(b_rg).at[SUBLANES:].set(b_re.reshape(-1))
    return wr.astype(BF16), jnp.broadcast_to(br[:, None], (ROUTER_ROWS, LANES))


def _strict_upper(t):
    i = jnp.arange(t)
    return (i[:, None] < i[None, :]).astype(BF16)


def _pad_rows(a, rows):
    return jnp.zeros((rows,) + a.shape[1:], a.dtype).at[:a.shape[0]].set(a)


def kernel(x_prompt, x_sample, state_conv, state_ret, p_prompt, p_sample, g_mix, w_in, conv_w, ret_gn, w_o, g_ffn,
           w_router_group, b_router_group, w_router_expert, b_router_expert, w_gate, w_up, w_down, g_ple,
           w_ple_proj, w_ple_gate, g_final):
    bp, tp, _ = x_prompt.shape
    bs = x_sample.shape[0]
    assert x_sample.shape[1] == 1 and g_mix.shape[0] == 1, "one layer, one new token per sample request"
    assert bs <= TOK_TILE and tp % MIX_TILE == 0
    n_p = bp * tp

    tabs = _rope_tables(jnp.arange(tp, dtype=I32)) + _decay_tables(TOK_TILE)
    wr, br = _router_params(w_router_group[0], b_router_group[0], w_router_expert[0], b_router_expert[0])
    win, wo, convw = w_in[0].astype(BF16), w_o[0].astype(BF16), _pad_rows(conv_w[0], SUBLANES)
    tri, ltri = _strict_upper(TOK_TILE), _strict_upper(N_EXPERTS).T

    h_p, ri_p, rw_p, n_p_tab, yl_p, conv_p, ret_p = _mixer_prompt(
        x_prompt, tabs, (g_mix, win, convw, ret_gn, wo, g_ffn, wr, br, tri, ltri))
    h_s, ri_s, rw_s, n_s_tab, yl_s, conv_s, ret_s = _mixer_sample(
        x_sample[:, 0], state_conv[0], state_ret[0], g_mix, win, convw, ret_gn, wo, g_ffn, wr, br, tri, ltri)
    h_p = h_p.reshape(n_p, D_MODEL)

    n_tab = jnp.concatenate([n_p_tab, n_s_tab], axis=0)[:, :, 0].astype(I32)
    counts, tab = _run_table(n_tab)
    n_tiles = n_tab.shape[0]

    xs = _dispatch(tab, yl_p, yl_s)
    ys = _expert_ffn(_work_items(counts, n_tiles * PAIR_ROWS), xs, w_gate[0], w_up[0], w_down[0])

    wpg, wpp = w_ple_gate[0].astype(BF16), w_ple_proj[0].astype(BF16)
    gfin = g_final[None, :]
    p_s = _pad_rows(p_sample[0].reshape(bs, PLE_DIM), TOK_TILE)
    y_p = _combine(tab[:n_tiles - 1], h_p, p_prompt[0].reshape(n_p, PLE_DIM), ri_p, rw_p, ys, g_ple, wpg, wpp, gfin)
    y_s = _combine(tab[n_tiles - 1:], h_s, p_s, ri_s, rw_s, ys, g_ple, wpg, wpp, gfin)
    return (y_p.reshape(bp, tp, D_MODEL), y_s[:bs].reshape(bs, 1, D_MODEL), conv_p[None], ret_p[None],
            conv_s[None], ret_s[None])
```

```python
import functools

import jax
import jax.numpy as jnp
import numpy as np
from jax import lax
from jax.experimental import pallas as pl
from jax.experimental.pallas import tpu as pltpu

F32, BF16, I32, U32 = jnp.float32, jnp.bfloat16, jnp.int32, jnp.uint32

D_MODEL = 1024
CONV_DIM = 512
CONV_WIDTH = 3
RET_DIM = 512
RET_HEADS = 4
RET_DK = 128
RET_DV = 128
RET_CHUNK = 128
ROPE_BASE = 10000.0
IN_PROJ_DIM = 3 * CONV_DIM + 4 * RET_DIM
N_GROUPS = 4
EXPERTS_PER_GROUP = 8
N_EXPERTS = 32
D_FF = 512
PLE_DIM = 256
EPS = 1e-6
PAST_LEN = 16384

LANES = 128
SUBLANES = 8
ROW_CHUNKS = D_MODEL // (2 * LANES)
ROUTER_ROWS = SUBLANES + N_EXPERTS
VMEM_LIMIT = 56 * 1024 * 1024
NEG_BIG = -1e30

TOK_TILE = 256
MIX_TILE = 1024
PAIR_ROWS = 2 * TOK_TILE
FFN_TILE = 256
SAMPLE_BLOCK = 8
RUN_BITS = 9
SRC_BITS = 10
TAB_COLS = 384


def _rms(x, g):
    return x * lax.rsqrt(jnp.mean(x * x, axis=-1, keepdims=True) + EPS) * g


def _dot(a, b):
    return jnp.dot(a, b, preferred_element_type=F32)


def _dot_nt(a, b):
    return lax.dot_general(a, b, (((1,), (1,)), ((), ())), preferred_element_type=F32)


def _dot_tn(a, b):
    return lax.dot_general(a, b, (((0,), (0,)), ((), ())), preferred_element_type=F32)


def _rope(x, cos, sin_signed):
    return x * cos + pltpu.roll(x, RET_DK // 2, 1) * sin_signed


def _const_spec(shape):
    nd = len(shape)
    return pl.BlockSpec(shape, lambda *_: (0,) * nd)


def _params(*sem):
    return pltpu.CompilerParams(dimension_semantics=sem, vmem_limit_bytes=VMEM_LIMIT)


def _route(h, gffn, wr, br, tri, ltri):
    t = h.shape[0]
    xn = _rms(h, gffn).astype(BF16)
    lt = _dot_nt(wr, xn) + br[:, 0:1]
    row8 = lax.broadcasted_iota(I32, (SUBLANES, t), 0).astype(F32)
    gl = lt[0:SUBLANES]
    m = jnp.max(gl, axis=0, keepdims=True)
    g_top = 1.0 / jnp.sum(jnp.exp(gl - m), axis=0, keepdims=True)
    gidx = jnp.min(jnp.where(gl == m, row8, float(SUBLANES)), axis=0, keepdims=True)
    e_sel = jnp.where(gidx == 0.0, lt[8:16],
                      jnp.where(gidx == 1.0, lt[16:24], jnp.where(gidx == 2.0, lt[24:32], lt[32:40])))
    m1 = jnp.max(e_sel, axis=0, keepdims=True)
    i1 = jnp.min(jnp.where(e_sel == m1, row8, float(SUBLANES)), axis=0, keepdims=True)
    rest = jnp.where(row8 == i1, -jnp.inf, e_sel)
    m2 = jnp.max(rest, axis=0, keepdims=True)
    i2 = jnp.min(jnp.where(rest == m2, row8, float(SUBLANES)), axis=0, keepdims=True)
    d = jnp.exp(m2 - m1)
    w1 = g_top / (1.0 + d)
    w2 = g_top * d / (1.0 + d)
    e1 = gidx * float(EXPERTS_PER_GROUP) + i1
    e2 = gidx * float(EXPERTS_PER_GROUP) + i2
    row32 = lax.broadcasted_iota(I32, (N_EXPERTS, t), 0).astype(F32)
    a1 = jnp.where(row32 == e1, 1.0, 0.0)
    a2 = jnp.where(row32 == e2, 1.0, 0.0)
    a = a1 + a2
    n = jnp.broadcast_to(jnp.sum(a, axis=1, keepdims=True), (N_EXPERTS, LANES))
    start = _dot(ltri, n.astype(BF16))
    base = _dot(a.astype(BF16), tri) + start[:, 0:1]
    r1 = jnp.sum(a1 * base, axis=0, keepdims=True)
    r2 = jnp.sum(a2 * base, axis=0, keepdims=True)
    ri = jnp.where(row8 == 0.0, e1, jnp.where(row8 == 1.0, e2, jnp.where(row8 == 2.0, r1,
                                                                         jnp.where(row8 == 3.0, r2, 0.0))))
    rw = jnp.where(row8 == 0.0, w1, jnp.where(row8 == 1.0, w2, 0.0))
    return ri.astype(I32), rw, n


def _group_norm_gate(o, gate):
    mu = jnp.mean(o, axis=-1, keepdims=True)
    oc = o - mu
    return oc * lax.rsqrt(jnp.mean(oc * oc, axis=-1, keepdims=True) + EPS) * gate


def _mixer_prompt_kernel(x_ref, cos_ref, sin_ref, qdec_ref, kdec_ref, dmask_ref, cdec_ref, gmix_ref, win_ref,
                         convw_ref, retgn_ref, wo_ref, gffn_ref, wr_ref, br_ref, tri_ref, ltri_ref,
                         h_ref, ri_ref, rw_ref, n_ref, conv_ref, ret_ref,
                         s_scr, z_scr, mix_scr):
    j = pl.program_id(1)
    n_j = pl.num_programs(1)
    tt = x_ref.shape[1]

    @pl.when(j == 0)
    def _():
        s_scr[...] = jnp.zeros_like(s_scr)
        z_scr[0:SUBLANES, :] = jnp.zeros((SUBLANES, CONV_DIM), F32)

    cw = convw_ref[...]
    retgn = retgn_ref[...]
    q0 = 3 * CONV_DIM

    def project(p0):
        x = x_ref[0, p0:p0 + TOK_TILE, :]
        proj = _dot(_rms(x, gmix_ref[...]).astype(BF16), win_ref[...])
        gb = proj[:, 0:CONV_DIM]
        z = proj[:, CONV_DIM:2 * CONV_DIM] * proj[:, 2 * CONV_DIM:3 * CONV_DIM]
        z0 = SUBLANES + p0
        z_scr[z0:z0 + TOK_TILE, :] = z
        yc = cw[0:1] * z_scr[z0 - 2:z0 - 2 + TOK_TILE, :] + cw[1:2] * z_scr[z0 - 1:z0 - 1 + TOK_TILE, :] + cw[2:3] * z
        mix_scr[p0:p0 + TOK_TILE, 0:CONV_DIM] = (gb * yc).astype(BF16)
        return x, proj

    def mix(p0, x, proj):
        cos = cos_ref[p0:p0 + TOK_TILE, :]
        sin = sin_ref[p0:p0 + TOK_TILE, :]
        for hh in range(RET_HEADS):
            l0 = hh * RET_DK
            qr = _rope(proj[:, q0 + l0:q0 + l0 + RET_DK], cos, sin)
            kr = _rope(proj[:, q0 + RET_DIM + l0:q0 + RET_DIM + l0 + RET_DK], cos, sin) * (RET_DK ** -0.5)
            v = proj[:, q0 + 2 * RET_DIM + l0:q0 + 2 * RET_DIM + l0 + RET_DV].astype(BF16)
            g = proj[:, q0 + 3 * RET_DIM + l0:q0 + 3 * RET_DIM + l0 + RET_DV]
            s_old = s_scr[hh]
            scores = _dot_nt(qr.astype(BF16), kr.astype(BF16)) * dmask_ref[hh]
            lhs = jnp.concatenate([scores.astype(BF16), (qr * qdec_ref[:, l0:l0 + RET_DK]).astype(BF16)], axis=1)
            o = _dot(lhs, jnp.concatenate([v, s_old.astype(BF16)], axis=0))
            kd = (kr * kdec_ref[:, l0:l0 + RET_DK]).T.astype(BF16)
            s_scr[hh] = s_old * cdec_ref[hh:hh + 1, :] + _dot(kd, v)
            gate = retgn[:, l0:l0 + RET_DV] * (g * jax.nn.sigmoid(g))
            mix_scr[p0:p0 + TOK_TILE, CONV_DIM + l0:CONV_DIM + l0 + RET_DV] = _group_norm_gate(o, gate).astype(BF16)
        h = x + _dot(mix_scr[p0:p0 + TOK_TILE, :], wo_ref[...])
        h_ref[0, p0:p0 + TOK_TILE, :] = h
        return h

    def route(part, h):
        p0 = part * TOK_TILE
        ri, rw, n = _route(h, gffn_ref[...], wr_ref[...], br_ref[...], tri_ref[...], ltri_ref[...])
        ri_ref[:, p0:p0 + TOK_TILE] = ri
        rw_ref[:, p0:p0 + TOK_TILE] = rw
        n_ref[part] = n

    parts = tt // TOK_TILE
    nxt = project(0)
    for part in range(parts):
        h = mix(part * TOK_TILE, *nxt)
        if part + 1 < parts:
            nxt = project((part + 1) * TOK_TILE)
        route(part, h)

    @pl.when(j == n_j - 1)
    def _():
        conv_ref[0] = z_scr[SUBLANES + tt - 2:SUBLANES + tt, :]
        ret_ref[0] = s_scr[...]

    z_scr[0:SUBLANES, :] = z_scr[tt:tt + SUBLANES, :]


def _mixer_prompt(x, tabs, wts):
    bsz, t, _ = x.shape
    tt = MIX_TILE
    n_j = t // tt
    parts = tt // TOK_TILE
    n_tok = bsz * t
    args = (x,) + tuple(tabs) + tuple(wts)
    in_specs = [
        pl.BlockSpec((1, tt, D_MODEL), lambda b, j: (b, j, 0)),
        pl.BlockSpec((tt, LANES), lambda b, j: (j, 0)),
        pl.BlockSpec((tt, LANES), lambda b, j: (j, 0)),
    ] + [_const_spec(a.shape) for a in args[3:]]
    out_shape = (
        jax.ShapeDtypeStruct((bsz, t, D_MODEL), F32),
        jax.ShapeDtypeStruct((SUBLANES, n_tok), I32),
        jax.ShapeDtypeStruct((SUBLANES, n_tok), F32),
        jax.ShapeDtypeStruct((bsz * n_j * parts, N_EXPERTS, LANES), F32),
        jax.ShapeDtypeStruct((bsz, CONV_WIDTH - 1, CONV_DIM), F32),
        jax.ShapeDtypeStruct((bsz, RET_HEADS, RET_DK, RET_DV), F32),
    )
    out_specs = (
        pl.BlockSpec((1, tt, D_MODEL), lambda b, j: (b, j, 0)),
        pl.BlockSpec((SUBLANES, tt), lambda b, j: (0, b * n_j + j)),
        pl.BlockSpec((SUBLANES, tt), lambda b, j: (0, b * n_j + j)),
        pl.BlockSpec((parts, N_EXPERTS, LANES), lambda b, j: (b * n_j + j, 0, 0)),
        pl.BlockSpec((1, CONV_WIDTH - 1, CONV_DIM), lambda b, j: (b, 0, 0)),
        pl.BlockSpec((1, RET_HEADS, RET_DK, RET_DV), lambda b, j: (b, 0, 0, 0)),
    )
    return pl.pallas_call(
        _mixer_prompt_kernel,
        grid=(bsz, n_j),
        in_specs=in_specs,
        out_specs=out_specs,
        out_shape=out_shape,
        scratch_shapes=[
            pltpu.VMEM((RET_HEADS, RET_DK, RET_DV), F32),
            pltpu.VMEM((tt + SUBLANES, CONV_DIM), F32),
            pltpu.VMEM((tt, D_MODEL), BF16),
        ],
        compiler_params=_params("arbitrary", "arbitrary"),
        name="mixer_prompt",
    )(*args)


def _sample_proj_kernel(x_ref, sc_ref, cos_ref, sin_ref, gmix_ref, win_ref, convw_ref, retgn_ref,
                        yc_ref, q_ref, k_ref, v_ref, gate_ref, conv_ref):
    x = x_ref[...]
    u = _rms(x, gmix_ref[...]).astype(BF16)
    proj = _dot(u, win_ref[...])
    gb = proj[:, 0:CONV_DIM]
    z = proj[:, CONV_DIM:2 * CONV_DIM] * proj[:, 2 * CONV_DIM:3 * CONV_DIM]
    cw = convw_ref[...]
    buf0, buf1 = sc_ref[:, 0:CONV_DIM], sc_ref[:, CONV_DIM:2 * CONV_DIM]
    yc_ref[...] = gb * (cw[0:1] * buf0 + cw[1:2] * buf1 + cw[2:3] * z)
    conv_ref[:, 0:CONV_DIM] = buf1
    conv_ref[:, CONV_DIM:2 * CONV_DIM] = z
    q0 = 3 * CONV_DIM
    cos, sin = cos_ref[0:1, :], sin_ref[0:1, :]
    retgn = retgn_ref[...]
    for hh in range(RET_HEADS):
        l0 = hh * RET_DK
        q_ref[:, l0:l0 + RET_DK] = _rope(proj[:, q0 + l0:q0 + l0 + RET_DK], cos, sin)
        k_ref[:, l0:l0 + RET_DK] = _rope(proj[:, q0 + RET_DIM + l0:q0 + RET_DIM + l0 + RET_DK], cos, sin) * (RET_DK ** -0.5)
    v_ref[...] = proj[:, q0 + 2 * RET_DIM:q0 + 3 * RET_DIM]
    g = proj[:, q0 + 3 * RET_DIM:q0 + 4 * RET_DIM]
    gate_ref[...] = retgn * (g * jax.nn.sigmoid(g))


def _sample_proj(x, sc, cos, sin, gmix, win, convw, retgn):
    n = x.shape[0]
    outs = (
        jax.ShapeDtypeStruct((n, CONV_DIM), F32), jax.ShapeDtypeStruct((n, RET_DIM), F32),
        jax.ShapeDtypeStruct((n, RET_DIM), F32), jax.ShapeDtypeStruct((n, RET_DIM), F32),
        jax.ShapeDtypeStruct((n, RET_DIM), F32), jax.ShapeDtypeStruct((n, 2 * CONV_DIM), F32),
    )
    args = (x, sc, cos, sin, gmix, win, convw, retgn)
    return pl.pallas_call(
        _sample_proj_kernel,
        grid=(1,),
        in_specs=[_const_spec(a.shape) for a in args],
        out_specs=tuple(_const_spec(o.shape) for o in outs),
        out_shape=outs,
        compiler_params=_params("arbitrary"),
        name="sample_proj",
    )(*args)


def _sample_state_kernel(s_ref, q_ref, k_ref, kt_ref, v_ref, dec_ref, snew_ref, o_ref):
    for r in range(s_ref.shape[0]):
        for hh in range(RET_HEADS):
            l0 = hh * RET_DV
            s = s_ref[r, hh]
            q = q_ref[r:r + 1, l0:l0 + RET_DK]
            k = k_ref[r:r + 1, l0:l0 + RET_DK]
            kc = kt_ref[0, hh, :, r:r + 1]
            vr = v_ref[r:r + 1, l0:l0 + RET_DV]
            qdec = dec_ref[hh:hh + 1, :]
            cdec = dec_ref[RET_HEADS + hh:RET_HEADS + hh + 1, :]
            qk = jnp.sum(q * k, axis=1, keepdims=True)
            o_ref[r:r + 1, l0:l0 + RET_DV] = qk * vr + _dot((q * qdec).astype(BF16), s.astype(BF16))
            snew_ref[r, hh] = s * cdec + kc * vr


def _sample_state(s, q, k, kt, v, dec):
    n = s.shape[0]
    nb = SAMPLE_BLOCK
    return pl.pallas_call(
        _sample_state_kernel,
        grid=(n // nb,),
        in_specs=[
            pl.BlockSpec((nb, RET_HEADS, RET_DK, RET_DV), lambda i: (i, 0, 0, 0)),
            pl.BlockSpec((nb, RET_DIM), lambda i: (i, 0)),
            pl.BlockSpec((nb, RET_DIM), lambda i: (i, 0)),
            pl.BlockSpec((1, RET_HEADS, RET_DK, nb), lambda i: (i, 0, 0, 0)),
            pl.BlockSpec((nb, RET_DIM), lambda i: (i, 0)),
            _const_spec(dec.shape),
        ],
        out_specs=(
            pl.BlockSpec((nb, RET_HEADS, RET_DK, RET_DV), lambda i: (i, 0, 0, 0)),
            pl.BlockSpec((nb, RET_DIM), lambda i: (i, 0)),
        ),
        out_shape=(jax.ShapeDtypeStruct(s.shape, F32), jax.ShapeDtypeStruct((n, RET_DIM), F32)),
        compiler_params=_params("arbitrary"),
        name="sample_state",
    )(s, q, k, kt, v, dec)


def _sample_out_kernel(x_ref, yc_ref, o_ref, gate_ref, wo_ref, gffn_ref, wr_ref, br_ref, tri_ref, ltri_ref,
                       h_ref, ri_ref, rw_ref, n_ref):
    n = x_ref.shape[0]
    parts = [yc_ref[...].astype(BF16)]
    for hh in range(RET_HEADS):
        l0 = hh * RET_DV
        parts.append(_group_norm_gate(o_ref[:, l0:l0 + RET_DV], gate_ref[:, l0:l0 + RET_DV]).astype(BF16))
    h_ref[0:n, :] = x_ref[...] + _dot(jnp.concatenate(parts, axis=-1), wo_ref[...])
    if n < TOK_TILE:
        h_ref[n:TOK_TILE, :] = jnp.zeros((TOK_TILE - n, D_MODEL), F32)
    ri, rw, cnt = _route(h_ref[...], gffn_ref[...], wr_ref[...], br_ref[...], tri_ref[...], ltri_ref[...])
    ri_ref[...] = ri
    rw_ref[...] = rw
    n_ref[0] = cnt


def _sample_out(x, yc, o, gate, wo, gffn, wr, br, tri, ltri):
    outs = (
        jax.ShapeDtypeStruct((TOK_TILE, D_MODEL), F32), jax.ShapeDtypeStruct((SUBLANES, TOK_TILE), I32),
        jax.ShapeDtypeStruct((SUBLANES, TOK_TILE), F32), jax.ShapeDtypeStruct((1, N_EXPERTS, LANES), F32),
    )
    args = (x, yc, o, gate, wo, gffn, wr, br, tri, ltri)
    return pl.pallas_call(
        _sample_out_kernel,
        grid=(1,),
        in_specs=[_const_spec(a.shape) for a in args],
        out_specs=tuple(_const_spec(o_.shape) for o_ in outs),
        out_shape=outs,
        compiler_params=_params("arbitrary"),
        name="sample_out",
    )(*args)


def _mixer_sample(x, state_conv, state_ret, gmix, win, convw, retgn, wo, gffn, wr, br, tri, ltri):
    n = x.shape[0]
    lg = _log_gamma()
    cos, sin = _rope_tables(np.full((SUBLANES,), PAST_LEN))
    yc, q, k, v, gate, conv_new = _sample_proj(x, state_conv.reshape(n, 2 * CONV_DIM), cos, sin, gmix, win, convw, retgn)
    nb = SAMPLE_BLOCK
    cols = lambda a: a.reshape(n // nb, nb, RET_HEADS, RET_DK).transpose(0, 2, 3, 1)
    step = np.exp(lg[:, None] * 1.0)
    dec = jnp.asarray(np.broadcast_to(np.concatenate([step, step], axis=0), (2 * RET_HEADS, LANES)), F32)
    s_new, o = _sample_state(state_ret, q, k, cols(k), v, dec)
    h, ri, rw, cnt = _sample_out(x, yc, o, gate, wo, gffn, wr, br, tri, ltri)
    return h, ri, rw, cnt, conv_new.reshape(n, 2, CONV_DIM), s_new


def _load_rows(ref, n_rows):
    words = [ref[pl.ds(c, n_rows, stride=ROW_CHUNKS), :] for c in range(ROW_CHUNKS)]
    half = lambda i: [pltpu.unpack_elementwise(w, index=i, packed_dtype=BF16, unpacked_dtype=F32) for w in words]
    return jnp.concatenate(half(0) + half(1), axis=-1)


def _store_rows(ref, val):
    half = D_MODEL // 2
    for c in range(ROW_CHUNKS):
        pair = [val[:, c * LANES:(c + 1) * LANES], val[:, half + c * LANES:half + (c + 1) * LANES]]
        ref[pl.ds(c, val.shape[0], stride=ROW_CHUNKS), :] = pltpu.pack_elementwise(pair, packed_dtype=BF16)


def _row_slice(ref, row, rows):
    return ref.at[pl.ds(pl.multiple_of(row * ROW_CHUNKS, ROW_CHUNKS), rows * ROW_CHUNKS)]


def _run_copies(idx_smem, slot, make_copy):
    for b in range(RUN_BITS):
        def body(j, carry, b=b):
            entry = idx_smem[slot, N_EXPERTS + N_EXPERTS * b + j]
            make_copy(entry & ((1 << SRC_BITS) - 1), entry >> SRC_BITS, 1 << b).start()
            return carry

        lax.fori_loop(0, idx_smem[slot, b], body, 0)


def _wait_tile_rows(hbm_ref, vmem_ref, sem):
    pltpu.make_async_copy(_row_slice(hbm_ref, 0, PAIR_ROWS), vmem_ref, sem).wait()


def _pair_rows_iota():
    return lax.broadcasted_iota(I32, (PAIR_ROWS, TOK_TILE), 0)


def _dispatch_kernel(n_p, tab_ref, hp_ref, hs_ref, ri_ref, gffn_ref, xs_ref, idx_smem, y_scr, isem, rsem):
    i, n = pl.program_id(0), pl.num_programs(0)
    slot = i % 2

    def idx_copy(step, sl):
        return pltpu.make_async_copy(tab_ref.at[step], idx_smem.at[sl], isem.at[sl])

    @pl.when(i == 0)
    def _():
        idx_copy(0, 0).start()

    @pl.when(i + 1 < n)
    def _():
        idx_copy(i + 1, 1 - slot).start()

    @pl.when(i >= 2)
    def _():
        _wait_tile_rows(xs_ref, y_scr.at[slot], rsem.at[slot])

    h = jnp.where(i < n_p, hp_ref[...], hs_ref[...])
    xn = _rms(h, gffn_ref[...]).astype(BF16)
    ri = ri_ref[...]
    rows = _pair_rows_iota()
    perm = jnp.where((rows == ri[2:3]) | (rows == ri[3:4]), 1.0, 0.0).astype(BF16)
    y = _dot(perm, xn)
    _store_rows(y_scr.at[slot], y)

    idx_copy(i, slot).wait()
    _run_copies(idx_smem, slot, lambda src, dst, rows_: pltpu.make_async_copy(
        _row_slice(y_scr.at[slot], src, rows_), _row_slice(xs_ref, dst, rows_), rsem.at[slot]))

    @pl.when(i == n - 1)
    def _():
        _wait_tile_rows(xs_ref, y_scr.at[slot], rsem.at[slot])

    @pl.when((i == n - 1) & (n >= 2))
    def _():
        _wait_tile_rows(xs_ref, y_scr.at[1 - slot], rsem.at[1 - slot])


def _dispatch(tab, h_p, h_s, ri, gffn):
    n_p = h_p.shape[0] // TOK_TILE
    n = tab.shape[0]
    return pl.pallas_call(
        functools.partial(_dispatch_kernel, n_p),
        grid=(n,),
        in_specs=[
            pl.BlockSpec(memory_space=pl.ANY),
            pl.BlockSpec((TOK_TILE, D_MODEL), lambda i: (jnp.minimum(i, n_p - 1), 0)),
            pl.BlockSpec((TOK_TILE, D_MODEL), lambda i: (jnp.maximum(i - n_p, 0), 0)),
            pl.BlockSpec((SUBLANES, TOK_TILE), lambda i: (0, i)),
            _const_spec(gffn.shape),
        ],
        out_specs=pl.BlockSpec(memory_space=pl.ANY),
        out_shape=jax.ShapeDtypeStruct((n * PAIR_ROWS * ROW_CHUNKS, LANES), U32),
        scratch_shapes=[
            pltpu.SMEM((2, TAB_COLS), I32),
            pltpu.VMEM((2, PAIR_ROWS * ROW_CHUNKS, LANES), U32),
            pltpu.SemaphoreType.DMA((2,)),
            pltpu.SemaphoreType.DMA((2,)),
        ],
        compiler_params=_params("arbitrary"),
        name="moe_dispatch",
    )(tab, h_p, h_s, ri, gffn)


def _ffn_kernel(tile_ref, exp_ref, lo_ref, hi_ref, nxt_ref, slot_ref, xs_ref, wg_ref, wu_ref, wd_ref, o_ref,
                x_buf, wg_buf, wu_buf, wd_buf, wgu_scr, wd_scr, y_scr, xsem, wsem):
    w, n_w = pl.program_id(0), pl.num_programs(0)
    prev = jnp.maximum(w - 1, 0)
    lo, hi = lo_ref[w], hi_ref[w]
    e, slot = exp_ref[w], slot_ref[w]
    first = (w == 0) | (tile_ref[w] != tile_ref[prev])

    def x_copy(step):
        rows = FFN_TILE * ROW_CHUNKS
        return pltpu.make_async_copy(xs_ref.at[pl.ds(pl.multiple_of(tile_ref[step] * rows, rows), rows)],
                                     x_buf.at[step % 3], xsem.at[step % 3])

    @pl.when(w == 0)
    def _():
        x_copy(0).start()

    @pl.when((w == 0) & (n_w > 1))
    def _():
        x_copy(1).start()

    @pl.when(w + 2 < n_w)
    def _():
        x_copy(w + 2).start()

    x_copy(w).wait()

    def weight_copies(expert, sl):
        return [pltpu.make_async_copy(src.at[expert], buf.at[sl], wsem.at[sl])
                for src, buf in ((wg_ref, wg_buf), (wu_ref, wu_buf), (wd_ref, wd_buf))]

    @pl.when(w == 0)
    def _():
        for cp in weight_copies(e, slot):
            cp.start(priority=1)

    @pl.when((w == 0) | (e != exp_ref[prev]))
    def _():
        for cp in weight_copies(e, slot):
            cp.wait()

        @pl.when(nxt_ref[w] != e)
        def _():
            for cp in weight_copies(nxt_ref[w], 1 - slot):
                cp.start(priority=1)

        wgu_scr[:, 0:D_FF] = wg_buf[slot].astype(BF16)
        wgu_scr[:, D_FF:2 * D_FF] = wu_buf[slot].astype(BF16)
        wd_scr[...] = wd_buf[slot].astype(BF16)

    @pl.when(w == 0)
    def _():
        y_scr[...] = jnp.zeros_like(y_scr)

    @pl.when(hi > lo)
    def _():
        gu = _dot(_load_rows(x_buf.at[w % 3], FFN_TILE).astype(BF16), wgu_scr[...])
        g, u = gu[:, 0:D_FF], gu[:, D_FF:2 * D_FF]
        y = _dot((g * jax.nn.sigmoid(g) * u).astype(BF16), wd_scr[...])
        row = lax.broadcasted_iota(I32, (FFN_TILE, 1), 0)
        y = jnp.where(((row >= lo) & (row < hi)) | first, y, y_scr[...])
        y_scr[...] = y
        _store_rows(o_ref, y)


def _expert_ffn(items, xs, w_gate, w_up, w_down):
    n_items = items[0].shape[0]
    row_spec = pl.BlockSpec((FFN_TILE * ROW_CHUNKS, LANES), lambda w, t, *_: (t[w], 0))
    grid_spec = pltpu.PrefetchScalarGridSpec(
        num_scalar_prefetch=len(items),
        grid=(n_items,),
        in_specs=[pl.BlockSpec(memory_space=pl.ANY)] * 4,
        out_specs=row_spec,
        scratch_shapes=[
            pltpu.VMEM((3, FFN_TILE * ROW_CHUNKS, LANES), U32),
            pltpu.VMEM((2, D_MODEL, D_FF), F32), pltpu.VMEM((2, D_MODEL, D_FF), F32),
            pltpu.VMEM((2, D_FF, D_MODEL), F32),
            pltpu.VMEM((D_MODEL, 2 * D_FF), BF16), pltpu.VMEM((D_FF, D_MODEL), BF16),
            pltpu.VMEM((FFN_TILE, D_MODEL), F32),
            pltpu.SemaphoreType.DMA((3,)),
            pltpu.SemaphoreType.DMA((2,)),
        ],
    )
    return pl.pallas_call(
        _ffn_kernel,
        grid_spec=grid_spec,
        out_shape=jax.ShapeDtypeStruct(xs.shape, U32),
        compiler_params=_params("arbitrary"),
        name="moe_ffn",
    )(*items, xs, w_gate, w_up, w_down)


def _work_items(counts, n_rows):
    n_tiles = n_rows // FFN_TILE
    n_items = n_tiles + N_EXPERTS - 1
    off = jnp.cumsum(counts) - counts
    first_tile = off // FFN_TILE
    last_tile = jnp.maximum(off + counts - 1, off) // FFN_TILE
    n_e = jnp.where(counts > 0, last_tile - first_tile + 1, 0)
    start = jnp.cumsum(n_e) - n_e
    total = jnp.sum(n_e)
    w = jnp.minimum(jnp.arange(n_items, dtype=I32), total - 1)
    ids = jnp.arange(N_EXPERTS, dtype=I32)[None, :]
    e = jnp.max(jnp.where((start[None, :] <= w[:, None]) & (n_e[None, :] > 0), ids, 0), axis=1)
    pick = lambda a: jnp.sum(jnp.where(ids == e[:, None], a[None, :], 0), axis=1)
    tile = pick(first_tile) + (w - pick(start))
    lo = jnp.clip(pick(off) - tile * FFN_TILE, 0, FFN_TILE)
    hi = jnp.clip(pick(off + counts) - tile * FFN_TILE, 0, FFN_TILE)
    hi = jnp.where(jnp.arange(n_items) < total, hi, lo)
    used = n_e[None, :] > 0
    nxt = jnp.min(jnp.where(used & (ids > e[:, None]), ids, N_EXPERTS), axis=1)
    nxt = jnp.where(nxt == N_EXPERTS, e, nxt)
    slot = jnp.sum(jnp.where(used & (ids < e[:, None]), 1, 0), axis=1) % 2
    return tuple(a.astype(I32) for a in (tile, e, lo, hi, nxt, slot))


def _run_table(n):
    tiles = n.shape[0]
    counts = jnp.sum(n, axis=0)
    goff = jnp.cumsum(counts) - counts
    dst0 = goff[None, :] + jnp.cumsum(n, axis=0) - n
    src0 = jnp.cumsum(n, axis=1) - n
    bits = jnp.arange(RUN_BITS, dtype=I32)[:, None, None]
    flag = (n[None] >> bits) & 1
    done = (n[None] >> (bits + 1)) << (bits + 1)
    entry = ((dst0[None] + done) << SRC_BITS) | (src0[None] + done)
    rank = jnp.cumsum(flag, axis=2) - flag
    slot = jnp.arange(N_EXPERTS, dtype=I32)
    sel = (flag[..., None] == 1) & (rank[..., None] == slot)
    packed = jnp.sum(jnp.where(sel, entry[..., None], 0), axis=2)
    m = jnp.sum(flag, axis=2).T
    tab = jnp.zeros((tiles, TAB_COLS), I32)
    tab = tab.at[:, :RUN_BITS].set(m)
    tab = tab.at[:, N_EXPERTS:N_EXPERTS * (RUN_BITS + 1)].set(packed.transpose(1, 0, 2).reshape(tiles, -1))
    return counts, tab


def _combine_kernel(tab_ref, h_ref, p_ref, ri_ref, rw_ref, ys_ref, gple_ref, wpg_ref, wpp_ref, gfin_ref, o_ref,
                    idx_smem, rows_scr, isem, rsem):
    i, n = pl.program_id(0), pl.num_programs(0)
    slot = i % 2

    def idx_copy(step, sl):
        return pltpu.make_async_copy(tab_ref.at[step], idx_smem.at[sl], isem.at[sl])

    def gather(sl):
        _run_copies(idx_smem, sl, lambda src, dst, rows_: pltpu.make_async_copy(
            _row_slice(ys_ref, dst, rows_), _row_slice(rows_scr.at[sl], src, rows_), rsem.at[sl]))

    @pl.when(i == 0)
    def _():
        idx_copy(0, 0).start()
        idx_copy(0, 0).wait()
        gather(0)

    @pl.when((i == 0) & (n >= 2))
    def _():
        idx_copy(1, 1).start()

    @pl.when(i + 1 < n)
    def _():
        idx_copy(i + 1, 1 - slot).wait()
        gather(1 - slot)

    @pl.when(i + 2 < n)
    def _():
        idx_copy(i + 2, slot).start()

    _wait_tile_rows(ys_ref, rows_scr.at[slot], rsem.at[slot])

    ri, rw = ri_ref[...], rw_ref[...]
    rows = _pair_rows_iota()
    wperm = jnp.where(rows == ri[2:3], rw[0:1], 0.0) + jnp.where(rows == ri[3:4], rw[1:2], 0.0)
    s = _load_rows(rows_scr.at[slot], PAIR_ROWS).astype(BF16)
    h = h_ref[...] + _dot_tn(wperm.astype(BF16), s)
    gate = jax.nn.sigmoid(_dot(_rms(h, gple_ref[...]).astype(BF16), wpg_ref[...]))
    h = h + _dot(p_ref[...].astype(BF16), wpp_ref[...]) * gate
    o_ref[...] = _rms(h, gfin_ref[...])


def _combine(tab, h, p, ri, rw, ys, gple, wpg, wpp, gfin):
    n = tab.shape[0]
    n_tok = n * TOK_TILE
    return pl.pallas_call(
        _combine_kernel,
        grid=(n,),
        in_specs=[
            pl.BlockSpec(memory_space=pl.ANY),
            pl.BlockSpec((TOK_TILE, D_MODEL), lambda i: (i, 0)),
            pl.BlockSpec((TOK_TILE, PLE_DIM), lambda i: (i, 0)),
            pl.BlockSpec((SUBLANES, TOK_TILE), lambda i: (0, i)),
            pl.BlockSpec((SUBLANES, TOK_TILE), lambda i: (0, i)),
            pl.BlockSpec(memory_space=pl.ANY),
            _const_spec(gple.shape), _const_spec(wpg.shape), _const_spec(wpp.shape), _const_spec(gfin.shape),
        ],
        out_specs=pl.BlockSpec((TOK_TILE, D_MODEL), lambda i: (i, 0)),
        out_shape=jax.ShapeDtypeStruct((n_tok, D_MODEL), F32),
        scratch_shapes=[
            pltpu.SMEM((2, TAB_COLS), I32),
            pltpu.VMEM((2, PAIR_ROWS * ROW_CHUNKS, LANES), U32),
            pltpu.SemaphoreType.DMA((2,)),
            pltpu.SemaphoreType.DMA((2,)),
        ],
        compiler_params=_params("arbitrary"),
        name="moe_combine",
    )(tab, h, p, ri, rw, ys, gple, wpg, wpp, gfin)


def _log_gamma():
    return np.log(1.0 - 2.0 ** (-5.0 - np.arange(RET_HEADS, dtype=np.float64)))


def _rope_tables(pos):
    inv = 1.0 / (ROPE_BASE ** (np.arange(0, RET_DK, 2, dtype=np.float64) / RET_DK))
    ang = np.asarray(pos, np.float64)[:, None] * inv[None, :]
    cos, sin = np.cos(ang), np.sin(ang)
    return (jnp.asarray(np.concatenate([cos, cos], axis=-1), F32),
            jnp.asarray(np.concatenate([-sin, sin], axis=-1), F32))


def _decay_tables(c):
    lg = _log_gamma()
    idx = np.arange(c, dtype=np.float64)
    diff = idx[:, None] - idx[None, :]
    dmask = np.where((diff >= 0.0)[None], np.exp(lg[:, None, None] * np.maximum(diff, 0.0)[None]), 0.0)
    q_dec = np.exp(lg[None, :] * (idx[:, None] + 1.0))
    k_dec = np.exp(lg[None, :] * (c - 1.0 - idx[:, None]))
    c_dec = np.exp(lg * c)
    lanes = lambda a: np.repeat(a, RET_DK, axis=1)
    cdec = np.zeros((SUBLANES, LANES))
    cdec[:RET_HEADS] = c_dec[:, None]
    return tuple(jnp.asarray(a, F32) for a in (lanes(q_dec), lanes(k_dec), dmask, cdec))


def _router_params(w_rg, b_rg, w_re, b_re):
    wr = jnp.zeros((ROUTER_ROWS, D_MODEL), F32).at[:N_GROUPS].set(w_rg.T).at[SUBLANES:].set(w_re.T)
    br = jnp.full((ROUTER_ROWS,), NEG_BIG, F32).at[:N_GROUPS].set(b_rg).at[SUBLANES:].set(b_re.reshape(-1))
    return wr.astype(BF16), jnp.broadcast_to(br[:, None], (ROUTER_ROWS, LANES))


def _strict_upper(t):
    i = np.arange(t)
    return jnp.asarray(i[:, None] < i[None, :], BF16)


def _pad_rows(a, rows):
    return jnp.zeros((rows,) + a.shape[1:], a.dtype).at[:a.shape[0]].set(a)


def kernel(x_prompt, x_sample, state_conv, state_ret, p_prompt, p_sample, g_mix, w_in, conv_w, ret_gn, w_o, g_ffn,
           w_router_group, b_router_group, w_router_expert, b_router_expert, w_gate, w_up, w_down, g_ple,
           w_ple_proj, w_ple_gate, g_final):
    bp, tp, _ = x_prompt.shape
    bs = x_sample.shape[0]
    assert x_sample.shape[1] == 1 and g_mix.shape[0] == 1, "one layer, one new token per sample request"
    assert bs <= TOK_TILE and tp % MIX_TILE == 0
    n_p = bp * tp

    tabs = _rope_tables(np.arange(tp)) + _decay_tables(TOK_TILE)
    wr, br = _router_params(w_router_group[0], b_router_group[0], w_router_expert[0], b_router_expert[0])
    win, wo, convw = w_in[0].astype(BF16), w_o[0].astype(BF16), _pad_rows(conv_w[0], SUBLANES)
    tri, ltri = _strict_upper(TOK_TILE), _strict_upper(N_EXPERTS).T

    h_p, ri_p, rw_p, n_p_tab, conv_p, ret_p = _mixer_prompt(
        x_prompt, tabs, (g_mix, win, convw, ret_gn, wo, g_ffn, wr, br, tri, ltri))
    h_s, ri_s, rw_s, n_s_tab, conv_s, ret_s = _mixer_sample(
        x_sample[:, 0], state_conv[0], state_ret[0], g_mix, win, convw, ret_gn, wo, g_ffn, wr, br, tri, ltri)
    h_p = h_p.reshape(n_p, D_MODEL)

    n_tab = jnp.concatenate([n_p_tab, n_s_tab], axis=0)[:, :, 0].astype(I32)
    counts, tab = _run_table(n_tab)
    ri = jnp.concatenate([ri_p, ri_s], axis=1)
    n_tiles = n_tab.shape[0]

    xs = _dispatch(tab, h_p, h_s, ri, g_ffn)
    ys = _expert_ffn(_work_items(counts, n_tiles * PAIR_ROWS), xs, w_gate[0], w_up[0], w_down[0])

    wpg, wpp = w_ple_gate[0].astype(BF16), w_ple_proj[0].astype(BF16)
    gfin = g_final[None, :]
    p_s = _pad_rows(p_sample[0].reshape(bs, PLE_DIM), TOK_TILE)
    y_p = _combine(tab[:n_tiles - 1], h_p, p_prompt[0].reshape(n_p, PLE_DIM), ri_p, rw_p, ys, g_ple, wpg, wpp, gfin)
    y_s = _combine(tab[n_tiles - 1:], h_s, p_s, ri_s, rw_s, ys, g_ple, wpg, wpp, gfin)
    return (y_p.reshape(bp, tp, D_MODEL), y_s[:bs].reshape(bs, 1, D_MODEL), conv_p[None], ret_p[None],
            conv_s[None], ret_s[None])
```

```python
import functools

import jax
import jax.numpy as jnp
import numpy as np
from jax import lax
from jax.experimental import pallas as pl
from jax.experimental.pallas import tpu as pltpu

F32, BF16, I32, U32 = jnp.float32, jnp.bfloat16, jnp.int32, jnp.uint32

D_MODEL = 1024
CONV_DIM = 512
CONV_WIDTH = 3
RET_DIM = 512
RET_HEADS = 4
RET_DK = 128
RET_DV = 128
RET_CHUNK = 128
ROPE_BASE = 10000.0
IN_PROJ_DIM = 3 * CONV_DIM + 4 * RET_DIM
N_GROUPS = 4
EXPERTS_PER_GROUP = 8
N_EXPERTS = 32
D_FF = 512
PLE_DIM = 256
EPS = 1e-6
PAST_LEN = 16384

LANES = 128
SUBLANES = 8
ROW_CHUNKS = D_MODEL // (2 * LANES)
ROUTER_ROWS = SUBLANES + N_EXPERTS
VMEM_LIMIT = 56 * 1024 * 1024
NEG_BIG = -1e30

TOK_TILE = 256
MIX_TILE = 1024
PAIR_ROWS = 2 * TOK_TILE
STEP_TILES = 2
SAMPLE_ROWS = STEP_TILES * TOK_TILE
FFN_TILE = 256
SAMPLE_BLOCK = 8
RUN_BITS = 9
SRC_BITS = 10
TAB_COLS = 384


def _rms(x, g):
    return x * lax.rsqrt(jnp.mean(x * x, axis=-1, keepdims=True) + EPS) * g


def _dot(a, b):
    return jnp.dot(a, b, preferred_element_type=F32)


def _dot_nt(a, b):
    return lax.dot_general(a, b, (((1,), (1,)), ((), ())), preferred_element_type=F32)


def _dot_tn(a, b):
    return lax.dot_general(a, b, (((0,), (0,)), ((), ())), preferred_element_type=F32)


def _rope(x, cos, sin_signed):
    return x * cos + pltpu.roll(x, RET_DK // 2, 1) * sin_signed


def _const_spec(shape):
    nd = len(shape)
    return pl.BlockSpec(shape, lambda *_: (0,) * nd)


def _params(*sem):
    return pltpu.CompilerParams(dimension_semantics=sem, vmem_limit_bytes=VMEM_LIMIT)


def _route(h, gffn, wr, br, tri, ltri):
    t = h.shape[0]
    xn = _rms(h, gffn).astype(BF16)
    lt = _dot_nt(wr, xn) + br[:, 0:1]
    row8 = lax.broadcasted_iota(I32, (SUBLANES, t), 0).astype(F32)
    gl = lt[0:SUBLANES]
    m = jnp.max(gl, axis=0, keepdims=True)
    g_top = 1.0 / jnp.sum(jnp.exp(gl - m), axis=0, keepdims=True)
    gidx = jnp.min(jnp.where(gl == m, row8, float(SUBLANES)), axis=0, keepdims=True)
    e_sel = jnp.where(gidx == 0.0, lt[8:16],
                      jnp.where(gidx == 1.0, lt[16:24], jnp.where(gidx == 2.0, lt[24:32], lt[32:40])))
    m1 = jnp.max(e_sel, axis=0, keepdims=True)
    i1 = jnp.min(jnp.where(e_sel == m1, row8, float(SUBLANES)), axis=0, keepdims=True)
    rest = jnp.where(row8 == i1, -jnp.inf, e_sel)
    m2 = jnp.max(rest, axis=0, keepdims=True)
    i2 = jnp.min(jnp.where(rest == m2, row8, float(SUBLANES)), axis=0, keepdims=True)
    d = jnp.exp(m2 - m1)
    w1 = g_top / (1.0 + d)
    w2 = g_top * d / (1.0 + d)
    e1 = gidx * float(EXPERTS_PER_GROUP) + i1
    e2 = gidx * float(EXPERTS_PER_GROUP) + i2
    row32 = lax.broadcasted_iota(I32, (N_EXPERTS, t), 0).astype(F32)
    a1 = jnp.where(row32 == e1, 1.0, 0.0)
    a2 = jnp.where(row32 == e2, 1.0, 0.0)
    a = a1 + a2
    n = jnp.broadcast_to(jnp.sum(a, axis=1, keepdims=True), (N_EXPERTS, LANES))
    start = _dot(ltri, n.astype(BF16))
    base = _dot(a.astype(BF16), tri) + start[:, 0:1]
    r1 = jnp.sum(a1 * base, axis=0, keepdims=True)
    r2 = jnp.sum(a2 * base, axis=0, keepdims=True)
    ri = jnp.where(row8 == 0.0, e1, jnp.where(row8 == 1.0, e2, jnp.where(row8 == 2.0, r1,
                                                                         jnp.where(row8 == 3.0, r2, 0.0))))
    rw = jnp.where(row8 == 0.0, w1, jnp.where(row8 == 1.0, w2, 0.0))
    return ri.astype(I32), rw, n, xn


def _group_norm_gate(o, gate):
    mu = jnp.mean(o, axis=-1, keepdims=True)
    oc = o - mu
    return oc * lax.rsqrt(jnp.mean(oc * oc, axis=-1, keepdims=True) + EPS) * gate


def _mixer_prompt_kernel(x_ref, cos_ref, sin_ref, qdec_ref, kdec_ref, dmask_ref, cdec_ref, gmix_ref, win_ref,
                         convw_ref, retgn_ref, wo_ref, gffn_ref, wr_ref, br_ref, tri_ref, ltri_ref,
                         h_ref, xn_ref, ri_ref, rw_ref, n_ref, conv_ref, ret_ref,
                         s_scr, z_scr, mix_scr):
    j = pl.program_id(1)
    n_j = pl.num_programs(1)
    tt = x_ref.shape[1]

    @pl.when(j == 0)
    def _():
        s_scr[...] = jnp.zeros_like(s_scr)
        z_scr[0:SUBLANES, :] = jnp.zeros((SUBLANES, CONV_DIM), F32)

    cw = convw_ref[...]
    retgn = retgn_ref[...]
    q0 = 3 * CONV_DIM

    def project(p0):
        x = x_ref[0, p0:p0 + TOK_TILE, :]
        proj = _dot(_rms(x, gmix_ref[...]).astype(BF16), win_ref[...])
        gb = proj[:, 0:CONV_DIM]
        z = proj[:, CONV_DIM:2 * CONV_DIM] * proj[:, 2 * CONV_DIM:3 * CONV_DIM]
        z0 = SUBLANES + p0
        z_scr[z0:z0 + TOK_TILE, :] = z
        yc = cw[0:1] * z_scr[z0 - 2:z0 - 2 + TOK_TILE, :] + cw[1:2] * z_scr[z0 - 1:z0 - 1 + TOK_TILE, :] + cw[2:3] * z
        mix_scr[p0:p0 + TOK_TILE, 0:CONV_DIM] = (gb * yc).astype(BF16)
        return x, proj

    def mix(p0, x, proj):
        cos = cos_ref[p0:p0 + TOK_TILE, :]
        sin = sin_ref[p0:p0 + TOK_TILE, :]
        for hh in range(RET_HEADS):
            l0 = hh * RET_DK
            qr = _rope(proj[:, q0 + l0:q0 + l0 + RET_DK], cos, sin)
            kr = _rope(proj[:, q0 + RET_DIM + l0:q0 + RET_DIM + l0 + RET_DK], cos, sin) * (RET_DK ** -0.5)
            v = proj[:, q0 + 2 * RET_DIM + l0:q0 + 2 * RET_DIM + l0 + RET_DV].astype(BF16)
            g = proj[:, q0 + 3 * RET_DIM + l0:q0 + 3 * RET_DIM + l0 + RET_DV]
            s_old = s_scr[hh]
            scores = _dot_nt(qr.astype(BF16), kr.astype(BF16)) * dmask_ref[hh]
            lhs = jnp.concatenate([scores.astype(BF16), (qr * qdec_ref[:, l0:l0 + RET_DK]).astype(BF16)], axis=1)
            o = _dot(lhs, jnp.concatenate([v, s_old.astype(BF16)], axis=0))
            kd = (kr * kdec_ref[:, l0:l0 + RET_DK]).T.astype(BF16)
            s_scr[hh] = s_old * cdec_ref[hh:hh + 1, :] + _dot(kd, v)
            gate = retgn[:, l0:l0 + RET_DV] * (g * jax.nn.sigmoid(g))
            mix_scr[p0:p0 + TOK_TILE, CONV_DIM + l0:CONV_DIM + l0 + RET_DV] = _group_norm_gate(o, gate).astype(BF16)
        h = x + _dot(mix_scr[p0:p0 + TOK_TILE, :], wo_ref[...])
        h_ref[0, p0:p0 + TOK_TILE, :] = h
        return h

    def route(part, h):
        p0 = part * TOK_TILE
        ri, rw, n, xn = _route(h, gffn_ref[...], wr_ref[...], br_ref[...], tri_ref[...], ltri_ref[...])
        ri_ref[:, p0:p0 + TOK_TILE] = ri
        rw_ref[:, p0:p0 + TOK_TILE] = rw
        n_ref[part] = n
        xn_ref[0, p0:p0 + TOK_TILE, :] = xn

    parts = tt // TOK_TILE
    nxt = project(0)
    for part in range(parts):
        h = mix(part * TOK_TILE, *nxt)
        if part + 1 < parts:
            nxt = project((part + 1) * TOK_TILE)
        route(part, h)

    @pl.when(j == n_j - 1)
    def _():
        conv_ref[0] = z_scr[SUBLANES + tt - 2:SUBLANES + tt, :]
        ret_ref[0] = s_scr[...]

    z_scr[0:SUBLANES, :] = z_scr[tt:tt + SUBLANES, :]


def _mixer_prompt(x, tabs, wts):
    bsz, t, _ = x.shape
    tt = MIX_TILE
    n_j = t // tt
    parts = tt // TOK_TILE
    n_tok = bsz * t
    args = (x,) + tuple(tabs) + tuple(wts)
    in_specs = [
        pl.BlockSpec((1, tt, D_MODEL), lambda b, j: (b, j, 0)),
        pl.BlockSpec((tt, LANES), lambda b, j: (j, 0)),
        pl.BlockSpec((tt, LANES), lambda b, j: (j, 0)),
    ] + [_const_spec(a.shape) for a in args[3:]]
    out_shape = (
        jax.ShapeDtypeStruct((bsz, t, D_MODEL), F32),
        jax.ShapeDtypeStruct((bsz, t, D_MODEL), BF16),
        jax.ShapeDtypeStruct((SUBLANES, n_tok), I32),
        jax.ShapeDtypeStruct((SUBLANES, n_tok), F32),
        jax.ShapeDtypeStruct((bsz * n_j * parts, N_EXPERTS, LANES), F32),
        jax.ShapeDtypeStruct((bsz, CONV_WIDTH - 1, CONV_DIM), F32),
        jax.ShapeDtypeStruct((bsz, RET_HEADS, RET_DK, RET_DV), F32),
    )
    out_specs = (
        pl.BlockSpec((1, tt, D_MODEL), lambda b, j: (b, j, 0)),
        pl.BlockSpec((1, tt, D_MODEL), lambda b, j: (b, j, 0)),
        pl.BlockSpec((SUBLANES, tt), lambda b, j: (0, b * n_j + j)),
        pl.BlockSpec((SUBLANES, tt), lambda b, j: (0, b * n_j + j)),
        pl.BlockSpec((parts, N_EXPERTS, LANES), lambda b, j: (b * n_j + j, 0, 0)),
        pl.BlockSpec((1, CONV_WIDTH - 1, CONV_DIM), lambda b, j: (b, 0, 0)),
        pl.BlockSpec((1, RET_HEADS, RET_DK, RET_DV), lambda b, j: (b, 0, 0, 0)),
    )
    return pl.pallas_call(
        _mixer_prompt_kernel,
        grid=(bsz, n_j),
        in_specs=in_specs,
        out_specs=out_specs,
        out_shape=out_shape,
        scratch_shapes=[
            pltpu.VMEM((RET_HEADS, RET_DK, RET_DV), F32),
            pltpu.VMEM((tt + SUBLANES, CONV_DIM), F32),
            pltpu.VMEM((tt, D_MODEL), BF16),
        ],
        compiler_params=_params("arbitrary", "arbitrary"),
        name="mixer_prompt",
    )(*args)


def _sample_proj_kernel(x_ref, sc_ref, cos_ref, sin_ref, gmix_ref, win_ref, convw_ref, retgn_ref,
                        yc_ref, q_ref, k_ref, v_ref, gate_ref, conv_ref):
    x = x_ref[...]
    u = _rms(x, gmix_ref[...]).astype(BF16)
    proj = _dot(u, win_ref[...])
    gb = proj[:, 0:CONV_DIM]
    z = proj[:, CONV_DIM:2 * CONV_DIM] * proj[:, 2 * CONV_DIM:3 * CONV_DIM]
    cw = convw_ref[...]
    buf0, buf1 = sc_ref[:, 0:CONV_DIM], sc_ref[:, CONV_DIM:2 * CONV_DIM]
    yc_ref[...] = gb * (cw[0:1] * buf0 + cw[1:2] * buf1 + cw[2:3] * z)
    conv_ref[:, 0:CONV_DIM] = buf1
    conv_ref[:, CONV_DIM:2 * CONV_DIM] = z
    q0 = 3 * CONV_DIM
    cos, sin = cos_ref[0:1, :], sin_ref[0:1, :]
    retgn = retgn_ref[...]
    for hh in range(RET_HEADS):
        l0 = hh * RET_DK
        q_ref[:, l0:l0 + RET_DK] = _rope(proj[:, q0 + l0:q0 + l0 + RET_DK], cos, sin)
        k_ref[:, l0:l0 + RET_DK] = _rope(proj[:, q0 + RET_DIM + l0:q0 + RET_DIM + l0 + RET_DK], cos, sin) * (RET_DK ** -0.5)
    v_ref[...] = proj[:, q0 + 2 * RET_DIM:q0 + 3 * RET_DIM]
    g = proj[:, q0 + 3 * RET_DIM:q0 + 4 * RET_DIM]
    gate_ref[...] = retgn * (g * jax.nn.sigmoid(g))


def _sample_proj(x, sc, cos, sin, gmix, win, convw, retgn):
    n = x.shape[0]
    outs = (
        jax.ShapeDtypeStruct((n, CONV_DIM), F32), jax.ShapeDtypeStruct((n, RET_DIM), F32),
        jax.ShapeDtypeStruct((n, RET_DIM), F32), jax.ShapeDtypeStruct((n, RET_DIM), F32),
        jax.ShapeDtypeStruct((n, RET_DIM), F32), jax.ShapeDtypeStruct((n, 2 * CONV_DIM), F32),
    )
    args = (x, sc, cos, sin, gmix, win, convw, retgn)
    return pl.pallas_call(
        _sample_proj_kernel,
        grid=(1,),
        in_specs=[_const_spec(a.shape) for a in args],
        out_specs=tuple(_const_spec(o.shape) for o in outs),
        out_shape=outs,
        compiler_params=_params("arbitrary"),
        name="sample_proj",
    )(*args)


def _sample_state_kernel(s_ref, q_ref, k_ref, kt_ref, v_ref, dec_ref, snew_ref, o_ref):
    for r in range(s_ref.shape[0]):
        for hh in range(RET_HEADS):
            l0 = hh * RET_DV
            s = s_ref[r, hh]
            q = q_ref[r:r + 1, l0:l0 + RET_DK]
            k = k_ref[r:r + 1, l0:l0 + RET_DK]
            kc = kt_ref[0, hh, :, r:r + 1]
            vr = v_ref[r:r + 1, l0:l0 + RET_DV]
            qdec = dec_ref[hh:hh + 1, :]
            cdec = dec_ref[RET_HEADS + hh:RET_HEADS + hh + 1, :]
            qk = jnp.sum(q * k, axis=1, keepdims=True)
            o_ref[r:r + 1, l0:l0 + RET_DV] = qk * vr + _dot((q * qdec).astype(BF16), s.astype(BF16))
            snew_ref[r, hh] = s * cdec + kc * vr


def _sample_state(s, q, k, kt, v, dec):
    n = s.shape[0]
    nb = SAMPLE_BLOCK
    return pl.pallas_call(
        _sample_state_kernel,
        grid=(n // nb,),
        in_specs=[
            pl.BlockSpec((nb, RET_HEADS, RET_DK, RET_DV), lambda i: (i, 0, 0, 0)),
            pl.BlockSpec((nb, RET_DIM), lambda i: (i, 0)),
            pl.BlockSpec((nb, RET_DIM), lambda i: (i, 0)),
            pl.BlockSpec((1, RET_HEADS, RET_DK, nb), lambda i: (i, 0, 0, 0)),
            pl.BlockSpec((nb, RET_DIM), lambda i: (i, 0)),
            _const_spec(dec.shape),
        ],
        out_specs=(
            pl.BlockSpec((nb, RET_HEADS, RET_DK, RET_DV), lambda i: (i, 0, 0, 0)),
            pl.BlockSpec((nb, RET_DIM), lambda i: (i, 0)),
        ),
        out_shape=(jax.ShapeDtypeStruct(s.shape, F32), jax.ShapeDtypeStruct((n, RET_DIM), F32)),
        compiler_params=_params("arbitrary"),
        name="sample_state",
    )(s, q, k, kt, v, dec)


def _sample_out_kernel(x_ref, yc_ref, o_ref, gate_ref, wo_ref, gffn_ref, wr_ref, br_ref, tri_ref, ltri_ref,
                       h_ref, xn_ref, ri_ref, rw_ref, n_ref):
    n = x_ref.shape[0]
    parts = [yc_ref[...].astype(BF16)]
    for hh in range(RET_HEADS):
        l0 = hh * RET_DV
        parts.append(_group_norm_gate(o_ref[:, l0:l0 + RET_DV], gate_ref[:, l0:l0 + RET_DV]).astype(BF16))
    h_ref[0:n, :] = x_ref[...] + _dot(jnp.concatenate(parts, axis=-1), wo_ref[...])
    h_ref[n:SAMPLE_ROWS, :] = jnp.zeros((SAMPLE_ROWS - n, D_MODEL), F32)
    for part in range(SAMPLE_ROWS // TOK_TILE):
        p0 = part * TOK_TILE
        ri, rw, cnt, xn = _route(h_ref[p0:p0 + TOK_TILE, :], gffn_ref[...], wr_ref[...], br_ref[...], tri_ref[...],
                                 ltri_ref[...])
        ri_ref[:, p0:p0 + TOK_TILE] = ri
        rw_ref[:, p0:p0 + TOK_TILE] = rw
        n_ref[part] = cnt
        xn_ref[p0:p0 + TOK_TILE, :] = xn


def _sample_out(x, yc, o, gate, wo, gffn, wr, br, tri, ltri):
    outs = (
        jax.ShapeDtypeStruct((SAMPLE_ROWS, D_MODEL), F32), jax.ShapeDtypeStruct((SAMPLE_ROWS, D_MODEL), BF16),
        jax.ShapeDtypeStruct((SUBLANES, SAMPLE_ROWS), I32), jax.ShapeDtypeStruct((SUBLANES, SAMPLE_ROWS), F32),
        jax.ShapeDtypeStruct((SAMPLE_ROWS // TOK_TILE, N_EXPERTS, LANES), F32),
    )
    args = (x, yc, o, gate, wo, gffn, wr, br, tri, ltri)
    return pl.pallas_call(
        _sample_out_kernel,
        grid=(1,),
        in_specs=[_const_spec(a.shape) for a in args],
        out_specs=tuple(_const_spec(o_.shape) for o_ in outs),
        out_shape=outs,
        compiler_params=_params("arbitrary"),
        name="sample_out",
    )(*args)


def _mixer_sample(x, state_conv, state_ret, gmix, win, convw, retgn, wo, gffn, wr, br, tri, ltri):
    n = x.shape[0]
    lg = _log_gamma()
    cos, sin = _rope_tables(np.full((SUBLANES,), PAST_LEN))
    yc, q, k, v, gate, conv_new = _sample_proj(x, state_conv.reshape(n, 2 * CONV_DIM), cos, sin, gmix, win, convw, retgn)
    nb = SAMPLE_BLOCK
    cols = lambda a: a.reshape(n // nb, nb, RET_HEADS, RET_DK).transpose(0, 2, 3, 1)
    step = np.exp(lg[:, None] * 1.0)
    dec = jnp.asarray(np.broadcast_to(np.concatenate([step, step], axis=0), (2 * RET_HEADS, LANES)), F32)
    s_new, o = _sample_state(state_ret, q, k, cols(k), v, dec)
    h, xn, ri, rw, cnt = _sample_out(x, yc, o, gate, wo, gffn, wr, br, tri, ltri)
    return h, xn, ri, rw, cnt, conv_new.reshape(n, 2, CONV_DIM), s_new


def _load_rows(ref, n_rows):
    words = [ref[pl.ds(c, n_rows, stride=ROW_CHUNKS), :] for c in range(ROW_CHUNKS)]
    half = lambda i: [pltpu.unpack_elementwise(w, index=i, packed_dtype=BF16, unpacked_dtype=F32) for w in words]
    return jnp.concatenate(half(0) + half(1), axis=-1)


def _store_rows(ref, val):
    half = D_MODEL // 2
    for c in range(ROW_CHUNKS):
        pair = [val[:, c * LANES:(c + 1) * LANES], val[:, half + c * LANES:half + (c + 1) * LANES]]
        ref[pl.ds(c, val.shape[0], stride=ROW_CHUNKS), :] = pltpu.pack_elementwise(pair, packed_dtype=BF16)


def _row_slice(ref, row, rows):
    return ref.at[pl.ds(pl.multiple_of(row * ROW_CHUNKS, ROW_CHUNKS), rows * ROW_CHUNKS)]


def _run_copies(table, make_copy):
    for b in range(RUN_BITS):
        def body(j, carry, b=b):
            entry = table(N_EXPERTS + N_EXPERTS * b + j)
            make_copy(entry & ((1 << SRC_BITS) - 1), entry >> SRC_BITS, 1 << b).start()
            return carry

        lax.fori_loop(0, table(b), body, 0)


def _pair_rows_iota():
    return lax.broadcasted_iota(I32, (PAIR_ROWS, TOK_TILE), 0)


def _dispatch_kernel(n_p, tab_ref, xp_ref, xs_in_ref, ri_ref, xs_ref, idx_smem, y_scr, isem, rsem):
    i, n = pl.program_id(0), pl.num_programs(0)
    slot = i % 2
    step_rows = STEP_TILES * PAIR_ROWS

    def idx_copy(step, sl):
        return pltpu.make_async_copy(tab_ref.at[pl.ds(step * STEP_TILES, STEP_TILES)], idx_smem.at[sl], isem.at[sl])

    def wait_step(sl):
        pltpu.make_async_copy(_row_slice(xs_ref, 0, step_rows), y_scr.at[sl], rsem.at[sl]).wait()

    @pl.when(i == 0)
    def _():
        idx_copy(0, 0).start()

    @pl.when(i + 1 < n)
    def _():
        idx_copy(i + 1, 1 - slot).start()

    @pl.when(i >= 2)
    def _():
        wait_step(slot)

    xn = jnp.where(i < n_p, xp_ref[...], xs_in_ref[...])
    ri = ri_ref[...]
    rows = _pair_rows_iota()
    for g in range(STEP_TILES):
        tok = slice(g * TOK_TILE, (g + 1) * TOK_TILE)
        perm = jnp.where((rows == ri[2:3, tok]) | (rows == ri[3:4, tok]), 1.0, 0.0).astype(BF16)
        y = _dot(perm, xn[tok, :])
        _store_rows(y_scr.at[slot, pl.ds(g * PAIR_ROWS * ROW_CHUNKS, PAIR_ROWS * ROW_CHUNKS)], y)

    idx_copy(i, slot).wait()
    for g in range(STEP_TILES):
        _run_copies(lambda col, g=g: idx_smem[slot, g, col], lambda src, dst, rows_, g=g: pltpu.make_async_copy(
            _row_slice(y_scr.at[slot], g * PAIR_ROWS + src, rows_), _row_slice(xs_ref, dst, rows_), rsem.at[slot]))

    @pl.when(i == n - 1)
    def _():
        wait_step(slot)

    @pl.when((i == n - 1) & (n >= 2))
    def _():
        wait_step(1 - slot)


def _dispatch(tab, xn_p, xn_s, ri):
    tt = STEP_TILES * TOK_TILE
    n_p = xn_p.shape[0] // tt
    n = tab.shape[0] // STEP_TILES
    return pl.pallas_call(
        functools.partial(_dispatch_kernel, n_p),
        grid=(n,),
        in_specs=[
            pl.BlockSpec(memory_space=pl.ANY),
            pl.BlockSpec((tt, D_MODEL), lambda i: (jnp.minimum(i, n_p - 1), 0)),
            pl.BlockSpec((tt, D_MODEL), lambda i: (jnp.maximum(i - n_p, 0), 0)),
            pl.BlockSpec((SUBLANES, tt), lambda i: (0, i)),
        ],
        out_specs=pl.BlockSpec(memory_space=pl.ANY),
        out_shape=jax.ShapeDtypeStruct((tab.shape[0] * PAIR_ROWS * ROW_CHUNKS, LANES), U32),
        scratch_shapes=[
            pltpu.SMEM((2, STEP_TILES, TAB_COLS), I32),
            pltpu.VMEM((2, STEP_TILES * PAIR_ROWS * ROW_CHUNKS, LANES), U32),
            pltpu.SemaphoreType.DMA((2,)),
            pltpu.SemaphoreType.DMA((2,)),
        ],
        compiler_params=_params("arbitrary"),
        name="moe_dispatch",
    )(tab, xn_p, xn_s, ri)


def _ffn_kernel(tile_ref, exp_ref, lo_ref, hi_ref, nxt_ref, slot_ref, xs_ref, wg_ref, wu_ref, wd_ref, o_ref,
                x_buf, wg_buf, wu_buf, wd_buf, wgu_scr, wd_scr, y_scr, xsem, wsem):
    w, n_w = pl.program_id(0), pl.num_programs(0)
    prev = jnp.maximum(w - 1, 0)
    lo, hi = lo_ref[w], hi_ref[w]
    e, slot = exp_ref[w], slot_ref[w]
    first = (w == 0) | (tile_ref[w] != tile_ref[prev])

    def x_copy(step):
        rows = FFN_TILE * ROW_CHUNKS
        return pltpu.make_async_copy(xs_ref.at[pl.ds(pl.multiple_of(tile_ref[step] * rows, rows), rows)],
                                     x_buf.at[step % 3], xsem.at[step % 3])

    @pl.when(w == 0)
    def _():
        x_copy(0).start()

    @pl.when((w == 0) & (n_w > 1))
    def _():
        x_copy(1).start()

    @pl.when(w + 2 < n_w)
    def _():
        x_copy(w + 2).start()

    x_copy(w).wait()

    def weight_copies(expert, sl):
        return [pltpu.make_async_copy(src.at[expert], buf.at[sl], wsem.at[sl])
                for src, buf in ((wg_ref, wg_buf), (wu_ref, wu_buf), (wd_ref, wd_buf))]

    @pl.when(w == 0)
    def _():
        for cp in weight_copies(e, slot):
            cp.start(priority=1)

    @pl.when((w == 0) | (e != exp_ref[prev]))
    def _():
        for cp in weight_copies(e, slot):
            cp.wait()

        @pl.when(nxt_ref[w] != e)
        def _():
            for cp in weight_copies(nxt_ref[w], 1 - slot):
                cp.start(priority=1)

        wgu_scr[:, 0:D_FF] = wg_buf[slot].astype(BF16)
        wgu_scr[:, D_FF:2 * D_FF] = wu_buf[slot].astype(BF16)
        wd_scr[...] = wd_buf[slot].astype(BF16)

    @pl.when(w == 0)
    def _():
        y_scr[...] = jnp.zeros_like(y_scr)

    @pl.when(hi > lo)
    def _():
        gu = _dot(_load_rows(x_buf.at[w % 3], FFN_TILE).astype(BF16), wgu_scr[...])
        g, u = gu[:, 0:D_FF], gu[:, D_FF:2 * D_FF]
        y = _dot((g * jax.nn.sigmoid(g) * u).astype(BF16), wd_scr[...])
        row = lax.broadcasted_iota(I32, (FFN_TILE, 1), 0)
        y = jnp.where(((row >= lo) & (row < hi)) | first, y, y_scr[...])
        y_scr[...] = y
        _store_rows(o_ref, y)


def _expert_ffn(items, xs, w_gate, w_up, w_down):
    n_items = items[0].shape[0]
    row_spec = pl.BlockSpec((FFN_TILE * ROW_CHUNKS, LANES), lambda w, t, *_: (t[w], 0))
    grid_spec = pltpu.PrefetchScalarGridSpec(
        num_scalar_prefetch=len(items),
        grid=(n_items,),
        in_specs=[pl.BlockSpec(memory_space=pl.ANY)] * 4,
        out_specs=row_spec,
        scratch_shapes=[
            pltpu.VMEM((3, FFN_TILE * ROW_CHUNKS, LANES), U32),
            pltpu.VMEM((2, D_MODEL, D_FF), F32), pltpu.VMEM((2, D_MODEL, D_FF), F32),
            pltpu.VMEM((2, D_FF, D_MODEL), F32),
            pltpu.VMEM((D_MODEL, 2 * D_FF), BF16), pltpu.VMEM((D_FF, D_MODEL), BF16),
            pltpu.VMEM((FFN_TILE, D_MODEL), F32),
            pltpu.SemaphoreType.DMA((3,)),
            pltpu.SemaphoreType.DMA((2,)),
        ],
    )
    return pl.pallas_call(
        _ffn_kernel,
        grid_spec=grid_spec,
        out_shape=jax.ShapeDtypeStruct(xs.shape, U32),
        compiler_params=_params("arbitrary"),
        name="moe_ffn",
    )(*items, xs, w_gate, w_up, w_down)


def _work_items(counts, n_rows):
    n_tiles = n_rows // FFN_TILE
    n_items = n_tiles + N_EXPERTS - 1
    off = jnp.cumsum(counts) - counts
    first_tile = off // FFN_TILE
    last_tile = jnp.maximum(off + counts - 1, off) // FFN_TILE
    n_e = jnp.where(counts > 0, last_tile - first_tile + 1, 0)
    start = jnp.cumsum(n_e) - n_e
    total = jnp.sum(n_e)
    w = jnp.minimum(jnp.arange(n_items, dtype=I32), total - 1)
    ids = jnp.arange(N_EXPERTS, dtype=I32)[None, :]
    e = jnp.max(jnp.where((start[None, :] <= w[:, None]) & (n_e[None, :] > 0), ids, 0), axis=1)
    pick = lambda a: jnp.sum(jnp.where(ids == e[:, None], a[None, :], 0), axis=1)
    tile = pick(first_tile) + (w - pick(start))
    lo = jnp.clip(pick(off) - tile * FFN_TILE, 0, FFN_TILE)
    hi = jnp.clip(pick(off + counts) - tile * FFN_TILE, 0, FFN_TILE)
    hi = jnp.where(jnp.arange(n_items) < total, hi, lo)
    used = n_e[None, :] > 0
    nxt = jnp.min(jnp.where(used & (ids > e[:, None]), ids, N_EXPERTS), axis=1)
    nxt = jnp.where(nxt == N_EXPERTS, e, nxt)
    slot = jnp.sum(jnp.where(used & (ids < e[:, None]), 1, 0), axis=1) % 2
    return tuple(a.astype(I32) for a in (tile, e, lo, hi, nxt, slot))


def _run_table(n):
    tiles = n.shape[0]
    counts = jnp.sum(n, axis=0)
    goff = jnp.cumsum(counts) - counts
    dst0 = goff[None, :] + jnp.cumsum(n, axis=0) - n
    src0 = jnp.cumsum(n, axis=1) - n
    bits = jnp.arange(RUN_BITS, dtype=I32)[:, None, None]
    flag = (n[None] >> bits) & 1
    done = (n[None] >> (bits + 1)) << (bits + 1)
    entry = ((dst0[None] + done) << SRC_BITS) | (src0[None] + done)
    rank = jnp.cumsum(flag, axis=2) - flag
    slot = jnp.arange(N_EXPERTS, dtype=I32)
    sel = (flag[..., None] == 1) & (rank[..., None] == slot)
    packed = jnp.sum(jnp.where(sel, entry[..., None], 0), axis=2)
    m = jnp.sum(flag, axis=2).T
    tab = jnp.zeros((tiles, TAB_COLS), I32)
    tab = tab.at[:, :RUN_BITS].set(m)
    tab = tab.at[:, N_EXPERTS:N_EXPERTS * (RUN_BITS + 1)].set(packed.transpose(1, 0, 2).reshape(tiles, -1))
    return counts, tab


def _combine_kernel(g_tiles, n, tab_ref, h_ref, p_ref, ri_ref, rw_ref, ys_ref, gple_ref, wpg_ref, wpp_ref, gfin_ref,
                    o_ref, idx_smem, rows_scr, isem, rsem):
    i = pl.program_id(0)
    slot = i % 2

    def idx_copy(step, sl):
        return pltpu.make_async_copy(tab_ref.at[pl.ds(step * g_tiles, g_tiles)], idx_smem.at[sl], isem.at[sl])

    def gather(sl):
        for g in range(g_tiles):
            _run_copies(lambda col, g=g: idx_smem[sl, g, col], lambda src, dst, rows_, g=g: pltpu.make_async_copy(
                _row_slice(ys_ref, dst, rows_), _row_slice(rows_scr.at[sl], g * PAIR_ROWS + src, rows_), rsem.at[sl]))

    @pl.when(i == 0)
    def _():
        idx_copy(0, 0).start()
        idx_copy(0, 0).wait()
        gather(0)

    if n >= 2:
        @pl.when(i == 0)
        def _():
            idx_copy(1, 1).start()

        @pl.when(i + 1 < n)
        def _():
            idx_copy(i + 1, 1 - slot).wait()
            gather(1 - slot)

    @pl.when(i + 2 < n)
    def _():
        idx_copy(i + 2, slot).start()

    pltpu.make_async_copy(_row_slice(ys_ref, 0, g_tiles * PAIR_ROWS), rows_scr.at[slot], rsem.at[slot]).wait()

    ri, rw = ri_ref[...], rw_ref[...]
    rows = _pair_rows_iota()
    tiles = range(g_tiles)
    tok = lambda g: slice(g * TOK_TILE, (g + 1) * TOK_TILE)
    h = []
    for g in tiles:
        r1, r2 = ri[2:3, tok(g)], ri[3:4, tok(g)]
        wperm = jnp.where(rows == r1, rw[0:1, tok(g)], 0.0) + jnp.where(rows == r2, rw[1:2, tok(g)], 0.0)
        srows = _load_rows(rows_scr.at[slot, pl.ds(g * PAIR_ROWS * ROW_CHUNKS, PAIR_ROWS * ROW_CHUNKS)], PAIR_ROWS)
        h.append(h_ref[tok(g), :] + _dot_tn(wperm.astype(BF16), srows.astype(BF16)))
    gate = [jax.nn.sigmoid(_dot(_rms(h[g], gple_ref[...]).astype(BF16), wpg_ref[...])) for g in tiles]
    for g in tiles:
        hg = h[g] + _dot(p_ref[tok(g), :].astype(BF16), wpp_ref[...]) * gate[g]
        o_ref[tok(g), :] = _rms(hg, gfin_ref[...])


def _combine(tab, h, p, ri, rw, ys, gple, wpg, wpp, gfin):
    n_tiles = tab.shape[0]
    g_tiles = STEP_TILES
    n = n_tiles // g_tiles
    tt = g_tiles * TOK_TILE
    return pl.pallas_call(
        functools.partial(_combine_kernel, g_tiles, n),
        grid=(n,),
        in_specs=[
            pl.BlockSpec(memory_space=pl.ANY),
            pl.BlockSpec((tt, D_MODEL), lambda i: (i, 0)),
            pl.BlockSpec((tt, PLE_DIM), lambda i: (i, 0)),
            pl.BlockSpec((SUBLANES, tt), lambda i: (0, i)),
            pl.BlockSpec((SUBLANES, tt), lambda i: (0, i)),
            pl.BlockSpec(memory_space=pl.ANY),
            _const_spec(gple.shape), _const_spec(wpg.shape), _const_spec(wpp.shape), _const_spec(gfin.shape),
        ],
        out_specs=pl.BlockSpec((tt, D_MODEL), lambda i: (i, 0)),
        out_shape=jax.ShapeDtypeStruct((n_tiles * TOK_TILE, D_MODEL), F32),
        scratch_shapes=[
            pltpu.SMEM((2, g_tiles, TAB_COLS), I32),
            pltpu.VMEM((2, g_tiles * PAIR_ROWS * ROW_CHUNKS, LANES), U32),
            pltpu.SemaphoreType.DMA((2,)),
            pltpu.SemaphoreType.DMA((2,)),
        ],
        compiler_params=_params("arbitrary"),
        name="moe_combine",
    )(tab, h, p, ri, rw, ys, gple, wpg, wpp, gfin)


def _log_gamma():
    return np.log(1.0 - 2.0 ** (-5.0 - np.arange(RET_HEADS, dtype=np.float64)))


def _rope_tables(pos):
    inv = 1.0 / (ROPE_BASE ** (np.arange(0, RET_DK, 2, dtype=np.float64) / RET_DK))
    ang = np.asarray(pos, np.float64)[:, None] * inv[None, :]
    cos, sin = np.cos(ang), np.sin(ang)
    return (jnp.asarray(np.concatenate([cos, cos], axis=-1), F32),
            jnp.asarray(np.concatenate([-sin, sin], axis=-1), F32))


def _decay_tables(c):
    lg = _log_gamma()
    idx = np.arange(c, dtype=np.float64)
    diff = idx[:, None] - idx[None, :]
    dmask = np.where((diff >= 0.0)[None], np.exp(lg[:, None, None] * np.maximum(diff, 0.0)[None]), 0.0)
    q_dec = np.exp(lg[None, :] * (idx[:, None] + 1.0))
    k_dec = np.exp(lg[None, :] * (c - 1.0 - idx[:, None]))
    c_dec = np.exp(lg * c)
    lanes = lambda a: np.repeat(a, RET_DK, axis=1)
    cdec = np.zeros((SUBLANES, LANES))
    cdec[:RET_HEADS] = c_dec[:, None]
    return tuple(jnp.asarray(a, F32) for a in (lanes(q_dec), lanes(k_dec), dmask, cdec))


def _router_params(w_rg, b_rg, w_re, b_re):
    wr = jnp.zeros((ROUTER_ROWS, D_MODEL), F32).at[:N_GROUPS].set(w_rg.T).at[SUBLANES:].set(w_re.T)
    br = jnp.full((ROUTER_ROWS,), NEG_BIG, F32).at[:N_GROUPS].set(b_rg).at[SUBLANES:].set(b_re.reshape(-1))
    return wr.astype(BF16), jnp.broadcast_to(br[:, None], (ROUTER_ROWS, LANES))


def _strict_upper(t):
    i = np.arange(t)
    return jnp.asarray(i[:, None] < i[None, :], BF16)


def _pad_rows(a, rows):
    return jnp.zeros((rows,) + a.shape[1:], a.dtype).at[:a.shape[0]].set(a)


def kernel(x_prompt, x_sample, state_conv, state_ret, p_prompt, p_sample, g_mix, w_in, conv_w, ret_gn, w_o, g_ffn,
           w_router_group, b_router_group, w_router_expert, b_router_expert, w_gate, w_up, w_down, g_ple,
           w_ple_proj, w_ple_gate, g_final):
    bp, tp, _ = x_prompt.shape
    bs = x_sample.shape[0]
    assert x_sample.shape[1] == 1 and g_mix.shape[0] == 1, "one layer, one new token per sample request"
    assert bs < SAMPLE_ROWS and tp % MIX_TILE == 0
    n_p = bp * tp

    tabs = _rope_tables(np.arange(tp)) + _decay_tables(TOK_TILE)
    wr, br = _router_params(w_router_group[0], b_router_group[0], w_router_expert[0], b_router_expert[0])
    win, wo, convw = w_in[0].astype(BF16), w_o[0].astype(BF16), _pad_rows(conv_w[0], SUBLANES)
    tri, ltri = _strict_upper(TOK_TILE), _strict_upper(N_EXPERTS).T

    h_p, xn_p, ri_p, rw_p, n_p_tab, conv_p, ret_p = _mixer_prompt(
        x_prompt, tabs, (g_mix, win, convw, ret_gn, wo, g_ffn, wr, br, tri, ltri))
    h_s, xn_s, ri_s, rw_s, n_s_tab, conv_s, ret_s = _mixer_sample(
        x_sample[:, 0], state_conv[0], state_ret[0], g_mix, win, convw, ret_gn, wo, g_ffn, wr, br, tri, ltri)
    h_p, xn_p = h_p.reshape(n_p, D_MODEL), xn_p.reshape(n_p, D_MODEL)

    n_tab = jnp.concatenate([n_p_tab, n_s_tab], axis=0)[:, :, 0].astype(I32)
    counts, tab = _run_table(n_tab)
    ri = jnp.concatenate([ri_p, ri_s], axis=1)
    n_tiles = n_tab.shape[0]

    xs = _dispatch(tab, xn_p, xn_s, ri)
    ys = _expert_ffn(_work_items(counts, n_tiles * PAIR_ROWS), xs, w_gate[0], w_up[0], w_down[0])

    wpg, wpp = w_ple_gate[0].astype(BF16), w_ple_proj[0].astype(BF16)
    gfin = g_final[None, :]
    p_s = _pad_rows(p_sample[0].reshape(bs, PLE_DIM), SAMPLE_ROWS)
    n_s_tiles = SAMPLE_ROWS // TOK_TILE
    y_p = _combine(tab[:n_tiles - n_s_tiles], h_p, p_prompt[0].reshape(n_p, PLE_DIM), ri_p, rw_p, ys, g_ple, wpg, wpp, gfin)
    y_s = _combine(tab[n_tiles - n_s_tiles:], h_s, p_s, ri_s, rw_s, ys, g_ple, wpg, wpp, gfin)
    return (y_p.reshape(bp, tp, D_MODEL), y_s[:bs].reshape(bs, 1, D_MODEL), conv_p[None], ret_p[None],
            conv_s[None], ret_s[None])
```

```python
import functools

import jax
import jax.numpy as jnp
import numpy as np
from jax import lax
from jax.experimental import pallas as pl
from jax.experimental.pallas import tpu as pltpu

F32, BF16, I32, U32 = jnp.float32, jnp.bfloat16, jnp.int32, jnp.uint32

D_MODEL = 1024
CONV_DIM = 512
CONV_WIDTH = 3
RET_DIM = 512
RET_HEADS = 4
RET_DK = 128
RET_DV = 128
RET_CHUNK = 128
ROPE_BASE = 10000.0
IN_PROJ_DIM = 3 * CONV_DIM + 4 * RET_DIM
N_GROUPS = 4
EXPERTS_PER_GROUP = 8
N_EXPERTS = 32
D_FF = 512
PLE_DIM = 256
EPS = 1e-6
PAST_LEN = 16384

LANES = 128
SUBLANES = 8
ROW_CHUNKS = D_MODEL // (2 * LANES)
ROUTER_ROWS = SUBLANES + N_EXPERTS
VMEM_LIMIT = 56 * 1024 * 1024
NEG_BIG = -1e30

TOK_TILE = 256
MIX_TILE = 1024
PAIR_ROWS = 2 * TOK_TILE
STEP_TILES = 2
SAMPLE_ROWS = STEP_TILES * TOK_TILE
FFN_TILE = 256
SAMPLE_BLOCK = 8
RUN_BITS = 9
SRC_BITS = 10
TAB_COLS = 384


def _rms(x, g):
    return x * lax.rsqrt(jnp.mean(x * x, axis=-1, keepdims=True) + EPS) * g


def _dot(a, b):
    return jnp.dot(a, b, preferred_element_type=F32)


def _dot_nt(a, b):
    return lax.dot_general(a, b, (((1,), (1,)), ((), ())), preferred_element_type=F32)


def _dot_tn(a, b):
    return lax.dot_general(a, b, (((0,), (0,)), ((), ())), preferred_element_type=F32)


def _rope(x, cos, sin_signed):
    return x * cos + pltpu.roll(x, RET_DK // 2, 1) * sin_signed


def _const_spec(shape):
    nd = len(shape)
    return pl.BlockSpec(shape, lambda *_: (0,) * nd)


def _params(*sem):
    return pltpu.CompilerParams(dimension_semantics=sem, vmem_limit_bytes=VMEM_LIMIT)


def _route(h, gffn, wr, br, tri, ltri):
    t = h.shape[0]
    xn = _rms(h, gffn).astype(BF16)
    lt = _dot_nt(wr, xn) + br[:, 0:1]
    row8 = lax.broadcasted_iota(I32, (SUBLANES, t), 0).astype(F32)
    gl = lt[0:SUBLANES]
    m = jnp.max(gl, axis=0, keepdims=True)
    g_top = 1.0 / jnp.sum(jnp.exp(gl - m), axis=0, keepdims=True)
    gidx = jnp.min(jnp.where(gl == m, row8, float(SUBLANES)), axis=0, keepdims=True)
    e_sel = jnp.where(gidx == 0.0, lt[8:16],
                      jnp.where(gidx == 1.0, lt[16:24], jnp.where(gidx == 2.0, lt[24:32], lt[32:40])))
    m1 = jnp.max(e_sel, axis=0, keepdims=True)
    i1 = jnp.min(jnp.where(e_sel == m1, row8, float(SUBLANES)), axis=0, keepdims=True)
    rest = jnp.where(row8 == i1, -jnp.inf, e_sel)
    m2 = jnp.max(rest, axis=0, keepdims=True)
    i2 = jnp.min(jnp.where(rest == m2, row8, float(SUBLANES)), axis=0, keepdims=True)
    d = jnp.exp(m2 - m1)
    w1 = g_top / (1.0 + d)
    w2 = g_top * d / (1.0 + d)
    e1 = gidx * float(EXPERTS_PER_GROUP) + i1
    e2 = gidx * float(EXPERTS_PER_GROUP) + i2
    row32 = lax.broadcasted_iota(I32, (N_EXPERTS, t), 0).astype(F32)
    a1 = jnp.where(row32 == e1, 1.0, 0.0)
    a2 = jnp.where(row32 == e2, 1.0, 0.0)
    a = a1 + a2
    n = jnp.broadcast_to(jnp.sum(a, axis=1, keepdims=True), (N_EXPERTS, LANES))
    start = _dot(ltri, n.astype(BF16))
    base = _dot(a.astype(BF16), tri) + start[:, 0:1]
    r1 = jnp.sum(a1 * base, axis=0, keepdims=True)
    r2 = jnp.sum(a2 * base, axis=0, keepdims=True)
    ri = jnp.where(row8 == 0.0, e1, jnp.where(row8 == 1.0, e2, jnp.where(row8 == 2.0, r1,
                                                                         jnp.where(row8 == 3.0, r2, 0.0))))
    rw = jnp.where(row8 == 0.0, w1, jnp.where(row8 == 1.0, w2, 0.0))
    return ri.astype(I32), rw, n, xn


def _group_norm_gate(o, gate):
    mu = jnp.mean(o, axis=-1, keepdims=True)
    oc = o - mu
    return oc * lax.rsqrt(jnp.mean(oc * oc, axis=-1, keepdims=True) + EPS) * gate


def _mixer_prompt_kernel(x_ref, cos_ref, sin_ref, qdec_ref, kdec_ref, dmask_ref, cdec_ref, gmix_ref, win_ref,
                         convw_ref, retgn_ref, wo_ref, gffn_ref, wr_ref, br_ref, tri_ref, ltri_ref,
                         h_ref, xn_ref, ri_ref, rw_ref, n_ref, conv_ref, ret_ref,
                         s_scr, z_scr, mix_scr):
    j = pl.program_id(1)
    n_j = pl.num_programs(1)
    tt = x_ref.shape[1]

    @pl.when(j == 0)
    def _():
        s_scr[...] = jnp.zeros_like(s_scr)
        z_scr[0:SUBLANES, :] = jnp.zeros((SUBLANES, CONV_DIM), F32)

    cw = convw_ref[...]
    retgn = retgn_ref[...]
    q0 = 3 * CONV_DIM

    def project(p0):
        x = x_ref[0, p0:p0 + TOK_TILE, :]
        proj = _dot(_rms(x, gmix_ref[...]).astype(BF16), win_ref[...])
        gb = proj[:, 0:CONV_DIM]
        z = proj[:, CONV_DIM:2 * CONV_DIM] * proj[:, 2 * CONV_DIM:3 * CONV_DIM]
        z0 = SUBLANES + p0
        z_scr[z0:z0 + TOK_TILE, :] = z
        yc = cw[0:1] * z_scr[z0 - 2:z0 - 2 + TOK_TILE, :] + cw[1:2] * z_scr[z0 - 1:z0 - 1 + TOK_TILE, :] + cw[2:3] * z
        mix_scr[p0:p0 + TOK_TILE, 0:CONV_DIM] = (gb * yc).astype(BF16)
        return x, proj

    def mix(p0, x, proj):
        cos = cos_ref[p0:p0 + TOK_TILE, :]
        sin = sin_ref[p0:p0 + TOK_TILE, :]
        for hh in range(RET_HEADS):
            l0 = hh * RET_DK
            qr = _rope(proj[:, q0 + l0:q0 + l0 + RET_DK], cos, sin)
            kr = _rope(proj[:, q0 + RET_DIM + l0:q0 + RET_DIM + l0 + RET_DK], cos, sin) * (RET_DK ** -0.5)
            v = proj[:, q0 + 2 * RET_DIM + l0:q0 + 2 * RET_DIM + l0 + RET_DV].astype(BF16)
            g = proj[:, q0 + 3 * RET_DIM + l0:q0 + 3 * RET_DIM + l0 + RET_DV]
            s_old = s_scr[hh]
            scores = _dot_nt(qr.astype(BF16), kr.astype(BF16)) * dmask_ref[hh]
            lhs = jnp.concatenate([scores.astype(BF16), (qr * qdec_ref[:, l0:l0 + RET_DK]).astype(BF16)], axis=1)
            o = _dot(lhs, jnp.concatenate([v, s_old.astype(BF16)], axis=0))
            kd = (kr * kdec_ref[:, l0:l0 + RET_DK]).T.astype(BF16)
            s_scr[hh] = s_old * cdec_ref[hh:hh + 1, :] + _dot(kd, v)
            gate = retgn[:, l0:l0 + RET_DV] * (g * jax.nn.sigmoid(g))
            mix_scr[p0:p0 + TOK_TILE, CONV_DIM + l0:CONV_DIM + l0 + RET_DV] = _group_norm_gate(o, gate).astype(BF16)
        h = x + _dot(mix_scr[p0:p0 + TOK_TILE, :], wo_ref[...])
        h_ref[0, p0:p0 + TOK_TILE, :] = h
        return h

    def route(part, h):
        p0 = part * TOK_TILE
        ri, rw, n, xn = _route(h, gffn_ref[...], wr_ref[...], br_ref[...], tri_ref[...], ltri_ref[...])
        ri_ref[:, p0:p0 + TOK_TILE] = ri
        rw_ref[:, p0:p0 + TOK_TILE] = rw
        n_ref[part] = n
        xn_ref[0, p0:p0 + TOK_TILE, :] = xn

    parts = tt // TOK_TILE
    nxt = project(0)
    for part in range(parts):
        h = mix(part * TOK_TILE, *nxt)
        if part + 1 < parts:
            nxt = project((part + 1) * TOK_TILE)
        route(part, h)

    @pl.when(j == n_j - 1)
    def _():
        conv_ref[0] = z_scr[SUBLANES + tt - 2:SUBLANES + tt, :]
        ret_ref[0] = s_scr[...]

    z_scr[0:SUBLANES, :] = z_scr[tt:tt + SUBLANES, :]


def _mixer_prompt(x, tabs, wts):
    bsz, t, _ = x.shape
    tt = MIX_TILE
    n_j = t // tt
    parts = tt // TOK_TILE
    n_tok = bsz * t
    args = (x,) + tuple(tabs) + tuple(wts)
    in_specs = [
        pl.BlockSpec((1, tt, D_MODEL), lambda b, j: (b, j, 0)),
        pl.BlockSpec((tt, LANES), lambda b, j: (j, 0)),
        pl.BlockSpec((tt, LANES), lambda b, j: (j, 0)),
    ] + [_const_spec(a.shape) for a in args[3:]]
    out_shape = (
        jax.ShapeDtypeStruct((bsz, t, D_MODEL), F32),
        jax.ShapeDtypeStruct((bsz, t, D_MODEL), BF16),
        jax.ShapeDtypeStruct((SUBLANES, n_tok), I32),
        jax.ShapeDtypeStruct((SUBLANES, n_tok), F32),
        jax.ShapeDtypeStruct((bsz * n_j * parts, N_EXPERTS, LANES), F32),
        jax.ShapeDtypeStruct((bsz, CONV_WIDTH - 1, CONV_DIM), F32),
        jax.ShapeDtypeStruct((bsz, RET_HEADS, RET_DK, RET_DV), F32),
    )
    out_specs = (
        pl.BlockSpec((1, tt, D_MODEL), lambda b, j: (b, j, 0)),
        pl.BlockSpec((1, tt, D_MODEL), lambda b, j: (b, j, 0)),
        pl.BlockSpec((SUBLANES, tt), lambda b, j: (0, b * n_j + j)),
        pl.BlockSpec((SUBLANES, tt), lambda b, j: (0, b * n_j + j)),
        pl.BlockSpec((parts, N_EXPERTS, LANES), lambda b, j: (b * n_j + j, 0, 0)),
        pl.BlockSpec((1, CONV_WIDTH - 1, CONV_DIM), lambda b, j: (b, 0, 0)),
        pl.BlockSpec((1, RET_HEADS, RET_DK, RET_DV), lambda b, j: (b, 0, 0, 0)),
    )
    return pl.pallas_call(
        _mixer_prompt_kernel,
        grid=(bsz, n_j),
        in_specs=in_specs,
        out_specs=out_specs,
        out_shape=out_shape,
        scratch_shapes=[
            pltpu.VMEM((RET_HEADS, RET_DK, RET_DV), F32),
            pltpu.VMEM((tt + SUBLANES, CONV_DIM), F32),
            pltpu.VMEM((tt, D_MODEL), BF16),
        ],
        compiler_params=_params("arbitrary", "arbitrary"),
        name="mixer_prompt",
    )(*args)


def _sample_proj_kernel(x_ref, sc_ref, cos_ref, sin_ref, gmix_ref, win_ref, convw_ref, retgn_ref,
                        yc_ref, q_ref, k_ref, v_ref, gate_ref, conv_ref):
    x = x_ref[...]
    u = _rms(x, gmix_ref[...]).astype(BF16)
    proj = _dot(u, win_ref[...])
    gb = proj[:, 0:CONV_DIM]
    z = proj[:, CONV_DIM:2 * CONV_DIM] * proj[:, 2 * CONV_DIM:3 * CONV_DIM]
    cw = convw_ref[...]
    buf0, buf1 = sc_ref[:, 0:CONV_DIM], sc_ref[:, CONV_DIM:2 * CONV_DIM]
    yc_ref[...] = gb * (cw[0:1] * buf0 + cw[1:2] * buf1 + cw[2:3] * z)
    conv_ref[:, 0:CONV_DIM] = buf1
    conv_ref[:, CONV_DIM:2 * CONV_DIM] = z
    q0 = 3 * CONV_DIM
    cos, sin = cos_ref[0:1, :], sin_ref[0:1, :]
    retgn = retgn_ref[...]
    for hh in range(RET_HEADS):
        l0 = hh * RET_DK
        q_ref[:, l0:l0 + RET_DK] = _rope(proj[:, q0 + l0:q0 + l0 + RET_DK], cos, sin)
        k_ref[:, l0:l0 + RET_DK] = _rope(proj[:, q0 + RET_DIM + l0:q0 + RET_DIM + l0 + RET_DK], cos, sin) * (RET_DK ** -0.5)
    v_ref[...] = proj[:, q0 + 2 * RET_DIM:q0 + 3 * RET_DIM]
    g = proj[:, q0 + 3 * RET_DIM:q0 + 4 * RET_DIM]
    gate_ref[...] = retgn * (g * jax.nn.sigmoid(g))


def _sample_proj(x, sc, cos, sin, gmix, win, convw, retgn):
    n = x.shape[0]
    outs = (
        jax.ShapeDtypeStruct((n, CONV_DIM), F32), jax.ShapeDtypeStruct((n, RET_DIM), F32),
        jax.ShapeDtypeStruct((n, RET_DIM), F32), jax.ShapeDtypeStruct((n, RET_DIM), F32),
        jax.ShapeDtypeStruct((n, RET_DIM), F32), jax.ShapeDtypeStruct((n, 2 * CONV_DIM), F32),
    )
    args = (x, sc, cos, sin, gmix, win, convw, retgn)
    return pl.pallas_call(
        _sample_proj_kernel,
        grid=(1,),
        in_specs=[_const_spec(a.shape) for a in args],
        out_specs=tuple(_const_spec(o.shape) for o in outs),
        out_shape=outs,
        compiler_params=_params("arbitrary"),
        name="sample_proj",
    )(*args)


def _sample_state_kernel(s_ref, q_ref, k_ref, kt_ref, v_ref, dec_ref, snew_ref, o_ref):
    for r in range(s_ref.shape[0]):
        for hh in range(RET_HEADS):
            l0 = hh * RET_DV
            s = s_ref[r, hh]
            q = q_ref[r:r + 1, l0:l0 + RET_DK]
            k = k_ref[r:r + 1, l0:l0 + RET_DK]
            kc = kt_ref[0, hh, :, r:r + 1]
            vr = v_ref[r:r + 1, l0:l0 + RET_DV]
            qdec = dec_ref[hh:hh + 1, :]
            cdec = dec_ref[RET_HEADS + hh:RET_HEADS + hh + 1, :]
            qk = jnp.sum(q * k, axis=1, keepdims=True)
            o_ref[r:r + 1, l0:l0 + RET_DV] = qk * vr + _dot((q * qdec).astype(BF16), s.astype(BF16))
            snew_ref[r, hh] = s * cdec + kc * vr


def _sample_state(s, q, k, kt, v, dec):
    n = s.shape[0]
    nb = SAMPLE_BLOCK
    return pl.pallas_call(
        _sample_state_kernel,
        grid=(n // nb,),
        in_specs=[
            pl.BlockSpec((nb, RET_HEADS, RET_DK, RET_DV), lambda i: (i, 0, 0, 0)),
            pl.BlockSpec((nb, RET_DIM), lambda i: (i, 0)),
            pl.BlockSpec((nb, RET_DIM), lambda i: (i, 0)),
            pl.BlockSpec((1, RET_HEADS, RET_DK, nb), lambda i: (i, 0, 0, 0)),
            pl.BlockSpec((nb, RET_DIM), lambda i: (i, 0)),
            _const_spec(dec.shape),
        ],
        out_specs=(
            pl.BlockSpec((nb, RET_HEADS, RET_DK, RET_DV), lambda i: (i, 0, 0, 0)),
            pl.BlockSpec((nb, RET_DIM), lambda i: (i, 0)),
        ),
        out_shape=(jax.ShapeDtypeStruct(s.shape, F32), jax.ShapeDtypeStruct((n, RET_DIM), F32)),
        compiler_params=_params("arbitrary"),
        name="sample_state",
    )(s, q, k, kt, v, dec)


def _sample_out_kernel(x_ref, yc_ref, o_ref, gate_ref, wo_ref, gffn_ref, wr_ref, br_ref, tri_ref, ltri_ref,
                       h_ref, xn_ref, ri_ref, rw_ref, n_ref):
    n = x_ref.shape[0]
    parts = [yc_ref[...].astype(BF16)]
    for hh in range(RET_HEADS):
        l0 = hh * RET_DV
        parts.append(_group_norm_gate(o_ref[:, l0:l0 + RET_DV], gate_ref[:, l0:l0 + RET_DV]).astype(BF16))
    h_ref[0:n, :] = x_ref[...] + _dot(jnp.concatenate(parts, axis=-1), wo_ref[...])
    h_ref[n:SAMPLE_ROWS, :] = jnp.zeros((SAMPLE_ROWS - n, D_MODEL), F32)
    for part in range(SAMPLE_ROWS // TOK_TILE):
        p0 = part * TOK_TILE
        ri, rw, cnt, xn = _route(h_ref[p0:p0 + TOK_TILE, :], gffn_ref[...], wr_ref[...], br_ref[...], tri_ref[...],
                                 ltri_ref[...])
        ri_ref[:, p0:p0 + TOK_TILE] = ri
        rw_ref[:, p0:p0 + TOK_TILE] = rw
        n_ref[part] = cnt
        xn_ref[p0:p0 + TOK_TILE, :] = xn


def _sample_out(x, yc, o, gate, wo, gffn, wr, br, tri, ltri):
    outs = (
        jax.ShapeDtypeStruct((SAMPLE_ROWS, D_MODEL), F32), jax.ShapeDtypeStruct((SAMPLE_ROWS, D_MODEL), BF16),
        jax.ShapeDtypeStruct((SUBLANES, SAMPLE_ROWS), I32), jax.ShapeDtypeStruct((SUBLANES, SAMPLE_ROWS), F32),
        jax.ShapeDtypeStruct((SAMPLE_ROWS // TOK_TILE, N_EXPERTS, LANES), F32),
    )
    args = (x, yc, o, gate, wo, gffn, wr, br, tri, ltri)
    return pl.pallas_call(
        _sample_out_kernel,
        grid=(1,),
        in_specs=[_const_spec(a.shape) for a in args],
        out_specs=tuple(_const_spec(o_.shape) for o_ in outs),
        out_shape=outs,
        compiler_params=_params("arbitrary"),
        name="sample_out",
    )(*args)


def _mixer_sample(x, state_conv, state_ret, gmix, win, convw, retgn, wo, gffn, wr, br, tri, ltri):
    n = x.shape[0]
    lg = _log_gamma()
    cos, sin = _rope_tables(np.full((SUBLANES,), PAST_LEN))
    yc, q, k, v, gate, conv_new = _sample_proj(x, state_conv.reshape(n, 2 * CONV_DIM), cos, sin, gmix, win, convw, retgn)
    nb = SAMPLE_BLOCK
    cols = lambda a: a.reshape(n // nb, nb, RET_HEADS, RET_DK).transpose(0, 2, 3, 1)
    step = np.exp(lg[:, None] * 1.0)
    dec = jnp.asarray(np.broadcast_to(np.concatenate([step, step], axis=0), (2 * RET_HEADS, LANES)), F32)
    s_new, o = _sample_state(state_ret, q, k, cols(k), v, dec)
    h, xn, ri, rw, cnt = _sample_out(x, yc, o, gate, wo, gffn, wr, br, tri, ltri)
    return h, xn, ri, rw, cnt, conv_new.reshape(n, 2, CONV_DIM), s_new


def _load_rows(ref, n_rows):
    words = [ref[pl.ds(c, n_rows, stride=ROW_CHUNKS), :] for c in range(ROW_CHUNKS)]
    half = lambda i: [pltpu.unpack_elementwise(w, index=i, packed_dtype=BF16, unpacked_dtype=F32) for w in words]
    return jnp.concatenate(half(0) + half(1), axis=-1)


def _store_rows(ref, val):
    half = D_MODEL // 2
    for c in range(ROW_CHUNKS):
        pair = [val[:, c * LANES:(c + 1) * LANES], val[:, half + c * LANES:half + (c + 1) * LANES]]
        ref[pl.ds(c, val.shape[0], stride=ROW_CHUNKS), :] = pltpu.pack_elementwise(pair, packed_dtype=BF16)


def _row_slice(ref, row, rows):
    return ref.at[pl.ds(pl.multiple_of(row * ROW_CHUNKS, ROW_CHUNKS), rows * ROW_CHUNKS)]


def _run_copies(table, make_copy):
    for b in range(RUN_BITS):
        def body(j, carry, b=b):
            entry = table(N_EXPERTS + N_EXPERTS * b + j)
            make_copy(entry & ((1 << SRC_BITS) - 1), entry >> SRC_BITS, 1 << b).start()
            return carry

        lax.fori_loop(0, table(b), body, 0)


def _pair_rows_iota():
    return lax.broadcasted_iota(I32, (PAIR_ROWS, TOK_TILE), 0)


def _dispatch_kernel(n_p, tab_ref, xp_ref, xs_in_ref, ri_ref, xs_ref, idx_smem, y_scr, isem, rsem):
    i, n = pl.program_id(0), pl.num_programs(0)
    slot = i % 2
    step_rows = STEP_TILES * PAIR_ROWS

    def idx_copy(step, sl):
        return pltpu.make_async_copy(tab_ref.at[pl.ds(step * STEP_TILES, STEP_TILES)], idx_smem.at[sl], isem.at[sl])

    def wait_step(sl):
        pltpu.make_async_copy(_row_slice(xs_ref, 0, step_rows), y_scr.at[sl], rsem.at[sl]).wait()

    @pl.when(i == 0)
    def _():
        idx_copy(0, 0).start()

    @pl.when(i + 1 < n)
    def _():
        idx_copy(i + 1, 1 - slot).start()

    @pl.when(i >= 2)
    def _():
        wait_step(slot)

    xn = jnp.where(i < n_p, xp_ref[...], xs_in_ref[...])
    ri = ri_ref[...]
    rows = _pair_rows_iota()
    for g in range(STEP_TILES):
        tok = slice(g * TOK_TILE, (g + 1) * TOK_TILE)
        perm = jnp.where((rows == ri[2:3, tok]) | (rows == ri[3:4, tok]), 1.0, 0.0).astype(BF16)
        y = _dot(perm, xn[tok, :])
        _store_rows(y_scr.at[slot, pl.ds(g * PAIR_ROWS * ROW_CHUNKS, PAIR_ROWS * ROW_CHUNKS)], y)

    idx_copy(i, slot).wait()
    for g in range(STEP_TILES):
        _run_copies(lambda col, g=g: idx_smem[slot, g, col], lambda src, dst, rows_, g=g: pltpu.make_async_copy(
            _row_slice(y_scr.at[slot], g * PAIR_ROWS + src, rows_), _row_slice(xs_ref, dst, rows_), rsem.at[slot]))

    @pl.when(i == n - 1)
    def _():
        wait_step(slot)

    @pl.when((i == n - 1) & (n >= 2))
    def _():
        wait_step(1 - slot)


def _dispatch(tab, xn_p, xn_s, ri):
    tt = STEP_TILES * TOK_TILE
    n_p = xn_p.shape[0] // tt
    n = tab.shape[0] // STEP_TILES
    return pl.pallas_call(
        functools.partial(_dispatch_kernel, n_p),
        grid=(n,),
        in_specs=[
            pl.BlockSpec(memory_space=pl.ANY),
            pl.BlockSpec((tt, D_MODEL), lambda i: (jnp.minimum(i, n_p - 1), 0)),
            pl.BlockSpec((tt, D_MODEL), lambda i: (jnp.maximum(i - n_p, 0), 0)),
            pl.BlockSpec((SUBLANES, tt), lambda i: (0, i)),
        ],
        out_specs=pl.BlockSpec(memory_space=pl.ANY),
        out_shape=jax.ShapeDtypeStruct((tab.shape[0] * PAIR_ROWS * ROW_CHUNKS, LANES), U32),
        scratch_shapes=[
            pltpu.SMEM((2, STEP_TILES, TAB_COLS), I32),
            pltpu.VMEM((2, STEP_TILES * PAIR_ROWS * ROW_CHUNKS, LANES), U32),
            pltpu.SemaphoreType.DMA((2,)),
            pltpu.SemaphoreType.DMA((2,)),
        ],
        compiler_params=_params("arbitrary"),
        name="moe_dispatch",
    )(tab, xn_p, xn_s, ri)


def _ffn_kernel(tile_ref, exp_ref, lo_ref, hi_ref, nxt_ref, slot_ref, xs_ref, wg_ref, wu_ref, wd_ref, o_ref,
                x_buf, wg_buf, wu_buf, wd_buf, wgu_scr, wd_scr, y_scr, xsem, wsem):
    w, n_w = pl.program_id(0), pl.num_programs(0)
    prev = jnp.maximum(w - 1, 0)
    lo, hi = lo_ref[w], hi_ref[w]
    e, slot = exp_ref[w], slot_ref[w]
    first = (w == 0) | (tile_ref[w] != tile_ref[prev])

    def x_copy(step):
        rows = FFN_TILE * ROW_CHUNKS
        return pltpu.make_async_copy(xs_ref.at[pl.ds(pl.multiple_of(tile_ref[step] * rows, rows), rows)],
                                     x_buf.at[step % 3], xsem.at[step % 3])

    @pl.when(w == 0)
    def _():
        x_copy(0).start()

    @pl.when((w == 0) & (n_w > 1))
    def _():
        x_copy(1).start()

    @pl.when(w + 2 < n_w)
    def _():
        x_copy(w + 2).start()

    x_copy(w).wait()

    def weight_copies(expert, sl):
        return [pltpu.make_async_copy(src.at[expert], buf.at[sl], wsem.at[sl])
                for src, buf in ((wg_ref, wg_buf), (wu_ref, wu_buf), (wd_ref, wd_buf))]

    @pl.when(w == 0)
    def _():
        for cp in weight_copies(e, slot):
            cp.start(priority=1)

    @pl.when((w == 0) | (e != exp_ref[prev]))
    def _():
        for cp in weight_copies(e, slot):
            cp.wait()

        @pl.when(nxt_ref[w] != e)
        def _():
            for cp in weight_copies(nxt_ref[w], 1 - slot):
                cp.start(priority=1)

        wgu_scr[:, 0:D_FF] = wg_buf[slot].astype(BF16)
        wgu_scr[:, D_FF:2 * D_FF] = wu_buf[slot].astype(BF16)
        wd_scr[...] = wd_buf[slot].astype(BF16)

    @pl.when(w == 0)
    def _():
        y_scr[...] = jnp.zeros_like(y_scr)

    def compute(r0, n_rows):
        chunk = pl.ds(r0 * ROW_CHUNKS, n_rows * ROW_CHUNKS)
        gu = _dot(_load_rows(x_buf.at[w % 3, chunk], n_rows).astype(BF16), wgu_scr[...])
        g, u = gu[:, 0:D_FF], gu[:, D_FF:2 * D_FF]
        y = _dot((g * jax.nn.sigmoid(g) * u).astype(BF16), wd_scr[...])
        row = r0 + lax.broadcasted_iota(I32, (n_rows, 1), 0)
        y = jnp.where(((row >= lo) & (row < hi)) | first, y, y_scr[r0:r0 + n_rows, :])
        y_scr[r0:r0 + n_rows, :] = y
        _store_rows(o_ref.at[chunk], y)

    half = FFN_TILE // 2
    top, bottom = hi <= half, lo >= half

    @pl.when((hi > lo) & top)
    def _():
        compute(0, half)

    @pl.when((hi > lo) & bottom)
    def _():
        compute(half, half)

    @pl.when((hi > lo) & jnp.logical_not(top | bottom))
    def _():
        compute(0, FFN_TILE)


def _expert_ffn(items, xs, w_gate, w_up, w_down):
    n_items = items[0].shape[0]
    row_spec = pl.BlockSpec((FFN_TILE * ROW_CHUNKS, LANES), lambda w, t, *_: (t[w], 0))
    grid_spec = pltpu.PrefetchScalarGridSpec(
        num_scalar_prefetch=len(items),
        grid=(n_items,),
        in_specs=[pl.BlockSpec(memory_space=pl.ANY)] * 4,
        out_specs=row_spec,
        scratch_shapes=[
            pltpu.VMEM((3, FFN_TILE * ROW_CHUNKS, LANES), U32),
            pltpu.VMEM((2, D_MODEL, D_FF), F32), pltpu.VMEM((2, D_MODEL, D_FF), F32),
            pltpu.VMEM((2, D_FF, D_MODEL), F32),
            pltpu.VMEM((D_MODEL, 2 * D_FF), BF16), pltpu.VMEM((D_FF, D_MODEL), BF16),
            pltpu.VMEM((FFN_TILE, D_MODEL), F32),
            pltpu.SemaphoreType.DMA((3,)),
            pltpu.SemaphoreType.DMA((2,)),
        ],
    )
    return pl.pallas_call(
        _ffn_kernel,
        grid_spec=grid_spec,
        out_shape=jax.ShapeDtypeStruct(xs.shape, U32),
        compiler_params=_params("arbitrary"),
        name="moe_ffn",
    )(*items, xs, w_gate, w_up, w_down)


def _work_items(counts, n_rows):
    n_tiles = n_rows // FFN_TILE
    n_items = n_tiles + N_EXPERTS - 1
    off = jnp.cumsum(counts) - counts
    first_tile = off // FFN_TILE
    last_tile = jnp.maximum(off + counts - 1, off) // FFN_TILE
    n_e = jnp.where(counts > 0, last_tile - first_tile + 1, 0)
    start = jnp.cumsum(n_e) - n_e
    total = jnp.sum(n_e)
    w = jnp.minimum(jnp.arange(n_items, dtype=I32), total - 1)
    ids = jnp.arange(N_EXPERTS, dtype=I32)[None, :]
    e = jnp.max(jnp.where((start[None, :] <= w[:, None]) & (n_e[None, :] > 0), ids, 0), axis=1)
    pick = lambda a: jnp.sum(jnp.where(ids == e[:, None], a[None, :], 0), axis=1)
    tile = pick(first_tile) + (w - pick(start))
    lo = jnp.clip(pick(off) - tile * FFN_TILE, 0, FFN_TILE)
    hi = jnp.clip(pick(off + counts) - tile * FFN_TILE, 0, FFN_TILE)
    hi = jnp.where(jnp.arange(n_items) < total, hi, lo)
    used = n_e[None, :] > 0
    nxt = jnp.min(jnp.where(used & (ids > e[:, None]), ids, N_EXPERTS), axis=1)
    nxt = jnp.where(nxt == N_EXPERTS, e, nxt)
    slot = jnp.sum(jnp.where(used & (ids < e[:, None]), 1, 0), axis=1) % 2
    return tuple(a.astype(I32) for a in (tile, e, lo, hi, nxt, slot))


def _run_table(n):
    tiles = n.shape[0]
    counts = jnp.sum(n, axis=0)
    goff = jnp.cumsum(counts) - counts
    dst0 = goff[None, :] + jnp.cumsum(n, axis=0) - n
    src0 = jnp.cumsum(n, axis=1) - n
    bits = jnp.arange(RUN_BITS, dtype=I32)[:, None, None]
    flag = (n[None] >> bits) & 1
    done = (n[None] >> (bits + 1)) << (bits + 1)
    entry = ((dst0[None] + done) << SRC_BITS) | (src0[None] + done)
    rank = jnp.cumsum(flag, axis=2) - flag
    slot = jnp.arange(N_EXPERTS, dtype=I32)
    sel = (flag[..., None] == 1) & (rank[..., None] == slot)
    packed = jnp.sum(jnp.where(sel, entry[..., None], 0), axis=2)
    m = jnp.sum(flag, axis=2).T
    tab = jnp.zeros((tiles, TAB_COLS), I32)
    tab = tab.at[:, :RUN_BITS].set(m)
    tab = tab.at[:, N_EXPERTS:N_EXPERTS * (RUN_BITS + 1)].set(packed.transpose(1, 0, 2).reshape(tiles, -1))
    return counts, tab


def _combine_kernel(g_tiles, n, tab_ref, h_ref, p_ref, ri_ref, rw_ref, ys_ref, gple_ref, wpg_ref, wpp_ref, gfin_ref,
                    o_ref, idx_smem, rows_scr, isem, rsem):
    i = pl.program_id(0)
    slot = i % 2

    def idx_copy(step, sl):
        return pltpu.make_async_copy(tab_ref.at[pl.ds(step * g_tiles, g_tiles)], idx_smem.at[sl], isem.at[sl])

    def gather(sl):
        for g in range(g_tiles):
            _run_copies(lambda col, g=g: idx_smem[sl, g, col], lambda src, dst, rows_, g=g: pltpu.make_async_copy(
                _row_slice(ys_ref, dst, rows_), _row_slice(rows_scr.at[sl], g * PAIR_ROWS + src, rows_), rsem.at[sl]))

    @pl.when(i == 0)
    def _():
        idx_copy(0, 0).start()
        idx_copy(0, 0).wait()
        gather(0)

    if n >= 2:
        @pl.when(i == 0)
        def _():
            idx_copy(1, 1).start()

        @pl.when(i + 1 < n)
        def _():
            idx_copy(i + 1, 1 - slot).wait()
            gather(1 - slot)

    @pl.when(i + 2 < n)
    def _():
        idx_copy(i + 2, slot).start()

    pltpu.make_async_copy(_row_slice(ys_ref, 0, g_tiles * PAIR_ROWS), rows_scr.at[slot], rsem.at[slot]).wait()

    ri, rw = ri_ref[...], rw_ref[...]
    rows = _pair_rows_iota()
    tiles = range(g_tiles)
    tok = lambda g: slice(g * TOK_TILE, (g + 1) * TOK_TILE)
    h = []
    for g in tiles:
        r1, r2 = ri[2:3, tok(g)], ri[3:4, tok(g)]
        wperm = jnp.where(rows == r1, rw[0:1, tok(g)], 0.0) + jnp.where(rows == r2, rw[1:2, tok(g)], 0.0)
        srows = _load_rows(rows_scr.at[slot, pl.ds(g * PAIR_ROWS * ROW_CHUNKS, PAIR_ROWS * ROW_CHUNKS)], PAIR_ROWS)
        h.append(h_ref[tok(g), :] + _dot_tn(wperm.astype(BF16), srows.astype(BF16)))
    gate = [jax.nn.sigmoid(_dot(_rms(h[g], gple_ref[...]).astype(BF16), wpg_ref[...])) for g in tiles]
    for g in tiles:
        hg = h[g] + _dot(p_ref[tok(g), :].astype(BF16), wpp_ref[...]) * gate[g]
        o_ref[tok(g), :] = _rms(hg, gfin_ref[...])


def _combine(tab, h, p, ri, rw, ys, gple, wpg, wpp, gfin):
    n_tiles = tab.shape[0]
    g_tiles = STEP_TILES
    n = n_tiles // g_tiles
    tt = g_tiles * TOK_TILE
    return pl.pallas_call(
        functools.partial(_combine_kernel, g_tiles, n),
        grid=(n,),
        in_specs=[
            pl.BlockSpec(memory_space=pl.ANY),
            pl.BlockSpec((tt, D_MODEL), lambda i: (i, 0)),
            pl.BlockSpec((tt, PLE_DIM), lambda i: (i, 0)),
            pl.BlockSpec((SUBLANES, tt), lambda i: (0, i)),
            pl.BlockSpec((SUBLANES, tt), lambda i: (0, i)),
            pl.BlockSpec(memory_space=pl.ANY),
            _const_spec(gple.shape), _const_spec(wpg.shape), _const_spec(wpp.shape), _const_spec(gfin.shape),
        ],
        out_specs=pl.BlockSpec((tt, D_MODEL), lambda i: (i, 0)),
        out_shape=jax.ShapeDtypeStruct((n_tiles * TOK_TILE, D_MODEL), F32),
        scratch_shapes=[
            pltpu.SMEM((2, g_tiles, TAB_COLS), I32),
            pltpu.VMEM((2, g_tiles * PAIR_ROWS * ROW_CHUNKS, LANES), U32),
            pltpu.SemaphoreType.DMA((2,)),
            pltpu.SemaphoreType.DMA((2,)),
        ],
        compiler_params=_params("arbitrary"),
        name="moe_combine",
    )(tab, h, p, ri, rw, ys, gple, wpg, wpp, gfin)


def _log_gamma():
    return np.log(1.0 - 2.0 ** (-5.0 - np.arange(RET_HEADS, dtype=np.float64)))


def _rope_tables(pos):
    inv = 1.0 / (ROPE_BASE ** (np.arange(0, RET_DK, 2, dtype=np.float64) / RET_DK))
    ang = np.asarray(pos, np.float64)[:, None] * inv[None, :]
    cos, sin = np.cos(ang), np.sin(ang)
    return (jnp.asarray(np.concatenate([cos, cos], axis=-1), F32),
            jnp.asarray(np.concatenate([-sin, sin], axis=-1), F32))


def _decay_tables(c):
    lg = _log_gamma()
    idx = np.arange(c, dtype=np.float64)
    diff = idx[:, None] - idx[None, :]
    dmask = np.where((diff >= 0.0)[None], np.exp(lg[:, None, None] * np.maximum(diff, 0.0)[None]), 0.0)
    q_dec = np.exp(lg[None, :] * (idx[:, None] + 1.0))
    k_dec = np.exp(lg[None, :] * (c - 1.0 - idx[:, None]))
    c_dec = np.exp(lg * c)
    lanes = lambda a: np.repeat(a, RET_DK, axis=1)
    cdec = np.zeros((SUBLANES, LANES))
    cdec[:RET_HEADS] = c_dec[:, None]
    return tuple(jnp.asarray(a, F32) for a in (lanes(q_dec), lanes(k_dec), dmask, cdec))


def _router_params(w_rg, b_rg, w_re, b_re):
    wr = jnp.zeros((ROUTER_ROWS, D_MODEL), F32).at[:N_GROUPS].set(w_rg.T).at[SUBLANES:].set(w_re.T)
    br = jnp.full((ROUTER_ROWS,), NEG_BIG, F32).at[:N_GROUPS].set(b_rg).at[SUBLANES:].set(b_re.reshape(-1))
    return wr.astype(BF16), jnp.broadcast_to(br[:, None], (ROUTER_ROWS, LANES))


def _strict_upper(t):
    i = np.arange(t)
    return jnp.asarray(i[:, None] < i[None, :], BF16)


def _pad_rows(a, rows):
    return jnp.zeros((rows,) + a.shape[1:], a.dtype).at[:a.shape[0]].set(a)


def kernel(x_prompt, x_sample, state_conv, state_ret, p_prompt, p_sample, g_mix, w_in, conv_w, ret_gn, w_o, g_ffn,
           w_router_group, b_router_group, w_router_expert, b_router_expert, w_gate, w_up, w_down, g_ple,
           w_ple_proj, w_ple_gate, g_final):
    bp, tp, _ = x_prompt.shape
    bs = x_sample.shape[0]
    assert x_sample.shape[1] == 1 and g_mix.shape[0] == 1, "one layer, one new token per sample request"
    assert bs < SAMPLE_ROWS and tp % MIX_TILE == 0
    n_p = bp * tp

    tabs = _rope_tables(np.arange(tp)) + _decay_tables(TOK_TILE)
    wr, br = _router_params(w_router_group[0], b_router_group[0], w_router_expert[0], b_router_expert[0])
    win, wo, convw = w_in[0].astype(BF16), w_o[0].astype(BF16), _pad_rows(conv_w[0], SUBLANES)
    tri, ltri = _strict_upper(TOK_TILE), _strict_upper(N_EXPERTS).T

    h_p, xn_p, ri_p, rw_p, n_p_tab, conv_p, ret_p = _mixer_prompt(
        x_prompt, tabs, (g_mix, win, convw, ret_gn, wo, g_ffn, wr, br, tri, ltri))
    h_s, xn_s, ri_s, rw_s, n_s_tab, conv_s, ret_s = _mixer_sample(
        x_sample[:, 0], state_conv[0], state_ret[0], g_mix, win, convw, ret_gn, wo, g_ffn, wr, br, tri, ltri)
    h_p, xn_p = h_p.reshape(n_p, D_MODEL), xn_p.reshape(n_p, D_MODEL)

    n_tab = jnp.concatenate([n_p_tab, n_s_tab], axis=0)[:, :, 0].astype(I32)
    counts, tab = _run_table(n_tab)
    ri = jnp.concatenate([ri_p, ri_s], axis=1)
    n_tiles = n_tab.shape[0]

    xs = _dispatch(tab, xn_p, xn_s, ri)
    ys = _expert_ffn(_work_items(counts, n_tiles * PAIR_ROWS), xs, w_gate[0], w_up[0], w_down[0])

    wpg, wpp = w_ple_gate[0].astype(BF16), w_ple_proj[0].astype(BF16)
    gfin = g_final[None, :]
    p_s = _pad_rows(p_sample[0].reshape(bs, PLE_DIM), SAMPLE_ROWS)
    n_s_tiles = SAMPLE_ROWS // TOK_TILE
    y_p = _combine(tab[:n_tiles - n_s_tiles], h_p, p_prompt[0].reshape(n_p, PLE_DIM), ri_p, rw_p, ys, g_ple, wpg, wpp, gfin)
    y_s = _combine(tab[n_tiles - n_s_tiles:], h_s, p_s, ri_s, rw_s, ys, g_ple, wpg, wpp, gfin)
    return (y_p.reshape(bp, tp, D_MODEL), y_s[:bs].reshape(bs, 1, D_MODEL), conv_p[None], ret_p[None],
            conv_s[None], ret_s[None])
```

```python
import functools

import jax
import jax.numpy as jnp
import numpy as np
from jax import lax
from jax.experimental import pallas as pl
from jax.experimental.pallas import tpu as pltpu

F32, BF16, I32, U32 = jnp.float32, jnp.bfloat16, jnp.int32, jnp.uint32

D_MODEL = 1024
CONV_DIM = 512
CONV_WIDTH = 3
RET_DIM = 512
RET_HEADS = 4
RET_DK = 128
RET_DV = 128
RET_CHUNK = 128
ROPE_BASE = 10000.0
IN_PROJ_DIM = 3 * CONV_DIM + 4 * RET_DIM
N_GROUPS = 4
EXPERTS_PER_GROUP = 8
N_EXPERTS = 32
D_FF = 512
PLE_DIM = 256
EPS = 1e-6
PAST_LEN = 16384

LANES = 128
SUBLANES = 8
ROW_CHUNKS = D_MODEL // (2 * LANES)
ROUTER_ROWS = SUBLANES + N_EXPERTS
VMEM_LIMIT = 56 * 1024 * 1024
NEG_BIG = -1e30

TOK_TILE = 256
MIX_TILE = 1024
PAIR_ROWS = 2 * TOK_TILE
STEP_TILES = 2
SAMPLE_ROWS = STEP_TILES * TOK_TILE
FFN_TILE = 256
SAMPLE_BLOCK = 8
RUN_BITS = 9
SRC_BITS = 10
TAB_COLS = 384


def _rms(x, g):
    return x * lax.rsqrt(jnp.mean(x * x, axis=-1, keepdims=True) + EPS) * g


def _dot(a, b):
    return jnp.dot(a, b, preferred_element_type=F32)


def _dot_nt(a, b):
    return lax.dot_general(a, b, (((1,), (1,)), ((), ())), preferred_element_type=F32)


def _dot_tn(a, b):
    return lax.dot_general(a, b, (((0,), (0,)), ((), ())), preferred_element_type=F32)


def _rope(x, cos, sin_signed):
    return x * cos + pltpu.roll(x, RET_DK // 2, 1) * sin_signed


def _const_spec(shape):
    nd = len(shape)
    return pl.BlockSpec(shape, lambda *_: (0,) * nd)


def _params(*sem):
    return pltpu.CompilerParams(dimension_semantics=sem, vmem_limit_bytes=VMEM_LIMIT)


def _route(h, gffn, wr, br, tri, ltri):
    t = h.shape[0]
    xn = _rms(h, gffn).astype(BF16)
    lt = _dot_nt(wr, xn) + br[:, 0:1]
    row8 = lax.broadcasted_iota(I32, (SUBLANES, t), 0).astype(F32)
    gl = lt[0:SUBLANES]
    m = jnp.max(gl, axis=0, keepdims=True)
    g_top = 1.0 / jnp.sum(jnp.exp(gl - m), axis=0, keepdims=True)
    gidx = jnp.min(jnp.where(gl == m, row8, float(SUBLANES)), axis=0, keepdims=True)
    e_sel = jnp.where(gidx == 0.0, lt[8:16],
                      jnp.where(gidx == 1.0, lt[16:24], jnp.where(gidx == 2.0, lt[24:32], lt[32:40])))
    m1 = jnp.max(e_sel, axis=0, keepdims=True)
    i1 = jnp.min(jnp.where(e_sel == m1, row8, float(SUBLANES)), axis=0, keepdims=True)
    rest = jnp.where(row8 == i1, -jnp.inf, e_sel)
    m2 = jnp.max(rest, axis=0, keepdims=True)
    i2 = jnp.min(jnp.where(rest == m2, row8, float(SUBLANES)), axis=0, keepdims=True)
    d = jnp.exp(m2 - m1)
    w1 = g_top / (1.0 + d)
    w2 = g_top * d / (1.0 + d)
    e1 = gidx * float(EXPERTS_PER_GROUP) + i1
    e2 = gidx * float(EXPERTS_PER_GROUP) + i2
    row32 = lax.broadcasted_iota(I32, (N_EXPERTS, t), 0).astype(F32)
    a1 = jnp.where(row32 == e1, 1.0, 0.0)
    a2 = jnp.where(row32 == e2, 1.0, 0.0)
    a = a1 + a2
    n = jnp.broadcast_to(jnp.sum(a, axis=1, keepdims=True), (N_EXPERTS, LANES))
    start = _dot(ltri, n.astype(BF16))
    base = _dot(a.astype(BF16), tri) + start[:, 0:1]
    r1 = jnp.sum(a1 * base, axis=0, keepdims=True)
    r2 = jnp.sum(a2 * base, axis=0, keepdims=True)
    ri = jnp.where(row8 == 0.0, e1, jnp.where(row8 == 1.0, e2, jnp.where(row8 == 2.0, r1,
                                                                         jnp.where(row8 == 3.0, r2, 0.0))))
    rw = jnp.where(row8 == 0.0, w1, jnp.where(row8 == 1.0, w2, 0.0))
    return ri.astype(I32), rw, n, xn


def _group_norm_gate(o, gate):
    mu = jnp.mean(o, axis=-1, keepdims=True)
    oc = o - mu
    return oc * lax.rsqrt(jnp.mean(oc * oc, axis=-1, keepdims=True) + EPS) * gate


def _mixer_prompt_kernel(x_ref, cos_ref, sin_ref, qdec_ref, kdec_ref, dmask_ref, cdec_ref, gmix_ref, win_ref,
                         convw_ref, retgn_ref, wo_ref, gffn_ref, wr_ref, br_ref, tri_ref, ltri_ref,
                         ss_ref, sq_ref, sk_ref, skt_ref, sv_ref, sdec_ref,
                         h_ref, xn_ref, ri_ref, rw_ref, n_ref, conv_ref, ret_ref, snew_ref, so_ref,
                         s_scr, z_scr, mix_scr):
    j = pl.program_id(1)
    n_j = pl.num_programs(1)
    tt = x_ref.shape[1]

    @pl.when(j == 0)
    def _():
        s_scr[...] = jnp.zeros_like(s_scr)
        z_scr[0:SUBLANES, :] = jnp.zeros((SUBLANES, CONV_DIM), F32)

    cw = convw_ref[...]
    retgn = retgn_ref[...]
    q0 = 3 * CONV_DIM

    def project(p0):
        x = x_ref[0, p0:p0 + TOK_TILE, :]
        proj = _dot(_rms(x, gmix_ref[...]).astype(BF16), win_ref[...])
        gb = proj[:, 0:CONV_DIM]
        z = proj[:, CONV_DIM:2 * CONV_DIM] * proj[:, 2 * CONV_DIM:3 * CONV_DIM]
        z0 = SUBLANES + p0
        z_scr[z0:z0 + TOK_TILE, :] = z
        yc = cw[0:1] * z_scr[z0 - 2:z0 - 2 + TOK_TILE, :] + cw[1:2] * z_scr[z0 - 1:z0 - 1 + TOK_TILE, :] + cw[2:3] * z
        mix_scr[p0:p0 + TOK_TILE, 0:CONV_DIM] = (gb * yc).astype(BF16)
        return x, proj

    def mix(p0, x, proj):
        cos = cos_ref[p0:p0 + TOK_TILE, :]
        sin = sin_ref[p0:p0 + TOK_TILE, :]
        for hh in range(RET_HEADS):
            l0 = hh * RET_DK
            qr = _rope(proj[:, q0 + l0:q0 + l0 + RET_DK], cos, sin)
            kr = _rope(proj[:, q0 + RET_DIM + l0:q0 + RET_DIM + l0 + RET_DK], cos, sin) * (RET_DK ** -0.5)
            v = proj[:, q0 + 2 * RET_DIM + l0:q0 + 2 * RET_DIM + l0 + RET_DV].astype(BF16)
            g = proj[:, q0 + 3 * RET_DIM + l0:q0 + 3 * RET_DIM + l0 + RET_DV]
            s_old = s_scr[hh]
            scores = _dot_nt(qr.astype(BF16), kr.astype(BF16)) * dmask_ref[hh]
            lhs = jnp.concatenate([scores.astype(BF16), (qr * qdec_ref[:, l0:l0 + RET_DK]).astype(BF16)], axis=1)
            o = _dot(lhs, jnp.concatenate([v, s_old.astype(BF16)], axis=0))
            kd = (kr * kdec_ref[:, l0:l0 + RET_DK]).T.astype(BF16)
            s_scr[hh] = s_old * cdec_ref[hh:hh + 1, :] + _dot(kd, v)
            gate = retgn[:, l0:l0 + RET_DV] * (g * jax.nn.sigmoid(g))
            mix_scr[p0:p0 + TOK_TILE, CONV_DIM + l0:CONV_DIM + l0 + RET_DV] = _group_norm_gate(o, gate).astype(BF16)
        h = x + _dot(mix_scr[p0:p0 + TOK_TILE, :], wo_ref[...])
        h_ref[0, p0:p0 + TOK_TILE, :] = h
        return h

    def route(part, h):
        p0 = part * TOK_TILE
        ri, rw, n, xn = _route(h, gffn_ref[...], wr_ref[...], br_ref[...], tri_ref[...], ltri_ref[...])
        ri_ref[:, p0:p0 + TOK_TILE] = ri
        rw_ref[:, p0:p0 + TOK_TILE] = rw
        n_ref[part] = n
        xn_ref[0, p0:p0 + TOK_TILE, :] = xn

    parts = tt // TOK_TILE
    nxt = project(0)
    per_part = SAMPLE_BLOCK // parts
    for part in range(parts):
        _sample_state_update(ss_ref, sq_ref, sk_ref, skt_ref, sv_ref, sdec_ref, snew_ref, so_ref,
                             range(part * per_part, (part + 1) * per_part))
        h = mix(part * TOK_TILE, *nxt)
        if part + 1 < parts:
            nxt = project((part + 1) * TOK_TILE)
        route(part, h)

    @pl.when(j == n_j - 1)
    def _():
        conv_ref[0] = z_scr[SUBLANES + tt - 2:SUBLANES + tt, :]
        ret_ref[0] = s_scr[...]

    z_scr[0:SUBLANES, :] = z_scr[tt:tt + SUBLANES, :]


def _mixer_prompt(x, tabs, wts, sample):
    bsz, t, _ = x.shape
    tt = MIX_TILE
    n_j = t // tt
    parts = tt // TOK_TILE
    n_tok = bsz * t
    nb = SAMPLE_BLOCK
    s_state, s_q, s_k, s_kt, s_v, s_dec = sample
    assert s_state.shape[0] == nb * bsz * n_j, "one block of sample requests per mixer grid step"
    args = (x,) + tuple(tabs) + tuple(wts) + tuple(sample)
    step = lambda b, j: b * n_j + j
    in_specs = [
        pl.BlockSpec((1, tt, D_MODEL), lambda b, j: (b, j, 0)),
        pl.BlockSpec((tt, LANES), lambda b, j: (j, 0)),
        pl.BlockSpec((tt, LANES), lambda b, j: (j, 0)),
    ] + [_const_spec(a.shape) for a in args[3:3 + len(tabs) - 2 + len(wts)]] + [
        pl.BlockSpec((nb, RET_HEADS, RET_DK, RET_DV), lambda b, j: (step(b, j), 0, 0, 0)),
        pl.BlockSpec((nb, RET_DIM), lambda b, j: (step(b, j), 0)),
        pl.BlockSpec((nb, RET_DIM), lambda b, j: (step(b, j), 0)),
        pl.BlockSpec((1, RET_HEADS, RET_DK, nb), lambda b, j: (step(b, j), 0, 0, 0)),
        pl.BlockSpec((nb, RET_DIM), lambda b, j: (step(b, j), 0)),
        _const_spec(s_dec.shape),
    ]
    out_shape = (
        jax.ShapeDtypeStruct((bsz, t, D_MODEL), F32),
        jax.ShapeDtypeStruct((bsz, t, D_MODEL), BF16),
        jax.ShapeDtypeStruct((SUBLANES, n_tok), I32),
        jax.ShapeDtypeStruct((SUBLANES, n_tok), F32),
        jax.ShapeDtypeStruct((bsz * n_j * parts, N_EXPERTS, LANES), F32),
        jax.ShapeDtypeStruct((bsz, CONV_WIDTH - 1, CONV_DIM), F32),
        jax.ShapeDtypeStruct((bsz, RET_HEADS, RET_DK, RET_DV), F32),
        jax.ShapeDtypeStruct(s_state.shape, F32),
        jax.ShapeDtypeStruct(s_q.shape, F32),
    )
    out_specs = (
        pl.BlockSpec((1, tt, D_MODEL), lambda b, j: (b, j, 0)),
        pl.BlockSpec((1, tt, D_MODEL), lambda b, j: (b, j, 0)),
        pl.BlockSpec((SUBLANES, tt), lambda b, j: (0, b * n_j + j)),
        pl.BlockSpec((SUBLANES, tt), lambda b, j: (0, b * n_j + j)),
        pl.BlockSpec((parts, N_EXPERTS, LANES), lambda b, j: (b * n_j + j, 0, 0)),
        pl.BlockSpec((1, CONV_WIDTH - 1, CONV_DIM), lambda b, j: (b, 0, 0)),
        pl.BlockSpec((1, RET_HEADS, RET_DK, RET_DV), lambda b, j: (b, 0, 0, 0)),
        pl.BlockSpec((nb, RET_HEADS, RET_DK, RET_DV), lambda b, j: (step(b, j), 0, 0, 0)),
        pl.BlockSpec((nb, RET_DIM), lambda b, j: (step(b, j), 0)),
    )
    return pl.pallas_call(
        _mixer_prompt_kernel,
        grid=(bsz, n_j),
        in_specs=in_specs,
        out_specs=out_specs,
        out_shape=out_shape,
        scratch_shapes=[
            pltpu.VMEM((RET_HEADS, RET_DK, RET_DV), F32),
            pltpu.VMEM((tt + SUBLANES, CONV_DIM), F32),
            pltpu.VMEM((tt, D_MODEL), BF16),
        ],
        compiler_params=_params("arbitrary", "arbitrary"),
        name="mixer_prompt",
    )(*args)


def _sample_proj_kernel(x_ref, sc_ref, cos_ref, sin_ref, gmix_ref, win_ref, convw_ref, retgn_ref,
                        yc_ref, q_ref, k_ref, v_ref, gate_ref, conv_ref):
    x = x_ref[...]
    u = _rms(x, gmix_ref[...]).astype(BF16)
    proj = _dot(u, win_ref[...])
    gb = proj[:, 0:CONV_DIM]
    z = proj[:, CONV_DIM:2 * CONV_DIM] * proj[:, 2 * CONV_DIM:3 * CONV_DIM]
    cw = convw_ref[...]
    buf0, buf1 = sc_ref[:, 0:CONV_DIM], sc_ref[:, CONV_DIM:2 * CONV_DIM]
    yc_ref[...] = gb * (cw[0:1] * buf0 + cw[1:2] * buf1 + cw[2:3] * z)
    conv_ref[:, 0:CONV_DIM] = buf1
    conv_ref[:, CONV_DIM:2 * CONV_DIM] = z
    q0 = 3 * CONV_DIM
    cos, sin = cos_ref[0:1, :], sin_ref[0:1, :]
    retgn = retgn_ref[...]
    for hh in range(RET_HEADS):
        l0 = hh * RET_DK
        q_ref[:, l0:l0 + RET_DK] = _rope(proj[:, q0 + l0:q0 + l0 + RET_DK], cos, sin)
        k_ref[:, l0:l0 + RET_DK] = _rope(proj[:, q0 + RET_DIM + l0:q0 + RET_DIM + l0 + RET_DK], cos, sin) * (RET_DK ** -0.5)
    v_ref[...] = proj[:, q0 + 2 * RET_DIM:q0 + 3 * RET_DIM]
    g = proj[:, q0 + 3 * RET_DIM:q0 + 4 * RET_DIM]
    gate_ref[...] = retgn * (g * jax.nn.sigmoid(g))


def _sample_proj(x, sc, cos, sin, gmix, win, convw, retgn):
    n = x.shape[0]
    outs = (
        jax.ShapeDtypeStruct((n, CONV_DIM), F32), jax.ShapeDtypeStruct((n, RET_DIM), F32),
        jax.ShapeDtypeStruct((n, RET_DIM), F32), jax.ShapeDtypeStruct((n, RET_DIM), F32),
        jax.ShapeDtypeStruct((n, RET_DIM), F32), jax.ShapeDtypeStruct((n, 2 * CONV_DIM), F32),
    )
    args = (x, sc, cos, sin, gmix, win, convw, retgn)
    return pl.pallas_call(
        _sample_proj_kernel,
        grid=(1,),
        in_specs=[_const_spec(a.shape) for a in args],
        out_specs=tuple(_const_spec(o.shape) for o in outs),
        out_shape=outs,
        compiler_params=_params("arbitrary"),
        name="sample_proj",
    )(*args)


def _sample_state_update(s_ref, q_ref, k_ref, kt_ref, v_ref, dec_ref, snew_ref, o_ref, requests):
    for r in requests:
        for hh in range(RET_HEADS):
            l0 = hh * RET_DV
            s = s_ref[r, hh]
            q = q_ref[r:r + 1, l0:l0 + RET_DK]
            k = k_ref[r:r + 1, l0:l0 + RET_DK]
            kc = kt_ref[0, hh, :, r:r + 1]
            vr = v_ref[r:r + 1, l0:l0 + RET_DV]
            qdec = dec_ref[hh:hh + 1, :]
            cdec = dec_ref[RET_HEADS + hh:RET_HEADS + hh + 1, :]
            qk = jnp.sum(q * k, axis=1, keepdims=True)
            o_ref[r:r + 1, l0:l0 + RET_DV] = qk * vr + _dot((q * qdec).astype(BF16), s.astype(BF16))
            snew_ref[r, hh] = s * cdec + kc * vr


def _sample_out_kernel(x_ref, yc_ref, o_ref, gate_ref, wo_ref, gffn_ref, wr_ref, br_ref, tri_ref, ltri_ref,
                       h_ref, xn_ref, ri_ref, rw_ref, n_ref):
    n = x_ref.shape[0]
    parts = [yc_ref[...].astype(BF16)]
    for hh in range(RET_HEADS):
        l0 = hh * RET_DV
        parts.append(_group_norm_gate(o_ref[:, l0:l0 + RET_DV], gate_ref[:, l0:l0 + RET_DV]).astype(BF16))
    h_ref[0:n, :] = x_ref[...] + _dot(jnp.concatenate(parts, axis=-1), wo_ref[...])
    h_ref[n:SAMPLE_ROWS, :] = jnp.zeros((SAMPLE_ROWS - n, D_MODEL), F32)
    for part in range(SAMPLE_ROWS // TOK_TILE):
        p0 = part * TOK_TILE
        ri, rw, cnt, xn = _route(h_ref[p0:p0 + TOK_TILE, :], gffn_ref[...], wr_ref[...], br_ref[...], tri_ref[...],
                                 ltri_ref[...])
        ri_ref[:, p0:p0 + TOK_TILE] = ri
        rw_ref[:, p0:p0 + TOK_TILE] = rw
        n_ref[part] = cnt
        xn_ref[p0:p0 + TOK_TILE, :] = xn


def _sample_out(x, yc, o, gate, wo, gffn, wr, br, tri, ltri):
    outs = (
        jax.ShapeDtypeStruct((SAMPLE_ROWS, D_MODEL), F32), jax.ShapeDtypeStruct((SAMPLE_ROWS, D_MODEL), BF16),
        jax.ShapeDtypeStruct((SUBLANES, SAMPLE_ROWS), I32), jax.ShapeDtypeStruct((SUBLANES, SAMPLE_ROWS), F32),
        jax.ShapeDtypeStruct((SAMPLE_ROWS // TOK_TILE, N_EXPERTS, LANES), F32),
    )
    args = (x, yc, o, gate, wo, gffn, wr, br, tri, ltri)
    return pl.pallas_call(
        _sample_out_kernel,
        grid=(1,),
        in_specs=[_const_spec(a.shape) for a in args],
        out_specs=tuple(_const_spec(o_.shape) for o_ in outs),
        out_shape=outs,
        compiler_params=_params("arbitrary"),
        name="sample_out",
    )(*args)


def _sample_inputs(x, state_conv, state_ret, gmix, win, convw, retgn):
    n = x.shape[0]
    lg = _log_gamma()
    cos, sin = _rope_tables(np.full((SUBLANES,), PAST_LEN))
    yc, q, k, v, gate, conv_new = _sample_proj(x, state_conv.reshape(n, 2 * CONV_DIM), cos, sin, gmix, win, convw, retgn)
    nb = SAMPLE_BLOCK
    cols = lambda a: a.reshape(n // nb, nb, RET_HEADS, RET_DK).transpose(0, 2, 3, 1)
    step = np.exp(lg[:, None] * 1.0)
    dec = jnp.asarray(np.broadcast_to(np.concatenate([step, step], axis=0), (2 * RET_HEADS, LANES)), F32)
    return (yc, gate, conv_new.reshape(n, 2, CONV_DIM)), (state_ret, q, k, cols(k), v, dec)


def _load_rows(ref, n_rows):
    words = [ref[pl.ds(c, n_rows, stride=ROW_CHUNKS), :] for c in range(ROW_CHUNKS)]
    half = lambda i: [pltpu.unpack_elementwise(w, index=i, packed_dtype=BF16, unpacked_dtype=F32) for w in words]
    return jnp.concatenate(half(0) + half(1), axis=-1)


def _store_rows(ref, val):
    half = D_MODEL // 2
    for c in range(ROW_CHUNKS):
        pair = [val[:, c * LANES:(c + 1) * LANES], val[:, half + c * LANES:half + (c + 1) * LANES]]
        ref[pl.ds(c, val.shape[0], stride=ROW_CHUNKS), :] = pltpu.pack_elementwise(pair, packed_dtype=BF16)


def _row_slice(ref, row, rows):
    return ref.at[pl.ds(pl.multiple_of(row * ROW_CHUNKS, ROW_CHUNKS), rows * ROW_CHUNKS)]


def _run_copies(table, make_copy):
    for b in range(RUN_BITS):
        def body(j, carry, b=b):
            entry = table(N_EXPERTS + N_EXPERTS * b + j)
            make_copy(entry & ((1 << SRC_BITS) - 1), entry >> SRC_BITS, 1 << b).start()
            return carry

        lax.fori_loop(0, table(b), body, 0)


def _pair_rows_iota():
    return lax.broadcasted_iota(I32, (PAIR_ROWS, TOK_TILE), 0)


def _dispatch_kernel(n_p, tab_ref, xp_ref, xs_in_ref, ri_ref, xs_ref, idx_smem, y_scr, isem, rsem):
    i, n = pl.program_id(0), pl.num_programs(0)
    slot = i % 2
    step_rows = STEP_TILES * PAIR_ROWS

    def idx_copy(step, sl):
        return pltpu.make_async_copy(tab_ref.at[pl.ds(step * STEP_TILES, STEP_TILES)], idx_smem.at[sl], isem.at[sl])

    def wait_step(sl):
        pltpu.make_async_copy(_row_slice(xs_ref, 0, step_rows), y_scr.at[sl], rsem.at[sl]).wait()

    @pl.when(i == 0)
    def _():
        idx_copy(0, 0).start()

    @pl.when(i + 1 < n)
    def _():
        idx_copy(i + 1, 1 - slot).start()

    @pl.when(i >= 2)
    def _():
        wait_step(slot)

    xn = jnp.where(i < n_p, xp_ref[...], xs_in_ref[...])
    ri = ri_ref[...]
    rows = _pair_rows_iota()
    for g in range(STEP_TILES):
        tok = slice(g * TOK_TILE, (g + 1) * TOK_TILE)
        perm = jnp.where((rows == ri[2:3, tok]) | (rows == ri[3:4, tok]), 1.0, 0.0).astype(BF16)
        y = _dot(perm, xn[tok, :])
        _store_rows(y_scr.at[slot, pl.ds(g * PAIR_ROWS * ROW_CHUNKS, PAIR_ROWS * ROW_CHUNKS)], y)

    idx_copy(i, slot).wait()
    for g in range(STEP_TILES):
        _run_copies(lambda col, g=g: idx_smem[slot, g, col], lambda src, dst, rows_, g=g: pltpu.make_async_copy(
            _row_slice(y_scr.at[slot], g * PAIR_ROWS + src, rows_), _row_slice(xs_ref, dst, rows_), rsem.at[slot]))

    @pl.when(i == n - 1)
    def _():
        wait_step(slot)

    @pl.when((i == n - 1) & (n >= 2))
    def _():
        wait_step(1 - slot)


def _dispatch(tab, xn_p, xn_s, ri):
    tt = STEP_TILES * TOK_TILE
    n_p = xn_p.shape[0] // tt
    n = tab.shape[0] // STEP_TILES
    return pl.pallas_call(
        functools.partial(_dispatch_kernel, n_p),
        grid=(n,),
        in_specs=[
            pl.BlockSpec(memory_space=pl.ANY),
            pl.BlockSpec((tt, D_MODEL), lambda i: (jnp.minimum(i, n_p - 1), 0)),
            pl.BlockSpec((tt, D_MODEL), lambda i: (jnp.maximum(i - n_p, 0), 0)),
            pl.BlockSpec((SUBLANES, tt), lambda i: (0, i)),
        ],
        out_specs=pl.BlockSpec(memory_space=pl.ANY),
        out_shape=jax.ShapeDtypeStruct((tab.shape[0] * PAIR_ROWS * ROW_CHUNKS, LANES), U32),
        scratch_shapes=[
            pltpu.SMEM((2, STEP_TILES, TAB_COLS), I32),
            pltpu.VMEM((2, STEP_TILES * PAIR_ROWS * ROW_CHUNKS, LANES), U32),
            pltpu.SemaphoreType.DMA((2,)),
            pltpu.SemaphoreType.DMA((2,)),
        ],
        compiler_params=_params("arbitrary"),
        name="moe_dispatch",
    )(tab, xn_p, xn_s, ri)


def _ffn_kernel(tile_ref, exp_ref, lo_ref, hi_ref, nxt_ref, slot_ref, xs_ref, wg_ref, wu_ref, wd_ref, o_ref,
                x_buf, wg_buf, wu_buf, wd_buf, wgu_scr, wd_scr, y_scr, xsem, wsem):
    w, n_w = pl.program_id(0), pl.num_programs(0)
    prev = jnp.maximum(w - 1, 0)
    lo, hi = lo_ref[w], hi_ref[w]
    e, slot = exp_ref[w], slot_ref[w]
    first = (w == 0) | (tile_ref[w] != tile_ref[prev])

    def x_copy(step):
        rows = FFN_TILE * ROW_CHUNKS
        return pltpu.make_async_copy(xs_ref.at[pl.ds(pl.multiple_of(tile_ref[step] * rows, rows), rows)],
                                     x_buf.at[step % 3], xsem.at[step % 3])

    @pl.when(w == 0)
    def _():
        x_copy(0).start()

    @pl.when((w == 0) & (n_w > 1))
    def _():
        x_copy(1).start()

    @pl.when(w + 2 < n_w)
    def _():
        x_copy(w + 2).start()

    x_copy(w).wait()

    def weight_copies(expert, sl):
        return [pltpu.make_async_copy(src.at[expert], buf.at[sl], wsem.at[sl])
                for src, buf in ((wg_ref, wg_buf), (wu_ref, wu_buf), (wd_ref, wd_buf))]

    @pl.when(w == 0)
    def _():
        for cp in weight_copies(e, slot):
            cp.start(priority=1)

    @pl.when((w == 0) | (e != exp_ref[prev]))
    def _():
        for cp in weight_copies(e, slot):
            cp.wait()

        @pl.when(nxt_ref[w] != e)
        def _():
            for cp in weight_copies(nxt_ref[w], 1 - slot):
                cp.start(priority=1)

        wgu_scr[:, 0:D_FF] = wg_buf[slot].astype(BF16)
        wgu_scr[:, D_FF:2 * D_FF] = wu_buf[slot].astype(BF16)
        wd_scr[...] = wd_buf[slot].astype(BF16)

    @pl.when(w == 0)
    def _():
        y_scr[...] = jnp.zeros_like(y_scr)

    def compute(r0, n_rows):
        chunk = pl.ds(r0 * ROW_CHUNKS, n_rows * ROW_CHUNKS)
        gu = _dot(_load_rows(x_buf.at[w % 3, chunk], n_rows).astype(BF16), wgu_scr[...])
        g, u = gu[:, 0:D_FF], gu[:, D_FF:2 * D_FF]
        y = _dot((g * jax.nn.sigmoid(g) * u).astype(BF16), wd_scr[...])
        row = r0 + lax.broadcasted_iota(I32, (n_rows, 1), 0)
        y = jnp.where(((row >= lo) & (row < hi)) | first, y, y_scr[r0:r0 + n_rows, :])
        y_scr[r0:r0 + n_rows, :] = y
        _store_rows(o_ref.at[chunk], y)

    half = FFN_TILE // 2
    top, bottom = hi <= half, lo >= half

    @pl.when((hi > lo) & top)
    def _():
        compute(0, half)

    @pl.when((hi > lo) & bottom)
    def _():
        compute(half, half)

    @pl.when((hi > lo) & jnp.logical_not(top | bottom))
    def _():
        compute(0, FFN_TILE)


def _expert_ffn(items, xs, w_gate, w_up, w_down):
    n_items = items[0].shape[0]
    row_spec = pl.BlockSpec((FFN_TILE * ROW_CHUNKS, LANES), lambda w, t, *_: (t[w], 0))
    grid_spec = pltpu.PrefetchScalarGridSpec(
        num_scalar_prefetch=len(items),
        grid=(n_items,),
        in_specs=[pl.BlockSpec(memory_space=pl.ANY)] * 4,
        out_specs=row_spec,
        scratch_shapes=[
            pltpu.VMEM((3, FFN_TILE * ROW_CHUNKS, LANES), U32),
            pltpu.VMEM((2, D_MODEL, D_FF), F32), pltpu.VMEM((2, D_MODEL, D_FF), F32),
            pltpu.VMEM((2, D_FF, D_MODEL), F32),
            pltpu.VMEM((D_MODEL, 2 * D_FF), BF16), pltpu.VMEM((D_FF, D_MODEL), BF16),
            pltpu.VMEM((FFN_TILE, D_MODEL), F32),
            pltpu.SemaphoreType.DMA((3,)),
            pltpu.SemaphoreType.DMA((2,)),
        ],
    )
    return pl.pallas_call(
        _ffn_kernel,
        grid_spec=grid_spec,
        out_shape=jax.ShapeDtypeStruct(xs.shape, U32),
        compiler_params=_params("arbitrary"),
        name="moe_ffn",
    )(*items, xs, w_gate, w_up, w_down)


def _work_items(counts, n_rows):
    n_tiles = n_rows // FFN_TILE
    n_items = n_tiles + N_EXPERTS - 1
    off = jnp.cumsum(counts) - counts
    first_tile = off // FFN_TILE
    last_tile = jnp.maximum(off + counts - 1, off) // FFN_TILE
    n_e = jnp.where(counts > 0, last_tile - first_tile + 1, 0)
    start = jnp.cumsum(n_e) - n_e
    total = jnp.sum(n_e)
    w = jnp.minimum(jnp.arange(n_items, dtype=I32), total - 1)
    ids = jnp.arange(N_EXPERTS, dtype=I32)[None, :]
    e = jnp.max(jnp.where((start[None, :] <= w[:, None]) & (n_e[None, :] > 0), ids, 0), axis=1)
    pick = lambda a: jnp.sum(jnp.where(ids == e[:, None], a[None, :], 0), axis=1)
    tile = pick(first_tile) + (w - pick(start))
    lo = jnp.clip(pick(off) - tile * FFN_TILE, 0, FFN_TILE)
    hi = jnp.clip(pick(off + counts) - tile * FFN_TILE, 0, FFN_TILE)
    hi = jnp.where(jnp.arange(n_items) < total, hi, lo)
    used = n_e[None, :] > 0
    nxt = jnp.min(jnp.where(used & (ids > e[:, None]), ids, N_EXPERTS), axis=1)
    nxt = jnp.where(nxt == N_EXPERTS, e, nxt)
    slot = jnp.sum(jnp.where(used & (ids < e[:, None]), 1, 0), axis=1) % 2
    return tuple(a.astype(I32) for a in (tile, e, lo, hi, nxt, slot))


def _run_table(n):
    tiles = n.shape[0]
    counts = jnp.sum(n, axis=0)
    goff = jnp.cumsum(counts) - counts
    dst0 = goff[None, :] + jnp.cumsum(n, axis=0) - n
    src0 = jnp.cumsum(n, axis=1) - n
    bits = jnp.arange(RUN_BITS, dtype=I32)[:, None, None]
    flag = (n[None] >> bits) & 1
    done = (n[None] >> (bits + 1)) << (bits + 1)
    entry = ((dst0[None] + done) << SRC_BITS) | (src0[None] + done)
    rank = jnp.cumsum(flag, axis=2) - flag
    slot = jnp.arange(N_EXPERTS, dtype=I32)
    sel = (flag[..., None] == 1) & (rank[..., None] == slot)
    packed = jnp.sum(jnp.where(sel, entry[..., None], 0), axis=2)
    m = jnp.sum(flag, axis=2).T
    tab = jnp.zeros((tiles, TAB_COLS), I32)
    tab = tab.at[:, :RUN_BITS].set(m)
    tab = tab.at[:, N_EXPERTS:N_EXPERTS * (RUN_BITS + 1)].set(packed.transpose(1, 0, 2).reshape(tiles, -1))
    return counts, tab


def _combine_kernel(g_tiles, n, tab_ref, h_ref, p_ref, ri_ref, rw_ref, ys_ref, gple_ref, wpg_ref, wpp_ref, gfin_ref,
                    o_ref, idx_smem, rows_scr, isem, rsem):
    i = pl.program_id(0)
    slot = i % 2

    def idx_copy(step, sl):
        return pltpu.make_async_copy(tab_ref.at[pl.ds(step * g_tiles, g_tiles)], idx_smem.at[sl], isem.at[sl])

    def gather(sl):
        for g in range(g_tiles):
            _run_copies(lambda col, g=g: idx_smem[sl, g, col], lambda src, dst, rows_, g=g: pltpu.make_async_copy(
                _row_slice(ys_ref, dst, rows_), _row_slice(rows_scr.at[sl], g * PAIR_ROWS + src, rows_), rsem.at[sl]))

    @pl.when(i == 0)
    def _():
        idx_copy(0, 0).start()
        idx_copy(0, 0).wait()
        gather(0)

    if n >= 2:
        @pl.when(i == 0)
        def _():
            idx_copy(1, 1).start()

        @pl.when(i + 1 < n)
        def _():
            idx_copy(i + 1, 1 - slot).wait()
            gather(1 - slot)

    @pl.when(i + 2 < n)
    def _():
        idx_copy(i + 2, slot).start()

    pltpu.make_async_copy(_row_slice(ys_ref, 0, g_tiles * PAIR_ROWS), rows_scr.at[slot], rsem.at[slot]).wait()

    ri, rw = ri_ref[...], rw_ref[...]
    rows = _pair_rows_iota()
    tiles = range(g_tiles)
    tok = lambda g: slice(g * TOK_TILE, (g + 1) * TOK_TILE)
    h = []
    for g in tiles:
        r1, r2 = ri[2:3, tok(g)], ri[3:4, tok(g)]
        wperm = jnp.where(rows == r1, rw[0:1, tok(g)], 0.0) + jnp.where(rows == r2, rw[1:2, tok(g)], 0.0)
        srows = _load_rows(rows_scr.at[slot, pl.ds(g * PAIR_ROWS * ROW_CHUNKS, PAIR_ROWS * ROW_CHUNKS)], PAIR_ROWS)
        h.append(h_ref[tok(g), :] + _dot_tn(wperm.astype(BF16), srows.astype(BF16)))
    gate = [jax.nn.sigmoid(_dot(_rms(h[g], gple_ref[...]).astype(BF16), wpg_ref[...])) for g in tiles]
    for g in tiles:
        hg = h[g] + _dot(p_ref[tok(g), :].astype(BF16), wpp_ref[...]) * gate[g]
        o_ref[tok(g), :] = _rms(hg, gfin_ref[...])


def _combine(tab, h, p, ri, rw, ys, gple, wpg, wpp, gfin):
    n_tiles = tab.shape[0]
    g_tiles = STEP_TILES
    n = n_tiles // g_tiles
    tt = g_tiles * TOK_TILE
    return pl.pallas_call(
        functools.partial(_combine_kernel, g_tiles, n),
        grid=(n,),
        in_specs=[
            pl.BlockSpec(memory_space=pl.ANY),
            pl.BlockSpec((tt, D_MODEL), lambda i: (i, 0)),
            pl.BlockSpec((tt, PLE_DIM), lambda i: (i, 0)),
            pl.BlockSpec((SUBLANES, tt), lambda i: (0, i)),
            pl.BlockSpec((SUBLANES, tt), lambda i: (0, i)),
            pl.BlockSpec(memory_space=pl.ANY),
            _const_spec(gple.shape), _const_spec(wpg.shape), _const_spec(wpp.shape), _const_spec(gfin.shape),
        ],
        out_specs=pl.BlockSpec((tt, D_MODEL), lambda i: (i, 0)),
        out_shape=jax.ShapeDtypeStruct((n_tiles * TOK_TILE, D_MODEL), F32),
        scratch_shapes=[
            pltpu.SMEM((2, g_tiles, TAB_COLS), I32),
            pltpu.VMEM((2, g_tiles * PAIR_ROWS * ROW_CHUNKS, LANES), U32),
            pltpu.SemaphoreType.DMA((2,)),
            pltpu.SemaphoreType.DMA((2,)),
        ],
        compiler_params=_params("arbitrary"),
        name="moe_combine",
    )(tab, h, p, ri, rw, ys, gple, wpg, wpp, gfin)


def _log_gamma():
    return np.log(1.0 - 2.0 ** (-5.0 - np.arange(RET_HEADS, dtype=np.float64)))


def _rope_tables(pos):
    inv = 1.0 / (ROPE_BASE ** (np.arange(0, RET_DK, 2, dtype=np.float64) / RET_DK))
    ang = np.asarray(pos, np.float64)[:, None] * inv[None, :]
    cos, sin = np.cos(ang), np.sin(ang)
    return (jnp.asarray(np.concatenate([cos, cos], axis=-1), F32),
            jnp.asarray(np.concatenate([-sin, sin], axis=-1), F32))


def _decay_tables(c):
    lg = _log_gamma()
    idx = np.arange(c, dtype=np.float64)
    diff = idx[:, None] - idx[None, :]
    dmask = np.where((diff >= 0.0)[None], np.exp(lg[:, None, None] * np.maximum(diff, 0.0)[None]), 0.0)
    q_dec = np.exp(lg[None, :] * (idx[:, None] + 1.0))
    k_dec = np.exp(lg[None, :] * (c - 1.0 - idx[:, None]))
    c_dec = np.exp(lg * c)
    lanes = lambda a: np.repeat(a, RET_DK, axis=1)
    cdec = np.zeros((SUBLANES, LANES))
    cdec[:RET_HEADS] = c_dec[:, None]
    return tuple(jnp.asarray(a, F32) for a in (lanes(q_dec), lanes(k_dec), dmask, cdec))


def _router_params(w_rg, b_rg, w_re, b_re):
    wr = jnp.zeros((ROUTER_ROWS, D_MODEL), F32).at[:N_GROUPS].set(w_rg.T).at[SUBLANES:].set(w_re.T)
    br = jnp.full((ROUTER_ROWS,), NEG_BIG, F32).at[:N_GROUPS].set(b_rg).at[SUBLANES:].set(b_re.reshape(-1))
    return wr.astype(BF16), jnp.broadcast_to(br[:, None], (ROUTER_ROWS, LANES))


def _strict_upper(t):
    i = np.arange(t)
    return jnp.asarray(i[:, None] < i[None, :], BF16)


def _pad_rows(a, rows):
    return jnp.zeros((rows,) + a.shape[1:], a.dtype).at[:a.shape[0]].set(a)


def kernel(x_prompt, x_sample, state_conv, state_ret, p_prompt, p_sample, g_mix, w_in, conv_w, ret_gn, w_o, g_ffn,
           w_router_group, b_router_group, w_router_expert, b_router_expert, w_gate, w_up, w_down, g_ple,
           w_ple_proj, w_ple_gate, g_final):
    bp, tp, _ = x_prompt.shape
    bs = x_sample.shape[0]
    assert x_sample.shape[1] == 1 and g_mix.shape[0] == 1, "one layer, one new token per sample request"
    assert bs < SAMPLE_ROWS and tp % MIX_TILE == 0
    n_p = bp * tp

    tabs = _rope_tables(np.arange(tp)) + _decay_tables(TOK_TILE)
    wr, br = _router_params(w_router_group[0], b_router_group[0], w_router_expert[0], b_router_expert[0])
    win, wo, convw = w_in[0].astype(BF16), w_o[0].astype(BF16), _pad_rows(conv_w[0], SUBLANES)
    tri, ltri = _strict_upper(TOK_TILE), _strict_upper(N_EXPERTS).T

    xs0 = x_sample[:, 0]
    (yc_s, gate_s, conv_s), sample = _sample_inputs(xs0, state_conv[0], state_ret[0], g_mix, win, convw, ret_gn)
    h_p, xn_p, ri_p, rw_p, n_p_tab, conv_p, ret_p, ret_s, o_s = _mixer_prompt(
        x_prompt, tabs, (g_mix, win, convw, ret_gn, wo, g_ffn, wr, br, tri, ltri), sample)
    h_s, xn_s, ri_s, rw_s, n_s_tab = _sample_out(xs0, yc_s, o_s, gate_s, wo, g_ffn, wr, br, tri, ltri)
    h_p, xn_p = h_p.reshape(n_p, D_MODEL), xn_p.reshape(n_p, D_MODEL)

    n_tab = jnp.concatenate([n_p_tab, n_s_tab], axis=0)[:, :, 0].astype(I32)
    counts, tab = _run_table(n_tab)
    ri = jnp.concatenate([ri_p, ri_s], axis=1)
    n_tiles = n_tab.shape[0]

    xs = _dispatch(tab, xn_p, xn_s, ri)
    ys = _expert_ffn(_work_items(counts, n_tiles * PAIR_ROWS), xs, w_gate[0], w_up[0], w_down[0])

    wpg, wpp = w_ple_gate[0].astype(BF16), w_ple_proj[0].astype(BF16)
    gfin = g_final[None, :]
    p_s = _pad_rows(p_sample[0].reshape(bs, PLE_DIM), SAMPLE_ROWS)
    n_s_tiles = SAMPLE_ROWS // TOK_TILE
    y_p = _combine(tab[:n_tiles - n_s_tiles], h_p, p_prompt[0].reshape(n_p, PLE_DIM), ri_p, rw_p, ys, g_ple, wpg, wpp, gfin)
    y_s = _combine(tab[n_tiles - n_s_tiles:], h_s, p_s, ri_s, rw_s, ys, g_ple, wpg, wpp, gfin)
    return (y_p.reshape(bp, tp, D_MODEL), y_s[:bs].reshape(bs, 1, D_MODEL), conv_p[None], ret_p[None],
            conv_s[None], ret_s[None])
```

```python
import functools

import jax
import jax.numpy as jnp
import numpy as np
from jax import lax
from jax.experimental import pallas as pl
from jax.experimental.pallas import tpu as pltpu

F32, BF16, I32, U32 = jnp.float32, jnp.bfloat16, jnp.int32, jnp.uint32

D_MODEL = 1024
CONV_DIM = 512
CONV_WIDTH = 3
RET_DIM = 512
RET_HEADS = 4
RET_DK = 128
RET_DV = 128
RET_CHUNK = 128
ROPE_BASE = 10000.0
IN_PROJ_DIM = 3 * CONV_DIM + 4 * RET_DIM
N_GROUPS = 4
EXPERTS_PER_GROUP = 8
N_EXPERTS = 32
D_FF = 512
PLE_DIM = 256
EPS = 1e-6
PAST_LEN = 16384

LANES = 128
SUBLANES = 8
ROW_CHUNKS = D_MODEL // (2 * LANES)
ROUTER_ROWS = SUBLANES + N_EXPERTS
VMEM_LIMIT = 56 * 1024 * 1024
NEG_BIG = -1e30

TOK_TILE = 256
MIX_TILE = 1024
PAIR_ROWS = 2 * TOK_TILE
STEP_TILES = 2
SAMPLE_ROWS = STEP_TILES * TOK_TILE
FFN_TILE = 256
SAMPLE_BLOCK = 8
RUN_BITS = 9
SRC_BITS = 10
TAB_COLS = 384


def _rms(x, g):
    return x * lax.rsqrt(jnp.mean(x * x, axis=-1, keepdims=True) + EPS) * g


def _dot(a, b):
    return jnp.dot(a, b, preferred_element_type=F32)


def _dot_nt(a, b):
    return lax.dot_general(a, b, (((1,), (1,)), ((), ())), preferred_element_type=F32)


def _dot_tn(a, b):
    return lax.dot_general(a, b, (((0,), (0,)), ((), ())), preferred_element_type=F32)


def _rope(x, cos, sin_signed):
    return x * cos + pltpu.roll(x, RET_DK // 2, 1) * sin_signed


def _const_spec(shape):
    nd = len(shape)
    return pl.BlockSpec(shape, lambda *_: (0,) * nd)


def _params(*sem):
    return pltpu.CompilerParams(dimension_semantics=sem, vmem_limit_bytes=VMEM_LIMIT)


def _route(h, gffn, wr, br, tri, ltri):
    t = h.shape[0]
    xn = _rms(h, gffn).astype(BF16)
    lt = _dot_nt(wr, xn) + br[:, 0:1]
    row8 = lax.broadcasted_iota(I32, (SUBLANES, t), 0).astype(F32)
    gl = lt[0:SUBLANES]
    m = jnp.max(gl, axis=0, keepdims=True)
    g_top = 1.0 / jnp.sum(jnp.exp(gl - m), axis=0, keepdims=True)
    gidx = jnp.min(jnp.where(gl == m, row8, float(SUBLANES)), axis=0, keepdims=True)
    e_sel = jnp.where(gidx == 0.0, lt[8:16],
                      jnp.where(gidx == 1.0, lt[16:24], jnp.where(gidx == 2.0, lt[24:32], lt[32:40])))
    m1 = jnp.max(e_sel, axis=0, keepdims=True)
    i1 = jnp.min(jnp.where(e_sel == m1, row8, float(SUBLANES)), axis=0, keepdims=True)
    rest = jnp.where(row8 == i1, -jnp.inf, e_sel)
    m2 = jnp.max(rest, axis=0, keepdims=True)
    i2 = jnp.min(jnp.where(rest == m2, row8, float(SUBLANES)), axis=0, keepdims=True)
    d = jnp.exp(m2 - m1)
    w1 = g_top / (1.0 + d)
    w2 = g_top * d / (1.0 + d)
    e1 = gidx * float(EXPERTS_PER_GROUP) + i1
    e2 = gidx * float(EXPERTS_PER_GROUP) + i2
    row32 = lax.broadcasted_iota(I32, (N_EXPERTS, t), 0).astype(F32)
    a1 = jnp.where(row32 == e1, 1.0, 0.0)
    a2 = jnp.where(row32 == e2, 1.0, 0.0)
    a = a1 + a2
    n = jnp.broadcast_to(jnp.sum(a, axis=1, keepdims=True), (N_EXPERTS, LANES))
    start = _dot(ltri, n.astype(BF16))
    base = _dot(a.astype(BF16), tri) + start[:, 0:1]
    r1 = jnp.sum(a1 * base, axis=0, keepdims=True)
    r2 = jnp.sum(a2 * base, axis=0, keepdims=True)
    ri = jnp.where(row8 == 0.0, e1, jnp.where(row8 == 1.0, e2, jnp.where(row8 == 2.0, r1,
                                                                         jnp.where(row8 == 3.0, r2, 0.0))))
    rw = jnp.where(row8 == 0.0, w1, jnp.where(row8 == 1.0, w2, 0.0))
    return ri.astype(I32), rw, n, xn


def _group_norm_gate(o, gate):
    mu = jnp.mean(o, axis=-1, keepdims=True)
    oc = o - mu
    return oc * lax.rsqrt(jnp.mean(oc * oc, axis=-1, keepdims=True) + EPS) * gate


def _mixer_prompt_kernel(x_ref, cos_ref, sin_ref, qdec_ref, kdec_ref, dmask_ref, cdec_ref, gmix_ref, win_ref,
                         convw_ref, retgn_ref, wo_ref, gffn_ref, wr_ref, br_ref, tri_ref, ltri_ref,
                         ss_ref, sq_ref, sk_ref, skt_ref, sv_ref, sdec_ref,
                         h_ref, xn_ref, ri_ref, rw_ref, n_ref, conv_ref, ret_ref, snew_ref, so_ref,
                         s_scr, z_scr, mix_scr):
    j = pl.program_id(1)
    n_j = pl.num_programs(1)
    tt = x_ref.shape[1]

    @pl.when(j == 0)
    def _():
        s_scr[...] = jnp.zeros_like(s_scr)
        z_scr[0:SUBLANES, :] = jnp.zeros((SUBLANES, CONV_DIM), F32)

    cw = convw_ref[...]
    retgn = retgn_ref[...]
    q0 = 3 * CONV_DIM

    def project(p0):
        x = x_ref[0, p0:p0 + TOK_TILE, :]
        proj = _dot(_rms(x, gmix_ref[...]).astype(BF16), win_ref[...])
        gb = proj[:, 0:CONV_DIM]
        z = proj[:, CONV_DIM:2 * CONV_DIM] * proj[:, 2 * CONV_DIM:3 * CONV_DIM]
        z0 = SUBLANES + p0
        z_scr[z0:z0 + TOK_TILE, :] = z
        yc = cw[0:1] * z_scr[z0 - 2:z0 - 2 + TOK_TILE, :] + cw[1:2] * z_scr[z0 - 1:z0 - 1 + TOK_TILE, :] + cw[2:3] * z
        mix_scr[p0:p0 + TOK_TILE, 0:CONV_DIM] = (gb * yc).astype(BF16)
        return x, proj

    def mix(p0, x, proj):
        cos = cos_ref[p0:p0 + TOK_TILE, :]
        sin = sin_ref[p0:p0 + TOK_TILE, :]
        heads = range(RET_HEADS)
        lane = lambda hh: slice(hh * RET_DK, (hh + 1) * RET_DK)
        qr = [_rope(proj[:, q0 + hh * RET_DK:q0 + (hh + 1) * RET_DK], cos, sin) for hh in heads]
        kr = [_rope(proj[:, q0 + RET_DIM + hh * RET_DK:q0 + RET_DIM + (hh + 1) * RET_DK], cos, sin) * (RET_DK ** -0.5)
              for hh in heads]
        v = [proj[:, q0 + 2 * RET_DIM + hh * RET_DV:q0 + 2 * RET_DIM + (hh + 1) * RET_DV].astype(BF16) for hh in heads]
        s_old = [s_scr[hh] for hh in heads]
        scores = [_dot_nt(qr[hh].astype(BF16), kr[hh].astype(BF16)) * dmask_ref[hh] for hh in heads]
        o = []
        for hh in heads:
            lhs = jnp.concatenate([scores[hh].astype(BF16), (qr[hh] * qdec_ref[:, lane(hh)]).astype(BF16)], axis=1)
            o.append(_dot(lhs, jnp.concatenate([v[hh], s_old[hh].astype(BF16)], axis=0)))
        for hh in heads:
            kd = (kr[hh] * kdec_ref[:, lane(hh)]).T.astype(BF16)
            s_scr[hh] = s_old[hh] * cdec_ref[hh:hh + 1, :] + _dot(kd, v[hh])
        for hh in heads:
            g = proj[:, q0 + 3 * RET_DIM + hh * RET_DV:q0 + 3 * RET_DIM + (hh + 1) * RET_DV]
            gate = retgn[:, lane(hh)] * (g * jax.nn.sigmoid(g))
            mix_scr[p0:p0 + TOK_TILE, CONV_DIM + hh * RET_DV:CONV_DIM + (hh + 1) * RET_DV] = (
                _group_norm_gate(o[hh], gate).astype(BF16))
        h = x + _dot(mix_scr[p0:p0 + TOK_TILE, :], wo_ref[...])
        h_ref[0, p0:p0 + TOK_TILE, :] = h
        return h

    def route(part, h):
        p0 = part * TOK_TILE
        ri, rw, n, xn = _route(h, gffn_ref[...], wr_ref[...], br_ref[...], tri_ref[...], ltri_ref[...])
        ri_ref[:, p0:p0 + TOK_TILE] = ri
        rw_ref[:, p0:p0 + TOK_TILE] = rw
        n_ref[part] = n
        xn_ref[0, p0:p0 + TOK_TILE, :] = xn

    parts = tt // TOK_TILE
    nxt = project(0)
    per_part = SAMPLE_BLOCK // parts
    for part in range(parts):
        _sample_state_update(ss_ref, sq_ref, sk_ref, skt_ref, sv_ref, sdec_ref, snew_ref, so_ref,
                             range(part * per_part, (part + 1) * per_part))
        h = mix(part * TOK_TILE, *nxt)
        if part + 1 < parts:
            nxt = project((part + 1) * TOK_TILE)
        route(part, h)

    @pl.when(j == n_j - 1)
    def _():
        conv_ref[0] = z_scr[SUBLANES + tt - 2:SUBLANES + tt, :]
        ret_ref[0] = s_scr[...]

    z_scr[0:SUBLANES, :] = z_scr[tt:tt + SUBLANES, :]


def _mixer_prompt(x, tabs, wts, sample):
    bsz, t, _ = x.shape
    tt = MIX_TILE
    n_j = t // tt
    parts = tt // TOK_TILE
    n_tok = bsz * t
    nb = SAMPLE_BLOCK
    s_state, s_q, s_k, s_kt, s_v, s_dec = sample
    assert s_state.shape[0] == nb * bsz * n_j, "one block of sample requests per mixer grid step"
    args = (x,) + tuple(tabs) + tuple(wts) + tuple(sample)
    step = lambda b, j: b * n_j + j
    in_specs = [
        pl.BlockSpec((1, tt, D_MODEL), lambda b, j: (b, j, 0)),
        pl.BlockSpec((tt, LANES), lambda b, j: (j, 0)),
        pl.BlockSpec((tt, LANES), lambda b, j: (j, 0)),
    ] + [_const_spec(a.shape) for a in args[3:3 + len(tabs) - 2 + len(wts)]] + [
        pl.BlockSpec((nb, RET_HEADS, RET_DK, RET_DV), lambda b, j: (step(b, j), 0, 0, 0)),
        pl.BlockSpec((nb, RET_DIM), lambda b, j: (step(b, j), 0)),
        pl.BlockSpec((nb, RET_DIM), lambda b, j: (step(b, j), 0)),
        pl.BlockSpec((1, RET_HEADS, RET_DK, nb), lambda b, j: (step(b, j), 0, 0, 0)),
        pl.BlockSpec((nb, RET_DIM), lambda b, j: (step(b, j), 0)),
        _const_spec(s_dec.shape),
    ]
    out_shape = (
        jax.ShapeDtypeStruct((bsz, t, D_MODEL), F32),
        jax.ShapeDtypeStruct((bsz, t, D_MODEL), BF16),
        jax.ShapeDtypeStruct((SUBLANES, n_tok), I32),
        jax.ShapeDtypeStruct((SUBLANES, n_tok), F32),
        jax.ShapeDtypeStruct((bsz * n_j * parts, N_EXPERTS, LANES), F32),
        jax.ShapeDtypeStruct((bsz, CONV_WIDTH - 1, CONV_DIM), F32),
        jax.ShapeDtypeStruct((bsz, RET_HEADS, RET_DK, RET_DV), F32),
        jax.ShapeDtypeStruct(s_state.shape, F32),
        jax.ShapeDtypeStruct(s_q.shape, F32),
    )
    out_specs = (
        pl.BlockSpec((1, tt, D_MODEL), lambda b, j: (b, j, 0)),
        pl.BlockSpec((1, tt, D_MODEL), lambda b, j: (b, j, 0)),
        pl.BlockSpec((SUBLANES, tt), lambda b, j: (0, b * n_j + j)),
        pl.BlockSpec((SUBLANES, tt), lambda b, j: (0, b * n_j + j)),
        pl.BlockSpec((parts, N_EXPERTS, LANES), lambda b, j: (b * n_j + j, 0, 0)),
        pl.BlockSpec((1, CONV_WIDTH - 1, CONV_DIM), lambda b, j: (b, 0, 0)),
        pl.BlockSpec((1, RET_HEADS, RET_DK, RET_DV), lambda b, j: (b, 0, 0, 0)),
        pl.BlockSpec((nb, RET_HEADS, RET_DK, RET_DV), lambda b, j: (step(b, j), 0, 0, 0)),
        pl.BlockSpec((nb, RET_DIM), lambda b, j: (step(b, j), 0)),
    )
    return pl.pallas_call(
        _mixer_prompt_kernel,
        grid=(bsz, n_j),
        in_specs=in_specs,
        out_specs=out_specs,
        out_shape=out_shape,
        scratch_shapes=[
            pltpu.VMEM((RET_HEADS, RET_DK, RET_DV), F32),
            pltpu.VMEM((tt + SUBLANES, CONV_DIM), F32),
            pltpu.VMEM((tt, D_MODEL), BF16),
        ],
        compiler_params=_params("arbitrary", "arbitrary"),
        name="mixer_prompt",
    )(*args)


def _sample_proj_kernel(x_ref, sc_ref, cos_ref, sin_ref, gmix_ref, win_ref, convw_ref, retgn_ref,
                        yc_ref, q_ref, k_ref, v_ref, gate_ref, conv_ref):
    x = x_ref[...]
    u = _rms(x, gmix_ref[...]).astype(BF16)
    proj = _dot(u, win_ref[...])
    gb = proj[:, 0:CONV_DIM]
    z = proj[:, CONV_DIM:2 * CONV_DIM] * proj[:, 2 * CONV_DIM:3 * CONV_DIM]
    cw = convw_ref[...]
    buf0, buf1 = sc_ref[:, 0:CONV_DIM], sc_ref[:, CONV_DIM:2 * CONV_DIM]
    yc_ref[...] = gb * (cw[0:1] * buf0 + cw[1:2] * buf1 + cw[2:3] * z)
    conv_ref[:, 0:CONV_DIM] = buf1
    conv_ref[:, CONV_DIM:2 * CONV_DIM] = z
    q0 = 3 * CONV_DIM
    cos, sin = cos_ref[0:1, :], sin_ref[0:1, :]
    retgn = retgn_ref[...]
    for hh in range(RET_HEADS):
        l0 = hh * RET_DK
        q_ref[:, l0:l0 + RET_DK] = _rope(proj[:, q0 + l0:q0 + l0 + RET_DK], cos, sin)
        k_ref[:, l0:l0 + RET_DK] = _rope(proj[:, q0 + RET_DIM + l0:q0 + RET_DIM + l0 + RET_DK], cos, sin) * (RET_DK ** -0.5)
    v_ref[...] = proj[:, q0 + 2 * RET_DIM:q0 + 3 * RET_DIM]
    g = proj[:, q0 + 3 * RET_DIM:q0 + 4 * RET_DIM]
    gate_ref[...] = retgn * (g * jax.nn.sigmoid(g))


def _sample_proj(x, sc, cos, sin, gmix, win, convw, retgn):
    n = x.shape[0]
    outs = (
        jax.ShapeDtypeStruct((n, CONV_DIM), F32), jax.ShapeDtypeStruct((n, RET_DIM), F32),
        jax.ShapeDtypeStruct((n, RET_DIM), F32), jax.ShapeDtypeStruct((n, RET_DIM), F32),
        jax.ShapeDtypeStruct((n, RET_DIM), F32), jax.ShapeDtypeStruct((n, 2 * CONV_DIM), F32),
    )
    args = (x, sc, cos, sin, gmix, win, convw, retgn)
    return pl.pallas_call(
        _sample_proj_kernel,
        grid=(1,),
        in_specs=[_const_spec(a.shape) for a in args],
        out_specs=tuple(_const_spec(o.shape) for o in outs),
        out_shape=outs,
        compiler_params=_params("arbitrary"),
        name="sample_proj",
    )(*args)


def _sample_state_update(s_ref, q_ref, k_ref, kt_ref, v_ref, dec_ref, snew_ref, o_ref, requests):
    for r in requests:
        for hh in range(RET_HEADS):
            l0 = hh * RET_DV
            s = s_ref[r, hh]
            q = q_ref[r:r + 1, l0:l0 + RET_DK]
            k = k_ref[r:r + 1, l0:l0 + RET_DK]
            kc = kt_ref[0, hh, :, r:r + 1]
            vr = v_ref[r:r + 1, l0:l0 + RET_DV]
            qdec = dec_ref[hh:hh + 1, :]
            cdec = dec_ref[RET_HEADS + hh:RET_HEADS + hh + 1, :]
            qk = jnp.sum(q * k, axis=1, keepdims=True)
            o_ref[r:r + 1, l0:l0 + RET_DV] = qk * vr + _dot((q * qdec).astype(BF16), s.astype(BF16))
            snew_ref[r, hh] = s * cdec + kc * vr


def _sample_out_kernel(x_ref, yc_ref, o_ref, gate_ref, wo_ref, gffn_ref, wr_ref, br_ref, tri_ref, ltri_ref,
                       h_ref, xn_ref, ri_ref, rw_ref, n_ref):
    n = x_ref.shape[0]
    parts = [yc_ref[...].astype(BF16)]
    for hh in range(RET_HEADS):
        l0 = hh * RET_DV
        parts.append(_group_norm_gate(o_ref[:, l0:l0 + RET_DV], gate_ref[:, l0:l0 + RET_DV]).astype(BF16))
    h_ref[0:n, :] = x_ref[...] + _dot(jnp.concatenate(parts, axis=-1), wo_ref[...])
    h_ref[n:SAMPLE_ROWS, :] = jnp.zeros((SAMPLE_ROWS - n, D_MODEL), F32)
    for part in range(SAMPLE_ROWS // TOK_TILE):
        p0 = part * TOK_TILE
        ri, rw, cnt, xn = _route(h_ref[p0:p0 + TOK_TILE, :], gffn_ref[...], wr_ref[...], br_ref[...], tri_ref[...],
                                 ltri_ref[...])
        ri_ref[:, p0:p0 + TOK_TILE] = ri
        rw_ref[:, p0:p0 + TOK_TILE] = rw
        n_ref[part] = cnt
        xn_ref[p0:p0 + TOK_TILE, :] = xn


def _sample_out(x, yc, o, gate, wo, gffn, wr, br, tri, ltri):
    outs = (
        jax.ShapeDtypeStruct((SAMPLE_ROWS, D_MODEL), F32), jax.ShapeDtypeStruct((SAMPLE_ROWS, D_MODEL), BF16),
        jax.ShapeDtypeStruct((SUBLANES, SAMPLE_ROWS), I32), jax.ShapeDtypeStruct((SUBLANES, SAMPLE_ROWS), F32),
        jax.ShapeDtypeStruct((SAMPLE_ROWS // TOK_TILE, N_EXPERTS, LANES), F32),
    )
    args = (x, yc, o, gate, wo, gffn, wr, br, tri, ltri)
    return pl.pallas_call(
        _sample_out_kernel,
        grid=(1,),
        in_specs=[_const_spec(a.shape) for a in args],
        out_specs=tuple(_const_spec(o_.shape) for o_ in outs),
        out_shape=outs,
        compiler_params=_params("arbitrary"),
        name="sample_out",
    )(*args)


def _sample_inputs(x, state_conv, state_ret, gmix, win, convw, retgn):
    n = x.shape[0]
    lg = _log_gamma()
    cos, sin = _rope_tables(np.full((SUBLANES,), PAST_LEN))
    yc, q, k, v, gate, conv_new = _sample_proj(x, state_conv.reshape(n, 2 * CONV_DIM), cos, sin, gmix, win, convw, retgn)
    nb = SAMPLE_BLOCK
    cols = lambda a: a.reshape(n // nb, nb, RET_HEADS, RET_DK).transpose(0, 2, 3, 1)
    step = np.exp(lg[:, None] * 1.0)
    dec = jnp.asarray(np.broadcast_to(np.concatenate([step, step], axis=0), (2 * RET_HEADS, LANES)), F32)
    return (yc, gate, conv_new.reshape(n, 2, CONV_DIM)), (state_ret, q, k, cols(k), v, dec)


def _load_rows(ref, n_rows):
    words = [ref[pl.ds(c, n_rows, stride=ROW_CHUNKS), :] for c in range(ROW_CHUNKS)]
    half = lambda i: [pltpu.unpack_elementwise(w, index=i, packed_dtype=BF16, unpacked_dtype=F32) for w in words]
    return jnp.concatenate(half(0) + half(1), axis=-1)


def _store_rows(ref, val):
    half = D_MODEL // 2
    for c in range(ROW_CHUNKS):
        pair = [val[:, c * LANES:(c + 1) * LANES], val[:, half + c * LANES:half + (c + 1) * LANES]]
        ref[pl.ds(c, val.shape[0], stride=ROW_CHUNKS), :] = pltpu.pack_elementwise(pair, packed_dtype=BF16)


def _row_slice(ref, row, rows):
    return ref.at[pl.ds(pl.multiple_of(row * ROW_CHUNKS, ROW_CHUNKS), rows * ROW_CHUNKS)]


def _run_copies(table, make_copy):
    for b in range(RUN_BITS):
        def body(j, carry, b=b):
            entry = table(N_EXPERTS + N_EXPERTS * b + j)
            make_copy(entry & ((1 << SRC_BITS) - 1), entry >> SRC_BITS, 1 << b).start()
            return carry

        lax.fori_loop(0, table(b), body, 0)


def _pair_rows_iota():
    return lax.broadcasted_iota(I32, (PAIR_ROWS, TOK_TILE), 0)


def _dispatch_kernel(n_p, tab_ref, xp_ref, xs_in_ref, ri_ref, xs_ref, idx_smem, y_scr, isem, rsem):
    i, n = pl.program_id(0), pl.num_programs(0)
    slot = i % 2
    step_rows = STEP_TILES * PAIR_ROWS

    def idx_copy(step, sl):
        return pltpu.make_async_copy(tab_ref.at[pl.ds(step * STEP_TILES, STEP_TILES)], idx_smem.at[sl], isem.at[sl])

    def wait_step(sl):
        pltpu.make_async_copy(_row_slice(xs_ref, 0, step_rows), y_scr.at[sl], rsem.at[sl]).wait()

    @pl.when(i == 0)
    def _():
        idx_copy(0, 0).start()

    @pl.when(i + 1 < n)
    def _():
        idx_copy(i + 1, 1 - slot).start()

    @pl.when(i >= 2)
    def _():
        wait_step(slot)

    xn = jnp.where(i < n_p, xp_ref[...], xs_in_ref[...])
    ri = ri_ref[...]
    rows = _pair_rows_iota()
    for g in range(STEP_TILES):
        tok = slice(g * TOK_TILE, (g + 1) * TOK_TILE)
        perm = jnp.where((rows == ri[2:3, tok]) | (rows == ri[3:4, tok]), 1.0, 0.0).astype(BF16)
        y = _dot(perm, xn[tok, :])
        _store_rows(y_scr.at[slot, pl.ds(g * PAIR_ROWS * ROW_CHUNKS, PAIR_ROWS * ROW_CHUNKS)], y)

    idx_copy(i, slot).wait()
    for g in range(STEP_TILES):
        _run_copies(lambda col, g=g: idx_smem[slot, g, col], lambda src, dst, rows_, g=g: pltpu.make_async_copy(
            _row_slice(y_scr.at[slot], g * PAIR_ROWS + src, rows_), _row_slice(xs_ref, dst, rows_), rsem.at[slot]))

    @pl.when(i == n - 1)
    def _():
        wait_step(slot)

    @pl.when((i == n - 1) & (n >= 2))
    def _():
        wait_step(1 - slot)


def _dispatch(tab, xn_p, xn_s, ri):
    tt = STEP_TILES * TOK_TILE
    n_p = xn_p.shape[0] // tt
    n = tab.shape[0] // STEP_TILES
    return pl.pallas_call(
        functools.partial(_dispatch_kernel, n_p),
        grid=(n,),
        in_specs=[
            pl.BlockSpec(memory_space=pl.ANY),
            pl.BlockSpec((tt, D_MODEL), lambda i: (jnp.minimum(i, n_p - 1), 0)),
            pl.BlockSpec((tt, D_MODEL), lambda i: (jnp.maximum(i - n_p, 0), 0)),
            pl.BlockSpec((SUBLANES, tt), lambda i: (0, i)),
        ],
        out_specs=pl.BlockSpec(memory_space=pl.ANY),
        out_shape=jax.ShapeDtypeStruct((tab.shape[0] * PAIR_ROWS * ROW_CHUNKS, LANES), U32),
        scratch_shapes=[
            pltpu.SMEM((2, STEP_TILES, TAB_COLS), I32),
            pltpu.VMEM((2, STEP_TILES * PAIR_ROWS * ROW_CHUNKS, LANES), U32),
            pltpu.SemaphoreType.DMA((2,)),
            pltpu.SemaphoreType.DMA((2,)),
        ],
        compiler_params=_params("arbitrary"),
        name="moe_dispatch",
    )(tab, xn_p, xn_s, ri)


def _ffn_kernel(tile_ref, exp_ref, lo_ref, hi_ref, nxt_ref, slot_ref, xs_ref, wg_ref, wu_ref, wd_ref, o_ref,
                x_buf, wg_buf, wu_buf, wd_buf, wgu_scr, wd_scr, y_scr, xsem, wsem):
    w, n_w = pl.program_id(0), pl.num_programs(0)
    prev = jnp.maximum(w - 1, 0)
    lo, hi = lo_ref[w], hi_ref[w]
    e, slot = exp_ref[w], slot_ref[w]
    first = (w == 0) | (tile_ref[w] != tile_ref[prev])

    def x_copy(step):
        rows = FFN_TILE * ROW_CHUNKS
        return pltpu.make_async_copy(xs_ref.at[pl.ds(pl.multiple_of(tile_ref[step] * rows, rows), rows)],
                                     x_buf.at[step % 3], xsem.at[step % 3])

    @pl.when(w == 0)
    def _():
        x_copy(0).start()

    @pl.when((w == 0) & (n_w > 1))
    def _():
        x_copy(1).start()

    @pl.when(w + 2 < n_w)
    def _():
        x_copy(w + 2).start()

    x_copy(w).wait()

    def weight_copies(expert, sl):
        return [pltpu.make_async_copy(src.at[expert], buf.at[sl], wsem.at[sl])
                for src, buf in ((wg_ref, wg_buf), (wu_ref, wu_buf), (wd_ref, wd_buf))]

    @pl.when(w == 0)
    def _():
        for cp in weight_copies(e, slot):
            cp.start(priority=1)

    @pl.when((w == 0) | (e != exp_ref[prev]))
    def _():
        for cp in weight_copies(e, slot):
            cp.wait()

        @pl.when(nxt_ref[w] != e)
        def _():
            for cp in weight_copies(nxt_ref[w], 1 - slot):
                cp.start(priority=1)

        wgu_scr[:, 0:D_FF] = wg_buf[slot].astype(BF16)
        wgu_scr[:, D_FF:2 * D_FF] = wu_buf[slot].astype(BF16)
        wd_scr[...] = wd_buf[slot].astype(BF16)

    @pl.when(w == 0)
    def _():
        y_scr[...] = jnp.zeros_like(y_scr)

    def compute(r0, n_rows):
        chunk = pl.ds(r0 * ROW_CHUNKS, n_rows * ROW_CHUNKS)
        gu = _dot(_load_rows(x_buf.at[w % 3, chunk], n_rows).astype(BF16), wgu_scr[...])
        g, u = gu[:, 0:D_FF], gu[:, D_FF:2 * D_FF]
        y = _dot((g * jax.nn.sigmoid(g) * u).astype(BF16), wd_scr[...])
        row = r0 + lax.broadcasted_iota(I32, (n_rows, 1), 0)
        y = jnp.where(((row >= lo) & (row < hi)) | first, y, y_scr[r0:r0 + n_rows, :])
        y_scr[r0:r0 + n_rows, :] = y
        _store_rows(o_ref.at[chunk], y)

    half = FFN_TILE // 2
    top, bottom = hi <= half, lo >= half

    @pl.when((hi > lo) & top)
    def _():
        compute(0, half)

    @pl.when((hi > lo) & bottom)
    def _():
        compute(half, half)

    @pl.when((hi > lo) & jnp.logical_not(top | bottom))
    def _():
        compute(0, FFN_TILE)


def _expert_ffn(items, xs, w_gate, w_up, w_down):
    n_items = items[0].shape[0]
    row_spec = pl.BlockSpec((FFN_TILE * ROW_CHUNKS, LANES), lambda w, t, *_: (t[w], 0))
    grid_spec = pltpu.PrefetchScalarGridSpec(
        num_scalar_prefetch=len(items),
        grid=(n_items,),
        in_specs=[pl.BlockSpec(memory_space=pl.ANY)] * 4,
        out_specs=row_spec,
        scratch_shapes=[
            pltpu.VMEM((3, FFN_TILE * ROW_CHUNKS, LANES), U32),
            pltpu.VMEM((2, D_MODEL, D_FF), F32), pltpu.VMEM((2, D_MODEL, D_FF), F32),
            pltpu.VMEM((2, D_FF, D_MODEL), F32),
            pltpu.VMEM((D_MODEL, 2 * D_FF), BF16), pltpu.VMEM((D_FF, D_MODEL), BF16),
            pltpu.VMEM((FFN_TILE, D_MODEL), F32),
            pltpu.SemaphoreType.DMA((3,)),
            pltpu.SemaphoreType.DMA((2,)),
        ],
    )
    return pl.pallas_call(
        _ffn_kernel,
        grid_spec=grid_spec,
        out_shape=jax.ShapeDtypeStruct(xs.shape, U32),
        compiler_params=_params("arbitrary"),
        name="moe_ffn",
    )(*items, xs, w_gate, w_up, w_down)


def _work_items(counts, n_rows):
    n_tiles = n_rows // FFN_TILE
    n_items = n_tiles + N_EXPERTS - 1
    off = jnp.cumsum(counts) - counts
    first_tile = off // FFN_TILE
    last_tile = jnp.maximum(off + counts - 1, off) // FFN_TILE
    n_e = jnp.where(counts > 0, last_tile - first_tile + 1, 0)
    start = jnp.cumsum(n_e) - n_e
    total = jnp.sum(n_e)
    w = jnp.minimum(jnp.arange(n_items, dtype=I32), total - 1)
    ids = jnp.arange(N_EXPERTS, dtype=I32)[None, :]
    e = jnp.max(jnp.where((start[None, :] <= w[:, None]) & (n_e[None, :] > 0), ids, 0), axis=1)
    pick = lambda a: jnp.sum(jnp.where(ids == e[:, None], a[None, :], 0), axis=1)
    tile = pick(first_tile) + (w - pick(start))
    lo = jnp.clip(pick(off) - tile * FFN_TILE, 0, FFN_TILE)
    hi = jnp.clip(pick(off + counts) - tile * FFN_TILE, 0, FFN_TILE)
    hi = jnp.where(jnp.arange(n_items) < total, hi, lo)
    used = n_e[None, :] > 0
    nxt = jnp.min(jnp.where(used & (ids > e[:, None]), ids, N_EXPERTS), axis=1)
    nxt = jnp.where(nxt == N_EXPERTS, e, nxt)
    slot = jnp.sum(jnp.where(used & (ids < e[:, None]), 1, 0), axis=1) % 2
    return tuple(a.astype(I32) for a in (tile, e, lo, hi, nxt, slot))


def _run_table(n):
    tiles = n.shape[0]
    counts = jnp.sum(n, axis=0)
    goff = jnp.cumsum(counts) - counts
    dst0 = goff[None, :] + jnp.cumsum(n, axis=0) - n
    src0 = jnp.cumsum(n, axis=1) - n
    bits = jnp.arange(RUN_BITS, dtype=I32)[:, None, None]
    flag = (n[None] >> bits) & 1
    done = (n[None] >> (bits + 1)) << (bits + 1)
    entry = ((dst0[None] + done) << SRC_BITS) | (src0[None] + done)
    rank = jnp.cumsum(flag, axis=2) - flag
    slot = jnp.arange(N_EXPERTS, dtype=I32)
    sel = (flag[..., None] == 1) & (rank[..., None] == slot)
    packed = jnp.sum(jnp.where(sel, entry[..., None], 0), axis=2)
    m = jnp.sum(flag, axis=2).T
    tab = jnp.zeros((tiles, TAB_COLS), I32)
    tab = tab.at[:, :RUN_BITS].set(m)
    tab = tab.at[:, N_EXPERTS:N_EXPERTS * (RUN_BITS + 1)].set(packed.transpose(1, 0, 2).reshape(tiles, -1))
    return counts, tab


def _combine_kernel(g_tiles, n, tab_ref, h_ref, p_ref, ri_ref, rw_ref, ys_ref, gple_ref, wpg_ref, wpp_ref, gfin_ref,
                    o_ref, idx_smem, rows_scr, isem, rsem):
    i = pl.program_id(0)
    slot = i % 2

    def idx_copy(step, sl):
        return pltpu.make_async_copy(tab_ref.at[pl.ds(step * g_tiles, g_tiles)], idx_smem.at[sl], isem.at[sl])

    def gather(sl):
        for g in range(g_tiles):
            _run_copies(lambda col, g=g: idx_smem[sl, g, col], lambda src, dst, rows_, g=g: pltpu.make_async_copy(
                _row_slice(ys_ref, dst, rows_), _row_slice(rows_scr.at[sl], g * PAIR_ROWS + src, rows_), rsem.at[sl]))

    @pl.when(i == 0)
    def _():
        idx_copy(0, 0).start()
        idx_copy(0, 0).wait()
        gather(0)

    if n >= 2:
        @pl.when(i == 0)
        def _():
            idx_copy(1, 1).start()

        @pl.when(i + 1 < n)
        def _():
            idx_copy(i + 1, 1 - slot).wait()
            gather(1 - slot)

    @pl.when(i + 2 < n)
    def _():
        idx_copy(i + 2, slot).start()

    pltpu.make_async_copy(_row_slice(ys_ref, 0, g_tiles * PAIR_ROWS), rows_scr.at[slot], rsem.at[slot]).wait()

    ri, rw = ri_ref[...], rw_ref[...]
    rows = _pair_rows_iota()
    tiles = range(g_tiles)
    tok = lambda g: slice(g * TOK_TILE, (g + 1) * TOK_TILE)
    h = []
    for g in tiles:
        r1, r2 = ri[2:3, tok(g)], ri[3:4, tok(g)]
        wperm = jnp.where(rows == r1, rw[0:1, tok(g)], 0.0) + jnp.where(rows == r2, rw[1:2, tok(g)], 0.0)
        srows = _load_rows(rows_scr.at[slot, pl.ds(g * PAIR_ROWS * ROW_CHUNKS, PAIR_ROWS * ROW_CHUNKS)], PAIR_ROWS)
        h.append(h_ref[tok(g), :] + _dot_tn(wperm.astype(BF16), srows.astype(BF16)))
    gate = [jax.nn.sigmoid(_dot(_rms(h[g], gple_ref[...]).astype(BF16), wpg_ref[...])) for g in tiles]
    for g in tiles:
        hg = h[g] + _dot(p_ref[tok(g), :].astype(BF16), wpp_ref[...]) * gate[g]
        o_ref[tok(g), :] = _rms(hg, gfin_ref[...])


def _combine(tab, h, p, ri, rw, ys, gple, wpg, wpp, gfin):
    n_tiles = tab.shape[0]
    g_tiles = STEP_TILES
    n = n_tiles // g_tiles
    tt = g_tiles * TOK_TILE
    return pl.pallas_call(
        functools.partial(_combine_kernel, g_tiles, n),
        grid=(n,),
        in_specs=[
            pl.BlockSpec(memory_space=pl.ANY),
            pl.BlockSpec((tt, D_MODEL), lambda i: (i, 0)),
            pl.BlockSpec((tt, PLE_DIM), lambda i: (i, 0)),
            pl.BlockSpec((SUBLANES, tt), lambda i: (0, i)),
            pl.BlockSpec((SUBLANES, tt), lambda i: (0, i)),
            pl.BlockSpec(memory_space=pl.ANY),
            _const_spec(gple.shape), _const_spec(wpg.shape), _const_spec(wpp.shape), _const_spec(gfin.shape),
        ],
        out_specs=pl.BlockSpec((tt, D_MODEL), lambda i: (i, 0)),
        out_shape=jax.ShapeDtypeStruct((n_tiles * TOK_TILE, D_MODEL), F32),
        scratch_shapes=[
            pltpu.SMEM((2, g_tiles, TAB_COLS), I32),
            pltpu.VMEM((2, g_tiles * PAIR_ROWS * ROW_CHUNKS, LANES), U32),
            pltpu.SemaphoreType.DMA((2,)),
            pltpu.SemaphoreType.DMA((2,)),
        ],
        compiler_params=_params("arbitrary"),
        name="moe_combine",
    )(tab, h, p, ri, rw, ys, gple, wpg, wpp, gfin)


def _log_gamma():
    return np.log(1.0 - 2.0 ** (-5.0 - np.arange(RET_HEADS, dtype=np.float64)))


def _rope_tables(pos):
    inv = 1.0 / (ROPE_BASE ** (np.arange(0, RET_DK, 2, dtype=np.float64) / RET_DK))
    ang = np.asarray(pos, np.float64)[:, None] * inv[None, :]
    cos, sin = np.cos(ang), np.sin(ang)
    return (jnp.asarray(np.concatenate([cos, cos], axis=-1), F32),
            jnp.asarray(np.concatenate([-sin, sin], axis=-1), F32))


def _decay_tables(c):
    lg = _log_gamma()
    idx = np.arange(c, dtype=np.float64)
    diff = idx[:, None] - idx[None, :]
    dmask = np.where((diff >= 0.0)[None], np.exp(lg[:, None, None] * np.maximum(diff, 0.0)[None]), 0.0)
    q_dec = np.exp(lg[None, :] * (idx[:, None] + 1.0))
    k_dec = np.exp(lg[None, :] * (c - 1.0 - idx[:, None]))
    c_dec = np.exp(lg * c)
    lanes = lambda a: np.repeat(a, RET_DK, axis=1)
    cdec = np.zeros((SUBLANES, LANES))
    cdec[:RET_HEADS] = c_dec[:, None]
    return tuple(jnp.asarray(a, F32) for a in (lanes(q_dec), lanes(k_dec), dmask, cdec))


def _router_params(w_rg, b_rg, w_re, b_re):
    wr = jnp.zeros((ROUTER_ROWS, D_MODEL), F32).at[:N_GROUPS].set(w_rg.T).at[SUBLANES:].set(w_re.T)
    br = jnp.full((ROUTER_ROWS,), NEG_BIG, F32).at[:N_GROUPS].set(b_rg).at[SUBLANES:].set(b_re.reshape(-1))
    return wr.astype(BF16), jnp.broadcast_to(br[:, None], (ROUTER_ROWS, LANES))


def _strict_upper(t):
    i = np.arange(t)
    return jnp.asarray(i[:, None] < i[None, :], BF16)


def _pad_rows(a, rows):
    return jnp.zeros((rows,) + a.shape[1:], a.dtype).at[:a.shape[0]].set(a)


def kernel(x_prompt, x_sample, state_conv, state_ret, p_prompt, p_sample, g_mix, w_in, conv_w, ret_gn, w_o, g_ffn,
           w_router_group, b_router_group, w_router_expert, b_router_expert, w_gate, w_up, w_down, g_ple,
           w_ple_proj, w_ple_gate, g_final):
    bp, tp, _ = x_prompt.shape
    bs = x_sample.shape[0]
    assert x_sample.shape[1] == 1 and g_mix.shape[0] == 1, "one layer, one new token per sample request"
    assert bs < SAMPLE_ROWS and tp % MIX_TILE == 0
    n_p = bp * tp

    tabs = _rope_tables(np.arange(tp)) + _decay_tables(TOK_TILE)
    wr, br = _router_params(w_router_group[0], b_router_group[0], w_router_expert[0], b_router_expert[0])
    win, wo, convw = w_in[0].astype(BF16), w_o[0].astype(BF16), _pad_rows(conv_w[0], SUBLANES)
    tri, ltri = _strict_upper(TOK_TILE), _strict_upper(N_EXPERTS).T

    xs0 = x_sample[:, 0]
    (yc_s, gate_s, conv_s), sample = _sample_inputs(xs0, state_conv[0], state_ret[0], g_mix, win, convw, ret_gn)
    h_p, xn_p, ri_p, rw_p, n_p_tab, conv_p, ret_p, ret_s, o_s = _mixer_prompt(
        x_prompt, tabs, (g_mix, win, convw, ret_gn, wo, g_ffn, wr, br, tri, ltri), sample)
    h_s, xn_s, ri_s, rw_s, n_s_tab = _sample_out(xs0, yc_s, o_s, gate_s, wo, g_ffn, wr, br, tri, ltri)
    h_p, xn_p = h_p.reshape(n_p, D_MODEL), xn_p.reshape(n_p, D_MODEL)

    n_tab = jnp.concatenate([n_p_tab, n_s_tab], axis=0)[:, :, 0].astype(I32)
    counts, tab = _run_table(n_tab)
    ri = jnp.concatenate([ri_p, ri_s], axis=1)
    n_tiles = n_tab.shape[0]

    xs = _dispatch(tab, xn_p, xn_s, ri)
    ys = _expert_ffn(_work_items(counts, n_tiles * PAIR_ROWS), xs, w_gate[0], w_up[0], w_down[0])

    wpg, wpp = w_ple_gate[0].astype(BF16), w_ple_proj[0].astype(BF16)
    gfin = g_final[None, :]
    p_s = _pad_rows(p_sample[0].reshape(bs, PLE_DIM), SAMPLE_ROWS)
    n_s_tiles = SAMPLE_ROWS // TOK_TILE
    y_p = _combine(tab[:n_tiles - n_s_tiles], h_p, p_prompt[0].reshape(n_p, PLE_DIM), ri_p, rw_p, ys, g_ple, wpg, wpp, gfin)
    y_s = _combine(tab[n_tiles - n_s_tiles:], h_s, p_s, ri_s, rw_s, ys, g_ple, wpg, wpp, gfin)
    return (y_p.reshape(bp, tp, D_MODEL), y_s[:bs].reshape(bs, 1, D_MODEL), conv_p[None], ret_p[None],
            conv_s[None], ret_s[None])
```

```python
import functools

import jax
import jax.numpy as jnp
import numpy as np
from jax import lax
from jax.experimental import pallas as pl
from jax.experimental.pallas import tpu as pltpu

F32, BF16, I32, U32 = jnp.float32, jnp.bfloat16, jnp.int32, jnp.uint32

D_MODEL = 1024
CONV_DIM = 512
CONV_WIDTH = 3
RET_DIM = 512
RET_HEADS = 4
RET_DK = 128
RET_DV = 128
ROPE_BASE = 10000.0
N_GROUPS = 4
EXPERTS_PER_GROUP = 8
N_EXPERTS = 32
D_FF = 512
PLE_DIM = 256
EPS = 1e-6
PAST_LEN = 16384

LANES = 128
SUBLANES = 8
ROW_CHUNKS = D_MODEL // (2 * LANES)
ROUTER_ROWS = SUBLANES + N_EXPERTS
VMEM_LIMIT = 56 * 1024 * 1024
NEG_BIG = -1e30

TOK_TILE = 256
MIX_TILE = 1024
PAIR_ROWS = 2 * TOK_TILE
STEP_TILES = 2
SAMPLE_ROWS = STEP_TILES * TOK_TILE
FFN_TILE = 512
FFN_RING = 3
SAMPLE_BLOCK = 8
RUN_BITS = 9
SRC_BITS = 10
TAB_COLS = 384


def _rms(x, g):
    return x * lax.rsqrt(jnp.mean(x * x, axis=-1, keepdims=True) + EPS) * g


def _dot(a, b):
    return jnp.dot(a, b, preferred_element_type=F32)


def _dot_nt(a, b):
    return lax.dot_general(a, b, (((1,), (1,)), ((), ())), preferred_element_type=F32)


def _dot_tn(a, b):
    return lax.dot_general(a, b, (((0,), (0,)), ((), ())), preferred_element_type=F32)


def _rope(x, cos, sin_signed):
    return x * cos + pltpu.roll(x, RET_DK // 2, 1) * sin_signed


def _const_spec(shape):
    nd = len(shape)
    return pl.BlockSpec(shape, lambda *_: (0,) * nd)


def _params(*sem):
    return pltpu.CompilerParams(dimension_semantics=sem, vmem_limit_bytes=VMEM_LIMIT)


def _route(h, gffn, wr, br, tri, ltri):
    t = h.shape[0]
    xn = _rms(h, gffn).astype(BF16)
    lt = _dot_nt(wr, xn) + br[:, 0:1]
    row8 = lax.broadcasted_iota(I32, (SUBLANES, t), 0).astype(F32)
    gl = lt[0:SUBLANES]
    m = jnp.max(gl, axis=0, keepdims=True)
    g_top = 1.0 / jnp.sum(jnp.exp(gl - m), axis=0, keepdims=True)
    gidx = jnp.min(jnp.where(gl == m, row8, float(SUBLANES)), axis=0, keepdims=True)
    e_sel = jnp.where(gidx == 0.0, lt[8:16],
                      jnp.where(gidx == 1.0, lt[16:24], jnp.where(gidx == 2.0, lt[24:32], lt[32:40])))
    m1 = jnp.max(e_sel, axis=0, keepdims=True)
    i1 = jnp.min(jnp.where(e_sel == m1, row8, float(SUBLANES)), axis=0, keepdims=True)
    rest = jnp.where(row8 == i1, -jnp.inf, e_sel)
    m2 = jnp.max(rest, axis=0, keepdims=True)
    i2 = jnp.min(jnp.where(rest == m2, row8, float(SUBLANES)), axis=0, keepdims=True)
    d = jnp.exp(m2 - m1)
    w1 = g_top / (1.0 + d)
    w2 = g_top * d / (1.0 + d)
    e1 = gidx * float(EXPERTS_PER_GROUP) + i1
    e2 = gidx * float(EXPERTS_PER_GROUP) + i2
    row32 = lax.broadcasted_iota(I32, (N_EXPERTS, t), 0).astype(F32)
    a1 = jnp.where(row32 == e1, 1.0, 0.0)
    a2 = jnp.where(row32 == e2, 1.0, 0.0)
    a = a1 + a2
    n = jnp.broadcast_to(jnp.sum(a, axis=1, keepdims=True), (N_EXPERTS, LANES))
    start = _dot(ltri, n.astype(BF16))
    base = _dot(a.astype(BF16), tri) + start[:, 0:1]
    r1 = jnp.sum(a1 * base, axis=0, keepdims=True)
    r2 = jnp.sum(a2 * base, axis=0, keepdims=True)
    ri = jnp.where(row8 == 0.0, e1, jnp.where(row8 == 1.0, e2, jnp.where(row8 == 2.0, r1,
                                                                         jnp.where(row8 == 3.0, r2, 0.0))))
    rw = jnp.where(row8 == 0.0, w1, jnp.where(row8 == 1.0, w2, 0.0))
    return ri.astype(I32), rw, n, xn


def _group_norm_gate(o, gate):
    mu = jnp.mean(o, axis=-1, keepdims=True)
    oc = o - mu
    return oc * lax.rsqrt(jnp.mean(oc * oc, axis=-1, keepdims=True) + EPS) * gate


def _mixer_prompt_kernel(x_ref, cos_ref, sin_ref, qdec_ref, kdec_ref, dmask_ref, cdec_ref, gmix_ref, win_ref,
                         convw_ref, retgn_ref, wo_ref, gffn_ref, wr_ref, br_ref, tri_ref, ltri_ref,
                         ss_ref, sq_ref, sk_ref, skt_ref, sv_ref, sdec_ref,
                         h_ref, xn_ref, ri_ref, rw_ref, n_ref, conv_ref, ret_ref, snew_ref, so_ref,
                         s_scr, z_scr, mix_scr):
    j = pl.program_id(1)
    n_j = pl.num_programs(1)
    tt = x_ref.shape[1]

    @pl.when(j == 0)
    def _():
        s_scr[...] = jnp.zeros_like(s_scr)
        z_scr[0:SUBLANES, :] = jnp.zeros((SUBLANES, CONV_DIM), F32)

    cw = convw_ref[...]
    retgn = retgn_ref[...]
    q0 = 3 * CONV_DIM

    def project(p0):
        x = x_ref[0, p0:p0 + TOK_TILE, :]
        proj = _dot(_rms(x, gmix_ref[...]).astype(BF16), win_ref[...])
        gb = proj[:, 0:CONV_DIM]
        z = proj[:, CONV_DIM:2 * CONV_DIM] * proj[:, 2 * CONV_DIM:3 * CONV_DIM]
        z0 = SUBLANES + p0
        z_scr[z0:z0 + TOK_TILE, :] = z
        yc = cw[0:1] * z_scr[z0 - 2:z0 - 2 + TOK_TILE, :] + cw[1:2] * z_scr[z0 - 1:z0 - 1 + TOK_TILE, :] + cw[2:3] * z
        mix_scr[p0:p0 + TOK_TILE, 0:CONV_DIM] = (gb * yc).astype(BF16)
        return x, proj

    def mix(p0, x, proj):
        cos = cos_ref[p0:p0 + TOK_TILE, :]
        sin = sin_ref[p0:p0 + TOK_TILE, :]
        heads = range(RET_HEADS)
        lane = lambda hh: slice(hh * RET_DK, (hh + 1) * RET_DK)
        qr = [_rope(proj[:, q0 + hh * RET_DK:q0 + (hh + 1) * RET_DK], cos, sin) for hh in heads]
        kr = [_rope(proj[:, q0 + RET_DIM + hh * RET_DK:q0 + RET_DIM + (hh + 1) * RET_DK], cos, sin) * (RET_DK ** -0.5)
              for hh in heads]
        v = [proj[:, q0 + 2 * RET_DIM + hh * RET_DV:q0 + 2 * RET_DIM + (hh + 1) * RET_DV].astype(BF16) for hh in heads]
        s_old = [s_scr[hh] for hh in heads]
        scores = [_dot_nt(qr[hh].astype(BF16), kr[hh].astype(BF16)) * dmask_ref[hh] for hh in heads]
        o = []
        for hh in heads:
            lhs = jnp.concatenate([scores[hh].astype(BF16), (qr[hh] * qdec_ref[:, lane(hh)]).astype(BF16)], axis=1)
            o.append(_dot(lhs, jnp.concatenate([v[hh], s_old[hh].astype(BF16)], axis=0)))
        for hh in heads:
            kd = (kr[hh] * kdec_ref[:, lane(hh)]).T.astype(BF16)
            s_scr[hh] = s_old[hh] * cdec_ref[hh:hh + 1, :] + _dot(kd, v[hh])
        for hh in heads:
            g = proj[:, q0 + 3 * RET_DIM + hh * RET_DV:q0 + 3 * RET_DIM + (hh + 1) * RET_DV]
            gate = retgn[:, lane(hh)] * (g * jax.nn.sigmoid(g))
            mix_scr[p0:p0 + TOK_TILE, CONV_DIM + hh * RET_DV:CONV_DIM + (hh + 1) * RET_DV] = (
                _group_norm_gate(o[hh], gate).astype(BF16))
        h = x + _dot(mix_scr[p0:p0 + TOK_TILE, :], wo_ref[...])
        h_ref[0, p0:p0 + TOK_TILE, :] = h
        return h

    def route(part, h):
        p0 = part * TOK_TILE
        ri, rw, n, xn = _route(h, gffn_ref[...], wr_ref[...], br_ref[...], tri_ref[...], ltri_ref[...])
        ri_ref[:, p0:p0 + TOK_TILE] = ri
        rw_ref[:, p0:p0 + TOK_TILE] = rw
        n_ref[part] = n
        xn_ref[0, p0:p0 + TOK_TILE, :] = xn

    parts = tt // TOK_TILE
    nxt = project(0)
    per_part = SAMPLE_BLOCK // parts
    for part in range(parts):
        _sample_state_update(ss_ref, sq_ref, sk_ref, skt_ref, sv_ref, sdec_ref, snew_ref, so_ref,
                             range(part * per_part, (part + 1) * per_part))
        h = mix(part * TOK_TILE, *nxt)
        if part + 1 < parts:
            nxt = project((part + 1) * TOK_TILE)
        route(part, h)

    @pl.when(j == n_j - 1)
    def _():
        conv_ref[0] = z_scr[SUBLANES + tt - 2:SUBLANES + tt, :]
        ret_ref[0] = s_scr[...]

    z_scr[0:SUBLANES, :] = z_scr[tt:tt + SUBLANES, :]


def _mixer_prompt(x, tabs, wts, sample):
    bsz, t, _ = x.shape
    tt = MIX_TILE
    n_j = t // tt
    parts = tt // TOK_TILE
    n_tok = bsz * t
    nb = SAMPLE_BLOCK
    s_state, s_q, s_k, s_kt, s_v, s_dec = sample
    assert s_state.shape[0] == nb * bsz * n_j, "one block of sample requests per mixer grid step"
    args = (x,) + tuple(tabs) + tuple(wts) + tuple(sample)
    step = lambda b, j: b * n_j + j
    in_specs = [
        pl.BlockSpec((1, tt, D_MODEL), lambda b, j: (b, j, 0)),
        pl.BlockSpec((tt, LANES), lambda b, j: (j, 0)),
        pl.BlockSpec((tt, LANES), lambda b, j: (j, 0)),
    ] + [_const_spec(a.shape) for a in args[3:3 + len(tabs) - 2 + len(wts)]] + [
        pl.BlockSpec((nb, RET_HEADS, RET_DK, RET_DV), lambda b, j: (step(b, j), 0, 0, 0)),
        pl.BlockSpec((nb, RET_DIM), lambda b, j: (step(b, j), 0)),
        pl.BlockSpec((nb, RET_DIM), lambda b, j: (step(b, j), 0)),
        pl.BlockSpec((1, RET_HEADS, RET_DK, nb), lambda b, j: (step(b, j), 0, 0, 0)),
        pl.BlockSpec((nb, RET_DIM), lambda b, j: (step(b, j), 0)),
        _const_spec(s_dec.shape),
    ]
    out_shape = (
        jax.ShapeDtypeStruct((bsz, t, D_MODEL), F32),
        jax.ShapeDtypeStruct((bsz, t, D_MODEL), BF16),
        jax.ShapeDtypeStruct((SUBLANES, n_tok), I32),
        jax.ShapeDtypeStruct((SUBLANES, n_tok), F32),
        jax.ShapeDtypeStruct((bsz * n_j * parts, N_EXPERTS, LANES), F32),
        jax.ShapeDtypeStruct((bsz, CONV_WIDTH - 1, CONV_DIM), F32),
        jax.ShapeDtypeStruct((bsz, RET_HEADS, RET_DK, RET_DV), F32),
        jax.ShapeDtypeStruct(s_state.shape, F32),
        jax.ShapeDtypeStruct(s_q.shape, F32),
    )
    out_specs = (
        pl.BlockSpec((1, tt, D_MODEL), lambda b, j: (b, j, 0)),
        pl.BlockSpec((1, tt, D_MODEL), lambda b, j: (b, j, 0)),
        pl.BlockSpec((SUBLANES, tt), lambda b, j: (0, b * n_j + j)),
        pl.BlockSpec((SUBLANES, tt), lambda b, j: (0, b * n_j + j)),
        pl.BlockSpec((parts, N_EXPERTS, LANES), lambda b, j: (b * n_j + j, 0, 0)),
        pl.BlockSpec((1, CONV_WIDTH - 1, CONV_DIM), lambda b, j: (b, 0, 0)),
        pl.BlockSpec((1, RET_HEADS, RET_DK, RET_DV), lambda b, j: (b, 0, 0, 0)),
        pl.BlockSpec((nb, RET_HEADS, RET_DK, RET_DV), lambda b, j: (step(b, j), 0, 0, 0)),
        pl.BlockSpec((nb, RET_DIM), lambda b, j: (step(b, j), 0)),
    )
    return pl.pallas_call(
        _mixer_prompt_kernel,
        grid=(bsz, n_j),
        in_specs=in_specs,
        out_specs=out_specs,
        out_shape=out_shape,
        scratch_shapes=[
            pltpu.VMEM((RET_HEADS, RET_DK, RET_DV), F32),
            pltpu.VMEM((tt + SUBLANES, CONV_DIM), F32),
            pltpu.VMEM((tt, D_MODEL), BF16),
        ],
        compiler_params=_params("arbitrary", "arbitrary"),
        name="mixer_prompt",
    )(*args)


def _sample_proj_kernel(x_ref, sc_ref, cos_ref, sin_ref, gmix_ref, win_ref, convw_ref, retgn_ref,
                        yc_ref, q_ref, k_ref, v_ref, gate_ref, conv_ref):
    x = x_ref[...]
    u = _rms(x, gmix_ref[...]).astype(BF16)
    proj = _dot(u, win_ref[...])
    gb = proj[:, 0:CONV_DIM]
    z = proj[:, CONV_DIM:2 * CONV_DIM] * proj[:, 2 * CONV_DIM:3 * CONV_DIM]
    cw = convw_ref[...]
    buf0, buf1 = sc_ref[:, 0:CONV_DIM], sc_ref[:, CONV_DIM:2 * CONV_DIM]
    yc_ref[...] = gb * (cw[0:1] * buf0 + cw[1:2] * buf1 + cw[2:3] * z)
    conv_ref[:, 0:CONV_DIM] = buf1
    conv_ref[:, CONV_DIM:2 * CONV_DIM] = z
    q0 = 3 * CONV_DIM
    cos, sin = cos_ref[0:1, :], sin_ref[0:1, :]
    retgn = retgn_ref[...]
    for hh in range(RET_HEADS):
        l0 = hh * RET_DK
        q_ref[:, l0:l0 + RET_DK] = _rope(proj[:, q0 + l0:q0 + l0 + RET_DK], cos, sin)
        k_ref[:, l0:l0 + RET_DK] = _rope(proj[:, q0 + RET_DIM + l0:q0 + RET_DIM + l0 + RET_DK], cos, sin) * (RET_DK ** -0.5)
    v_ref[...] = proj[:, q0 + 2 * RET_DIM:q0 + 3 * RET_DIM]
    g = proj[:, q0 + 3 * RET_DIM:q0 + 4 * RET_DIM]
    gate_ref[...] = retgn * (g * jax.nn.sigmoid(g))


def _sample_proj(x, sc, cos, sin, gmix, win, convw, retgn):
    n = x.shape[0]
    outs = (
        jax.ShapeDtypeStruct((n, CONV_DIM), F32), jax.ShapeDtypeStruct((n, RET_DIM), F32),
        jax.ShapeDtypeStruct((n, RET_DIM), F32), jax.ShapeDtypeStruct((n, RET_DIM), F32),
        jax.ShapeDtypeStruct((n, RET_DIM), F32), jax.ShapeDtypeStruct((n, 2 * CONV_DIM), F32),
    )
    args = (x, sc, cos, sin, gmix, win, convw, retgn)
    return pl.pallas_call(
        _sample_proj_kernel,
        grid=(1,),
        in_specs=[_const_spec(a.shape) for a in args],
        out_specs=tuple(_const_spec(o.shape) for o in outs),
        out_shape=outs,
        compiler_params=_params("arbitrary"),
        name="sample_proj",
    )(*args)


def _sample_state_update(s_ref, q_ref, k_ref, kt_ref, v_ref, dec_ref, snew_ref, o_ref, requests):
    for r in requests:
        for hh in range(RET_HEADS):
            l0 = hh * RET_DV
            s = s_ref[r, hh]
            q = q_ref[r:r + 1, l0:l0 + RET_DK]
            k = k_ref[r:r + 1, l0:l0 + RET_DK]
            kc = kt_ref[0, hh, :, r:r + 1]
            vr = v_ref[r:r + 1, l0:l0 + RET_DV]
            qdec = dec_ref[hh:hh + 1, :]
            cdec = dec_ref[RET_HEADS + hh:RET_HEADS + hh + 1, :]
            qk = jnp.sum(q * k, axis=1, keepdims=True)
            o_ref[r:r + 1, l0:l0 + RET_DV] = qk * vr + _dot((q * qdec).astype(BF16), s.astype(BF16))
            snew_ref[r, hh] = s * cdec + kc * vr


def _sample_out_kernel(x_ref, yc_ref, o_ref, gate_ref, wo_ref, gffn_ref, wr_ref, br_ref, tri_ref, ltri_ref,
                       h_ref, xn_ref, ri_ref, rw_ref, n_ref):
    n = x_ref.shape[0]
    parts = [yc_ref[...].astype(BF16)]
    for hh in range(RET_HEADS):
        l0 = hh * RET_DV
        parts.append(_group_norm_gate(o_ref[:, l0:l0 + RET_DV], gate_ref[:, l0:l0 + RET_DV]).astype(BF16))
    h_ref[0:n, :] = x_ref[...] + _dot(jnp.concatenate(parts, axis=-1), wo_ref[...])
    h_ref[n:SAMPLE_ROWS, :] = jnp.zeros((SAMPLE_ROWS - n, D_MODEL), F32)
    for part in range(SAMPLE_ROWS // TOK_TILE):
        p0 = part * TOK_TILE
        ri, rw, cnt, xn = _route(h_ref[p0:p0 + TOK_TILE, :], gffn_ref[...], wr_ref[...], br_ref[...], tri_ref[...],
                                 ltri_ref[...])
        ri_ref[:, p0:p0 + TOK_TILE] = ri
        rw_ref[:, p0:p0 + TOK_TILE] = rw
        n_ref[part] = cnt
        xn_ref[p0:p0 + TOK_TILE, :] = xn


def _sample_out(x, yc, o, gate, wo, gffn, wr, br, tri, ltri):
    outs = (
        jax.ShapeDtypeStruct((SAMPLE_ROWS, D_MODEL), F32), jax.ShapeDtypeStruct((SAMPLE_ROWS, D_MODEL), BF16),
        jax.ShapeDtypeStruct((SUBLANES, SAMPLE_ROWS), I32), jax.ShapeDtypeStruct((SUBLANES, SAMPLE_ROWS), F32),
        jax.ShapeDtypeStruct((SAMPLE_ROWS // TOK_TILE, N_EXPERTS, LANES), F32),
    )
    args = (x, yc, o, gate, wo, gffn, wr, br, tri, ltri)
    return pl.pallas_call(
        _sample_out_kernel,
        grid=(1,),
        in_specs=[_const_spec(a.shape) for a in args],
        out_specs=tuple(_const_spec(o_.shape) for o_ in outs),
        out_shape=outs,
        compiler_params=_params("arbitrary"),
        name="sample_out",
    )(*args)


def _sample_inputs(x, state_conv, state_ret, gmix, win, convw, retgn):
    n = x.shape[0]
    lg = _log_gamma()
    cos, sin = _rope_tables(np.full((SUBLANES,), PAST_LEN))
    yc, q, k, v, gate, conv_new = _sample_proj(x, state_conv.reshape(n, 2 * CONV_DIM), cos, sin, gmix, win, convw, retgn)
    nb = SAMPLE_BLOCK
    cols = lambda a: a.reshape(n // nb, nb, RET_HEADS, RET_DK).transpose(0, 2, 3, 1)
    step = np.exp(lg[:, None] * 1.0)
    dec = jnp.asarray(np.broadcast_to(np.concatenate([step, step], axis=0), (2 * RET_HEADS, LANES)), F32)
    return (yc, gate, conv_new.reshape(n, 2, CONV_DIM)), (state_ret, q, k, cols(k), v, dec)


def _load_rows(ref, n_rows):
    words = [ref[pl.ds(c, n_rows, stride=ROW_CHUNKS), :] for c in range(ROW_CHUNKS)]
    half = lambda i: [pltpu.unpack_elementwise(w, index=i, packed_dtype=BF16, unpacked_dtype=F32) for w in words]
    return jnp.concatenate(half(0) + half(1), axis=-1)


def _store_rows(ref, val):
    half = D_MODEL // 2
    for c in range(ROW_CHUNKS):
        pair = [val[:, c * LANES:(c + 1) * LANES], val[:, half + c * LANES:half + (c + 1) * LANES]]
        ref[pl.ds(c, val.shape[0], stride=ROW_CHUNKS), :] = pltpu.pack_elementwise(pair, packed_dtype=BF16)


def _row_slice(ref, row, rows):
    return ref.at[pl.ds(pl.multiple_of(row * ROW_CHUNKS, ROW_CHUNKS), rows * ROW_CHUNKS)]


def _run_copies(table, make_copy):
    for b in range(RUN_BITS):
        def body(j, carry, b=b):
            entry = table(N_EXPERTS + N_EXPERTS * b + j)
            make_copy(entry & ((1 << SRC_BITS) - 1), entry >> SRC_BITS, 1 << b).start()
            return carry

        lax.fori_loop(0, table(b), body, 0)


def _pair_rows_iota():
    return lax.broadcasted_iota(I32, (PAIR_ROWS, TOK_TILE), 0)


def _dispatch_kernel(n_p, tab_ref, xp_ref, xs_in_ref, ri_ref, xs_ref, idx_smem, y_scr, isem, rsem):
    i, n = pl.program_id(0), pl.num_programs(0)
    slot = i % 2
    step_rows = STEP_TILES * PAIR_ROWS

    def idx_copy(step, sl):
        return pltpu.make_async_copy(tab_ref.at[pl.ds(step * STEP_TILES, STEP_TILES)], idx_smem.at[sl], isem.at[sl])

    def wait_step(sl):
        pltpu.make_async_copy(_row_slice(xs_ref, 0, step_rows), y_scr.at[sl], rsem.at[sl]).wait()

    @pl.when(i == 0)
    def _():
        idx_copy(0, 0).start()

    @pl.when(i + 1 < n)
    def _():
        idx_copy(i + 1, 1 - slot).start()

    @pl.when(i >= 2)
    def _():
        wait_step(slot)

    xn = jnp.where(i < n_p, xp_ref[...], xs_in_ref[...])
    ri = ri_ref[...]
    rows = _pair_rows_iota()
    for g in range(STEP_TILES):
        tok = slice(g * TOK_TILE, (g + 1) * TOK_TILE)
        perm = jnp.where((rows == ri[2:3, tok]) | (rows == ri[3:4, tok]), 1.0, 0.0).astype(BF16)
        y = _dot(perm, xn[tok, :])
        _store_rows(y_scr.at[slot, pl.ds(g * PAIR_ROWS * ROW_CHUNKS, PAIR_ROWS * ROW_CHUNKS)], y)

    idx_copy(i, slot).wait()
    for g in range(STEP_TILES):
        _run_copies(lambda col, g=g: idx_smem[slot, g, col], lambda src, dst, rows_, g=g: pltpu.make_async_copy(
            _row_slice(y_scr.at[slot], g * PAIR_ROWS + src, rows_), _row_slice(xs_ref, dst, rows_), rsem.at[slot]))

    @pl.when(i == n - 1)
    def _():
        wait_step(slot)

    @pl.when((i == n - 1) & (n >= 2))
    def _():
        wait_step(1 - slot)


def _dispatch(tab, xn_p, xn_s, ri):
    tt = STEP_TILES * TOK_TILE
    n_p = xn_p.shape[0] // tt
    n = tab.shape[0] // STEP_TILES
    return pl.pallas_call(
        functools.partial(_dispatch_kernel, n_p),
        grid=(n,),
        in_specs=[
            pl.BlockSpec(memory_space=pl.ANY),
            pl.BlockSpec((tt, D_MODEL), lambda i: (jnp.minimum(i, n_p - 1), 0)),
            pl.BlockSpec((tt, D_MODEL), lambda i: (jnp.maximum(i - n_p, 0), 0)),
            pl.BlockSpec((SUBLANES, tt), lambda i: (0, i)),
        ],
        out_specs=pl.BlockSpec(memory_space=pl.ANY),
        out_shape=jax.ShapeDtypeStruct((tab.shape[0] * PAIR_ROWS * ROW_CHUNKS, LANES), U32),
        scratch_shapes=[
            pltpu.SMEM((2, STEP_TILES, TAB_COLS), I32),
            pltpu.VMEM((2, STEP_TILES * PAIR_ROWS * ROW_CHUNKS, LANES), U32),
            pltpu.SemaphoreType.DMA((2,)),
            pltpu.SemaphoreType.DMA((2,)),
        ],
        compiler_params=_params("arbitrary"),
        name="moe_dispatch",
    )(tab, xn_p, xn_s, ri)


def _ffn_kernel(tile_ref, exp_ref, lo_ref, hi_ref, nxt_ref, slot_ref, xs_ref, wg_ref, wu_ref, wd_ref, o_ref,
                x_buf, wg_buf, wu_buf, wd_buf, wgu_scr, wd_scr, y_scr, xsem, wsem):
    w, n_w = pl.program_id(0), pl.num_programs(0)
    prev = jnp.maximum(w - 1, 0)
    lo, hi = lo_ref[w], hi_ref[w]
    e, slot = exp_ref[w], slot_ref[w]
    first = (w == 0) | (tile_ref[w] != tile_ref[prev])

    ahead = FFN_RING - 1

    def x_copy(step):
        rows = FFN_TILE * ROW_CHUNKS
        return pltpu.make_async_copy(xs_ref.at[pl.ds(pl.multiple_of(tile_ref[step] * rows, rows), rows)],
                                     x_buf.at[step % FFN_RING], xsem.at[step % FFN_RING])

    for step in range(ahead):
        @pl.when((w == 0) & (step < n_w))
        def _():
            x_copy(step).start()

    @pl.when(w + ahead < n_w)
    def _():
        x_copy(w + ahead).start()

    x_copy(w).wait()

    def weight_copies(expert, sl):
        return [pltpu.make_async_copy(src.at[expert], buf.at[sl], wsem.at[sl])
                for src, buf in ((wg_ref, wg_buf), (wu_ref, wu_buf), (wd_ref, wd_buf))]

    @pl.when(w == 0)
    def _():
        for cp in weight_copies(e, slot):
            cp.start(priority=1)

    @pl.when((w == 0) | (e != exp_ref[prev]))
    def _():
        for cp in weight_copies(e, slot):
            cp.wait()

        @pl.when(nxt_ref[w] != e)
        def _():
            for cp in weight_copies(nxt_ref[w], 1 - slot):
                cp.start(priority=1)

        wgu_scr[:, 0:D_FF] = wg_buf[slot].astype(BF16)
        wgu_scr[:, D_FF:2 * D_FF] = wu_buf[slot].astype(BF16)
        wd_scr[...] = wd_buf[slot].astype(BF16)

    @pl.when(w == 0)
    def _():
        y_scr[...] = jnp.zeros_like(y_scr)

    def compute(r0, n_rows):
        chunk = pl.ds(r0 * ROW_CHUNKS, n_rows * ROW_CHUNKS)
        gu = _dot(_load_rows(x_buf.at[w % FFN_RING, chunk], n_rows).astype(BF16), wgu_scr[...])
        g, u = gu[:, 0:D_FF], gu[:, D_FF:2 * D_FF]
        y = _dot((g * jax.nn.sigmoid(g) * u).astype(BF16), wd_scr[...])
        row = r0 + lax.broadcasted_iota(I32, (n_rows, 1), 0)
        y = jnp.where(((row >= lo) & (row < hi)) | first, y, y_scr[r0:r0 + n_rows, :])
        y_scr[r0:r0 + n_rows, :] = y
        _store_rows(o_ref.at[chunk], y)

    half = FFN_TILE // 2
    top, bottom = hi <= half, lo >= half

    @pl.when((hi > lo) & top)
    def _():
        compute(0, half)

    @pl.when((hi > lo) & bottom)
    def _():
        compute(half, half)

    @pl.when((hi > lo) & jnp.logical_not(top | bottom))
    def _():
        compute(0, FFN_TILE)


def _expert_ffn(items, xs, w_gate, w_up, w_down):
    n_items = items[0].shape[0]
    row_spec = pl.BlockSpec((FFN_TILE * ROW_CHUNKS, LANES), lambda w, t, *_: (t[w], 0))
    grid_spec = pltpu.PrefetchScalarGridSpec(
        num_scalar_prefetch=len(items),
        grid=(n_items,),
        in_specs=[pl.BlockSpec(memory_space=pl.ANY)] * 4,
        out_specs=row_spec,
        scratch_shapes=[
            pltpu.VMEM((FFN_RING, FFN_TILE * ROW_CHUNKS, LANES), U32),
            pltpu.VMEM((2, D_MODEL, D_FF), F32), pltpu.VMEM((2, D_MODEL, D_FF), F32),
            pltpu.VMEM((2, D_FF, D_MODEL), F32),
            pltpu.VMEM((D_MODEL, 2 * D_FF), BF16), pltpu.VMEM((D_FF, D_MODEL), BF16),
            pltpu.VMEM((FFN_TILE, D_MODEL), F32),
            pltpu.SemaphoreType.DMA((FFN_RING,)),
            pltpu.SemaphoreType.DMA((2,)),
        ],
    )
    return pl.pallas_call(
        _ffn_kernel,
        grid_spec=grid_spec,
        out_shape=jax.ShapeDtypeStruct(xs.shape, U32),
        compiler_params=_params("arbitrary"),
        name="moe_ffn",
    )(*items, xs, w_gate, w_up, w_down)


def _work_items(counts, n_rows):
    n_tiles = n_rows // FFN_TILE
    n_items = n_tiles + N_EXPERTS - 1
    off = jnp.cumsum(counts) - counts
    first_tile = off // FFN_TILE
    last_tile = jnp.maximum(off + counts - 1, off) // FFN_TILE
    n_e = jnp.where(counts > 0, last_tile - first_tile + 1, 0)
    start = jnp.cumsum(n_e) - n_e
    total = jnp.sum(n_e)
    w = jnp.minimum(jnp.arange(n_items, dtype=I32), total - 1)
    ids = jnp.arange(N_EXPERTS, dtype=I32)[None, :]
    e = jnp.max(jnp.where((start[None, :] <= w[:, None]) & (n_e[None, :] > 0), ids, 0), axis=1)
    pick = lambda a: jnp.sum(jnp.where(ids == e[:, None], a[None, :], 0), axis=1)
    tile = pick(first_tile) + (w - pick(start))
    lo = jnp.clip(pick(off) - tile * FFN_TILE, 0, FFN_TILE)
    hi = jnp.clip(pick(off + counts) - tile * FFN_TILE, 0, FFN_TILE)
    hi = jnp.where(jnp.arange(n_items) < total, hi, lo)
    used = n_e[None, :] > 0
    nxt = jnp.min(jnp.where(used & (ids > e[:, None]), ids, N_EXPERTS), axis=1)
    nxt = jnp.where(nxt == N_EXPERTS, e, nxt)
    slot = jnp.sum(jnp.where(used & (ids < e[:, None]), 1, 0), axis=1) % 2
    return tuple(a.astype(I32) for a in (tile, e, lo, hi, nxt, slot))


def _run_table(n):
    tiles = n.shape[0]
    counts = jnp.sum(n, axis=0)
    goff = jnp.cumsum(counts) - counts
    dst0 = goff[None, :] + jnp.cumsum(n, axis=0) - n
    src0 = jnp.cumsum(n, axis=1) - n
    bits = jnp.arange(RUN_BITS, dtype=I32)[:, None, None]
    flag = (n[None] >> bits) & 1
    done = (n[None] >> (bits + 1)) << (bits + 1)
    entry = ((dst0[None] + done) << SRC_BITS) | (src0[None] + done)
    rank = jnp.cumsum(flag, axis=2) - flag
    slot = jnp.arange(N_EXPERTS, dtype=I32)
    sel = (flag[..., None] == 1) & (rank[..., None] == slot)
    packed = jnp.sum(jnp.where(sel, entry[..., None], 0), axis=2)
    m = jnp.sum(flag, axis=2).T
    tab = jnp.zeros((tiles, TAB_COLS), I32)
    tab = tab.at[:, :RUN_BITS].set(m)
    tab = tab.at[:, N_EXPERTS:N_EXPERTS * (RUN_BITS + 1)].set(packed.transpose(1, 0, 2).reshape(tiles, -1))
    return counts, tab


def _combine_kernel(g_tiles, n, tab_ref, h_ref, p_ref, ri_ref, rw_ref, ys_ref, gple_ref, wpg_ref, wpp_ref, gfin_ref,
                    o_ref, idx_smem, rows_scr, isem, rsem):
    i = pl.program_id(0)
    slot = i % 2

    def idx_copy(step, sl):
        return pltpu.make_async_copy(tab_ref.at[pl.ds(step * g_tiles, g_tiles)], idx_smem.at[sl], isem.at[sl])

    def gather(sl):
        for g in range(g_tiles):
            _run_copies(lambda col, g=g: idx_smem[sl, g, col], lambda src, dst, rows_, g=g: pltpu.make_async_copy(
                _row_slice(ys_ref, dst, rows_), _row_slice(rows_scr.at[sl], g * PAIR_ROWS + src, rows_), rsem.at[sl]))

    @pl.when(i == 0)
    def _():
        idx_copy(0, 0).start()
        idx_copy(0, 0).wait()
        gather(0)

    if n >= 2:
        @pl.when(i == 0)
        def _():
            idx_copy(1, 1).start()

        @pl.when(i + 1 < n)
        def _():
            idx_copy(i + 1, 1 - slot).wait()
            gather(1 - slot)

    @pl.when(i + 2 < n)
    def _():
        idx_copy(i + 2, slot).start()

    pltpu.make_async_copy(_row_slice(ys_ref, 0, g_tiles * PAIR_ROWS), rows_scr.at[slot], rsem.at[slot]).wait()

    ri, rw = ri_ref[...], rw_ref[...]
    rows = _pair_rows_iota()
    tiles = range(g_tiles)
    tok = lambda g: slice(g * TOK_TILE, (g + 1) * TOK_TILE)
    h = []
    for g in tiles:
        r1, r2 = ri[2:3, tok(g)], ri[3:4, tok(g)]
        wperm = jnp.where(rows == r1, rw[0:1, tok(g)], 0.0) + jnp.where(rows == r2, rw[1:2, tok(g)], 0.0)
        srows = _load_rows(rows_scr.at[slot, pl.ds(g * PAIR_ROWS * ROW_CHUNKS, PAIR_ROWS * ROW_CHUNKS)], PAIR_ROWS)
        h.append(h_ref[tok(g), :] + _dot_tn(wperm.astype(BF16), srows.astype(BF16)))
    gate = [jax.nn.sigmoid(_dot(_rms(h[g], gple_ref[...]).astype(BF16), wpg_ref[...])) for g in tiles]
    for g in tiles:
        hg = h[g] + _dot(p_ref[tok(g), :].astype(BF16), wpp_ref[...]) * gate[g]
        o_ref[tok(g), :] = _rms(hg, gfin_ref[...])


def _combine(tab, h, p, ri, rw, ys, gple, wpg, wpp, gfin):
    n_tiles = tab.shape[0]
    g_tiles = STEP_TILES
    n = n_tiles // g_tiles
    tt = g_tiles * TOK_TILE
    return pl.pallas_call(
        functools.partial(_combine_kernel, g_tiles, n),
        grid=(n,),
        in_specs=[
            pl.BlockSpec(memory_space=pl.ANY),
            pl.BlockSpec((tt, D_MODEL), lambda i: (i, 0)),
            pl.BlockSpec((tt, PLE_DIM), lambda i: (i, 0)),
            pl.BlockSpec((SUBLANES, tt), lambda i: (0, i)),
            pl.BlockSpec((SUBLANES, tt), lambda i: (0, i)),
            pl.BlockSpec(memory_space=pl.ANY),
            _const_spec(gple.shape), _const_spec(wpg.shape), _const_spec(wpp.shape), _const_spec(gfin.shape),
        ],
        out_specs=pl.BlockSpec((tt, D_MODEL), lambda i: (i, 0)),
        out_shape=jax.ShapeDtypeStruct((n_tiles * TOK_TILE, D_MODEL), F32),
        scratch_shapes=[
            pltpu.SMEM((2, g_tiles, TAB_COLS), I32),
            pltpu.VMEM((2, g_tiles * PAIR_ROWS * ROW_CHUNKS, LANES), U32),
            pltpu.SemaphoreType.DMA((2,)),
            pltpu.SemaphoreType.DMA((2,)),
        ],
        compiler_params=_params("arbitrary"),
        name="moe_combine",
    )(tab, h, p, ri, rw, ys, gple, wpg, wpp, gfin)


def _log_gamma():
    return np.log(1.0 - 2.0 ** (-5.0 - np.arange(RET_HEADS, dtype=np.float64)))


def _rope_tables(pos):
    inv = 1.0 / (ROPE_BASE ** (np.arange(0, RET_DK, 2, dtype=np.float64) / RET_DK))
    ang = np.asarray(pos, np.float64)[:, None] * inv[None, :]
    cos, sin = np.cos(ang), np.sin(ang)
    return (jnp.asarray(np.concatenate([cos, cos], axis=-1), F32),
            jnp.asarray(np.concatenate([-sin, sin], axis=-1), F32))


def _decay_tables(c):
    lg = _log_gamma()
    idx = np.arange(c, dtype=np.float64)
    diff = idx[:, None] - idx[None, :]
    dmask = np.where((diff >= 0.0)[None], np.exp(lg[:, None, None] * np.maximum(diff, 0.0)[None]), 0.0)
    q_dec = np.exp(lg[None, :] * (idx[:, None] + 1.0))
    k_dec = np.exp(lg[None, :] * (c - 1.0 - idx[:, None]))
    c_dec = np.exp(lg * c)
    lanes = lambda a: np.repeat(a, RET_DK, axis=1)
    cdec = np.zeros((SUBLANES, LANES))
    cdec[:RET_HEADS] = c_dec[:, None]
    return tuple(jnp.asarray(a, F32) for a in (lanes(q_dec), lanes(k_dec), dmask, cdec))


def _router_params(w_rg, b_rg, w_re, b_re):
    wr = jnp.zeros((ROUTER_ROWS, D_MODEL), F32).at[:N_GROUPS].set(w_rg.T).at[SUBLANES:].set(w_re.T)
    br = jnp.full((ROUTER_ROWS,), NEG_BIG, F32).at[:N_GROUPS].set(b_rg).at[SUBLANES:].set(b_re.reshape(-1))
    return wr.astype(BF16), jnp.broadcast_to(br[:, None], (ROUTER_ROWS, LANES))


def _strict_upper(t):
    i = np.arange(t)
    return jnp.asarray(i[:, None] < i[None, :], BF16)


def _pad_rows(a, rows):
    return jnp.zeros((rows,) + a.shape[1:], a.dtype).at[:a.shape[0]].set(a)


def kernel(x_prompt, x_sample, state_conv, state_ret, p_prompt, p_sample, g_mix, w_in, conv_w, ret_gn, w_o, g_ffn,
           w_router_group, b_router_group, w_router_expert, b_router_expert, w_gate, w_up, w_down, g_ple,
           w_ple_proj, w_ple_gate, g_final):
    bp, tp, _ = x_prompt.shape
    bs = x_sample.shape[0]
    assert x_sample.shape[1] == 1 and g_mix.shape[0] == 1, "one layer, one new token per sample request"
    assert bs < SAMPLE_ROWS and tp % MIX_TILE == 0
    n_p = bp * tp

    tabs = _rope_tables(np.arange(tp)) + _decay_tables(TOK_TILE)
    wr, br = _router_params(w_router_group[0], b_router_group[0], w_router_expert[0], b_router_expert[0])
    win, wo, convw = w_in[0].astype(BF16), w_o[0].astype(BF16), _pad_rows(conv_w[0], SUBLANES)
    tri, ltri = _strict_upper(TOK_TILE), _strict_upper(N_EXPERTS).T

    xs0 = x_sample[:, 0]
    (yc_s, gate_s, conv_s), sample = _sample_inputs(xs0, state_conv[0], state_ret[0], g_mix, win, convw, ret_gn)
    h_p, xn_p, ri_p, rw_p, n_p_tab, conv_p, ret_p, ret_s, o_s = _mixer_prompt(
        x_prompt, tabs, (g_mix, win, convw, ret_gn, wo, g_ffn, wr, br, tri, ltri), sample)
    h_s, xn_s, ri_s, rw_s, n_s_tab = _sample_out(xs0, yc_s, o_s, gate_s, wo, g_ffn, wr, br, tri, ltri)
    h_p, xn_p = h_p.reshape(n_p, D_MODEL), xn_p.reshape(n_p, D_MODEL)

    n_tab = jnp.concatenate([n_p_tab, n_s_tab], axis=0)[:, :, 0].astype(I32)
    counts, tab = _run_table(n_tab)
    ri = jnp.concatenate([ri_p, ri_s], axis=1)
    n_tiles = n_tab.shape[0]

    xs = _dispatch(tab, xn_p, xn_s, ri)
    ys = _expert_ffn(_work_items(counts, n_tiles * PAIR_ROWS), xs, w_gate[0], w_up[0], w_down[0])

    wpg, wpp = w_ple_gate[0].astype(BF16), w_ple_proj[0].astype(BF16)
    gfin = g_final[None, :]
    p_s = _pad_rows(p_sample[0].reshape(bs, PLE_DIM), SAMPLE_ROWS)
    n_s_tiles = SAMPLE_ROWS // TOK_TILE
    y_p = _combine(tab[:n_tiles - n_s_tiles], h_p, p_prompt[0].reshape(n_p, PLE_DIM), ri_p, rw_p, ys, g_ple, wpg, wpp, gfin)
    y_s = _combine(tab[n_tiles - n_s_tiles:], h_s, p_s, ri_s, rw_s, ys, g_ple, wpg, wpp, gfin)
    return (y_p.reshape(bp, tp, D_MODEL), y_s[:bs].reshape(bs, 1, D_MODEL), conv_p[None], ret_p[None],
            conv_s[None], ret_s[None])
```

```python
import functools

import jax
import jax.numpy as jnp
import numpy as np
from jax import lax
from jax.experimental import pallas as pl
from jax.experimental.pallas import tpu as pltpu

F32, BF16, I32, U32 = jnp.float32, jnp.bfloat16, jnp.int32, jnp.uint32

D_MODEL = 1024
CONV_DIM = 512
CONV_WIDTH = 3
RET_DIM = 512
RET_HEADS = 4
RET_DK = 128
RET_DV = 128
ROPE_BASE = 10000.0
N_GROUPS = 4
EXPERTS_PER_GROUP = 8
N_EXPERTS = 32
D_FF = 512
PLE_DIM = 256
EPS = 1e-6
PAST_LEN = 16384

LANES = 128
SUBLANES = 8
ROW_CHUNKS = D_MODEL // (2 * LANES)
ROUTER_ROWS = SUBLANES + N_EXPERTS
VMEM_LIMIT = 56 * 1024 * 1024
NEG_BIG = -1e30

TOK_TILE = 256
MIX_TILE = 1024
PAIR_ROWS = 2 * TOK_TILE
STEP_TILES = 2
SAMPLE_ROWS = STEP_TILES * TOK_TILE
FFN_TILE = 512
FFN_RING = 3
SAMPLE_BLOCK = 8
RUN_BITS = 9
SRC_BITS = 10
RUN_UNROLL = 4
TAB_COLS = 384


def _rms(x, g):
    return x * lax.rsqrt(jnp.mean(x * x, axis=-1, keepdims=True) + EPS) * g


def _dot(a, b):
    return jnp.dot(a, b, preferred_element_type=F32)


def _dot_nt(a, b):
    return lax.dot_general(a, b, (((1,), (1,)), ((), ())), preferred_element_type=F32)


def _dot_tn(a, b):
    return lax.dot_general(a, b, (((0,), (0,)), ((), ())), preferred_element_type=F32)


def _rope(x, cos, sin_signed):
    return x * cos + pltpu.roll(x, RET_DK // 2, 1) * sin_signed


def _const_spec(shape):
    nd = len(shape)
    return pl.BlockSpec(shape, lambda *_: (0,) * nd)


def _params(*sem):
    return pltpu.CompilerParams(dimension_semantics=sem, vmem_limit_bytes=VMEM_LIMIT)


def _route(h, gffn, wr, br, tri, ltri):
    t = h.shape[0]
    xn = _rms(h, gffn).astype(BF16)
    lt = _dot_nt(wr, xn) + br[:, 0:1]
    row8 = lax.broadcasted_iota(I32, (SUBLANES, t), 0).astype(F32)
    gl = lt[0:SUBLANES]
    m = jnp.max(gl, axis=0, keepdims=True)
    g_top = 1.0 / jnp.sum(jnp.exp(gl - m), axis=0, keepdims=True)
    gidx = jnp.min(jnp.where(gl == m, row8, float(SUBLANES)), axis=0, keepdims=True)
    e_sel = jnp.where(gidx == 0.0, lt[8:16],
                      jnp.where(gidx == 1.0, lt[16:24], jnp.where(gidx == 2.0, lt[24:32], lt[32:40])))
    m1 = jnp.max(e_sel, axis=0, keepdims=True)
    i1 = jnp.min(jnp.where(e_sel == m1, row8, float(SUBLANES)), axis=0, keepdims=True)
    rest = jnp.where(row8 == i1, -jnp.inf, e_sel)
    m2 = jnp.max(rest, axis=0, keepdims=True)
    i2 = jnp.min(jnp.where(rest == m2, row8, float(SUBLANES)), axis=0, keepdims=True)
    d = jnp.exp(m2 - m1)
    w1 = g_top / (1.0 + d)
    w2 = g_top * d / (1.0 + d)
    e1 = gidx * float(EXPERTS_PER_GROUP) + i1
    e2 = gidx * float(EXPERTS_PER_GROUP) + i2
    row32 = lax.broadcasted_iota(I32, (N_EXPERTS, t), 0).astype(F32)
    a1 = jnp.where(row32 == e1, 1.0, 0.0)
    a2 = jnp.where(row32 == e2, 1.0, 0.0)
    a = a1 + a2
    n = jnp.broadcast_to(jnp.sum(a, axis=1, keepdims=True), (N_EXPERTS, LANES))
    start = _dot(ltri, n.astype(BF16))
    base = _dot(a.astype(BF16), tri) + start[:, 0:1]
    r1 = jnp.sum(a1 * base, axis=0, keepdims=True)
    r2 = jnp.sum(a2 * base, axis=0, keepdims=True)
    ri = jnp.where(row8 == 0.0, e1, jnp.where(row8 == 1.0, e2, jnp.where(row8 == 2.0, r1,
                                                                         jnp.where(row8 == 3.0, r2, 0.0))))
    rw = jnp.where(row8 == 0.0, w1, jnp.where(row8 == 1.0, w2, 0.0))
    return ri.astype(I32), rw, n, xn


def _group_norm_gate(o, gate):
    mu = jnp.mean(o, axis=-1, keepdims=True)
    oc = o - mu
    return oc * lax.rsqrt(jnp.mean(oc * oc, axis=-1, keepdims=True) + EPS) * gate


def _mixer_prompt_kernel(x_ref, cos_ref, sin_ref, qdec_ref, kdec_ref, dmask_ref, cdec_ref, gmix_ref, win_ref,
                         convw_ref, retgn_ref, wo_ref, gffn_ref, wr_ref, br_ref, tri_ref, ltri_ref,
                         ss_ref, sq_ref, sk_ref, skt_ref, sv_ref, sdec_ref,
                         h_ref, xn_ref, ri_ref, rw_ref, n_ref, conv_ref, ret_ref, snew_ref, so_ref,
                         s_scr, z_scr, mix_scr):
    j = pl.program_id(1)
    n_j = pl.num_programs(1)
    tt = x_ref.shape[1]

    @pl.when(j == 0)
    def _():
        s_scr[...] = jnp.zeros_like(s_scr)
        z_scr[0:SUBLANES, :] = jnp.zeros((SUBLANES, CONV_DIM), F32)

    cw = convw_ref[...]
    retgn = retgn_ref[...]
    q0 = 3 * CONV_DIM

    def project(p0):
        x = x_ref[0, p0:p0 + TOK_TILE, :]
        proj = _dot(_rms(x, gmix_ref[...]).astype(BF16), win_ref[...])
        gb = proj[:, 0:CONV_DIM]
        z = proj[:, CONV_DIM:2 * CONV_DIM] * proj[:, 2 * CONV_DIM:3 * CONV_DIM]
        z0 = SUBLANES + p0
        z_scr[z0:z0 + TOK_TILE, :] = z
        yc = cw[0:1] * z_scr[z0 - 2:z0 - 2 + TOK_TILE, :] + cw[1:2] * z_scr[z0 - 1:z0 - 1 + TOK_TILE, :] + cw[2:3] * z
        mix_scr[p0:p0 + TOK_TILE, 0:CONV_DIM] = (gb * yc).astype(BF16)
        return x, proj

    def mix(p0, x, proj):
        cos = cos_ref[p0:p0 + TOK_TILE, :]
        sin = sin_ref[p0:p0 + TOK_TILE, :]
        heads = range(RET_HEADS)
        lane = lambda hh: slice(hh * RET_DK, (hh + 1) * RET_DK)
        qr = [_rope(proj[:, q0 + hh * RET_DK:q0 + (hh + 1) * RET_DK], cos, sin) for hh in heads]
        kr = [_rope(proj[:, q0 + RET_DIM + hh * RET_DK:q0 + RET_DIM + (hh + 1) * RET_DK], cos, sin) * (RET_DK ** -0.5)
              for hh in heads]
        v = [proj[:, q0 + 2 * RET_DIM + hh * RET_DV:q0 + 2 * RET_DIM + (hh + 1) * RET_DV].astype(BF16) for hh in heads]
        s_old = [s_scr[hh] for hh in heads]
        scores = [_dot_nt(qr[hh].astype(BF16), kr[hh].astype(BF16)) * dmask_ref[hh] for hh in heads]
        o = []
        for hh in heads:
            lhs = jnp.concatenate([scores[hh].astype(BF16), (qr[hh] * qdec_ref[:, lane(hh)]).astype(BF16)], axis=1)
            o.append(_dot(lhs, jnp.concatenate([v[hh], s_old[hh].astype(BF16)], axis=0)))
        for hh in heads:
            kd = (kr[hh] * kdec_ref[:, lane(hh)]).T.astype(BF16)
            s_scr[hh] = s_old[hh] * cdec_ref[hh:hh + 1, :] + _dot(kd, v[hh])
        for hh in heads:
            g = proj[:, q0 + 3 * RET_DIM + hh * RET_DV:q0 + 3 * RET_DIM + (hh + 1) * RET_DV]
            gate = retgn[:, lane(hh)] * (g * jax.nn.sigmoid(g))
            mix_scr[p0:p0 + TOK_TILE, CONV_DIM + hh * RET_DV:CONV_DIM + (hh + 1) * RET_DV] = (
                _group_norm_gate(o[hh], gate).astype(BF16))
        h = x + _dot(mix_scr[p0:p0 + TOK_TILE, :], wo_ref[...])
        h_ref[0, p0:p0 + TOK_TILE, :] = h
        return h

    def route(part, h):
        p0 = part * TOK_TILE
        ri, rw, n, xn = _route(h, gffn_ref[...], wr_ref[...], br_ref[...], tri_ref[...], ltri_ref[...])
        ri_ref[:, p0:p0 + TOK_TILE] = ri
        rw_ref[:, p0:p0 + TOK_TILE] = rw
        n_ref[part] = n
        xn_ref[0, p0:p0 + TOK_TILE, :] = xn

    parts = tt // TOK_TILE
    nxt = project(0)
    per_part = SAMPLE_BLOCK // parts
    for part in range(parts):
        _sample_state_update(ss_ref, sq_ref, sk_ref, skt_ref, sv_ref, sdec_ref, snew_ref, so_ref,
                             range(part * per_part, (part + 1) * per_part))
        h = mix(part * TOK_TILE, *nxt)
        if part + 1 < parts:
            nxt = project((part + 1) * TOK_TILE)
        route(part, h)

    @pl.when(j == n_j - 1)
    def _():
        conv_ref[0] = z_scr[SUBLANES + tt - 2:SUBLANES + tt, :]
        ret_ref[0] = s_scr[...]

    z_scr[0:SUBLANES, :] = z_scr[tt:tt + SUBLANES, :]


def _mixer_prompt(x, tabs, wts, sample):
    bsz, t, _ = x.shape
    tt = MIX_TILE
    n_j = t // tt
    parts = tt // TOK_TILE
    n_tok = bsz * t
    nb = SAMPLE_BLOCK
    s_state, s_q, s_k, s_kt, s_v, s_dec = sample
    assert s_state.shape[0] == nb * bsz * n_j, "one block of sample requests per mixer grid step"
    args = (x,) + tuple(tabs) + tuple(wts) + tuple(sample)
    step = lambda b, j: b * n_j + j
    in_specs = [
        pl.BlockSpec((1, tt, D_MODEL), lambda b, j: (b, j, 0)),
        pl.BlockSpec((tt, LANES), lambda b, j: (j, 0)),
        pl.BlockSpec((tt, LANES), lambda b, j: (j, 0)),
    ] + [_const_spec(a.shape) for a in args[3:3 + len(tabs) - 2 + len(wts)]] + [
        pl.BlockSpec((nb, RET_HEADS, RET_DK, RET_DV), lambda b, j: (step(b, j), 0, 0, 0)),
        pl.BlockSpec((nb, RET_DIM), lambda b, j: (step(b, j), 0)),
        pl.BlockSpec((nb, RET_DIM), lambda b, j: (step(b, j), 0)),
        pl.BlockSpec((1, RET_HEADS, RET_DK, nb), lambda b, j: (step(b, j), 0, 0, 0)),
        pl.BlockSpec((nb, RET_DIM), lambda b, j: (step(b, j), 0)),
        _const_spec(s_dec.shape),
    ]
    out_shape = (
        jax.ShapeDtypeStruct((bsz, t, D_MODEL), F32),
        jax.ShapeDtypeStruct((bsz, t, D_MODEL), BF16),
        jax.ShapeDtypeStruct((SUBLANES, n_tok), I32),
        jax.ShapeDtypeStruct((SUBLANES, n_tok), F32),
        jax.ShapeDtypeStruct((bsz * n_j * parts, N_EXPERTS, LANES), F32),
        jax.ShapeDtypeStruct((bsz, CONV_WIDTH - 1, CONV_DIM), F32),
        jax.ShapeDtypeStruct((bsz, RET_HEADS, RET_DK, RET_DV), F32),
        jax.ShapeDtypeStruct(s_state.shape, F32),
        jax.ShapeDtypeStruct(s_q.shape, F32),
    )
    out_specs = (
        pl.BlockSpec((1, tt, D_MODEL), lambda b, j: (b, j, 0)),
        pl.BlockSpec((1, tt, D_MODEL), lambda b, j: (b, j, 0)),
        pl.BlockSpec((SUBLANES, tt), lambda b, j: (0, b * n_j + j)),
        pl.BlockSpec((SUBLANES, tt), lambda b, j: (0, b * n_j + j)),
        pl.BlockSpec((parts, N_EXPERTS, LANES), lambda b, j: (b * n_j + j, 0, 0)),
        pl.BlockSpec((1, CONV_WIDTH - 1, CONV_DIM), lambda b, j: (b, 0, 0)),
        pl.BlockSpec((1, RET_HEADS, RET_DK, RET_DV), lambda b, j: (b, 0, 0, 0)),
        pl.BlockSpec((nb, RET_HEADS, RET_DK, RET_DV), lambda b, j: (step(b, j), 0, 0, 0)),
        pl.BlockSpec((nb, RET_DIM), lambda b, j: (step(b, j), 0)),
    )
    return pl.pallas_call(
        _mixer_prompt_kernel,
        grid=(bsz, n_j),
        in_specs=in_specs,
        out_specs=out_specs,
        out_shape=out_shape,
        scratch_shapes=[
            pltpu.VMEM((RET_HEADS, RET_DK, RET_DV), F32),
            pltpu.VMEM((tt + SUBLANES, CONV_DIM), F32),
            pltpu.VMEM((tt, D_MODEL), BF16),
        ],
        compiler_params=_params("arbitrary", "arbitrary"),
        name="mixer_prompt",
    )(*args)


def _sample_proj_kernel(x_ref, sc_ref, cos_ref, sin_ref, gmix_ref, win_ref, convw_ref, retgn_ref,
                        yc_ref, q_ref, k_ref, v_ref, gate_ref, conv_ref):
    x = x_ref[...]
    u = _rms(x, gmix_ref[...]).astype(BF16)
    proj = _dot(u, win_ref[...])
    gb = proj[:, 0:CONV_DIM]
    z = proj[:, CONV_DIM:2 * CONV_DIM] * proj[:, 2 * CONV_DIM:3 * CONV_DIM]
    cw = convw_ref[...]
    buf0, buf1 = sc_ref[:, 0:CONV_DIM], sc_ref[:, CONV_DIM:2 * CONV_DIM]
    yc_ref[...] = gb * (cw[0:1] * buf0 + cw[1:2] * buf1 + cw[2:3] * z)
    conv_ref[:, 0:CONV_DIM] = buf1
    conv_ref[:, CONV_DIM:2 * CONV_DIM] = z
    q0 = 3 * CONV_DIM
    cos, sin = cos_ref[0:1, :], sin_ref[0:1, :]
    retgn = retgn_ref[...]
    for hh in range(RET_HEADS):
        l0 = hh * RET_DK
        q_ref[:, l0:l0 + RET_DK] = _rope(proj[:, q0 + l0:q0 + l0 + RET_DK], cos, sin)
        k_ref[:, l0:l0 + RET_DK] = _rope(proj[:, q0 + RET_DIM + l0:q0 + RET_DIM + l0 + RET_DK], cos, sin) * (RET_DK ** -0.5)
    v_ref[...] = proj[:, q0 + 2 * RET_DIM:q0 + 3 * RET_DIM]
    g = proj[:, q0 + 3 * RET_DIM:q0 + 4 * RET_DIM]
    gate_ref[...] = retgn * (g * jax.nn.sigmoid(g))


def _sample_proj(x, sc, cos, sin, gmix, win, convw, retgn):
    n = x.shape[0]
    outs = (
        jax.ShapeDtypeStruct((n, CONV_DIM), F32), jax.ShapeDtypeStruct((n, RET_DIM), F32),
        jax.ShapeDtypeStruct((n, RET_DIM), F32), jax.ShapeDtypeStruct((n, RET_DIM), F32),
        jax.ShapeDtypeStruct((n, RET_DIM), F32), jax.ShapeDtypeStruct((n, 2 * CONV_DIM), F32),
    )
    args = (x, sc, cos, sin, gmix, win, convw, retgn)
    return pl.pallas_call(
        _sample_proj_kernel,
        grid=(1,),
        in_specs=[_const_spec(a.shape) for a in args],
        out_specs=tuple(_const_spec(o.shape) for o in outs),
        out_shape=outs,
        compiler_params=_params("arbitrary"),
        name="sample_proj",
    )(*args)


def _sample_state_update(s_ref, q_ref, k_ref, kt_ref, v_ref, dec_ref, snew_ref, o_ref, requests):
    for r in requests:
        for hh in range(RET_HEADS):
            l0 = hh * RET_DV
            s = s_ref[r, hh]
            q = q_ref[r:r + 1, l0:l0 + RET_DK]
            k = k_ref[r:r + 1, l0:l0 + RET_DK]
            kc = kt_ref[0, hh, :, r:r + 1]
            vr = v_ref[r:r + 1, l0:l0 + RET_DV]
            qdec = dec_ref[hh:hh + 1, :]
            cdec = dec_ref[RET_HEADS + hh:RET_HEADS + hh + 1, :]
            qk = jnp.sum(q * k, axis=1, keepdims=True)
            o_ref[r:r + 1, l0:l0 + RET_DV] = qk * vr + _dot((q * qdec).astype(BF16), s.astype(BF16))
            snew_ref[r, hh] = s * cdec + kc * vr


def _sample_out_kernel(x_ref, yc_ref, o_ref, gate_ref, wo_ref, gffn_ref, wr_ref, br_ref, tri_ref, ltri_ref,
                       h_ref, xn_ref, ri_ref, rw_ref, n_ref):
    n = x_ref.shape[0]
    parts = [yc_ref[...].astype(BF16)]
    for hh in range(RET_HEADS):
        l0 = hh * RET_DV
        parts.append(_group_norm_gate(o_ref[:, l0:l0 + RET_DV], gate_ref[:, l0:l0 + RET_DV]).astype(BF16))
    h_ref[0:n, :] = x_ref[...] + _dot(jnp.concatenate(parts, axis=-1), wo_ref[...])
    h_ref[n:SAMPLE_ROWS, :] = jnp.zeros((SAMPLE_ROWS - n, D_MODEL), F32)
    for part in range(SAMPLE_ROWS // TOK_TILE):
        p0 = part * TOK_TILE
        ri, rw, cnt, xn = _route(h_ref[p0:p0 + TOK_TILE, :], gffn_ref[...], wr_ref[...], br_ref[...], tri_ref[...],
                                 ltri_ref[...])
        ri_ref[:, p0:p0 + TOK_TILE] = ri
        rw_ref[:, p0:p0 + TOK_TILE] = rw
        n_ref[part] = cnt
        xn_ref[p0:p0 + TOK_TILE, :] = xn


def _sample_out(x, yc, o, gate, wo, gffn, wr, br, tri, ltri):
    outs = (
        jax.ShapeDtypeStruct((SAMPLE_ROWS, D_MODEL), F32), jax.ShapeDtypeStruct((SAMPLE_ROWS, D_MODEL), BF16),
        jax.ShapeDtypeStruct((SUBLANES, SAMPLE_ROWS), I32), jax.ShapeDtypeStruct((SUBLANES, SAMPLE_ROWS), F32),
        jax.ShapeDtypeStruct((SAMPLE_ROWS // TOK_TILE, N_EXPERTS, LANES), F32),
    )
    args = (x, yc, o, gate, wo, gffn, wr, br, tri, ltri)
    return pl.pallas_call(
        _sample_out_kernel,
        grid=(1,),
        in_specs=[_const_spec(a.shape) for a in args],
        out_specs=tuple(_const_spec(o_.shape) for o_ in outs),
        out_shape=outs,
        compiler_params=_params("arbitrary"),
        name="sample_out",
    )(*args)


def _sample_inputs(x, state_conv, state_ret, gmix, win, convw, retgn):
    n = x.shape[0]
    lg = _log_gamma()
    cos, sin = _rope_tables(np.full((SUBLANES,), PAST_LEN))
    yc, q, k, v, gate, conv_new = _sample_proj(x, state_conv.reshape(n, 2 * CONV_DIM), cos, sin, gmix, win, convw, retgn)
    nb = SAMPLE_BLOCK
    cols = lambda a: a.reshape(n // nb, nb, RET_HEADS, RET_DK).transpose(0, 2, 3, 1)
    step = np.exp(lg[:, None] * 1.0)
    dec = jnp.asarray(np.broadcast_to(np.concatenate([step, step], axis=0), (2 * RET_HEADS, LANES)), F32)
    return (yc, gate, conv_new.reshape(n, 2, CONV_DIM)), (state_ret, q, k, cols(k), v, dec)


def _load_rows(ref, n_rows):
    words = [ref[pl.ds(c, n_rows, stride=ROW_CHUNKS), :] for c in range(ROW_CHUNKS)]
    half = lambda i: [pltpu.unpack_elementwise(w, index=i, packed_dtype=BF16, unpacked_dtype=F32) for w in words]
    return jnp.concatenate(half(0) + half(1), axis=-1)


def _store_rows(ref, val):
    half = D_MODEL // 2
    for c in range(ROW_CHUNKS):
        pair = [val[:, c * LANES:(c + 1) * LANES], val[:, half + c * LANES:half + (c + 1) * LANES]]
        ref[pl.ds(c, val.shape[0], stride=ROW_CHUNKS), :] = pltpu.pack_elementwise(pair, packed_dtype=BF16)


def _row_slice(ref, row, rows):
    return ref.at[pl.ds(pl.multiple_of(row * ROW_CHUNKS, ROW_CHUNKS), rows * ROW_CHUNKS)]


def _run_copies(table, make_copy):
    for b in range(RUN_BITS):
        first = N_EXPERTS + N_EXPERTS * b
        count = table(b)

        def start(col, b=b):
            entry = table(col)
            make_copy(entry & ((1 << SRC_BITS) - 1), entry >> SRC_BITS, 1 << b).start()

        for r in range(RUN_UNROLL - 1):
            @pl.when(r < count % RUN_UNROLL)
            def _():
                start(first + count - 1 - r)

        def body(j, carry, first=first, start=start):
            for u in range(RUN_UNROLL):
                start(first + RUN_UNROLL * j + u)
            return carry

        lax.fori_loop(0, count // RUN_UNROLL, body, 0)


def _pair_rows_iota():
    return lax.broadcasted_iota(I32, (PAIR_ROWS, TOK_TILE), 0)


def _dispatch_kernel(n_p, tab_ref, xp_ref, xs_in_ref, ri_ref, xs_ref, idx_smem, y_scr, isem, rsem):
    i, n = pl.program_id(0), pl.num_programs(0)
    slot = i % 2
    step_rows = STEP_TILES * PAIR_ROWS

    def idx_copy(step, sl):
        return pltpu.make_async_copy(tab_ref.at[pl.ds(step * STEP_TILES, STEP_TILES)], idx_smem.at[sl], isem.at[sl])

    def wait_step(sl):
        pltpu.make_async_copy(_row_slice(xs_ref, 0, step_rows), y_scr.at[sl], rsem.at[sl]).wait()

    @pl.when(i == 0)
    def _():
        idx_copy(0, 0).start()

    @pl.when(i + 1 < n)
    def _():
        idx_copy(i + 1, 1 - slot).start()

    @pl.when(i >= 2)
    def _():
        wait_step(slot)

    xn = jnp.where(i < n_p, xp_ref[...], xs_in_ref[...])
    ri = ri_ref[...]
    rows = _pair_rows_iota()
    for g in range(STEP_TILES):
        tok = slice(g * TOK_TILE, (g + 1) * TOK_TILE)
        perm = jnp.where((rows == ri[2:3, tok]) | (rows == ri[3:4, tok]), 1.0, 0.0).astype(BF16)
        y = _dot(perm, xn[tok, :])
        _store_rows(y_scr.at[slot, pl.ds(g * PAIR_ROWS * ROW_CHUNKS, PAIR_ROWS * ROW_CHUNKS)], y)

    idx_copy(i, slot).wait()
    for g in range(STEP_TILES):
        _run_copies(lambda col, g=g: idx_smem[slot, g, col], lambda src, dst, rows_, g=g: pltpu.make_async_copy(
            _row_slice(y_scr.at[slot], g * PAIR_ROWS + src, rows_), _row_slice(xs_ref, dst, rows_), rsem.at[slot]))

    @pl.when(i == n - 1)
    def _():
        wait_step(slot)

    @pl.when((i == n - 1) & (n >= 2))
    def _():
        wait_step(1 - slot)


def _dispatch(tab, xn_p, xn_s, ri):
    tt = STEP_TILES * TOK_TILE
    n_p = xn_p.shape[0] // tt
    n = tab.shape[0] // STEP_TILES
    return pl.pallas_call(
        functools.partial(_dispatch_kernel, n_p),
        grid=(n,),
        in_specs=[
            pl.BlockSpec(memory_space=pl.ANY),
            pl.BlockSpec((tt, D_MODEL), lambda i: (jnp.minimum(i, n_p - 1), 0)),
            pl.BlockSpec((tt, D_MODEL), lambda i: (jnp.maximum(i - n_p, 0), 0)),
            pl.BlockSpec((SUBLANES, tt), lambda i: (0, i)),
        ],
        out_specs=pl.BlockSpec(memory_space=pl.ANY),
        out_shape=jax.ShapeDtypeStruct((tab.shape[0] * PAIR_ROWS * ROW_CHUNKS, LANES), U32),
        scratch_shapes=[
            pltpu.SMEM((2, STEP_TILES, TAB_COLS), I32),
            pltpu.VMEM((2, STEP_TILES * PAIR_ROWS * ROW_CHUNKS, LANES), U32),
            pltpu.SemaphoreType.DMA((2,)),
            pltpu.SemaphoreType.DMA((2,)),
        ],
        compiler_params=_params("arbitrary"),
        name="moe_dispatch",
    )(tab, xn_p, xn_s, ri)


def _ffn_kernel(tile_ref, exp_ref, lo_ref, hi_ref, nxt_ref, slot_ref, xs_ref, wg_ref, wu_ref, wd_ref, o_ref,
                x_buf, wg_buf, wu_buf, wd_buf, wgu_scr, wd_scr, y_scr, xsem, wsem):
    w, n_w = pl.program_id(0), pl.num_programs(0)
    prev = jnp.maximum(w - 1, 0)
    lo, hi = lo_ref[w], hi_ref[w]
    e, slot = exp_ref[w], slot_ref[w]
    first = (w == 0) | (tile_ref[w] != tile_ref[prev])

    ahead = FFN_RING - 1

    def x_copy(step):
        rows = FFN_TILE * ROW_CHUNKS
        return pltpu.make_async_copy(xs_ref.at[pl.ds(pl.multiple_of(tile_ref[step] * rows, rows), rows)],
                                     x_buf.at[step % FFN_RING], xsem.at[step % FFN_RING])

    for step in range(ahead):
        @pl.when((w == 0) & (step < n_w))
        def _():
            x_copy(step).start()

    @pl.when(w + ahead < n_w)
    def _():
        x_copy(w + ahead).start()

    x_copy(w).wait()

    def weight_copies(expert, sl):
        return [pltpu.make_async_copy(src.at[expert], buf.at[sl], wsem.at[sl])
                for src, buf in ((wg_ref, wg_buf), (wu_ref, wu_buf), (wd_ref, wd_buf))]

    @pl.when(w == 0)
    def _():
        for cp in weight_copies(e, slot):
            cp.start(priority=1)

    @pl.when((w == 0) | (e != exp_ref[prev]))
    def _():
        for cp in weight_copies(e, slot):
            cp.wait()

        @pl.when(nxt_ref[w] != e)
        def _():
            for cp in weight_copies(nxt_ref[w], 1 - slot):
                cp.start(priority=1)

        wgu_scr[:, 0:D_FF] = wg_buf[slot].astype(BF16)
        wgu_scr[:, D_FF:2 * D_FF] = wu_buf[slot].astype(BF16)
        wd_scr[...] = wd_buf[slot].astype(BF16)

    @pl.when(w == 0)
    def _():
        y_scr[...] = jnp.zeros_like(y_scr)

    def compute(r0, n_rows):
        chunk = pl.ds(r0 * ROW_CHUNKS, n_rows * ROW_CHUNKS)
        gu = _dot(_load_rows(x_buf.at[w % FFN_RING, chunk], n_rows).astype(BF16), wgu_scr[...])
        g, u = gu[:, 0:D_FF], gu[:, D_FF:2 * D_FF]
        y = _dot((g * jax.nn.sigmoid(g) * u).astype(BF16), wd_scr[...])
        row = r0 + lax.broadcasted_iota(I32, (n_rows, 1), 0)
        y = jnp.where(((row >= lo) & (row < hi)) | first, y, y_scr[r0:r0 + n_rows, :])
        y_scr[r0:r0 + n_rows, :] = y
        _store_rows(o_ref.at[chunk], y)

    half = FFN_TILE // 2
    top, bottom = hi <= half, lo >= half

    @pl.when((hi > lo) & top)
    def _():
        compute(0, half)

    @pl.when((hi > lo) & bottom)
    def _():
        compute(half, half)

    @pl.when((hi > lo) & jnp.logical_not(top | bottom))
    def _():
        compute(0, FFN_TILE)


def _expert_ffn(items, xs, w_gate, w_up, w_down):
    n_items = items[0].shape[0]
    row_spec = pl.BlockSpec((FFN_TILE * ROW_CHUNKS, LANES), lambda w, t, *_: (t[w], 0))
    grid_spec = pltpu.PrefetchScalarGridSpec(
        num_scalar_prefetch=len(items),
        grid=(n_items,),
        in_specs=[pl.BlockSpec(memory_space=pl.ANY)] * 4,
        out_specs=row_spec,
        scratch_shapes=[
            pltpu.VMEM((FFN_RING, FFN_TILE * ROW_CHUNKS, LANES), U32),
            pltpu.VMEM((2, D_MODEL, D_FF), F32), pltpu.VMEM((2, D_MODEL, D_FF), F32),
            pltpu.VMEM((2, D_FF, D_MODEL), F32),
            pltpu.VMEM((D_MODEL, 2 * D_FF), BF16), pltpu.VMEM((D_FF, D_MODEL), BF16),
            pltpu.VMEM((FFN_TILE, D_MODEL), F32),
            pltpu.SemaphoreType.DMA((FFN_RING,)),
            pltpu.SemaphoreType.DMA((2,)),
        ],
    )
    return pl.pallas_call(
        _ffn_kernel,
        grid_spec=grid_spec,
        out_shape=jax.ShapeDtypeStruct(xs.shape, U32),
        compiler_params=_params("arbitrary"),
        name="moe_ffn",
    )(*items, xs, w_gate, w_up, w_down)


def _work_items(counts, n_rows):
    n_tiles = n_rows // FFN_TILE
    n_items = n_tiles + N_EXPERTS - 1
    off = jnp.cumsum(counts) - counts
    first_tile = off // FFN_TILE
    last_tile = jnp.maximum(off + counts - 1, off) // FFN_TILE
    n_e = jnp.where(counts > 0, last_tile - first_tile + 1, 0)
    start = jnp.cumsum(n_e) - n_e
    total = jnp.sum(n_e)
    w = jnp.minimum(jnp.arange(n_items, dtype=I32), total - 1)
    ids = jnp.arange(N_EXPERTS, dtype=I32)[None, :]
    e = jnp.max(jnp.where((start[None, :] <= w[:, None]) & (n_e[None, :] > 0), ids, 0), axis=1)
    pick = lambda a: jnp.sum(jnp.where(ids == e[:, None], a[None, :], 0), axis=1)
    tile = pick(first_tile) + (w - pick(start))
    lo = jnp.clip(pick(off) - tile * FFN_TILE, 0, FFN_TILE)
    hi = jnp.clip(pick(off + counts) - tile * FFN_TILE, 0, FFN_TILE)
    hi = jnp.where(jnp.arange(n_items) < total, hi, lo)
    used = n_e[None, :] > 0
    nxt = jnp.min(jnp.where(used & (ids > e[:, None]), ids, N_EXPERTS), axis=1)
    nxt = jnp.where(nxt == N_EXPERTS, e, nxt)
    slot = jnp.sum(jnp.where(used & (ids < e[:, None]), 1, 0), axis=1) % 2
    return tuple(a.astype(I32) for a in (tile, e, lo, hi, nxt, slot))


def _run_table(n):
    tiles = n.shape[0]
    counts = jnp.sum(n, axis=0)
    goff = jnp.cumsum(counts) - counts
    dst0 = goff[None, :] + jnp.cumsum(n, axis=0) - n
    src0 = jnp.cumsum(n, axis=1) - n
    bits = jnp.arange(RUN_BITS, dtype=I32)[:, None, None]
    flag = (n[None] >> bits) & 1
    done = (n[None] >> (bits + 1)) << (bits + 1)
    entry = ((dst0[None] + done) << SRC_BITS) | (src0[None] + done)
    rank = jnp.cumsum(flag, axis=2) - flag
    slot = jnp.arange(N_EXPERTS, dtype=I32)
    sel = (flag[..., None] == 1) & (rank[..., None] == slot)
    packed = jnp.sum(jnp.where(sel, entry[..., None], 0), axis=2)
    m = jnp.sum(flag, axis=2).T
    tab = jnp.zeros((tiles, TAB_COLS), I32)
    tab = tab.at[:, :RUN_BITS].set(m)
    tab = tab.at[:, N_EXPERTS:N_EXPERTS * (RUN_BITS + 1)].set(packed.transpose(1, 0, 2).reshape(tiles, -1))
    return counts, tab


def _combine_kernel(g_tiles, n, tab_ref, h_ref, p_ref, ri_ref, rw_ref, ys_ref, gple_ref, wpg_ref, wpp_ref, gfin_ref,
                    o_ref, idx_smem, rows_scr, isem, rsem):
    i = pl.program_id(0)
    slot = i % 2

    def idx_copy(step, sl):
        return pltpu.make_async_copy(tab_ref.at[pl.ds(step * g_tiles, g_tiles)], idx_smem.at[sl], isem.at[sl])

    def gather(sl):
        for g in range(g_tiles):
            _run_copies(lambda col, g=g: idx_smem[sl, g, col], lambda src, dst, rows_, g=g: pltpu.make_async_copy(
                _row_slice(ys_ref, dst, rows_), _row_slice(rows_scr.at[sl], g * PAIR_ROWS + src, rows_), rsem.at[sl]))

    @pl.when(i == 0)
    def _():
        idx_copy(0, 0).start()
        idx_copy(0, 0).wait()
        gather(0)

    if n >= 2:
        @pl.when(i == 0)
        def _():
            idx_copy(1, 1).start()

        @pl.when(i + 1 < n)
        def _():
            idx_copy(i + 1, 1 - slot).wait()
            gather(1 - slot)

    @pl.when(i + 2 < n)
    def _():
        idx_copy(i + 2, slot).start()

    pltpu.make_async_copy(_row_slice(ys_ref, 0, g_tiles * PAIR_ROWS), rows_scr.at[slot], rsem.at[slot]).wait()

    ri, rw = ri_ref[...], rw_ref[...]
    rows = _pair_rows_iota()
    tiles = range(g_tiles)
    tok = lambda g: slice(g * TOK_TILE, (g + 1) * TOK_TILE)
    h = []
    for g in tiles:
        r1, r2 = ri[2:3, tok(g)], ri[3:4, tok(g)]
        wperm = jnp.where(rows == r1, rw[0:1, tok(g)], 0.0) + jnp.where(rows == r2, rw[1:2, tok(g)], 0.0)
        srows = _load_rows(rows_scr.at[slot, pl.ds(g * PAIR_ROWS * ROW_CHUNKS, PAIR_ROWS * ROW_CHUNKS)], PAIR_ROWS)
        h.append(h_ref[tok(g), :] + _dot_tn(wperm.astype(BF16), srows.astype(BF16)))
    gate = [jax.nn.sigmoid(_dot(_rms(h[g], gple_ref[...]).astype(BF16), wpg_ref[...])) for g in tiles]
    for g in tiles:
        hg = h[g] + _dot(p_ref[tok(g), :].astype(BF16), wpp_ref[...]) * gate[g]
        o_ref[tok(g), :] = _rms(hg, gfin_ref[...])


def _combine(tab, h, p, ri, rw, ys, gple, wpg, wpp, gfin):
    n_tiles = tab.shape[0]
    g_tiles = STEP_TILES
    n = n_tiles // g_tiles
    tt = g_tiles * TOK_TILE
    return pl.pallas_call(
        functools.partial(_combine_kernel, g_tiles, n),
        grid=(n,),
        in_specs=[
            pl.BlockSpec(memory_space=pl.ANY),
            pl.BlockSpec((tt, D_MODEL), lambda i: (i, 0)),
            pl.BlockSpec((tt, PLE_DIM), lambda i: (i, 0)),
            pl.BlockSpec((SUBLANES, tt), lambda i: (0, i)),
            pl.BlockSpec((SUBLANES, tt), lambda i: (0, i)),
            pl.BlockSpec(memory_space=pl.ANY),
            _const_spec(gple.shape), _const_spec(wpg.shape), _const_spec(wpp.shape), _const_spec(gfin.shape),
        ],
        out_specs=pl.BlockSpec((tt, D_MODEL), lambda i: (i, 0)),
        out_shape=jax.ShapeDtypeStruct((n_tiles * TOK_TILE, D_MODEL), F32),
        scratch_shapes=[
            pltpu.SMEM((2, g_tiles, TAB_COLS), I32),
            pltpu.VMEM((2, g_tiles * PAIR_ROWS * ROW_CHUNKS, LANES), U32),
            pltpu.SemaphoreType.DMA((2,)),
            pltpu.SemaphoreType.DMA((2,)),
        ],
        compiler_params=_params("arbitrary"),
        name="moe_combine",
    )(tab, h, p, ri, rw, ys, gple, wpg, wpp, gfin)


def _log_gamma():
    return np.log(1.0 - 2.0 ** (-5.0 - np.arange(RET_HEADS, dtype=np.float64)))


def _rope_tables(pos):
    inv = 1.0 / (ROPE_BASE ** (np.arange(0, RET_DK, 2, dtype=np.float64) / RET_DK))
    ang = np.asarray(pos, np.float64)[:, None] * inv[None, :]
    cos, sin = np.cos(ang), np.sin(ang)
    return (jnp.asarray(np.concatenate([cos, cos], axis=-1), F32),
            jnp.asarray(np.concatenate([-sin, sin], axis=-1), F32))


def _decay_tables(c):
    lg = _log_gamma()
    idx = np.arange(c, dtype=np.float64)
    diff = idx[:, None] - idx[None, :]
    dmask = np.where((diff >= 0.0)[None], np.exp(lg[:, None, None] * np.maximum(diff, 0.0)[None]), 0.0)
    q_dec = np.exp(lg[None, :] * (idx[:, None] + 1.0))
    k_dec = np.exp(lg[None, :] * (c - 1.0 - idx[:, None]))
    c_dec = np.exp(lg * c)
    lanes = lambda a: np.repeat(a, RET_DK, axis=1)
    cdec = np.zeros((SUBLANES, LANES))
    cdec[:RET_HEADS] = c_dec[:, None]
    return tuple(jnp.asarray(a, F32) for a in (lanes(q_dec), lanes(k_dec), dmask, cdec))


def _router_params(w_rg, b_rg, w_re, b_re):
    wr = jnp.zeros((ROUTER_ROWS, D_MODEL), F32).at[:N_GROUPS].set(w_rg.T).at[SUBLANES:].set(w_re.T)
    br = jnp.full((ROUTER_ROWS,), NEG_BIG, F32).at[:N_GROUPS].set(b_rg).at[SUBLANES:].set(b_re.reshape(-1))
    return wr.astype(BF16), jnp.broadcast_to(br[:, None], (ROUTER_ROWS, LANES))


def _strict_upper(t):
    i = np.arange(t)
    return jnp.asarray(i[:, None] < i[None, :], BF16)


def _pad_rows(a, rows):
    return jnp.zeros((rows,) + a.shape[1:], a.dtype).at[:a.shape[0]].set(a)


def kernel(x_prompt, x_sample, state_conv, state_ret, p_prompt, p_sample, g_mix, w_in, conv_w, ret_gn, w_o, g_ffn,
           w_router_group, b_router_group, w_router_expert, b_router_expert, w_gate, w_up, w_down, g_ple,
           w_ple_proj, w_ple_gate, g_final):
    bp, tp, _ = x_prompt.shape
    bs = x_sample.shape[0]
    assert x_sample.shape[1] == 1 and g_mix.shape[0] == 1, "one layer, one new token per sample request"
    assert bs < SAMPLE_ROWS and tp % MIX_TILE == 0
    n_p = bp * tp

    tabs = _rope_tables(np.arange(tp)) + _decay_tables(TOK_TILE)
    wr, br = _router_params(w_router_group[0], b_router_group[0], w_router_expert[0], b_router_expert[0])
    win, wo, convw = w_in[0].astype(BF16), w_o[0].astype(BF16), _pad_rows(conv_w[0], SUBLANES)
    tri, ltri = _strict_upper(TOK_TILE), _strict_upper(N_EXPERTS).T

    xs0 = x_sample[:, 0]
    (yc_s, gate_s, conv_s), sample = _sample_inputs(xs0, state_conv[0], state_ret[0], g_mix, win, convw, ret_gn)
    h_p, xn_p, ri_p, rw_p, n_p_tab, conv_p, ret_p, ret_s, o_s = _mixer_prompt(
        x_prompt, tabs, (g_mix, win, convw, ret_gn, wo, g_ffn, wr, br, tri, ltri), sample)
    h_s, xn_s, ri_s, rw_s, n_s_tab = _sample_out(xs0, yc_s, o_s, gate_s, wo, g_ffn, wr, br, tri, ltri)
    h_p, xn_p = h_p.reshape(n_p, D_MODEL), xn_p.reshape(n_p, D_MODEL)

    n_tab = jnp.concatenate([n_p_tab, n_s_tab], axis=0)[:, :, 0].astype(I32)
    counts, tab = _run_table(n_tab)
    ri = jnp.concatenate([ri_p, ri_s], axis=1)
    n_tiles = n_tab.shape[0]

    xs = _dispatch(tab, xn_p, xn_s, ri)
    ys = _expert_ffn(_work_items(counts, n_tiles * PAIR_ROWS), xs, w_gate[0], w_up[0], w_down[0])

    wpg, wpp = w_ple_gate[0].astype(BF16), w_ple_proj[0].astype(BF16)
    gfin = g_final[None, :]
    p_s = _pad_rows(p_sample[0].reshape(bs, PLE_DIM), SAMPLE_ROWS)
    n_s_tiles = SAMPLE_ROWS // TOK_TILE
    y_p = _combine(tab[:n_tiles - n_s_tiles], h_p, p_prompt[0].reshape(n_p, PLE_DIM), ri_p, rw_p, ys, g_ple, wpg, wpp, gfin)
    y_s = _combine(tab[n_tiles - n_s_tiles:], h_s, p_s, ri_s, rw_s, ys, g_ple, wpg, wpp, gfin)
    return (y_p.reshape(bp, tp, D_MODEL), y_s[:bs].reshape(bs, 1, D_MODEL), conv_p[None], ret_p[None],
            conv_s[None], ret_s[None])
```

```python
import functools

import jax
import jax.numpy as jnp
import numpy as np
from jax import lax
from jax.experimental import pallas as pl
from jax.experimental.pallas import tpu as pltpu

F32, BF16, I32, U32 = jnp.float32, jnp.bfloat16, jnp.int32, jnp.uint32

D_MODEL = 1024
CONV_DIM = 512
CONV_WIDTH = 3
RET_DIM = 512
RET_HEADS = 4
RET_DK = 128
RET_DV = 128
ROPE_BASE = 10000.0
N_GROUPS = 4
EXPERTS_PER_GROUP = 8
N_EXPERTS = 32
D_FF = 512
PLE_DIM = 256
EPS = 1e-6
PAST_LEN = 16384

LANES = 128
SUBLANES = 8
ROW_CHUNKS = D_MODEL // (2 * LANES)
ROUTER_ROWS = SUBLANES + N_EXPERTS
VMEM_LIMIT = 56 * 1024 * 1024
NEG_BIG = -1e30

TOK_TILE = 256
MIX_TILE = 1024
PAIR_ROWS = 2 * TOK_TILE
STEP_TILES = 2
SAMPLE_ROWS = STEP_TILES * TOK_TILE
FFN_TILE = 512
FFN_RING = 3
SAMPLE_BLOCK = 8
RUN_BITS = 9
SRC_BITS = 10
RUN_UNROLL = 4
TAB_COLS = 384


def _rms(x, g):
    return x * lax.rsqrt(jnp.mean(x * x, axis=-1, keepdims=True) + EPS) * g


def _dot(a, b):
    return jnp.dot(a, b, preferred_element_type=F32)


def _dot_nt(a, b):
    return lax.dot_general(a, b, (((1,), (1,)), ((), ())), preferred_element_type=F32)


def _dot_tn(a, b):
    return lax.dot_general(a, b, (((0,), (0,)), ((), ())), preferred_element_type=F32)


def _rope(x, cos, sin_signed):
    return x * cos + pltpu.roll(x, RET_DK // 2, 1) * sin_signed


def _const_spec(shape):
    nd = len(shape)
    return pl.BlockSpec(shape, lambda *_: (0,) * nd)


def _params(*sem):
    return pltpu.CompilerParams(dimension_semantics=sem, vmem_limit_bytes=VMEM_LIMIT)


def _route(h, gffn, wr, br, tri, ltri):
    t = h.shape[0]
    xn = _rms(h, gffn).astype(BF16)
    lt = _dot_nt(wr, xn) + br[:, 0:1]
    row8 = lax.broadcasted_iota(I32, (SUBLANES, t), 0).astype(F32)
    gl = lt[0:SUBLANES]
    m = jnp.max(gl, axis=0, keepdims=True)
    g_top = 1.0 / jnp.sum(jnp.exp(gl - m), axis=0, keepdims=True)
    gidx = jnp.min(jnp.where(gl == m, row8, float(SUBLANES)), axis=0, keepdims=True)
    e_sel = jnp.where(gidx == 0.0, lt[8:16],
                      jnp.where(gidx == 1.0, lt[16:24], jnp.where(gidx == 2.0, lt[24:32], lt[32:40])))
    m1 = jnp.max(e_sel, axis=0, keepdims=True)
    i1 = jnp.min(jnp.where(e_sel == m1, row8, float(SUBLANES)), axis=0, keepdims=True)
    rest = jnp.where(row8 == i1, -jnp.inf, e_sel)
    m2 = jnp.max(rest, axis=0, keepdims=True)
    i2 = jnp.min(jnp.where(rest == m2, row8, float(SUBLANES)), axis=0, keepdims=True)
    d = jnp.exp(m2 - m1)
    w1 = g_top / (1.0 + d)
    w2 = g_top * d / (1.0 + d)
    e1 = gidx * float(EXPERTS_PER_GROUP) + i1
    e2 = gidx * float(EXPERTS_PER_GROUP) + i2
    row32 = lax.broadcasted_iota(I32, (N_EXPERTS, t), 0).astype(F32)
    a1 = jnp.where(row32 == e1, 1.0, 0.0)
    a2 = jnp.where(row32 == e2, 1.0, 0.0)
    a = a1 + a2
    n = jnp.broadcast_to(jnp.sum(a, axis=1, keepdims=True), (N_EXPERTS, LANES))
    start = _dot(ltri, n.astype(BF16))
    base = _dot(a.astype(BF16), tri) + start[:, 0:1]
    r1 = jnp.sum(a1 * base, axis=0, keepdims=True)
    r2 = jnp.sum(a2 * base, axis=0, keepdims=True)
    ri = jnp.where(row8 == 0.0, e1, jnp.where(row8 == 1.0, e2, jnp.where(row8 == 2.0, r1,
                                                                         jnp.where(row8 == 3.0, r2, 0.0))))
    rw = jnp.where(row8 == 0.0, w1, jnp.where(row8 == 1.0, w2, 0.0))
    return ri.astype(I32), rw, n, xn


def _group_norm_gate(o, gate):
    mu = jnp.mean(o, axis=-1, keepdims=True)
    oc = o - mu
    return oc * lax.rsqrt(jnp.mean(oc * oc, axis=-1, keepdims=True) + EPS) * gate


def _mixer_prompt_kernel(x_ref, cos_ref, sin_ref, qdec_ref, kdec_ref, dmask_ref, cdec_ref, gmix_ref, win_ref,
                         convw_ref, retgn_ref, wo_ref, gffn_ref, wr_ref, br_ref, tri_ref, ltri_ref,
                         ss_ref, sq_ref, sk_ref, skt_ref, sv_ref, sdec_ref,
                         h_ref, xn_ref, ri_ref, rw_ref, n_ref, conv_ref, ret_ref, snew_ref, so_ref,
                         s_scr, z_scr, mix_scr):
    j = pl.program_id(1)
    n_j = pl.num_programs(1)
    tt = x_ref.shape[1]

    @pl.when(j == 0)
    def _():
        s_scr[...] = jnp.zeros_like(s_scr)
        z_scr[0:SUBLANES, :] = jnp.zeros((SUBLANES, CONV_DIM), F32)

    cw = convw_ref[...]
    retgn = retgn_ref[...]
    q0 = 3 * CONV_DIM

    def project(p0):
        x = x_ref[0, p0:p0 + TOK_TILE, :]
        proj = _dot(_rms(x, gmix_ref[...]).astype(BF16), win_ref[...])
        gb = proj[:, 0:CONV_DIM]
        z = proj[:, CONV_DIM:2 * CONV_DIM] * proj[:, 2 * CONV_DIM:3 * CONV_DIM]
        z0 = SUBLANES + p0
        z_scr[z0:z0 + TOK_TILE, :] = z
        yc = cw[0:1] * z_scr[z0 - 2:z0 - 2 + TOK_TILE, :] + cw[1:2] * z_scr[z0 - 1:z0 - 1 + TOK_TILE, :] + cw[2:3] * z
        mix_scr[p0:p0 + TOK_TILE, 0:CONV_DIM] = (gb * yc).astype(BF16)
        return x, proj

    def mix(p0, x, proj):
        cos = cos_ref[p0:p0 + TOK_TILE, :]
        sin = sin_ref[p0:p0 + TOK_TILE, :]
        heads = range(RET_HEADS)
        lane = lambda hh: slice(hh * RET_DK, (hh + 1) * RET_DK)
        qr = [_rope(proj[:, q0 + hh * RET_DK:q0 + (hh + 1) * RET_DK], cos, sin) for hh in heads]
        kr = [_rope(proj[:, q0 + RET_DIM + hh * RET_DK:q0 + RET_DIM + (hh + 1) * RET_DK], cos, sin) * (RET_DK ** -0.5)
              for hh in heads]
        v = [proj[:, q0 + 2 * RET_DIM + hh * RET_DV:q0 + 2 * RET_DIM + (hh + 1) * RET_DV].astype(BF16) for hh in heads]
        s_old = [s_scr[hh] for hh in heads]
        scores = [_dot_nt(qr[hh].astype(BF16), kr[hh].astype(BF16)) * dmask_ref[hh] for hh in heads]
        o = []
        for hh in heads:
            lhs = jnp.concatenate([scores[hh].astype(BF16), (qr[hh] * qdec_ref[:, lane(hh)]).astype(BF16)], axis=1)
            o.append(_dot(lhs, jnp.concatenate([v[hh], s_old[hh].astype(BF16)], axis=0)))
        for hh in heads:
            kd = (kr[hh] * kdec_ref[:, lane(hh)]).T.astype(BF16)
            s_scr[hh] = s_old[hh] * cdec_ref[hh:hh + 1, :] + _dot(kd, v[hh])
        for hh in heads:
            g = proj[:, q0 + 3 * RET_DIM + hh * RET_DV:q0 + 3 * RET_DIM + (hh + 1) * RET_DV]
            gate = retgn[:, lane(hh)] * (g * jax.nn.sigmoid(g))
            mix_scr[p0:p0 + TOK_TILE, CONV_DIM + hh * RET_DV:CONV_DIM + (hh + 1) * RET_DV] = (
                _group_norm_gate(o[hh], gate).astype(BF16))
        h = x + _dot(mix_scr[p0:p0 + TOK_TILE, :], wo_ref[...])
        h_ref[0, p0:p0 + TOK_TILE, :] = h
        return h

    def route(part, h):
        p0 = part * TOK_TILE
        ri, rw, n, xn = _route(h, gffn_ref[...], wr_ref[...], br_ref[...], tri_ref[...], ltri_ref[...])
        ri_ref[:, p0:p0 + TOK_TILE] = ri
        rw_ref[:, p0:p0 + TOK_TILE] = rw
        n_ref[part] = n
        xn_ref[0, p0:p0 + TOK_TILE, :] = xn

    parts = tt // TOK_TILE
    nxt = project(0)
    per_part = SAMPLE_BLOCK // parts
    for part in range(parts):
        _sample_state_update(ss_ref, sq_ref, sk_ref, skt_ref, sv_ref, sdec_ref, snew_ref, so_ref,
                             range(part * per_part, (part + 1) * per_part))
        h = mix(part * TOK_TILE, *nxt)
        if part + 1 < parts:
            nxt = project((part + 1) * TOK_TILE)
        route(part, h)

    @pl.when(j == n_j - 1)
    def _():
        conv_ref[0] = z_scr[SUBLANES + tt - 2:SUBLANES + tt, :]
        ret_ref[0] = s_scr[...]

    z_scr[0:SUBLANES, :] = z_scr[tt:tt + SUBLANES, :]


def _mixer_prompt(x, tabs, wts, sample):
    bsz, t, _ = x.shape
    tt = MIX_TILE
    n_j = t // tt
    parts = tt // TOK_TILE
    n_tok = bsz * t
    nb = SAMPLE_BLOCK
    s_state, s_q, s_k, s_kt, s_v, s_dec = sample
    assert s_state.shape[0] == nb * bsz * n_j, "one block of sample requests per mixer grid step"
    args = (x,) + tuple(tabs) + tuple(wts) + tuple(sample)
    step = lambda b, j: b * n_j + j
    in_specs = [
        pl.BlockSpec((1, tt, D_MODEL), lambda b, j: (b, j, 0)),
        pl.BlockSpec((tt, LANES), lambda b, j: (j, 0)),
        pl.BlockSpec((tt, LANES), lambda b, j: (j, 0)),
    ] + [_const_spec(a.shape) for a in args[3:3 + len(tabs) - 2 + len(wts)]] + [
        pl.BlockSpec((nb, RET_HEADS, RET_DK, RET_DV), lambda b, j: (step(b, j), 0, 0, 0)),
        pl.BlockSpec((nb, RET_DIM), lambda b, j: (step(b, j), 0)),
        pl.BlockSpec((nb, RET_DIM), lambda b, j: (step(b, j), 0)),
        pl.BlockSpec((1, RET_HEADS, RET_DK, nb), lambda b, j: (step(b, j), 0, 0, 0)),
        pl.BlockSpec((nb, RET_DIM), lambda b, j: (step(b, j), 0)),
        _const_spec(s_dec.shape),
    ]
    out_shape = (
        jax.ShapeDtypeStruct((bsz, t, D_MODEL), F32),
        jax.ShapeDtypeStruct((bsz, t, D_MODEL), BF16),
        jax.ShapeDtypeStruct((SUBLANES, n_tok), I32),
        jax.ShapeDtypeStruct((SUBLANES, n_tok), F32),
        jax.ShapeDtypeStruct((bsz * n_j * parts, N_EXPERTS, LANES), F32),
        jax.ShapeDtypeStruct((bsz, CONV_WIDTH - 1, CONV_DIM), F32),
        jax.ShapeDtypeStruct((bsz, RET_HEADS, RET_DK, RET_DV), F32),
        jax.ShapeDtypeStruct(s_state.shape, F32),
        jax.ShapeDtypeStruct(s_q.shape, F32),
    )
    out_specs = (
        pl.BlockSpec((1, tt, D_MODEL), lambda b, j: (b, j, 0)),
        pl.BlockSpec((1, tt, D_MODEL), lambda b, j: (b, j, 0)),
        pl.BlockSpec((SUBLANES, tt), lambda b, j: (0, b * n_j + j)),
        pl.BlockSpec((SUBLANES, tt), lambda b, j: (0, b * n_j + j)),
        pl.BlockSpec((parts, N_EXPERTS, LANES), lambda b, j: (b * n_j + j, 0, 0)),
        pl.BlockSpec((1, CONV_WIDTH - 1, CONV_DIM), lambda b, j: (b, 0, 0)),
        pl.BlockSpec((1, RET_HEADS, RET_DK, RET_DV), lambda b, j: (b, 0, 0, 0)),
        pl.BlockSpec((nb, RET_HEADS, RET_DK, RET_DV), lambda b, j: (step(b, j), 0, 0, 0)),
        pl.BlockSpec((nb, RET_DIM), lambda b, j: (step(b, j), 0)),
    )
    return pl.pallas_call(
        _mixer_prompt_kernel,
        grid=(bsz, n_j),
        in_specs=in_specs,
        out_specs=out_specs,
        out_shape=out_shape,
        scratch_shapes=[
            pltpu.VMEM((RET_HEADS, RET_DK, RET_DV), F32),
            pltpu.VMEM((tt + SUBLANES, CONV_DIM), F32),
            pltpu.VMEM((tt, D_MODEL), BF16),
        ],
        compiler_params=_params("arbitrary", "arbitrary"),
        name="mixer_prompt",
    )(*args)


def _sample_proj_kernel(x_ref, sc_ref, cos_ref, sin_ref, gmix_ref, win_ref, convw_ref, retgn_ref,
                        yc_ref, q_ref, k_ref, v_ref, gate_ref, conv_ref):
    x = x_ref[...]
    u = _rms(x, gmix_ref[...]).astype(BF16)
    proj = _dot(u, win_ref[...])
    gb = proj[:, 0:CONV_DIM]
    z = proj[:, CONV_DIM:2 * CONV_DIM] * proj[:, 2 * CONV_DIM:3 * CONV_DIM]
    cw = convw_ref[...]
    buf0, buf1 = sc_ref[:, 0:CONV_DIM], sc_ref[:, CONV_DIM:2 * CONV_DIM]
    yc_ref[...] = gb * (cw[0:1] * buf0 + cw[1:2] * buf1 + cw[2:3] * z)
    conv_ref[:, 0:CONV_DIM] = buf1
    conv_ref[:, CONV_DIM:2 * CONV_DIM] = z
    q0 = 3 * CONV_DIM
    cos, sin = cos_ref[0:1, :], sin_ref[0:1, :]
    retgn = retgn_ref[...]
    for hh in range(RET_HEADS):
        l0 = hh * RET_DK
        q_ref[:, l0:l0 + RET_DK] = _rope(proj[:, q0 + l0:q0 + l0 + RET_DK], cos, sin)
        k_ref[:, l0:l0 + RET_DK] = _rope(proj[:, q0 + RET_DIM + l0:q0 + RET_DIM + l0 + RET_DK], cos, sin) * (RET_DK ** -0.5)
    v_ref[...] = proj[:, q0 + 2 * RET_DIM:q0 + 3 * RET_DIM]
    g = proj[:, q0 + 3 * RET_DIM:q0 + 4 * RET_DIM]
    gate_ref[...] = retgn * (g * jax.nn.sigmoid(g))


def _sample_proj(x, sc, cos, sin, gmix, win, convw, retgn):
    n = x.shape[0]
    outs = (
        jax.ShapeDtypeStruct((n, CONV_DIM), F32), jax.ShapeDtypeStruct((n, RET_DIM), F32),
        jax.ShapeDtypeStruct((n, RET_DIM), F32), jax.ShapeDtypeStruct((n, RET_DIM), F32),
        jax.ShapeDtypeStruct((n, RET_DIM), F32), jax.ShapeDtypeStruct((n, 2 * CONV_DIM), F32),
    )
    args = (x, sc, cos, sin, gmix, win, convw, retgn)
    return pl.pallas_call(
        _sample_proj_kernel,
        grid=(1,),
        in_specs=[_const_spec(a.shape) for a in args],
        out_specs=tuple(_const_spec(o.shape) for o in outs),
        out_shape=outs,
        compiler_params=_params("arbitrary"),
        name="sample_proj",
    )(*args)


def _sample_state_update(s_ref, q_ref, k_ref, kt_ref, v_ref, dec_ref, snew_ref, o_ref, requests):
    for r in requests:
        for hh in range(RET_HEADS):
            l0 = hh * RET_DV
            s = s_ref[r, hh]
            q = q_ref[r:r + 1, l0:l0 + RET_DK]
            k = k_ref[r:r + 1, l0:l0 + RET_DK]
            kc = kt_ref[0, hh, :, r:r + 1]
            vr = v_ref[r:r + 1, l0:l0 + RET_DV]
            qdec = dec_ref[hh:hh + 1, :]
            cdec = dec_ref[RET_HEADS + hh:RET_HEADS + hh + 1, :]
            qk = jnp.sum(q * k, axis=1, keepdims=True)
            o_ref[r:r + 1, l0:l0 + RET_DV] = qk * vr + _dot((q * qdec).astype(BF16), s.astype(BF16))
            snew_ref[r, hh] = s * cdec + kc * vr


def _sample_out_kernel(x_ref, yc_ref, o_ref, gate_ref, wo_ref, gffn_ref, wr_ref, br_ref, tri_ref, ltri_ref,
                       h_ref, xn_ref, ri_ref, rw_ref, n_ref):
    n = x_ref.shape[0]
    parts = [yc_ref[...].astype(BF16)]
    for hh in range(RET_HEADS):
        l0 = hh * RET_DV
        parts.append(_group_norm_gate(o_ref[:, l0:l0 + RET_DV], gate_ref[:, l0:l0 + RET_DV]).astype(BF16))
    h_ref[0:n, :] = x_ref[...] + _dot(jnp.concatenate(parts, axis=-1), wo_ref[...])
    h_ref[n:SAMPLE_ROWS, :] = jnp.zeros((SAMPLE_ROWS - n, D_MODEL), F32)
    for part in range(SAMPLE_ROWS // TOK_TILE):
        p0 = part * TOK_TILE
        ri, rw, cnt, xn = _route(h_ref[p0:p0 + TOK_TILE, :], gffn_ref[...], wr_ref[...], br_ref[...], tri_ref[...],
                                 ltri_ref[...])
        ri_ref[:, p0:p0 + TOK_TILE] = ri
        rw_ref[:, p0:p0 + TOK_TILE] = rw
        n_ref[part] = cnt
        xn_ref[p0:p0 + TOK_TILE, :] = xn


def _sample_out(x, yc, o, gate, wo, gffn, wr, br, tri, ltri):
    outs = (
        jax.ShapeDtypeStruct((SAMPLE_ROWS, D_MODEL), F32), jax.ShapeDtypeStruct((SAMPLE_ROWS, D_MODEL), BF16),
        jax.ShapeDtypeStruct((SUBLANES, SAMPLE_ROWS), I32), jax.ShapeDtypeStruct((SUBLANES, SAMPLE_ROWS), F32),
        jax.ShapeDtypeStruct((SAMPLE_ROWS // TOK_TILE, N_EXPERTS, LANES), F32),
    )
    args = (x, yc, o, gate, wo, gffn, wr, br, tri, ltri)
    return pl.pallas_call(
        _sample_out_kernel,
        grid=(1,),
        in_specs=[_const_spec(a.shape) for a in args],
        out_specs=tuple(_const_spec(o_.shape) for o_ in outs),
        out_shape=outs,
        compiler_params=_params("arbitrary"),
        name="sample_out",
    )(*args)


def _sample_inputs(x, state_conv, state_ret, gmix, win, convw, retgn):
    n = x.shape[0]
    lg = _log_gamma()
    cos, sin = _rope_tables(np.full((SUBLANES,), PAST_LEN))
    yc, q, k, v, gate, conv_new = _sample_proj(x, state_conv.reshape(n, 2 * CONV_DIM), cos, sin, gmix, win, convw, retgn)
    nb = SAMPLE_BLOCK
    cols = lambda a: a.reshape(n // nb, nb, RET_HEADS, RET_DK).transpose(0, 2, 3, 1)
    step = np.exp(lg[:, None] * 1.0)
    dec = jnp.asarray(np.broadcast_to(np.concatenate([step, step], axis=0), (2 * RET_HEADS, LANES)), F32)
    return (yc, gate, conv_new.reshape(n, 2, CONV_DIM)), (state_ret, q, k, cols(k), v, dec)


def _load_rows(ref, n_rows):
    words = [ref[pl.ds(c, n_rows, stride=ROW_CHUNKS), :] for c in range(ROW_CHUNKS)]
    half = lambda i: [pltpu.unpack_elementwise(w, index=i, packed_dtype=BF16, unpacked_dtype=F32) for w in words]
    return jnp.concatenate(half(0) + half(1), axis=-1)


def _store_rows(ref, val):
    half = D_MODEL // 2
    for c in range(ROW_CHUNKS):
        pair = [val[:, c * LANES:(c + 1) * LANES], val[:, half + c * LANES:half + (c + 1) * LANES]]
        ref[pl.ds(c, val.shape[0], stride=ROW_CHUNKS), :] = pltpu.pack_elementwise(pair, packed_dtype=BF16)


def _row_slice(ref, row, rows):
    return ref.at[pl.ds(pl.multiple_of(row * ROW_CHUNKS, ROW_CHUNKS), rows * ROW_CHUNKS)]


def _run_copies(table, make_copy):
    for b in range(RUN_BITS):
        first = N_EXPERTS + N_EXPERTS * b
        count = table(b)

        def start(col, b=b):
            entry = table(col)
            make_copy(entry & ((1 << SRC_BITS) - 1), entry >> SRC_BITS, 1 << b).start()

        for r in range(RUN_UNROLL - 1):
            @pl.when(r < count % RUN_UNROLL)
            def _():
                start(first + count - 1 - r)

        def body(j, carry, first=first, start=start):
            for u in range(RUN_UNROLL):
                start(first + RUN_UNROLL * j + u)
            return carry

        lax.fori_loop(0, count // RUN_UNROLL, body, 0)


def _pair_rows_iota():
    return lax.broadcasted_iota(I32, (PAIR_ROWS, TOK_TILE), 0)


def _dispatch_kernel(n_p, tab_ref, xp_ref, xs_in_ref, ri_ref, xs_ref, idx_smem, y_scr, isem, rsem):
    i, n = pl.program_id(0), pl.num_programs(0)
    slot = i % 2
    step_rows = STEP_TILES * PAIR_ROWS

    def idx_copy(step, sl):
        return pltpu.make_async_copy(tab_ref.at[pl.ds(step * STEP_TILES * TAB_COLS, STEP_TILES * TAB_COLS)],
                                     idx_smem.at[pl.ds(sl * STEP_TILES * TAB_COLS, STEP_TILES * TAB_COLS)], isem.at[sl])

    def wait_step(sl):
        pltpu.make_async_copy(_row_slice(xs_ref, 0, step_rows), y_scr.at[sl], rsem.at[sl]).wait()

    @pl.when(i == 0)
    def _():
        idx_copy(0, 0).start()

    @pl.when(i + 1 < n)
    def _():
        idx_copy(i + 1, 1 - slot).start()

    @pl.when(i >= 2)
    def _():
        wait_step(slot)

    xn = jnp.where(i < n_p, xp_ref[...], xs_in_ref[...])
    ri = ri_ref[...]
    rows = _pair_rows_iota()
    for g in range(STEP_TILES):
        tok = slice(g * TOK_TILE, (g + 1) * TOK_TILE)
        perm = jnp.where((rows == ri[2:3, tok]) | (rows == ri[3:4, tok]), 1.0, 0.0).astype(BF16)
        y = _dot(perm, xn[tok, :])
        _store_rows(y_scr.at[slot, pl.ds(g * PAIR_ROWS * ROW_CHUNKS, PAIR_ROWS * ROW_CHUNKS)], y)

    idx_copy(i, slot).wait()
    for g in range(STEP_TILES):
        tab0 = (slot * STEP_TILES + g) * TAB_COLS
        _run_copies(lambda col, tab0=tab0: idx_smem[tab0 + col], lambda src, dst, rows_, g=g: pltpu.make_async_copy(
            _row_slice(y_scr.at[slot], g * PAIR_ROWS + src, rows_), _row_slice(xs_ref, dst, rows_), rsem.at[slot]))

    @pl.when(i == n - 1)
    def _():
        wait_step(slot)

    @pl.when((i == n - 1) & (n >= 2))
    def _():
        wait_step(1 - slot)


def _dispatch(tab, xn_p, xn_s, ri):
    tt = STEP_TILES * TOK_TILE
    n_p = xn_p.shape[0] // tt
    n = tab.shape[0] // STEP_TILES
    return pl.pallas_call(
        functools.partial(_dispatch_kernel, n_p),
        grid=(n,),
        in_specs=[
            pl.BlockSpec(memory_space=pl.ANY),
            pl.BlockSpec((tt, D_MODEL), lambda i: (jnp.minimum(i, n_p - 1), 0)),
            pl.BlockSpec((tt, D_MODEL), lambda i: (jnp.maximum(i - n_p, 0), 0)),
            pl.BlockSpec((SUBLANES, tt), lambda i: (0, i)),
        ],
        out_specs=pl.BlockSpec(memory_space=pl.ANY),
        out_shape=jax.ShapeDtypeStruct((tab.shape[0] * PAIR_ROWS * ROW_CHUNKS, LANES), U32),
        scratch_shapes=[
            pltpu.SMEM((2 * STEP_TILES * TAB_COLS,), I32),
            pltpu.VMEM((2, STEP_TILES * PAIR_ROWS * ROW_CHUNKS, LANES), U32),
            pltpu.SemaphoreType.DMA((2,)),
            pltpu.SemaphoreType.DMA((2,)),
        ],
        compiler_params=_params("arbitrary"),
        name="moe_dispatch",
    )(tab.reshape(-1), xn_p, xn_s, ri)


def _ffn_kernel(tile_ref, exp_ref, lo_ref, hi_ref, nxt_ref, slot_ref, xs_ref, wg_ref, wu_ref, wd_ref, o_ref,
                x_buf, wg_buf, wu_buf, wd_buf, wgu_scr, wd_scr, y_scr, xsem, wsem):
    w, n_w = pl.program_id(0), pl.num_programs(0)
    prev = jnp.maximum(w - 1, 0)
    lo, hi = lo_ref[w], hi_ref[w]
    e, slot = exp_ref[w], slot_ref[w]
    first = (w == 0) | (tile_ref[w] != tile_ref[prev])

    ahead = FFN_RING - 1

    def x_copy(step):
        rows = FFN_TILE * ROW_CHUNKS
        return pltpu.make_async_copy(xs_ref.at[pl.ds(pl.multiple_of(tile_ref[step] * rows, rows), rows)],
                                     x_buf.at[step % FFN_RING], xsem.at[step % FFN_RING])

    for step in range(ahead):
        @pl.when((w == 0) & (step < n_w))
        def _():
            x_copy(step).start()

    @pl.when(w + ahead < n_w)
    def _():
        x_copy(w + ahead).start()

    x_copy(w).wait()

    def weight_copies(expert, sl):
        return [pltpu.make_async_copy(src.at[expert], buf.at[sl], wsem.at[sl])
                for src, buf in ((wg_ref, wg_buf), (wu_ref, wu_buf), (wd_ref, wd_buf))]

    @pl.when(w == 0)
    def _():
        for cp in weight_copies(e, slot):
            cp.start(priority=1)

    @pl.when((w == 0) | (e != exp_ref[prev]))
    def _():
        for cp in weight_copies(e, slot):
            cp.wait()

        @pl.when(nxt_ref[w] != e)
        def _():
            for cp in weight_copies(nxt_ref[w], 1 - slot):
                cp.start(priority=1)

        wgu_scr[:, 0:D_FF] = wg_buf[slot].astype(BF16)
        wgu_scr[:, D_FF:2 * D_FF] = wu_buf[slot].astype(BF16)
        wd_scr[...] = wd_buf[slot].astype(BF16)

    @pl.when(w == 0)
    def _():
        y_scr[...] = jnp.zeros_like(y_scr)

    def compute(r0, n_rows):
        chunk = pl.ds(r0 * ROW_CHUNKS, n_rows * ROW_CHUNKS)
        gu = _dot(_load_rows(x_buf.at[w % FFN_RING, chunk], n_rows).astype(BF16), wgu_scr[...])
        g, u = gu[:, 0:D_FF], gu[:, D_FF:2 * D_FF]
        y = _dot((g * jax.nn.sigmoid(g) * u).astype(BF16), wd_scr[...])
        row = r0 + lax.broadcasted_iota(I32, (n_rows, 1), 0)
        y = jnp.where(((row >= lo) & (row < hi)) | first, y, y_scr[r0:r0 + n_rows, :])
        y_scr[r0:r0 + n_rows, :] = y
        _store_rows(o_ref.at[chunk], y)

    half = FFN_TILE // 2
    top, bottom = hi <= half, lo >= half

    @pl.when((hi > lo) & top)
    def _():
        compute(0, half)

    @pl.when((hi > lo) & bottom)
    def _():
        compute(half, half)

    @pl.when((hi > lo) & jnp.logical_not(top | bottom))
    def _():
        compute(0, FFN_TILE)


def _expert_ffn(items, xs, w_gate, w_up, w_down):
    n_items = items[0].shape[0]
    row_spec = pl.BlockSpec((FFN_TILE * ROW_CHUNKS, LANES), lambda w, t, *_: (t[w], 0))
    grid_spec = pltpu.PrefetchScalarGridSpec(
        num_scalar_prefetch=len(items),
        grid=(n_items,),
        in_specs=[pl.BlockSpec(memory_space=pl.ANY)] * 4,
        out_specs=row_spec,
        scratch_shapes=[
            pltpu.VMEM((FFN_RING, FFN_TILE * ROW_CHUNKS, LANES), U32),
            pltpu.VMEM((2, D_MODEL, D_FF), F32), pltpu.VMEM((2, D_MODEL, D_FF), F32),
            pltpu.VMEM((2, D_FF, D_MODEL), F32),
            pltpu.VMEM((D_MODEL, 2 * D_FF), BF16), pltpu.VMEM((D_FF, D_MODEL), BF16),
            pltpu.VMEM((FFN_TILE, D_MODEL), F32),
            pltpu.SemaphoreType.DMA((FFN_RING,)),
            pltpu.SemaphoreType.DMA((2,)),
        ],
    )
    return pl.pallas_call(
        _ffn_kernel,
        grid_spec=grid_spec,
        out_shape=jax.ShapeDtypeStruct(xs.shape, U32),
        compiler_params=_params("arbitrary"),
        name="moe_ffn",
    )(*items, xs, w_gate, w_up, w_down)


def _work_items(counts, n_rows):
    n_tiles = n_rows // FFN_TILE
    n_items = n_tiles + N_EXPERTS - 1
    off = jnp.cumsum(counts) - counts
    first_tile = off // FFN_TILE
    last_tile = jnp.maximum(off + counts - 1, off) // FFN_TILE
    n_e = jnp.where(counts > 0, last_tile - first_tile + 1, 0)
    start = jnp.cumsum(n_e) - n_e
    total = jnp.sum(n_e)
    w = jnp.minimum(jnp.arange(n_items, dtype=I32), total - 1)
    ids = jnp.arange(N_EXPERTS, dtype=I32)[None, :]
    e = jnp.max(jnp.where((start[None, :] <= w[:, None]) & (n_e[None, :] > 0), ids, 0), axis=1)
    pick = lambda a: jnp.sum(jnp.where(ids == e[:, None], a[None, :], 0), axis=1)
    tile = pick(first_tile) + (w - pick(start))
    lo = jnp.clip(pick(off) - tile * FFN_TILE, 0, FFN_TILE)
    hi = jnp.clip(pick(off + counts) - tile * FFN_TILE, 0, FFN_TILE)
    hi = jnp.where(jnp.arange(n_items) < total, hi, lo)
    used = n_e[None, :] > 0
    nxt = jnp.min(jnp.where(used & (ids > e[:, None]), ids, N_EXPERTS), axis=1)
    nxt = jnp.where(nxt == N_EXPERTS, e, nxt)
    slot = jnp.sum(jnp.where(used & (ids < e[:, None]), 1, 0), axis=1) % 2
    return tuple(a.astype(I32) for a in (tile, e, lo, hi, nxt, slot))


def _run_table(n):
    tiles = n.shape[0]
    counts = jnp.sum(n, axis=0)
    goff = jnp.cumsum(counts) - counts
    dst0 = goff[None, :] + jnp.cumsum(n, axis=0) - n
    src0 = jnp.cumsum(n, axis=1) - n
    bits = jnp.arange(RUN_BITS, dtype=I32)[:, None, None]
    flag = (n[None] >> bits) & 1
    done = (n[None] >> (bits + 1)) << (bits + 1)
    entry = ((dst0[None] + done) << SRC_BITS) | (src0[None] + done)
    rank = jnp.cumsum(flag, axis=2) - flag
    slot = jnp.arange(N_EXPERTS, dtype=I32)
    sel = (flag[..., None] == 1) & (rank[..., None] == slot)
    packed = jnp.sum(jnp.where(sel, entry[..., None], 0), axis=2)
    m = jnp.sum(flag, axis=2).T
    tab = jnp.zeros((tiles, TAB_COLS), I32)
    tab = tab.at[:, :RUN_BITS].set(m)
    tab = tab.at[:, N_EXPERTS:N_EXPERTS * (RUN_BITS + 1)].set(packed.transpose(1, 0, 2).reshape(tiles, -1))
    return counts, tab


def _combine_kernel(g_tiles, n, tab_ref, h_ref, p_ref, ri_ref, rw_ref, ys_ref, gple_ref, wpg_ref, wpp_ref, gfin_ref,
                    o_ref, idx_smem, rows_scr, isem, rsem):
    i = pl.program_id(0)
    slot = i % 2

    def idx_copy(step, sl):
        return pltpu.make_async_copy(tab_ref.at[pl.ds(step * g_tiles * TAB_COLS, g_tiles * TAB_COLS)],
                                     idx_smem.at[pl.ds(sl * g_tiles * TAB_COLS, g_tiles * TAB_COLS)], isem.at[sl])

    def gather(sl):
        for g in range(g_tiles):
            tab0 = (sl * g_tiles + g) * TAB_COLS
            _run_copies(lambda col, tab0=tab0: idx_smem[tab0 + col], lambda src, dst, rows_, g=g: pltpu.make_async_copy(
                _row_slice(ys_ref, dst, rows_), _row_slice(rows_scr.at[sl], g * PAIR_ROWS + src, rows_), rsem.at[sl]))

    @pl.when(i == 0)
    def _():
        idx_copy(0, 0).start()
        idx_copy(0, 0).wait()
        gather(0)

    if n >= 2:
        @pl.when(i == 0)
        def _():
            idx_copy(1, 1).start()

        @pl.when(i + 1 < n)
        def _():
            idx_copy(i + 1, 1 - slot).wait()
            gather(1 - slot)

    @pl.when(i + 2 < n)
    def _():
        idx_copy(i + 2, slot).start()

    pltpu.make_async_copy(_row_slice(ys_ref, 0, g_tiles * PAIR_ROWS), rows_scr.at[slot], rsem.at[slot]).wait()

    ri, rw = ri_ref[...], rw_ref[...]
    rows = _pair_rows_iota()
    tiles = range(g_tiles)
    tok = lambda g: slice(g * TOK_TILE, (g + 1) * TOK_TILE)
    h = []
    for g in tiles:
        r1, r2 = ri[2:3, tok(g)], ri[3:4, tok(g)]
        wperm = jnp.where(rows == r1, rw[0:1, tok(g)], 0.0) + jnp.where(rows == r2, rw[1:2, tok(g)], 0.0)
        srows = _load_rows(rows_scr.at[slot, pl.ds(g * PAIR_ROWS * ROW_CHUNKS, PAIR_ROWS * ROW_CHUNKS)], PAIR_ROWS)
        h.append(h_ref[tok(g), :] + _dot_tn(wperm.astype(BF16), srows.astype(BF16)))
    gate = [jax.nn.sigmoid(_dot(_rms(h[g], gple_ref[...]).astype(BF16), wpg_ref[...])) for g in tiles]
    for g in tiles:
        hg = h[g] + _dot(p_ref[tok(g), :].astype(BF16), wpp_ref[...]) * gate[g]
        o_ref[tok(g), :] = _rms(hg, gfin_ref[...])


def _combine(tab, h, p, ri, rw, ys, gple, wpg, wpp, gfin):
    n_tiles = tab.shape[0]
    g_tiles = STEP_TILES
    n = n_tiles // g_tiles
    tt = g_tiles * TOK_TILE
    return pl.pallas_call(
        functools.partial(_combine_kernel, g_tiles, n),
        grid=(n,),
        in_specs=[
            pl.BlockSpec(memory_space=pl.ANY),
            pl.BlockSpec((tt, D_MODEL), lambda i: (i, 0)),
            pl.BlockSpec((tt, PLE_DIM), lambda i: (i, 0)),
            pl.BlockSpec((SUBLANES, tt), lambda i: (0, i)),
            pl.BlockSpec((SUBLANES, tt), lambda i: (0, i)),
            pl.BlockSpec(memory_space=pl.ANY),
            _const_spec(gple.shape), _const_spec(wpg.shape), _const_spec(wpp.shape), _const_spec(gfin.shape),
        ],
        out_specs=pl.BlockSpec((tt, D_MODEL), lambda i: (i, 0)),
        out_shape=jax.ShapeDtypeStruct((n_tiles * TOK_TILE, D_MODEL), F32),
        scratch_shapes=[
            pltpu.SMEM((2 * g_tiles * TAB_COLS,), I32),
            pltpu.VMEM((2, g_tiles * PAIR_ROWS * ROW_CHUNKS, LANES), U32),
            pltpu.SemaphoreType.DMA((2,)),
            pltpu.SemaphoreType.DMA((2,)),
        ],
        compiler_params=_params("arbitrary"),
        name="moe_combine",
    )(tab.reshape(-1), h, p, ri, rw, ys, gple, wpg, wpp, gfin)


def _log_gamma():
    return np.log(1.0 - 2.0 ** (-5.0 - np.arange(RET_HEADS, dtype=np.float64)))


def _rope_tables(pos):
    inv = 1.0 / (ROPE_BASE ** (np.arange(0, RET_DK, 2, dtype=np.float64) / RET_DK))
    ang = np.asarray(pos, np.float64)[:, None] * inv[None, :]
    cos, sin = np.cos(ang), np.sin(ang)
    return (jnp.asarray(np.concatenate([cos, cos], axis=-1), F32),
            jnp.asarray(np.concatenate([-sin, sin], axis=-1), F32))


def _decay_tables(c):
    lg = _log_gamma()
    idx = np.arange(c, dtype=np.float64)
    diff = idx[:, None] - idx[None, :]
    dmask = np.where((diff >= 0.0)[None], np.exp(lg[:, None, None] * np.maximum(diff, 0.0)[None]), 0.0)
    q_dec = np.exp(lg[None, :] * (idx[:, None] + 1.0))
    k_dec = np.exp(lg[None, :] * (c - 1.0 - idx[:, None]))
    c_dec = np.exp(lg * c)
    lanes = lambda a: np.repeat(a, RET_DK, axis=1)
    cdec = np.zeros((SUBLANES, LANES))
    cdec[:RET_HEADS] = c_dec[:, None]
    return tuple(jnp.asarray(a, F32) for a in (lanes(q_dec), lanes(k_dec), dmask, cdec))


def _router_params(w_rg, b_rg, w_re, b_re):
    wr = jnp.zeros((ROUTER_ROWS, D_MODEL), F32).at[:N_GROUPS].set(w_rg.T).at[SUBLANES:].set(w_re.T)
    br = jnp.full((ROUTER_ROWS,), NEG_BIG, F32).at[:N_GROUPS].set(b_rg).at[SUBLANES:].set(b_re.reshape(-1))
    return wr.astype(BF16), jnp.broadcast_to(br[:, None], (ROUTER_ROWS, LANES))


def _strict_upper(t):
    i = np.arange(t)
    return jnp.asarray(i[:, None] < i[None, :], BF16)


def _pad_rows(a, rows):
    return jnp.zeros((rows,) + a.shape[1:], a.dtype).at[:a.shape[0]].set(a)


def kernel(x_prompt, x_sample, state_conv, state_ret, p_prompt, p_sample, g_mix, w_in, conv_w, ret_gn, w_o, g_ffn,
           w_router_group, b_router_group, w_router_expert, b_router_expert, w_gate, w_up, w_down, g_ple,
           w_ple_proj, w_ple_gate, g_final):
    bp, tp, _ = x_prompt.shape
    bs = x_sample.shape[0]
    assert x_sample.shape[1] == 1 and g_mix.shape[0] == 1, "one layer, one new token per sample request"
    assert bs < SAMPLE_ROWS and tp % MIX_TILE == 0
    n_p = bp * tp

    tabs = _rope_tables(np.arange(tp)) + _decay_tables(TOK_TILE)
    wr, br = _router_params(w_router_group[0], b_router_group[0], w_router_expert[0], b_router_expert[0])
    win, wo, convw = w_in[0].astype(BF16), w_o[0].astype(BF16), _pad_rows(conv_w[0], SUBLANES)
    tri, ltri = _strict_upper(TOK_TILE), _strict_upper(N_EXPERTS).T

    xs0 = x_sample[:, 0]
    (yc_s, gate_s, conv_s), sample = _sample_inputs(xs0, state_conv[0], state_ret[0], g_mix, win, convw, ret_gn)
    h_p, xn_p, ri_p, rw_p, n_p_tab, conv_p, ret_p, ret_s, o_s = _mixer_prompt(
        x_prompt, tabs, (g_mix, win, convw, ret_gn, wo, g_ffn, wr, br, tri, ltri), sample)
    h_s, xn_s, ri_s, rw_s, n_s_tab = _sample_out(xs0, yc_s, o_s, gate_s, wo, g_ffn, wr, br, tri, ltri)
    h_p, xn_p = h_p.reshape(n_p, D_MODEL), xn_p.reshape(n_p, D_MODEL)

    n_tab = jnp.concatenate([n_p_tab, n_s_tab], axis=0)[:, :, 0].astype(I32)
    counts, tab = _run_table(n_tab)
    ri = jnp.concatenate([ri_p, ri_s], axis=1)
    n_tiles = n_tab.shape[0]

    xs = _dispatch(tab, xn_p, xn_s, ri)
    ys = _expert_ffn(_work_items(counts, n_tiles * PAIR_ROWS), xs, w_gate[0], w_up[0], w_down[0])

    wpg, wpp = w_ple_gate[0].astype(BF16), w_ple_proj[0].astype(BF16)
    gfin = g_final[None, :]
    p_s = _pad_rows(p_sample[0].reshape(bs, PLE_DIM), SAMPLE_ROWS)
    n_s_tiles = SAMPLE_ROWS // TOK_TILE
    y_p = _combine(tab[:n_tiles - n_s_tiles], h_p, p_prompt[0].reshape(n_p, PLE_DIM), ri_p, rw_p, ys, g_ple, wpg, wpp, gfin)
    y_s = _combine(tab[n_tiles - n_s_tiles:], h_s, p_s, ri_s, rw_s, ys, g_ple, wpg, wpp, gfin)
    return (y_p.reshape(bp, tp, D_MODEL), y_s[:bs].reshape(bs, 1, D_MODEL), conv_p[None], ret_p[None],
            conv_s[None], ret_s[None])
```

```python
import functools

import jax
import jax.numpy as jnp
import numpy as np
from jax import lax
from jax.experimental import pallas as pl
from jax.experimental.pallas import tpu as pltpu

F32, BF16, I32, U32 = jnp.float32, jnp.bfloat16, jnp.int32, jnp.uint32

D_MODEL = 1024
CONV_DIM = 512
CONV_WIDTH = 3
RET_DIM = 512
RET_HEADS = 4
RET_DK = 128
RET_DV = 128
ROPE_BASE = 10000.0
N_GROUPS = 4
EXPERTS_PER_GROUP = 8
N_EXPERTS = 32
D_FF = 512
PLE_DIM = 256
EPS = 1e-6
PAST_LEN = 16384

LANES = 128
SUBLANES = 8
ROW_CHUNKS = D_MODEL // (2 * LANES)
ROUTER_ROWS = SUBLANES + N_EXPERTS
VMEM_LIMIT = 56 * 1024 * 1024
NEG_BIG = -1e30

TOK_TILE = 256
MIX_TILE = 1024
PAIR_ROWS = 2 * TOK_TILE
STEP_TILES = 2
SAMPLE_ROWS = STEP_TILES * TOK_TILE
FFN_TILE = 512
FFN_RING = 3
SAMPLE_BLOCK = 8
RUN_BITS = 9
SRC_BITS = 10
RUN_UNROLL = 4
TAB_COLS = 384


def _rms(x, g):
    return x * lax.rsqrt(jnp.mean(x * x, axis=-1, keepdims=True) + EPS) * g


def _dot(a, b):
    return jnp.dot(a, b, preferred_element_type=F32)


def _dot_nt(a, b):
    return lax.dot_general(a, b, (((1,), (1,)), ((), ())), preferred_element_type=F32)


def _dot_tn(a, b):
    return lax.dot_general(a, b, (((0,), (0,)), ((), ())), preferred_element_type=F32)


def _rope(x, cos, sin_signed):
    return x * cos + pltpu.roll(x, RET_DK // 2, 1) * sin_signed


def _const_spec(shape):
    nd = len(shape)
    return pl.BlockSpec(shape, lambda *_: (0,) * nd)


def _params(*sem):
    return pltpu.CompilerParams(dimension_semantics=sem, vmem_limit_bytes=VMEM_LIMIT)


def _route(h, gffn, wr, br, tri, ltri):
    t = h.shape[0]
    xn = _rms(h, gffn).astype(BF16)
    lt = _dot_nt(wr, xn) + br[:, 0:1]
    row8 = lax.broadcasted_iota(I32, (SUBLANES, t), 0).astype(F32)
    gl = lt[0:SUBLANES]
    m = jnp.max(gl, axis=0, keepdims=True)
    g_top = 1.0 / jnp.sum(jnp.exp(gl - m), axis=0, keepdims=True)
    gidx = jnp.min(jnp.where(gl == m, row8, float(SUBLANES)), axis=0, keepdims=True)
    e_sel = jnp.where(gidx == 0.0, lt[8:16],
                      jnp.where(gidx == 1.0, lt[16:24], jnp.where(gidx == 2.0, lt[24:32], lt[32:40])))
    m1 = jnp.max(e_sel, axis=0, keepdims=True)
    i1 = jnp.min(jnp.where(e_sel == m1, row8, float(SUBLANES)), axis=0, keepdims=True)
    rest = jnp.where(row8 == i1, -jnp.inf, e_sel)
    m2 = jnp.max(rest, axis=0, keepdims=True)
    i2 = jnp.min(jnp.where(rest == m2, row8, float(SUBLANES)), axis=0, keepdims=True)
    d = jnp.exp(m2 - m1)
    w1 = g_top / (1.0 + d)
    w2 = g_top * d / (1.0 + d)
    e1 = gidx * float(EXPERTS_PER_GROUP) + i1
    e2 = gidx * float(EXPERTS_PER_GROUP) + i2
    row32 = lax.broadcasted_iota(I32, (N_EXPERTS, t), 0).astype(F32)
    a1 = jnp.where(row32 == e1, 1.0, 0.0)
    a2 = jnp.where(row32 == e2, 1.0, 0.0)
    a = a1 + a2
    n = jnp.broadcast_to(jnp.sum(a, axis=1, keepdims=True), (N_EXPERTS, LANES))
    start = _dot(ltri, n.astype(BF16))
    base = _dot(a.astype(BF16), tri) + start[:, 0:1]
    r1 = jnp.sum(a1 * base, axis=0, keepdims=True)
    r2 = jnp.sum(a2 * base, axis=0, keepdims=True)
    ri = jnp.where(row8 == 0.0, e1, jnp.where(row8 == 1.0, e2, jnp.where(row8 == 2.0, r1,
                                                                         jnp.where(row8 == 3.0, r2, 0.0))))
    rw = jnp.where(row8 == 0.0, w1, jnp.where(row8 == 1.0, w2, 0.0))
    return ri.astype(I32), rw, n, xn


def _group_norm_gate(o, gate):
    mu = jnp.mean(o, axis=-1, keepdims=True)
    oc = o - mu
    return oc * lax.rsqrt(jnp.mean(oc * oc, axis=-1, keepdims=True) + EPS) * gate


def _mixer_prompt_kernel(x_ref, cos_ref, sin_ref, qdec_ref, kdec_ref, dmask_ref, cdec_ref, gmix_ref, win_ref,
                         convw_ref, retgn_ref, wo_ref, gffn_ref, wr_ref, br_ref, tri_ref, ltri_ref,
                         ss_ref, sq_ref, sk_ref, skt_ref, sv_ref, sdec_ref,
                         h_ref, xn_ref, ri_ref, rw_ref, n_ref, conv_ref, ret_ref, snew_ref, so_ref,
                         s_scr, z_scr, mix_scr):
    j = pl.program_id(1)
    n_j = pl.num_programs(1)
    tt = x_ref.shape[1]

    @pl.when(j == 0)
    def _():
        s_scr[...] = jnp.zeros_like(s_scr)
        z_scr[0:SUBLANES, :] = jnp.zeros((SUBLANES, CONV_DIM), F32)

    cw = convw_ref[...]
    retgn = retgn_ref[...]
    q0 = 3 * CONV_DIM

    def project(p0):
        x = x_ref[0, p0:p0 + TOK_TILE, :]
        proj = _dot(_rms(x, gmix_ref[...]).astype(BF16), win_ref[...])
        gb = proj[:, 0:CONV_DIM]
        z = proj[:, CONV_DIM:2 * CONV_DIM] * proj[:, 2 * CONV_DIM:3 * CONV_DIM]
        z0 = SUBLANES + p0
        z_scr[z0:z0 + TOK_TILE, :] = z
        yc = cw[0:1] * z_scr[z0 - 2:z0 - 2 + TOK_TILE, :] + cw[1:2] * z_scr[z0 - 1:z0 - 1 + TOK_TILE, :] + cw[2:3] * z
        mix_scr[p0:p0 + TOK_TILE, 0:CONV_DIM] = (gb * yc).astype(BF16)
        return x, proj

    def mix(p0, x, proj):
        cos = cos_ref[p0:p0 + TOK_TILE, :]
        sin = sin_ref[p0:p0 + TOK_TILE, :]
        heads = range(RET_HEADS)
        lane = lambda hh: slice(hh * RET_DK, (hh + 1) * RET_DK)
        qr = [_rope(proj[:, q0 + hh * RET_DK:q0 + (hh + 1) * RET_DK], cos, sin) for hh in heads]
        kr = [_rope(proj[:, q0 + RET_DIM + hh * RET_DK:q0 + RET_DIM + (hh + 1) * RET_DK], cos, sin) * (RET_DK ** -0.5)
              for hh in heads]
        v = [proj[:, q0 + 2 * RET_DIM + hh * RET_DV:q0 + 2 * RET_DIM + (hh + 1) * RET_DV].astype(BF16) for hh in heads]
        s_old = [s_scr[hh] for hh in heads]
        scores = [_dot_nt(qr[hh].astype(BF16), kr[hh].astype(BF16)) * dmask_ref[hh] for hh in heads]
        o = []
        for hh in heads:
            lhs = jnp.concatenate([scores[hh].astype(BF16), (qr[hh] * qdec_ref[:, lane(hh)]).astype(BF16)], axis=1)
            o.append(_dot(lhs, jnp.concatenate([v[hh], s_old[hh].astype(BF16)], axis=0)))
        for hh in heads:
            kd = (kr[hh] * kdec_ref[:, lane(hh)]).T.astype(BF16)
            s_scr[hh] = s_old[hh] * cdec_ref[hh:hh + 1, :] + _dot(kd, v[hh])
        for hh in heads:
            g = proj[:, q0 + 3 * RET_DIM + hh * RET_DV:q0 + 3 * RET_DIM + (hh + 1) * RET_DV]
            gate = retgn[:, lane(hh)] * (g * jax.nn.sigmoid(g))
            mix_scr[p0:p0 + TOK_TILE, CONV_DIM + hh * RET_DV:CONV_DIM + (hh + 1) * RET_DV] = (
                _group_norm_gate(o[hh], gate).astype(BF16))
        h = x + _dot(mix_scr[p0:p0 + TOK_TILE, :], wo_ref[...])
        h_ref[0, p0:p0 + TOK_TILE, :] = h
        return h

    def route(part, h):
        p0 = part * TOK_TILE
        ri, rw, n, xn = _route(h, gffn_ref[...], wr_ref[...], br_ref[...], tri_ref[...], ltri_ref[...])
        ri_ref[:, p0:p0 + TOK_TILE] = ri
        rw_ref[:, p0:p0 + TOK_TILE] = rw
        n_ref[part] = n
        xn_ref[0, p0:p0 + TOK_TILE, :] = xn

    parts = tt // TOK_TILE
    nxt = project(0)
    per_part = SAMPLE_BLOCK // parts
    for part in range(parts):
        _sample_state_update(ss_ref, sq_ref, sk_ref, skt_ref, sv_ref, sdec_ref, snew_ref, so_ref,
                             range(part * per_part, (part + 1) * per_part))
        h = mix(part * TOK_TILE, *nxt)
        if part + 1 < parts:
            nxt = project((part + 1) * TOK_TILE)
        route(part, h)

    @pl.when(j == n_j - 1)
    def _():
        conv_ref[0] = z_scr[SUBLANES + tt - 2:SUBLANES + tt, :]
        ret_ref[0] = s_scr[...]

    z_scr[0:SUBLANES, :] = z_scr[tt:tt + SUBLANES, :]


def _mixer_prompt(x, tabs, wts, sample):
    bsz, t, _ = x.shape
    tt = MIX_TILE
    n_j = t // tt
    parts = tt // TOK_TILE
    n_tok = bsz * t
    nb = SAMPLE_BLOCK
    s_state, s_q, s_k, s_kt, s_v, s_dec = sample
    assert s_state.shape[0] == nb * bsz * n_j, "one block of sample requests per mixer grid step"
    args = (x,) + tuple(tabs) + tuple(wts) + tuple(sample)
    step = lambda b, j: b * n_j + j
    in_specs = [
        pl.BlockSpec((1, tt, D_MODEL), lambda b, j: (b, j, 0)),
        pl.BlockSpec((tt, LANES), lambda b, j: (j, 0)),
        pl.BlockSpec((tt, LANES), lambda b, j: (j, 0)),
    ] + [_const_spec(a.shape) for a in args[3:3 + len(tabs) - 2 + len(wts)]] + [
        pl.BlockSpec((nb, RET_HEADS, RET_DK, RET_DV), lambda b, j: (step(b, j), 0, 0, 0)),
        pl.BlockSpec((nb, RET_DIM), lambda b, j: (step(b, j), 0)),
        pl.BlockSpec((nb, RET_DIM), lambda b, j: (step(b, j), 0)),
        pl.BlockSpec((1, RET_HEADS, RET_DK, nb), lambda b, j: (step(b, j), 0, 0, 0)),
        pl.BlockSpec((nb, RET_DIM), lambda b, j: (step(b, j), 0)),
        _const_spec(s_dec.shape),
    ]
    out_shape = (
        jax.ShapeDtypeStruct((bsz, t, D_MODEL), F32),
        jax.ShapeDtypeStruct((bsz, t, D_MODEL), BF16),
        jax.ShapeDtypeStruct((SUBLANES, n_tok), I32),
        jax.ShapeDtypeStruct((SUBLANES, n_tok), F32),
        jax.ShapeDtypeStruct((bsz * n_j * parts, N_EXPERTS, LANES), F32),
        jax.ShapeDtypeStruct((bsz, CONV_WIDTH - 1, CONV_DIM), F32),
        jax.ShapeDtypeStruct((bsz, RET_HEADS, RET_DK, RET_DV), F32),
        jax.ShapeDtypeStruct(s_state.shape, F32),
        jax.ShapeDtypeStruct(s_q.shape, F32),
    )
    out_specs = (
        pl.BlockSpec((1, tt, D_MODEL), lambda b, j: (b, j, 0)),
        pl.BlockSpec((1, tt, D_MODEL), lambda b, j: (b, j, 0)),
        pl.BlockSpec((SUBLANES, tt), lambda b, j: (0, b * n_j + j)),
        pl.BlockSpec((SUBLANES, tt), lambda b, j: (0, b * n_j + j)),
        pl.BlockSpec((parts, N_EXPERTS, LANES), lambda b, j: (b * n_j + j, 0, 0)),
        pl.BlockSpec((1, CONV_WIDTH - 1, CONV_DIM), lambda b, j: (b, 0, 0)),
        pl.BlockSpec((1, RET_HEADS, RET_DK, RET_DV), lambda b, j: (b, 0, 0, 0)),
        pl.BlockSpec((nb, RET_HEADS, RET_DK, RET_DV), lambda b, j: (step(b, j), 0, 0, 0)),
        pl.BlockSpec((nb, RET_DIM), lambda b, j: (step(b, j), 0)),
    )
    return pl.pallas_call(
        _mixer_prompt_kernel,
        grid=(bsz, n_j),
        in_specs=in_specs,
        out_specs=out_specs,
        out_shape=out_shape,
        scratch_shapes=[
            pltpu.VMEM((RET_HEADS, RET_DK, RET_DV), F32),
            pltpu.VMEM((tt + SUBLANES, CONV_DIM), F32),
            pltpu.VMEM((tt, D_MODEL), BF16),
        ],
        compiler_params=_params("arbitrary", "arbitrary"),
        name="mixer_prompt",
    )(*args)


def _sample_proj_kernel(x_ref, sc_ref, cos_ref, sin_ref, gmix_ref, win_ref, convw_ref, retgn_ref,
                        yc_ref, q_ref, k_ref, v_ref, gate_ref, conv_ref):
    x = x_ref[...]
    u = _rms(x, gmix_ref[...]).astype(BF16)
    proj = _dot(u, win_ref[...])
    gb = proj[:, 0:CONV_DIM]
    z = proj[:, CONV_DIM:2 * CONV_DIM] * proj[:, 2 * CONV_DIM:3 * CONV_DIM]
    cw = convw_ref[...]
    buf0, buf1 = sc_ref[:, 0:CONV_DIM], sc_ref[:, CONV_DIM:2 * CONV_DIM]
    yc_ref[...] = gb * (cw[0:1] * buf0 + cw[1:2] * buf1 + cw[2:3] * z)
    conv_ref[:, 0:CONV_DIM] = buf1
    conv_ref[:, CONV_DIM:2 * CONV_DIM] = z
    q0 = 3 * CONV_DIM
    cos, sin = cos_ref[0:1, :], sin_ref[0:1, :]
    retgn = retgn_ref[...]
    for hh in range(RET_HEADS):
        l0 = hh * RET_DK
        q_ref[:, l0:l0 + RET_DK] = _rope(proj[:, q0 + l0:q0 + l0 + RET_DK], cos, sin)
        k_ref[:, l0:l0 + RET_DK] = _rope(proj[:, q0 + RET_DIM + l0:q0 + RET_DIM + l0 + RET_DK], cos, sin) * (RET_DK ** -0.5)
    v_ref[...] = proj[:, q0 + 2 * RET_DIM:q0 + 3 * RET_DIM]
    g = proj[:, q0 + 3 * RET_DIM:q0 + 4 * RET_DIM]
    gate_ref[...] = retgn * (g * jax.nn.sigmoid(g))


def _sample_proj(x, sc, cos, sin, gmix, win, convw, retgn):
    n = x.shape[0]
    outs = (
        jax.ShapeDtypeStruct((n, CONV_DIM), F32), jax.ShapeDtypeStruct((n, RET_DIM), F32),
        jax.ShapeDtypeStruct((n, RET_DIM), F32), jax.ShapeDtypeStruct((n, RET_DIM), F32),
        jax.ShapeDtypeStruct((n, RET_DIM), F32), jax.ShapeDtypeStruct((n, 2 * CONV_DIM), F32),
    )
    args = (x, sc, cos, sin, gmix, win, convw, retgn)
    return pl.pallas_call(
        _sample_proj_kernel,
        grid=(1,),
        in_specs=[_const_spec(a.shape) for a in args],
        out_specs=tuple(_const_spec(o.shape) for o in outs),
        out_shape=outs,
        compiler_params=_params("arbitrary"),
        name="sample_proj",
    )(*args)


def _sample_state_update(s_ref, q_ref, k_ref, kt_ref, v_ref, dec_ref, snew_ref, o_ref, requests):
    for r in requests:
        for hh in range(RET_HEADS):
            l0 = hh * RET_DV
            s = s_ref[r, hh]
            q = q_ref[r:r + 1, l0:l0 + RET_DK]
            k = k_ref[r:r + 1, l0:l0 + RET_DK]
            kc = kt_ref[0, hh, :, r:r + 1]
            vr = v_ref[r:r + 1, l0:l0 + RET_DV]
            qdec = dec_ref[hh:hh + 1, :]
            cdec = dec_ref[RET_HEADS + hh:RET_HEADS + hh + 1, :]
            qk = jnp.sum(q * k, axis=1, keepdims=True)
            o_ref[r:r + 1, l0:l0 + RET_DV] = qk * vr + _dot((q * qdec).astype(BF16), s.astype(BF16))
            snew_ref[r, hh] = s * cdec + kc * vr


def _sample_out_kernel(x_ref, yc_ref, o_ref, gate_ref, wo_ref, gffn_ref, wr_ref, br_ref, tri_ref, ltri_ref,
                       h_ref, xn_ref, ri_ref, rw_ref, n_ref):
    n = x_ref.shape[0]
    parts = [yc_ref[...].astype(BF16)]
    for hh in range(RET_HEADS):
        l0 = hh * RET_DV
        parts.append(_group_norm_gate(o_ref[:, l0:l0 + RET_DV], gate_ref[:, l0:l0 + RET_DV]).astype(BF16))
    h_ref[0:n, :] = x_ref[...] + _dot(jnp.concatenate(parts, axis=-1), wo_ref[...])
    h_ref[n:SAMPLE_ROWS, :] = jnp.zeros((SAMPLE_ROWS - n, D_MODEL), F32)
    for part in range(SAMPLE_ROWS // TOK_TILE):
        p0 = part * TOK_TILE
        ri, rw, cnt, xn = _route(h_ref[p0:p0 + TOK_TILE, :], gffn_ref[...], wr_ref[...], br_ref[...], tri_ref[...],
                                 ltri_ref[...])
        ri_ref[:, p0:p0 + TOK_TILE] = ri
        rw_ref[:, p0:p0 + TOK_TILE] = rw
        n_ref[part] = cnt
        xn_ref[p0:p0 + TOK_TILE, :] = xn


def _sample_out(x, yc, o, gate, wo, gffn, wr, br, tri, ltri):
    outs = (
        jax.ShapeDtypeStruct((SAMPLE_ROWS, D_MODEL), F32), jax.ShapeDtypeStruct((SAMPLE_ROWS, D_MODEL), BF16),
        jax.ShapeDtypeStruct((SUBLANES, SAMPLE_ROWS), I32), jax.ShapeDtypeStruct((SUBLANES, SAMPLE_ROWS), F32),
        jax.ShapeDtypeStruct((SAMPLE_ROWS // TOK_TILE, N_EXPERTS, LANES), F32),
    )
    args = (x, yc, o, gate, wo, gffn, wr, br, tri, ltri)
    return pl.pallas_call(
        _sample_out_kernel,
        grid=(1,),
        in_specs=[_const_spec(a.shape) for a in args],
        out_specs=tuple(_const_spec(o_.shape) for o_ in outs),
        out_shape=outs,
        compiler_params=_params("arbitrary"),
        name="sample_out",
    )(*args)


def _sample_inputs(x, state_conv, state_ret, gmix, win, convw, retgn):
    n = x.shape[0]
    lg = _log_gamma()
    cos, sin = _rope_tables(np.full((SUBLANES,), PAST_LEN))
    yc, q, k, v, gate, conv_new = _sample_proj(x, state_conv.reshape(n, 2 * CONV_DIM), cos, sin, gmix, win, convw, retgn)
    nb = SAMPLE_BLOCK
    cols = lambda a: a.reshape(n // nb, nb, RET_HEADS, RET_DK).transpose(0, 2, 3, 1)
    step = np.exp(lg[:, None] * 1.0)
    dec = jnp.asarray(np.broadcast_to(np.concatenate([step, step], axis=0), (2 * RET_HEADS, LANES)), F32)
    return (yc, gate, conv_new.reshape(n, 2, CONV_DIM)), (state_ret, q, k, cols(k), v, dec)


def _load_rows(ref, n_rows):
    words = [ref[pl.ds(c, n_rows, stride=ROW_CHUNKS), :] for c in range(ROW_CHUNKS)]
    half = lambda i: [pltpu.unpack_elementwise(w, index=i, packed_dtype=BF16, unpacked_dtype=F32) for w in words]
    return jnp.concatenate(half(0) + half(1), axis=-1)


def _store_rows(ref, val):
    half = D_MODEL // 2
    for c in range(ROW_CHUNKS):
        pair = [val[:, c * LANES:(c + 1) * LANES], val[:, half + c * LANES:half + (c + 1) * LANES]]
        ref[pl.ds(c, val.shape[0], stride=ROW_CHUNKS), :] = pltpu.pack_elementwise(pair, packed_dtype=BF16)


def _row_slice(ref, row, rows):
    return ref.at[pl.ds(pl.multiple_of(row * ROW_CHUNKS, ROW_CHUNKS), rows * ROW_CHUNKS)]


def _run_copies(table, make_copy):
    for b in range(RUN_BITS):
        first = N_EXPERTS + N_EXPERTS * b
        count = table(b)

        def start(col, b=b):
            entry = table(col)
            make_copy(entry & ((1 << SRC_BITS) - 1), entry >> SRC_BITS, 1 << b).start()

        unroll = RUN_UNROLL if (1 << b) <= PAIR_ROWS // N_EXPERTS else 1
        for r in range(unroll - 1):
            @pl.when(r < count % unroll)
            def _():
                start(first + count - 1 - r)

        def body(j, carry, first=first, start=start, unroll=unroll):
            for u in range(unroll):
                start(first + unroll * j + u)
            return carry

        lax.fori_loop(0, count // unroll, body, 0)


def _pair_rows_iota():
    return lax.broadcasted_iota(I32, (PAIR_ROWS, TOK_TILE), 0)


def _dispatch_kernel(n_p, tab_ref, xp_ref, xs_in_ref, ri_ref, xs_ref, idx_smem, y_scr, isem, rsem):
    i, n = pl.program_id(0), pl.num_programs(0)
    slot = i % 2
    step_rows = STEP_TILES * PAIR_ROWS

    def idx_copy(step, sl):
        return pltpu.make_async_copy(tab_ref.at[pl.ds(step * STEP_TILES * TAB_COLS, STEP_TILES * TAB_COLS)],
                                     idx_smem.at[pl.ds(sl * STEP_TILES * TAB_COLS, STEP_TILES * TAB_COLS)], isem.at[sl])

    def wait_step(sl):
        pltpu.make_async_copy(_row_slice(xs_ref, 0, step_rows), y_scr.at[sl], rsem.at[sl]).wait()

    @pl.when(i == 0)
    def _():
        idx_copy(0, 0).start()

    @pl.when(i + 1 < n)
    def _():
        idx_copy(i + 1, 1 - slot).start()

    @pl.when(i >= 2)
    def _():
        wait_step(slot)

    xn = jnp.where(i < n_p, xp_ref[...], xs_in_ref[...])
    ri = ri_ref[...]
    rows = _pair_rows_iota()
    for g in range(STEP_TILES):
        tok = slice(g * TOK_TILE, (g + 1) * TOK_TILE)
        perm = jnp.where((rows == ri[2:3, tok]) | (rows == ri[3:4, tok]), 1.0, 0.0).astype(BF16)
        y = _dot(perm, xn[tok, :])
        _store_rows(y_scr.at[slot, pl.ds(g * PAIR_ROWS * ROW_CHUNKS, PAIR_ROWS * ROW_CHUNKS)], y)

    idx_copy(i, slot).wait()
    for g in range(STEP_TILES):
        tab0 = (slot * STEP_TILES + g) * TAB_COLS
        _run_copies(lambda col, tab0=tab0: idx_smem[tab0 + col], lambda src, dst, rows_, g=g: pltpu.make_async_copy(
            _row_slice(y_scr.at[slot], g * PAIR_ROWS + src, rows_), _row_slice(xs_ref, dst, rows_), rsem.at[slot]))

    @pl.when(i == n - 1)
    def _():
        wait_step(slot)

    @pl.when((i == n - 1) & (n >= 2))
    def _():
        wait_step(1 - slot)


def _dispatch(tab, xn_p, xn_s, ri):
    tt = STEP_TILES * TOK_TILE
    n_p = xn_p.shape[0] // tt
    n = tab.shape[0] // STEP_TILES
    return pl.pallas_call(
        functools.partial(_dispatch_kernel, n_p),
        grid=(n,),
        in_specs=[
            pl.BlockSpec(memory_space=pl.ANY),
            pl.BlockSpec((tt, D_MODEL), lambda i: (jnp.minimum(i, n_p - 1), 0)),
            pl.BlockSpec((tt, D_MODEL), lambda i: (jnp.maximum(i - n_p, 0), 0)),
            pl.BlockSpec((SUBLANES, tt), lambda i: (0, i)),
        ],
        out_specs=pl.BlockSpec(memory_space=pl.ANY),
        out_shape=jax.ShapeDtypeStruct((tab.shape[0] * PAIR_ROWS * ROW_CHUNKS, LANES), U32),
        scratch_shapes=[
            pltpu.SMEM((2 * STEP_TILES * TAB_COLS,), I32),
            pltpu.VMEM((2, STEP_TILES * PAIR_ROWS * ROW_CHUNKS, LANES), U32),
            pltpu.SemaphoreType.DMA((2,)),
            pltpu.SemaphoreType.DMA((2,)),
        ],
        compiler_params=_params("arbitrary"),
        name="moe_dispatch",
    )(tab.reshape(-1), xn_p, xn_s, ri)


def _ffn_kernel(tile_ref, exp_ref, lo_ref, hi_ref, nxt_ref, slot_ref, xs_ref, wg_ref, wu_ref, wd_ref, o_ref,
                x_buf, wg_buf, wu_buf, wd_buf, wgu_scr, wd_scr, y_scr, xsem, wsem):
    w, n_w = pl.program_id(0), pl.num_programs(0)
    prev = jnp.maximum(w - 1, 0)
    lo, hi = lo_ref[w], hi_ref[w]
    e, slot = exp_ref[w], slot_ref[w]
    first = (w == 0) | (tile_ref[w] != tile_ref[prev])

    ahead = FFN_RING - 1

    def x_copy(step):
        rows = FFN_TILE * ROW_CHUNKS
        return pltpu.make_async_copy(xs_ref.at[pl.ds(pl.multiple_of(tile_ref[step] * rows, rows), rows)],
                                     x_buf.at[step % FFN_RING], xsem.at[step % FFN_RING])

    for step in range(ahead):
        @pl.when((w == 0) & (step < n_w))
        def _():
            x_copy(step).start()

    @pl.when(w + ahead < n_w)
    def _():
        x_copy(w + ahead).start()

    x_copy(w).wait()

    def weight_copies(expert, sl):
        return [pltpu.make_async_copy(src.at[expert], buf.at[sl], wsem.at[sl])
                for src, buf in ((wg_ref, wg_buf), (wu_ref, wu_buf), (wd_ref, wd_buf))]

    @pl.when(w == 0)
    def _():
        for cp in weight_copies(e, slot):
            cp.start(priority=1)

    @pl.when((w == 0) | (e != exp_ref[prev]))
    def _():
        for cp in weight_copies(e, slot):
            cp.wait()

        @pl.when(nxt_ref[w] != e)
        def _():
            for cp in weight_copies(nxt_ref[w], 1 - slot):
                cp.start(priority=1)

        wgu_scr[:, 0:D_FF] = wg_buf[slot].astype(BF16)
        wgu_scr[:, D_FF:2 * D_FF] = wu_buf[slot].astype(BF16)
        wd_scr[...] = wd_buf[slot].astype(BF16)

    @pl.when(w == 0)
    def _():
        y_scr[...] = jnp.zeros_like(y_scr)

    def compute(r0, n_rows):
        chunk = pl.ds(r0 * ROW_CHUNKS, n_rows * ROW_CHUNKS)
        gu = _dot(_load_rows(x_buf.at[w % FFN_RING, chunk], n_rows).astype(BF16), wgu_scr[...])
        g, u = gu[:, 0:D_FF], gu[:, D_FF:2 * D_FF]
        y = _dot((g * jax.nn.sigmoid(g) * u).astype(BF16), wd_scr[...])
        row = r0 + lax.broadcasted_iota(I32, (n_rows, 1), 0)
        y = jnp.where(((row >= lo) & (row < hi)) | first, y, y_scr[r0:r0 + n_rows, :])
        y_scr[r0:r0 + n_rows, :] = y
        _store_rows(o_ref.at[chunk], y)

    half = FFN_TILE // 2
    top, bottom = hi <= half, lo >= half

    @pl.when((hi > lo) & top)
    def _():
        compute(0, half)

    @pl.when((hi > lo) & bottom)
    def _():
        compute(half, half)

    @pl.when((hi > lo) & jnp.logical_not(top | bottom))
    def _():
        compute(0, FFN_TILE)


def _expert_ffn(items, xs, w_gate, w_up, w_down):
    n_items = items[0].shape[0]
    row_spec = pl.BlockSpec((FFN_TILE * ROW_CHUNKS, LANES), lambda w, t, *_: (t[w], 0))
    grid_spec = pltpu.PrefetchScalarGridSpec(
        num_scalar_prefetch=len(items),
        grid=(n_items,),
        in_specs=[pl.BlockSpec(memory_space=pl.ANY)] * 4,
        out_specs=row_spec,
        scratch_shapes=[
            pltpu.VMEM((FFN_RING, FFN_TILE * ROW_CHUNKS, LANES), U32),
            pltpu.VMEM((2, D_MODEL, D_FF), F32), pltpu.VMEM((2, D_MODEL, D_FF), F32),
            pltpu.VMEM((2, D_FF, D_MODEL), F32),
            pltpu.VMEM((D_MODEL, 2 * D_FF), BF16), pltpu.VMEM((D_FF, D_MODEL), BF16),
            pltpu.VMEM((FFN_TILE, D_MODEL), F32),
            pltpu.SemaphoreType.DMA((FFN_RING,)),
            pltpu.SemaphoreType.DMA((2,)),
        ],
    )
    return pl.pallas_call(
        _ffn_kernel,
        grid_spec=grid_spec,
        out_shape=jax.ShapeDtypeStruct(xs.shape, U32),
        compiler_params=_params("arbitrary"),
        name="moe_ffn",
    )(*items, xs, w_gate, w_up, w_down)


def _work_items(counts, n_rows):
    n_tiles = n_rows // FFN_TILE
    n_items = n_tiles + N_EXPERTS - 1
    off = jnp.cumsum(counts) - counts
    first_tile = off // FFN_TILE
    last_tile = jnp.maximum(off + counts - 1, off) // FFN_TILE
    n_e = jnp.where(counts > 0, last_tile - first_tile + 1, 0)
    start = jnp.cumsum(n_e) - n_e
    total = jnp.sum(n_e)
    w = jnp.minimum(jnp.arange(n_items, dtype=I32), total - 1)
    ids = jnp.arange(N_EXPERTS, dtype=I32)[None, :]
    e = jnp.max(jnp.where((start[None, :] <= w[:, None]) & (n_e[None, :] > 0), ids, 0), axis=1)
    pick = lambda a: jnp.sum(jnp.where(ids == e[:, None], a[None, :], 0), axis=1)
    tile = pick(first_tile) + (w - pick(start))
    lo = jnp.clip(pick(off) - tile * FFN_TILE, 0, FFN_TILE)
    hi = jnp.clip(pick(off + counts) - tile * FFN_TILE, 0, FFN_TILE)
    hi = jnp.where(jnp.arange(n_items) < total, hi, lo)
    used = n_e[None, :] > 0
    nxt = jnp.min(jnp.where(used & (ids > e[:, None]), ids, N_EXPERTS), axis=1)
    nxt = jnp.where(nxt == N_EXPERTS, e, nxt)
    slot = jnp.sum(jnp.where(used & (ids < e[:, None]), 1, 0), axis=1) % 2
    return tuple(a.astype(I32) for a in (tile, e, lo, hi, nxt, slot))


def _run_table(n):
    tiles = n.shape[0]
    counts = jnp.sum(n, axis=0)
    goff = jnp.cumsum(counts) - counts
    dst0 = goff[None, :] + jnp.cumsum(n, axis=0) - n
    src0 = jnp.cumsum(n, axis=1) - n
    bits = jnp.arange(RUN_BITS, dtype=I32)[:, None, None]
    flag = (n[None] >> bits) & 1
    done = (n[None] >> (bits + 1)) << (bits + 1)
    entry = ((dst0[None] + done) << SRC_BITS) | (src0[None] + done)
    rank = jnp.cumsum(flag, axis=2) - flag
    slot = jnp.arange(N_EXPERTS, dtype=I32)
    sel = (flag[..., None] == 1) & (rank[..., None] == slot)
    packed = jnp.sum(jnp.where(sel, entry[..., None], 0), axis=2)
    m = jnp.sum(flag, axis=2).T
    tab = jnp.zeros((tiles, TAB_COLS), I32)
    tab = tab.at[:, :RUN_BITS].set(m)
    tab = tab.at[:, N_EXPERTS:N_EXPERTS * (RUN_BITS + 1)].set(packed.transpose(1, 0, 2).reshape(tiles, -1))
    return counts, tab


def _combine_kernel(g_tiles, n, tab_ref, h_ref, p_ref, ri_ref, rw_ref, ys_ref, gple_ref, wpg_ref, wpp_ref, gfin_ref,
                    o_ref, idx_smem, rows_scr, isem, rsem):
    i = pl.program_id(0)
    slot = i % 2

    def idx_copy(step, sl):
        return pltpu.make_async_copy(tab_ref.at[pl.ds(step * g_tiles * TAB_COLS, g_tiles * TAB_COLS)],
                                     idx_smem.at[pl.ds(sl * g_tiles * TAB_COLS, g_tiles * TAB_COLS)], isem.at[sl])

    def gather(sl):
        for g in range(g_tiles):
            tab0 = (sl * g_tiles + g) * TAB_COLS
            _run_copies(lambda col, tab0=tab0: idx_smem[tab0 + col], lambda src, dst, rows_, g=g: pltpu.make_async_copy(
                _row_slice(ys_ref, dst, rows_), _row_slice(rows_scr.at[sl], g * PAIR_ROWS + src, rows_), rsem.at[sl]))

    @pl.when(i == 0)
    def _():
        idx_copy(0, 0).start()
        idx_copy(0, 0).wait()
        gather(0)

    if n >= 2:
        @pl.when(i == 0)
        def _():
            idx_copy(1, 1).start()

        @pl.when(i + 1 < n)
        def _():
            idx_copy(i + 1, 1 - slot).wait()
            gather(1 - slot)

    @pl.when(i + 2 < n)
    def _():
        idx_copy(i + 2, slot).start()

    pltpu.make_async_copy(_row_slice(ys_ref, 0, g_tiles * PAIR_ROWS), rows_scr.at[slot], rsem.at[slot]).wait()

    ri, rw = ri_ref[...], rw_ref[...]
    rows = _pair_rows_iota()
    tiles = range(g_tiles)
    tok = lambda g: slice(g * TOK_TILE, (g + 1) * TOK_TILE)
    h = []
    for g in tiles:
        r1, r2 = ri[2:3, tok(g)], ri[3:4, tok(g)]
        wperm = jnp.where(rows == r1, rw[0:1, tok(g)], 0.0) + jnp.where(rows == r2, rw[1:2, tok(g)], 0.0)
        srows = _load_rows(rows_scr.at[slot, pl.ds(g * PAIR_ROWS * ROW_CHUNKS, PAIR_ROWS * ROW_CHUNKS)], PAIR_ROWS)
        h.append(h_ref[tok(g), :] + _dot_tn(wperm.astype(BF16), srows.astype(BF16)))
    gate = [jax.nn.sigmoid(_dot(_rms(h[g], gple_ref[...]).astype(BF16), wpg_ref[...])) for g in tiles]
    for g in tiles:
        hg = h[g] + _dot(p_ref[tok(g), :].astype(BF16), wpp_ref[...]) * gate[g]
        o_ref[tok(g), :] = _rms(hg, gfin_ref[...])


def _combine(tab, h, p, ri, rw, ys, gple, wpg, wpp, gfin):
    n_tiles = tab.shape[0]
    g_tiles = STEP_TILES
    n = n_tiles // g_tiles
    tt = g_tiles * TOK_TILE
    return pl.pallas_call(
        functools.partial(_combine_kernel, g_tiles, n),
        grid=(n,),
        in_specs=[
            pl.BlockSpec(memory_space=pl.ANY),
            pl.BlockSpec((tt, D_MODEL), lambda i: (i, 0)),
            pl.BlockSpec((tt, PLE_DIM), lambda i: (i, 0)),
            pl.BlockSpec((SUBLANES, tt), lambda i: (0, i)),
            pl.BlockSpec((SUBLANES, tt), lambda i: (0, i)),
            pl.BlockSpec(memory_space=pl.ANY),
            _const_spec(gple.shape), _const_spec(wpg.shape), _const_spec(wpp.shape), _const_spec(gfin.shape),
        ],
        out_specs=pl.BlockSpec((tt, D_MODEL), lambda i: (i, 0)),
        out_shape=jax.ShapeDtypeStruct((n_tiles * TOK_TILE, D_MODEL), F32),
        scratch_shapes=[
            pltpu.SMEM((2 * g_tiles * TAB_COLS,), I32),
            pltpu.VMEM((2, g_tiles * PAIR_ROWS * ROW_CHUNKS, LANES), U32),
            pltpu.SemaphoreType.DMA((2,)),
            pltpu.SemaphoreType.DMA((2,)),
        ],
        compiler_params=_params("arbitrary"),
        name="moe_combine",
    )(tab.reshape(-1), h, p, ri, rw, ys, gple, wpg, wpp, gfin)


def _log_gamma():
    return np.log(1.0 - 2.0 ** (-5.0 - np.arange(RET_HEADS, dtype=np.float64)))


def _rope_tables(pos):
    inv = 1.0 / (ROPE_BASE ** (np.arange(0, RET_DK, 2, dtype=np.float64) / RET_DK))
    ang = np.asarray(pos, np.float64)[:, None] * inv[None, :]
    cos, sin = np.cos(ang), np.sin(ang)
    return (jnp.asarray(np.concatenate([cos, cos], axis=-1), F32),
            jnp.asarray(np.concatenate([-sin, sin], axis=-1), F32))


def _decay_tables(c):
    lg = _log_gamma()
    idx = np.arange(c, dtype=np.float64)
    diff = idx[:, None] - idx[None, :]
    dmask = np.where((diff >= 0.0)[None], np.exp(lg[:, None, None] * np.maximum(diff, 0.0)[None]), 0.0)
    q_dec = np.exp(lg[None, :] * (idx[:, None] + 1.0))
    k_dec = np.exp(lg[None, :] * (c - 1.0 - idx[:, None]))
    c_dec = np.exp(lg * c)
    lanes = lambda a: np.repeat(a, RET_DK, axis=1)
    cdec = np.zeros((SUBLANES, LANES))
    cdec[:RET_HEADS] = c_dec[:, None]
    return tuple(jnp.asarray(a, F32) for a in (lanes(q_dec), lanes(k_dec), dmask, cdec))


def _router_params(w_rg, b_rg, w_re, b_re):
    wr = jnp.zeros((ROUTER_ROWS, D_MODEL), F32).at[:N_GROUPS].set(w_rg.T).at[SUBLANES:].set(w_re.T)
    br = jnp.full((ROUTER_ROWS,), NEG_BIG, F32).at[:N_GROUPS].set(b_rg).at[SUBLANES:].set(b_re.reshape(-1))
    return wr.astype(BF16), jnp.broadcast_to(br[:, None], (ROUTER_ROWS, LANES))


def _strict_upper(t):
    i = np.arange(t)
    return jnp.asarray(i[:, None] < i[None, :], BF16)


def _pad_rows(a, rows):
    return jnp.zeros((rows,) + a.shape[1:], a.dtype).at[:a.shape[0]].set(a)


def kernel(x_prompt, x_sample, state_conv, state_ret, p_prompt, p_sample, g_mix, w_in, conv_w, ret_gn, w_o, g_ffn,
           w_router_group, b_router_group, w_router_expert, b_router_expert, w_gate, w_up, w_down, g_ple,
           w_ple_proj, w_ple_gate, g_final):
    bp, tp, _ = x_prompt.shape
    bs = x_sample.shape[0]
    assert x_sample.shape[1] == 1 and g_mix.shape[0] == 1, "one layer, one new token per sample request"
    assert bs < SAMPLE_ROWS and tp % MIX_TILE == 0
    n_p = bp * tp

    tabs = _rope_tables(np.arange(tp)) + _decay_tables(TOK_TILE)
    wr, br = _router_params(w_router_group[0], b_router_group[0], w_router_expert[0], b_router_expert[0])
    win, wo, convw = w_in[0].astype(BF16), w_o[0].astype(BF16), _pad_rows(conv_w[0], SUBLANES)
    tri, ltri = _strict_upper(TOK_TILE), _strict_upper(N_EXPERTS).T

    xs0 = x_sample[:, 0]
    (yc_s, gate_s, conv_s), sample = _sample_inputs(xs0, state_conv[0], state_ret[0], g_mix, win, convw, ret_gn)
    h_p, xn_p, ri_p, rw_p, n_p_tab, conv_p, ret_p, ret_s, o_s = _mixer_prompt(
        x_prompt, tabs, (g_mix, win, convw, ret_gn, wo, g_ffn, wr, br, tri, ltri), sample)
    h_s, xn_s, ri_s, rw_s, n_s_tab = _sample_out(xs0, yc_s, o_s, gate_s, wo, g_ffn, wr, br, tri, ltri)
    h_p, xn_p = h_p.reshape(n_p, D_MODEL), xn_p.reshape(n_p, D_MODEL)

    n_tab = jnp.concatenate([n_p_tab, n_s_tab], axis=0)[:, :, 0].astype(I32)
    counts, tab = _run_table(n_tab)
    ri = jnp.concatenate([ri_p, ri_s], axis=1)
    n_tiles = n_tab.shape[0]

    xs = _dispatch(tab, xn_p, xn_s, ri)
    ys = _expert_ffn(_work_items(counts, n_tiles * PAIR_ROWS), xs, w_gate[0], w_up[0], w_down[0])

    wpg, wpp = w_ple_gate[0].astype(BF16), w_ple_proj[0].astype(BF16)
    gfin = g_final[None, :]
    p_s = _pad_rows(p_sample[0].reshape(bs, PLE_DIM), SAMPLE_ROWS)
    n_s_tiles = SAMPLE_ROWS // TOK_TILE
    y_p = _combine(tab[:n_tiles - n_s_tiles], h_p, p_prompt[0].reshape(n_p, PLE_DIM), ri_p, rw_p, ys, g_ple, wpg, wpp, gfin)
    y_s = _combine(tab[n_tiles - n_s_tiles:], h_s, p_s, ri_s, rw_s, ys, g_ple, wpg, wpp, gfin)
    return (y_p.reshape(bp, tp, D_MODEL), y_s[:bs].reshape(bs, 1, D_MODEL), conv_p[None], ret_p[None],
            conv_s[None], ret_s[None])
```

```python
import functools

import jax
import jax.numpy as jnp
import numpy as np
from jax import lax
from jax.experimental import pallas as pl
from jax.experimental.pallas import tpu as pltpu

F32, BF16, I32, U32 = jnp.float32, jnp.bfloat16, jnp.int32, jnp.uint32

D_MODEL = 1024
CONV_DIM = 512
CONV_WIDTH = 3
RET_DIM = 512
RET_HEADS = 4
RET_DK = 128
RET_DV = 128
ROPE_BASE = 10000.0
N_GROUPS = 4
EXPERTS_PER_GROUP = 8
N_EXPERTS = 32
D_FF = 512
PLE_DIM = 256
EPS = 1e-6
PAST_LEN = 16384

LANES = 128
SUBLANES = 8
ROW_CHUNKS = D_MODEL // (2 * LANES)
ROUTER_ROWS = SUBLANES + N_EXPERTS
VMEM_LIMIT = 56 * 1024 * 1024
NEG_BIG = -1e30

TOK_TILE = 256
MIX_TILE = 1024
PAIR_ROWS = 2 * TOK_TILE
STEP_TILES = 2
SAMPLE_ROWS = STEP_TILES * TOK_TILE
FFN_TILE = 512
FFN_RING = 3
SAMPLE_BLOCK = 8
RUN_BITS = 9
SRC_BITS = 10
RUN_UNROLL = 4
TAB_COLS = 384


def _rms(x, g):
    return x * lax.rsqrt(jnp.mean(x * x, axis=-1, keepdims=True) + EPS) * g


def _dot(a, b):
    return jnp.dot(a, b, preferred_element_type=F32)


def _dot_nt(a, b):
    return lax.dot_general(a, b, (((1,), (1,)), ((), ())), preferred_element_type=F32)


def _dot_tn(a, b):
    return lax.dot_general(a, b, (((0,), (0,)), ((), ())), preferred_element_type=F32)


def _rope(x, cos, sin_signed):
    return x * cos + pltpu.roll(x, RET_DK // 2, 1) * sin_signed


def _const_spec(shape):
    nd = len(shape)
    return pl.BlockSpec(shape, lambda *_: (0,) * nd)


def _params(*sem):
    return pltpu.CompilerParams(dimension_semantics=sem, vmem_limit_bytes=VMEM_LIMIT)


def _route(h, gffn, wr, br, tri, ltri):
    t = h.shape[0]
    xn = _rms(h, gffn).astype(BF16)
    lt = _dot_nt(wr, xn) + br[:, 0:1]
    row8 = lax.broadcasted_iota(I32, (SUBLANES, t), 0).astype(F32)
    gl = lt[0:SUBLANES]
    m = jnp.max(gl, axis=0, keepdims=True)
    g_top = 1.0 / jnp.sum(jnp.exp(gl - m), axis=0, keepdims=True)
    gidx = jnp.min(jnp.where(gl == m, row8, float(SUBLANES)), axis=0, keepdims=True)
    e_sel = jnp.where(gidx == 0.0, lt[8:16],
                      jnp.where(gidx == 1.0, lt[16:24], jnp.where(gidx == 2.0, lt[24:32], lt[32:40])))
    m1 = jnp.max(e_sel, axis=0, keepdims=True)
    i1 = jnp.min(jnp.where(e_sel == m1, row8, float(SUBLANES)), axis=0, keepdims=True)
    rest = jnp.where(row8 == i1, -jnp.inf, e_sel)
    m2 = jnp.max(rest, axis=0, keepdims=True)
    i2 = jnp.min(jnp.where(rest == m2, row8, float(SUBLANES)), axis=0, keepdims=True)
    d = jnp.exp(m2 - m1)
    w1 = g_top / (1.0 + d)
    w2 = g_top * d / (1.0 + d)
    e1 = gidx * float(EXPERTS_PER_GROUP) + i1
    e2 = gidx * float(EXPERTS_PER_GROUP) + i2
    row32 = lax.broadcasted_iota(I32, (N_EXPERTS, t), 0).astype(F32)
    a1 = jnp.where(row32 == e1, 1.0, 0.0)
    a2 = jnp.where(row32 == e2, 1.0, 0.0)
    a = a1 + a2
    n = jnp.broadcast_to(jnp.sum(a, axis=1, keepdims=True), (N_EXPERTS, LANES))
    start = _dot(ltri, n.astype(BF16))
    base = _dot(a.astype(BF16), tri) + start[:, 0:1]
    r1 = jnp.sum(a1 * base, axis=0, keepdims=True)
    r2 = jnp.sum(a2 * base, axis=0, keepdims=True)
    ri = jnp.where(row8 == 0.0, e1, jnp.where(row8 == 1.0, e2, jnp.where(row8 == 2.0, r1,
                                                                         jnp.where(row8 == 3.0, r2, 0.0))))
    rw = jnp.where(row8 == 0.0, w1, jnp.where(row8 == 1.0, w2, 0.0))
    return ri.astype(I32), rw, n, xn


def _group_norm_gate(o, gate):
    mu = jnp.mean(o, axis=-1, keepdims=True)
    oc = o - mu
    return oc * lax.rsqrt(jnp.mean(oc * oc, axis=-1, keepdims=True) + EPS) * gate


def _mixer_prompt_kernel(x_ref, cos_ref, sin_ref, qdec_ref, kdec_ref, dmask_ref, cdec_ref, gmix_ref, win_ref,
                         convw_ref, retgn_ref, wo_ref, gffn_ref, wr_ref, br_ref, tri_ref, ltri_ref,
                         ss_ref, sq_ref, sk_ref, skt_ref, sv_ref, sdec_ref,
                         h_ref, xn_ref, ri_ref, rw_ref, n_ref, conv_ref, ret_ref, snew_ref, so_ref,
                         s_scr, z_scr, mix_scr):
    j = pl.program_id(1)
    n_j = pl.num_programs(1)
    tt = x_ref.shape[1]

    @pl.when(j == 0)
    def _():
        s_scr[...] = jnp.zeros_like(s_scr)
        z_scr[0:SUBLANES, :] = jnp.zeros((SUBLANES, CONV_DIM), F32)

    cw = convw_ref[...]
    retgn = retgn_ref[...]
    q0 = 3 * CONV_DIM

    def project(p0):
        x = x_ref[0, p0:p0 + TOK_TILE, :]
        proj = _dot(_rms(x, gmix_ref[...]).astype(BF16), win_ref[...])
        gb = proj[:, 0:CONV_DIM]
        z = proj[:, CONV_DIM:2 * CONV_DIM] * proj[:, 2 * CONV_DIM:3 * CONV_DIM]
        z0 = SUBLANES + p0
        z_scr[z0:z0 + TOK_TILE, :] = z
        yc = cw[0:1] * z_scr[z0 - 2:z0 - 2 + TOK_TILE, :] + cw[1:2] * z_scr[z0 - 1:z0 - 1 + TOK_TILE, :] + cw[2:3] * z
        mix_scr[p0:p0 + TOK_TILE, 0:CONV_DIM] = (gb * yc).astype(BF16)
        return x, proj

    def mix(p0, x, proj):
        cos = cos_ref[p0:p0 + TOK_TILE, :]
        sin = sin_ref[p0:p0 + TOK_TILE, :]
        heads = range(RET_HEADS)
        lane = lambda hh: slice(hh * RET_DK, (hh + 1) * RET_DK)
        qr = [_rope(proj[:, q0 + hh * RET_DK:q0 + (hh + 1) * RET_DK], cos, sin) for hh in heads]
        kr = [_rope(proj[:, q0 + RET_DIM + hh * RET_DK:q0 + RET_DIM + (hh + 1) * RET_DK], cos, sin) * (RET_DK ** -0.5)
              for hh in heads]
        v = [proj[:, q0 + 2 * RET_DIM + hh * RET_DV:q0 + 2 * RET_DIM + (hh + 1) * RET_DV].astype(BF16) for hh in heads]
        s_old = [s_scr[hh] for hh in heads]
        scores = [_dot_nt(qr[hh].astype(BF16), kr[hh].astype(BF16)) * dmask_ref[hh] for hh in heads]
        o = []
        for hh in heads:
            lhs = jnp.concatenate([scores[hh].astype(BF16), (qr[hh] * qdec_ref[:, lane(hh)]).astype(BF16)], axis=1)
            o.append(_dot(lhs, jnp.concatenate([v[hh], s_old[hh].astype(BF16)], axis=0)))
        for hh in heads:
            kd = (kr[hh] * kdec_ref[:, lane(hh)]).T.astype(BF16)
            s_scr[hh] = s_old[hh] * cdec_ref[hh:hh + 1, :] + _dot(kd, v[hh])
        for hh in heads:
            g = proj[:, q0 + 3 * RET_DIM + hh * RET_DV:q0 + 3 * RET_DIM + (hh + 1) * RET_DV]
            gate = retgn[:, lane(hh)] * (g * jax.nn.sigmoid(g))
            mix_scr[p0:p0 + TOK_TILE, CONV_DIM + hh * RET_DV:CONV_DIM + (hh + 1) * RET_DV] = (
                _group_norm_gate(o[hh], gate).astype(BF16))
        h = x + _dot(mix_scr[p0:p0 + TOK_TILE, :], wo_ref[...])
        h_ref[0, p0:p0 + TOK_TILE, :] = h
        return h

    def route(part, h):
        p0 = part * TOK_TILE
        ri, rw, n, xn = _route(h, gffn_ref[...], wr_ref[...], br_ref[...], tri_ref[...], ltri_ref[...])
        ri_ref[:, p0:p0 + TOK_TILE] = ri
        rw_ref[:, p0:p0 + TOK_TILE] = rw
        n_ref[part] = n
        xn_ref[0, p0:p0 + TOK_TILE, :] = xn

    parts = tt // TOK_TILE
    nxt = project(0)
    per_part = SAMPLE_BLOCK // parts
    for part in range(parts):
        _sample_state_update(ss_ref, sq_ref, sk_ref, skt_ref, sv_ref, sdec_ref, snew_ref, so_ref,
                             range(part * per_part, (part + 1) * per_part))
        h = mix(part * TOK_TILE, *nxt)
        if part + 1 < parts:
            nxt = project((part + 1) * TOK_TILE)
        route(part, h)

    @pl.when(j == n_j - 1)
    def _():
        conv_ref[0] = z_scr[SUBLANES + tt - 2:SUBLANES + tt, :]
        ret_ref[0] = s_scr[...]

    z_scr[0:SUBLANES, :] = z_scr[tt:tt + SUBLANES, :]


def _mixer_prompt(x, tabs, wts, sample):
    bsz, t, _ = x.shape
    tt = MIX_TILE
    n_j = t // tt
    parts = tt // TOK_TILE
    n_tok = bsz * t
    nb = SAMPLE_BLOCK
    s_state, s_q, s_k, s_kt, s_v, s_dec = sample
    assert s_state.shape[0] == nb * bsz * n_j, "one block of sample requests per mixer grid step"
    args = (x,) + tuple(tabs) + tuple(wts) + tuple(sample)
    step = lambda b, j: b * n_j + j
    in_specs = [
        pl.BlockSpec((1, tt, D_MODEL), lambda b, j: (b, j, 0)),
        pl.BlockSpec((tt, LANES), lambda b, j: (j, 0)),
        pl.BlockSpec((tt, LANES), lambda b, j: (j, 0)),
    ] + [_const_spec(a.shape) for a in args[3:3 + len(tabs) - 2 + len(wts)]] + [
        pl.BlockSpec((nb, RET_HEADS, RET_DK, RET_DV), lambda b, j: (step(b, j), 0, 0, 0)),
        pl.BlockSpec((nb, RET_DIM), lambda b, j: (step(b, j), 0)),
        pl.BlockSpec((nb, RET_DIM), lambda b, j: (step(b, j), 0)),
        pl.BlockSpec((1, RET_HEADS, RET_DK, nb), lambda b, j: (step(b, j), 0, 0, 0)),
        pl.BlockSpec((nb, RET_DIM), lambda b, j: (step(b, j), 0)),
        _const_spec(s_dec.shape),
    ]
    out_shape = (
        jax.ShapeDtypeStruct((bsz, t, D_MODEL), F32),
        jax.ShapeDtypeStruct((bsz, t, D_MODEL), BF16),
        jax.ShapeDtypeStruct((SUBLANES, n_tok), I32),
        jax.ShapeDtypeStruct((SUBLANES, n_tok), F32),
        jax.ShapeDtypeStruct((bsz * n_j * parts, N_EXPERTS, LANES), F32),
        jax.ShapeDtypeStruct((bsz, CONV_WIDTH - 1, CONV_DIM), F32),
        jax.ShapeDtypeStruct((bsz, RET_HEADS, RET_DK, RET_DV), F32),
        jax.ShapeDtypeStruct(s_state.shape, F32),
        jax.ShapeDtypeStruct(s_q.shape, F32),
    )
    out_specs = (
        pl.BlockSpec((1, tt, D_MODEL), lambda b, j: (b, j, 0)),
        pl.BlockSpec((1, tt, D_MODEL), lambda b, j: (b, j, 0)),
        pl.BlockSpec((SUBLANES, tt), lambda b, j: (0, b * n_j + j)),
        pl.BlockSpec((SUBLANES, tt), lambda b, j: (0, b * n_j + j)),
        pl.BlockSpec((parts, N_EXPERTS, LANES), lambda b, j: (b * n_j + j, 0, 0)),
        pl.BlockSpec((1, CONV_WIDTH - 1, CONV_DIM), lambda b, j: (b, 0, 0)),
        pl.BlockSpec((1, RET_HEADS, RET_DK, RET_DV), lambda b, j: (b, 0, 0, 0)),
        pl.BlockSpec((nb, RET_HEADS, RET_DK, RET_DV), lambda b, j: (step(b, j), 0, 0, 0)),
        pl.BlockSpec((nb, RET_DIM), lambda b, j: (step(b, j), 0)),
    )
    return pl.pallas_call(
        _mixer_prompt_kernel,
        grid=(bsz, n_j),
        in_specs=in_specs,
        out_specs=out_specs,
        out_shape=out_shape,
        scratch_shapes=[
            pltpu.VMEM((RET_HEADS, RET_DK, RET_DV), F32),
            pltpu.VMEM((tt + SUBLANES, CONV_DIM), F32),
            pltpu.VMEM((tt, D_MODEL), BF16),
        ],
        compiler_params=_params("arbitrary", "arbitrary"),
        name="mixer_prompt",
    )(*args)


def _sample_proj_kernel(x_ref, sc_ref, cos_ref, sin_ref, gmix_ref, win_ref, convw_ref, retgn_ref,
                        yc_ref, q_ref, k_ref, v_ref, gate_ref, conv_ref):
    x = x_ref[...]
    u = _rms(x, gmix_ref[...]).astype(BF16)
    proj = _dot(u, win_ref[...])
    gb = proj[:, 0:CONV_DIM]
    z = proj[:, CONV_DIM:2 * CONV_DIM] * proj[:, 2 * CONV_DIM:3 * CONV_DIM]
    cw = convw_ref[...]
    buf0, buf1 = sc_ref[:, 0:CONV_DIM], sc_ref[:, CONV_DIM:2 * CONV_DIM]
    yc_ref[...] = gb * (cw[0:1] * buf0 + cw[1:2] * buf1 + cw[2:3] * z)
    conv_ref[:, 0:CONV_DIM] = buf1
    conv_ref[:, CONV_DIM:2 * CONV_DIM] = z
    q0 = 3 * CONV_DIM
    cos, sin = cos_ref[0:1, :], sin_ref[0:1, :]
    retgn = retgn_ref[...]
    for hh in range(RET_HEADS):
        l0 = hh * RET_DK
        q_ref[:, l0:l0 + RET_DK] = _rope(proj[:, q0 + l0:q0 + l0 + RET_DK], cos, sin)
        k_ref[:, l0:l0 + RET_DK] = _rope(proj[:, q0 + RET_DIM + l0:q0 + RET_DIM + l0 + RET_DK], cos, sin) * (RET_DK ** -0.5)
    v_ref[...] = proj[:, q0 + 2 * RET_DIM:q0 + 3 * RET_DIM]
    g = proj[:, q0 + 3 * RET_DIM:q0 + 4 * RET_DIM]
    gate_ref[...] = retgn * (g * jax.nn.sigmoid(g))


def _sample_proj(x, sc, cos, sin, gmix, win, convw, retgn):
    n = x.shape[0]
    outs = (
        jax.ShapeDtypeStruct((n, CONV_DIM), F32), jax.ShapeDtypeStruct((n, RET_DIM), F32),
        jax.ShapeDtypeStruct((n, RET_DIM), F32), jax.ShapeDtypeStruct((n, RET_DIM), F32),
        jax.ShapeDtypeStruct((n, RET_DIM), F32), jax.ShapeDtypeStruct((n, 2 * CONV_DIM), F32),
    )
    args = (x, sc, cos, sin, gmix, win, convw, retgn)
    return pl.pallas_call(
        _sample_proj_kernel,
        grid=(1,),
        in_specs=[_const_spec(a.shape) for a in args],
        out_specs=tuple(_const_spec(o.shape) for o in outs),
        out_shape=outs,
        compiler_params=_params("arbitrary"),
        name="sample_proj",
    )(*args)


def _sample_state_update(s_ref, q_ref, k_ref, kt_ref, v_ref, dec_ref, snew_ref, o_ref, requests):
    for r in requests:
        for hh in range(RET_HEADS):
            l0 = hh * RET_DV
            s = s_ref[r, hh]
            q = q_ref[r:r + 1, l0:l0 + RET_DK]
            k = k_ref[r:r + 1, l0:l0 + RET_DK]
            kc = kt_ref[0, hh, :, r:r + 1]
            vr = v_ref[r:r + 1, l0:l0 + RET_DV]
            qdec = dec_ref[hh:hh + 1, :]
            cdec = dec_ref[RET_HEADS + hh:RET_HEADS + hh + 1, :]
            qk = jnp.sum(q * k, axis=1, keepdims=True)
            o_ref[r:r + 1, l0:l0 + RET_DV] = qk * vr + _dot((q * qdec).astype(BF16), s.astype(BF16))
            snew_ref[r, hh] = s * cdec + kc * vr


def _sample_out_kernel(x_ref, yc_ref, o_ref, gate_ref, wo_ref, gffn_ref, wr_ref, br_ref, tri_ref, ltri_ref,
                       h_ref, xn_ref, ri_ref, rw_ref, n_ref):
    n = x_ref.shape[0]
    parts = [yc_ref[...].astype(BF16)]
    for hh in range(RET_HEADS):
        l0 = hh * RET_DV
        parts.append(_group_norm_gate(o_ref[:, l0:l0 + RET_DV], gate_ref[:, l0:l0 + RET_DV]).astype(BF16))
    h_ref[0:n, :] = x_ref[...] + _dot(jnp.concatenate(parts, axis=-1), wo_ref[...])
    h_ref[n:SAMPLE_ROWS, :] = jnp.zeros((SAMPLE_ROWS - n, D_MODEL), F32)
    for part in range(SAMPLE_ROWS // TOK_TILE):
        p0 = part * TOK_TILE
        ri, rw, cnt, xn = _route(h_ref[p0:p0 + TOK_TILE, :], gffn_ref[...], wr_ref[...], br_ref[...], tri_ref[...],
                                 ltri_ref[...])
        ri_ref[:, p0:p0 + TOK_TILE] = ri
        rw_ref[:, p0:p0 + TOK_TILE] = rw
        n_ref[part] = cnt
        xn_ref[p0:p0 + TOK_TILE, :] = xn


def _sample_out(x, yc, o, gate, wo, gffn, wr, br, tri, ltri):
    outs = (
        jax.ShapeDtypeStruct((SAMPLE_ROWS, D_MODEL), F32), jax.ShapeDtypeStruct((SAMPLE_ROWS, D_MODEL), BF16),
        jax.ShapeDtypeStruct((SUBLANES, SAMPLE_ROWS), I32), jax.ShapeDtypeStruct((SUBLANES, SAMPLE_ROWS), F32),
        jax.ShapeDtypeStruct((SAMPLE_ROWS // TOK_TILE, N_EXPERTS, LANES), F32),
    )
    args = (x, yc, o, gate, wo, gffn, wr, br, tri, ltri)
    return pl.pallas_call(
        _sample_out_kernel,
        grid=(1,),
        in_specs=[_const_spec(a.shape) for a in args],
        out_specs=tuple(_const_spec(o_.shape) for o_ in outs),
        out_shape=outs,
        compiler_params=_params("arbitrary"),
        name="sample_out",
    )(*args)


def _sample_inputs(x, state_conv, state_ret, gmix, win, convw, retgn):
    n = x.shape[0]
    lg = _log_gamma()
    cos, sin = _rope_tables(np.full((SUBLANES,), PAST_LEN))
    yc, q, k, v, gate, conv_new = _sample_proj(x, state_conv.reshape(n, 2 * CONV_DIM), cos, sin, gmix, win, convw, retgn)
    nb = SAMPLE_BLOCK
    cols = lambda a: a.reshape(n // nb, nb, RET_HEADS, RET_DK).transpose(0, 2, 3, 1)
    step = np.exp(lg[:, None] * 1.0)
    dec = jnp.asarray(np.broadcast_to(np.concatenate([step, step], axis=0), (2 * RET_HEADS, LANES)), F32)
    return (yc, gate, conv_new.reshape(n, 2, CONV_DIM)), (state_ret, q, k, cols(k), v, dec)


def _load_rows(ref, n_rows):
    words = [ref[pl.ds(c, n_rows, stride=ROW_CHUNKS), :] for c in range(ROW_CHUNKS)]
    half = lambda i: [pltpu.unpack_elementwise(w, index=i, packed_dtype=BF16, unpacked_dtype=F32) for w in words]
    return jnp.concatenate(half(0) + half(1), axis=-1)


def _store_rows(ref, val):
    half = D_MODEL // 2
    for c in range(ROW_CHUNKS):
        pair = [val[:, c * LANES:(c + 1) * LANES], val[:, half + c * LANES:half + (c + 1) * LANES]]
        ref[pl.ds(c, val.shape[0], stride=ROW_CHUNKS), :] = pltpu.pack_elementwise(pair, packed_dtype=BF16)


def _row_slice(ref, row, rows):
    return ref.at[pl.ds(pl.multiple_of(row * ROW_CHUNKS, ROW_CHUNKS), rows * ROW_CHUNKS)]


def _run_copies(table, make_copy):
    for b in range(RUN_BITS):
        first = N_EXPERTS + N_EXPERTS * b
        count = table(b)

        def start(col, queue, b=b):
            entry = table(col)
            make_copy(entry & ((1 << SRC_BITS) - 1), entry >> SRC_BITS, 1 << b).start(priority=queue % 2)

        unroll = RUN_UNROLL if (1 << b) <= PAIR_ROWS // N_EXPERTS else 1
        for r in range(unroll - 1):
            @pl.when(r < count % unroll)
            def _():
                start(first + count - 1 - r, r)

        def body(j, carry, first=first, start=start, unroll=unroll):
            for u in range(unroll):
                start(first + unroll * j + u, u)
            return carry

        lax.fori_loop(0, count // unroll, body, 0)


def _pair_rows_iota():
    return lax.broadcasted_iota(I32, (PAIR_ROWS, TOK_TILE), 0)


def _dispatch_kernel(n_p, tab_ref, xp_ref, xs_in_ref, ri_ref, xs_ref, idx_smem, y_scr, isem, rsem):
    i, n = pl.program_id(0), pl.num_programs(0)
    slot = i % 2
    step_rows = STEP_TILES * PAIR_ROWS

    def idx_copy(step, sl):
        return pltpu.make_async_copy(tab_ref.at[pl.ds(step * STEP_TILES * TAB_COLS, STEP_TILES * TAB_COLS)],
                                     idx_smem.at[pl.ds(sl * STEP_TILES * TAB_COLS, STEP_TILES * TAB_COLS)], isem.at[sl])

    def wait_step(sl):
        pltpu.make_async_copy(_row_slice(xs_ref, 0, step_rows), y_scr.at[sl], rsem.at[sl]).wait()

    @pl.when(i == 0)
    def _():
        idx_copy(0, 0).start()

    @pl.when(i + 1 < n)
    def _():
        idx_copy(i + 1, 1 - slot).start()

    @pl.when(i >= 2)
    def _():
        wait_step(slot)

    xn = jnp.where(i < n_p, xp_ref[...], xs_in_ref[...])
    ri = ri_ref[...]
    rows = _pair_rows_iota()
    for g in range(STEP_TILES):
        tok = slice(g * TOK_TILE, (g + 1) * TOK_TILE)
        perm = jnp.where((rows == ri[2:3, tok]) | (rows == ri[3:4, tok]), 1.0, 0.0).astype(BF16)
        y = _dot(perm, xn[tok, :])
        _store_rows(y_scr.at[slot, pl.ds(g * PAIR_ROWS * ROW_CHUNKS, PAIR_ROWS * ROW_CHUNKS)], y)

    idx_copy(i, slot).wait()
    for g in range(STEP_TILES):
        tab0 = (slot * STEP_TILES + g) * TAB_COLS
        _run_copies(lambda col, tab0=tab0: idx_smem[tab0 + col], lambda src, dst, rows_, g=g: pltpu.make_async_copy(
            _row_slice(y_scr.at[slot], g * PAIR_ROWS + src, rows_), _row_slice(xs_ref, dst, rows_), rsem.at[slot]))

    @pl.when(i == n - 1)
    def _():
        wait_step(slot)

    @pl.when((i == n - 1) & (n >= 2))
    def _():
        wait_step(1 - slot)


def _dispatch(tab, xn_p, xn_s, ri):
    tt = STEP_TILES * TOK_TILE
    n_p = xn_p.shape[0] // tt
    n = tab.shape[0] // STEP_TILES
    return pl.pallas_call(
        functools.partial(_dispatch_kernel, n_p),
        grid=(n,),
        in_specs=[
            pl.BlockSpec(memory_space=pl.ANY),
            pl.BlockSpec((tt, D_MODEL), lambda i: (jnp.minimum(i, n_p - 1), 0)),
            pl.BlockSpec((tt, D_MODEL), lambda i: (jnp.maximum(i - n_p, 0), 0)),
            pl.BlockSpec((SUBLANES, tt), lambda i: (0, i)),
        ],
        out_specs=pl.BlockSpec(memory_space=pl.ANY),
        out_shape=jax.ShapeDtypeStruct((tab.shape[0] * PAIR_ROWS * ROW_CHUNKS, LANES), U32),
        scratch_shapes=[
            pltpu.SMEM((2 * STEP_TILES * TAB_COLS,), I32),
            pltpu.VMEM((2, STEP_TILES * PAIR_ROWS * ROW_CHUNKS, LANES), U32),
            pltpu.SemaphoreType.DMA((2,)),
            pltpu.SemaphoreType.DMA((2,)),
        ],
        compiler_params=_params("arbitrary"),
        name="moe_dispatch",
    )(tab.reshape(-1), xn_p, xn_s, ri)


def _ffn_kernel(tile_ref, exp_ref, lo_ref, hi_ref, nxt_ref, slot_ref, xs_ref, wg_ref, wu_ref, wd_ref, o_ref,
                x_buf, wg_buf, wu_buf, wd_buf, wgu_scr, wd_scr, y_scr, xsem, wsem):
    w, n_w = pl.program_id(0), pl.num_programs(0)
    prev = jnp.maximum(w - 1, 0)
    lo, hi = lo_ref[w], hi_ref[w]
    e, slot = exp_ref[w], slot_ref[w]
    first = (w == 0) | (tile_ref[w] != tile_ref[prev])

    ahead = FFN_RING - 1

    def x_copy(step):
        rows = FFN_TILE * ROW_CHUNKS
        return pltpu.make_async_copy(xs_ref.at[pl.ds(pl.multiple_of(tile_ref[step] * rows, rows), rows)],
                                     x_buf.at[step % FFN_RING], xsem.at[step % FFN_RING])

    for step in range(ahead):
        @pl.when((w == 0) & (step < n_w))
        def _():
            x_copy(step).start()

    @pl.when(w + ahead < n_w)
    def _():
        x_copy(w + ahead).start()

    x_copy(w).wait()

    def weight_copies(expert, sl):
        return [pltpu.make_async_copy(src.at[expert], buf.at[sl], wsem.at[sl])
                for src, buf in ((wg_ref, wg_buf), (wu_ref, wu_buf), (wd_ref, wd_buf))]

    @pl.when(w == 0)
    def _():
        for cp in weight_copies(e, slot):
            cp.start(priority=1)

    @pl.when((w == 0) | (e != exp_ref[prev]))
    def _():
        for cp in weight_copies(e, slot):
            cp.wait()

        @pl.when(nxt_ref[w] != e)
        def _():
            for cp in weight_copies(nxt_ref[w], 1 - slot):
                cp.start(priority=1)

        wgu_scr[:, 0:D_FF] = wg_buf[slot].astype(BF16)
        wgu_scr[:, D_FF:2 * D_FF] = wu_buf[slot].astype(BF16)
        wd_scr[...] = wd_buf[slot].astype(BF16)

    @pl.when(w == 0)
    def _():
        y_scr[...] = jnp.zeros_like(y_scr)

    def compute(r0, n_rows):
        chunk = pl.ds(r0 * ROW_CHUNKS, n_rows * ROW_CHUNKS)
        gu = _dot(_load_rows(x_buf.at[w % FFN_RING, chunk], n_rows).astype(BF16), wgu_scr[...])
        g, u = gu[:, 0:D_FF], gu[:, D_FF:2 * D_FF]
        y = _dot((g * jax.nn.sigmoid(g) * u).astype(BF16), wd_scr[...])
        row = r0 + lax.broadcasted_iota(I32, (n_rows, 1), 0)
        y = jnp.where(((row >= lo) & (row < hi)) | first, y, y_scr[r0:r0 + n_rows, :])
        y_scr[r0:r0 + n_rows, :] = y
        _store_rows(o_ref.at[chunk], y)

    half = FFN_TILE // 2
    top, bottom = hi <= half, lo >= half

    @pl.when((hi > lo) & top)
    def _():
        compute(0, half)

    @pl.when((hi > lo) & bottom)
    def _():
        compute(half, half)

    @pl.when((hi > lo) & jnp.logical_not(top | bottom))
    def _():
        compute(0, FFN_TILE)


def _expert_ffn(items, xs, w_gate, w_up, w_down):
    n_items = items[0].shape[0]
    row_spec = pl.BlockSpec((FFN_TILE * ROW_CHUNKS, LANES), lambda w, t, *_: (t[w], 0))
    grid_spec = pltpu.PrefetchScalarGridSpec(
        num_scalar_prefetch=len(items),
        grid=(n_items,),
        in_specs=[pl.BlockSpec(memory_space=pl.ANY)] * 4,
        out_specs=row_spec,
        scratch_shapes=[
            pltpu.VMEM((FFN_RING, FFN_TILE * ROW_CHUNKS, LANES), U32),
            pltpu.VMEM((2, D_MODEL, D_FF), F32), pltpu.VMEM((2, D_MODEL, D_FF), F32),
            pltpu.VMEM((2, D_FF, D_MODEL), F32),
            pltpu.VMEM((D_MODEL, 2 * D_FF), BF16), pltpu.VMEM((D_FF, D_MODEL), BF16),
            pltpu.VMEM((FFN_TILE, D_MODEL), F32),
            pltpu.SemaphoreType.DMA((FFN_RING,)),
            pltpu.SemaphoreType.DMA((2,)),
        ],
    )
    return pl.pallas_call(
        _ffn_kernel,
        grid_spec=grid_spec,
        out_shape=jax.ShapeDtypeStruct(xs.shape, U32),
        compiler_params=_params("arbitrary"),
        name="moe_ffn",
    )(*items, xs, w_gate, w_up, w_down)


def _work_items(counts, n_rows):
    n_tiles = n_rows // FFN_TILE
    n_items = n_tiles + N_EXPERTS - 1
    off = jnp.cumsum(counts) - counts
    first_tile = off // FFN_TILE
    last_tile = jnp.maximum(off + counts - 1, off) // FFN_TILE
    n_e = jnp.where(counts > 0, last_tile - first_tile + 1, 0)
    start = jnp.cumsum(n_e) - n_e
    total = jnp.sum(n_e)
    w = jnp.minimum(jnp.arange(n_items, dtype=I32), total - 1)
    ids = jnp.arange(N_EXPERTS, dtype=I32)[None, :]
    e = jnp.max(jnp.where((start[None, :] <= w[:, None]) & (n_e[None, :] > 0), ids, 0), axis=1)
    pick = lambda a: jnp.sum(jnp.where(ids == e[:, None], a[None, :], 0), axis=1)
    tile = pick(first_tile) + (w - pick(start))
    lo = jnp.clip(pick(off) - tile * FFN_TILE, 0, FFN_TILE)
    hi = jnp.clip(pick(off + counts) - tile * FFN_TILE, 0, FFN_TILE)
    hi = jnp.where(jnp.arange(n_items) < total, hi, lo)
    used = n_e[None, :] > 0
    nxt = jnp.min(jnp.where(used & (ids > e[:, None]), ids, N_EXPERTS), axis=1)
    nxt = jnp.where(nxt == N_EXPERTS, e, nxt)
    slot = jnp.sum(jnp.where(used & (ids < e[:, None]), 1, 0), axis=1) % 2
    return tuple(a.astype(I32) for a in (tile, e, lo, hi, nxt, slot))


def _run_table(n):
    tiles = n.shape[0]
    counts = jnp.sum(n, axis=0)
    goff = jnp.cumsum(counts) - counts
    dst0 = goff[None, :] + jnp.cumsum(n, axis=0) - n
    src0 = jnp.cumsum(n, axis=1) - n
    bits = jnp.arange(RUN_BITS, dtype=I32)[:, None, None]
    flag = (n[None] >> bits) & 1
    done = (n[None] >> (bits + 1)) << (bits + 1)
    entry = ((dst0[None] + done) << SRC_BITS) | (src0[None] + done)
    rank = jnp.cumsum(flag, axis=2) - flag
    slot = jnp.arange(N_EXPERTS, dtype=I32)
    sel = (flag[..., None] == 1) & (rank[..., None] == slot)
    packed = jnp.sum(jnp.where(sel, entry[..., None], 0), axis=2)
    m = jnp.sum(flag, axis=2).T
    tab = jnp.zeros((tiles, TAB_COLS), I32)
    tab = tab.at[:, :RUN_BITS].set(m)
    tab = tab.at[:, N_EXPERTS:N_EXPERTS * (RUN_BITS + 1)].set(packed.transpose(1, 0, 2).reshape(tiles, -1))
    return counts, tab


def _combine_kernel(g_tiles, n, tab_ref, h_ref, p_ref, ri_ref, rw_ref, ys_ref, gple_ref, wpg_ref, wpp_ref, gfin_ref,
                    o_ref, idx_smem, rows_scr, isem, rsem):
    i = pl.program_id(0)
    slot = i % 2

    def idx_copy(step, sl):
        return pltpu.make_async_copy(tab_ref.at[pl.ds(step * g_tiles * TAB_COLS, g_tiles * TAB_COLS)],
                                     idx_smem.at[pl.ds(sl * g_tiles * TAB_COLS, g_tiles * TAB_COLS)], isem.at[sl])

    def gather(sl):
        for g in range(g_tiles):
            tab0 = (sl * g_tiles + g) * TAB_COLS
            _run_copies(lambda col, tab0=tab0: idx_smem[tab0 + col], lambda src, dst, rows_, g=g: pltpu.make_async_copy(
                _row_slice(ys_ref, dst, rows_), _row_slice(rows_scr.at[sl], g * PAIR_ROWS + src, rows_), rsem.at[sl]))

    @pl.when(i == 0)
    def _():
        idx_copy(0, 0).start()
        idx_copy(0, 0).wait()
        gather(0)

    if n >= 2:
        @pl.when(i == 0)
        def _():
            idx_copy(1, 1).start()

        @pl.when(i + 1 < n)
        def _():
            idx_copy(i + 1, 1 - slot).wait()
            gather(1 - slot)

    @pl.when(i + 2 < n)
    def _():
        idx_copy(i + 2, slot).start()

    pltpu.make_async_copy(_row_slice(ys_ref, 0, g_tiles * PAIR_ROWS), rows_scr.at[slot], rsem.at[slot]).wait()

    ri, rw = ri_ref[...], rw_ref[...]
    rows = _pair_rows_iota()
    tiles = range(g_tiles)
    tok = lambda g: slice(g * TOK_TILE, (g + 1) * TOK_TILE)
    h = []
    for g in tiles:
        r1, r2 = ri[2:3, tok(g)], ri[3:4, tok(g)]
        wperm = jnp.where(rows == r1, rw[0:1, tok(g)], 0.0) + jnp.where(rows == r2, rw[1:2, tok(g)], 0.0)
        srows = _load_rows(rows_scr.at[slot, pl.ds(g * PAIR_ROWS * ROW_CHUNKS, PAIR_ROWS * ROW_CHUNKS)], PAIR_ROWS)
        h.append(h_ref[tok(g), :] + _dot_tn(wperm.astype(BF16), srows.astype(BF16)))
    gate = [jax.nn.sigmoid(_dot(_rms(h[g], gple_ref[...]).astype(BF16), wpg_ref[...])) for g in tiles]
    for g in tiles:
        hg = h[g] + _dot(p_ref[tok(g), :].astype(BF16), wpp_ref[...]) * gate[g]
        o_ref[tok(g), :] = _rms(hg, gfin_ref[...])


def _combine(tab, h, p, ri, rw, ys, gple, wpg, wpp, gfin):
    n_tiles = tab.shape[0]
    g_tiles = STEP_TILES
    n = n_tiles // g_tiles
    tt = g_tiles * TOK_TILE
    return pl.pallas_call(
        functools.partial(_combine_kernel, g_tiles, n),
        grid=(n,),
        in_specs=[
            pl.BlockSpec(memory_space=pl.ANY),
            pl.BlockSpec((tt, D_MODEL), lambda i: (i, 0)),
            pl.BlockSpec((tt, PLE_DIM), lambda i: (i, 0)),
            pl.BlockSpec((SUBLANES, tt), lambda i: (0, i)),
            pl.BlockSpec((SUBLANES, tt), lambda i: (0, i)),
            pl.BlockSpec(memory_space=pl.ANY),
            _const_spec(gple.shape), _const_spec(wpg.shape), _const_spec(wpp.shape), _const_spec(gfin.shape),
        ],
        out_specs=pl.BlockSpec((tt, D_MODEL), lambda i: (i, 0)),
        out_shape=jax.ShapeDtypeStruct((n_tiles * TOK_TILE, D_MODEL), F32),
        scratch_shapes=[
            pltpu.SMEM((2 * g_tiles * TAB_COLS,), I32),
            pltpu.VMEM((2, g_tiles * PAIR_ROWS * ROW_CHUNKS, LANES), U32),
            pltpu.SemaphoreType.DMA((2,)),
            pltpu.SemaphoreType.DMA((2,)),
        ],
        compiler_params=_params("arbitrary"),
        name="moe_combine",
    )(tab.reshape(-1), h, p, ri, rw, ys, gple, wpg, wpp, gfin)


def _log_gamma():
    return np.log(1.0 - 2.0 ** (-5.0 - np.arange(RET_HEADS, dtype=np.float64)))


def _rope_tables(pos):
    inv = 1.0 / (ROPE_BASE ** (np.arange(0, RET_DK, 2, dtype=np.float64) / RET_DK))
    ang = np.asarray(pos, np.float64)[:, None] * inv[None, :]
    cos, sin = np.cos(ang), np.sin(ang)
    return (jnp.asarray(np.concatenate([cos, cos], axis=-1), F32),
            jnp.asarray(np.concatenate([-sin, sin], axis=-1), F32))


def _decay_tables(c):
    lg = _log_gamma()
    idx = np.arange(c, dtype=np.float64)
    diff = idx[:, None] - idx[None, :]
    dmask = np.where((diff >= 0.0)[None], np.exp(lg[:, None, None] * np.maximum(diff, 0.0)[None]), 0.0)
    q_dec = np.exp(lg[None, :] * (idx[:, None] + 1.0))
    k_dec = np.exp(lg[None, :] * (c - 1.0 - idx[:, None]))
    c_dec = np.exp(lg * c)
    lanes = lambda a: np.repeat(a, RET_DK, axis=1)
    cdec = np.zeros((SUBLANES, LANES))
    cdec[:RET_HEADS] = c_dec[:, None]
    return tuple(jnp.asarray(a, F32) for a in (lanes(q_dec), lanes(k_dec), dmask, cdec))


def _router_params(w_rg, b_rg, w_re, b_re):
    wr = jnp.zeros((ROUTER_ROWS, D_MODEL), F32).at[:N_GROUPS].set(w_rg.T).at[SUBLANES:].set(w_re.T)
    br = jnp.full((ROUTER_ROWS,), NEG_BIG, F32).at[:N_GROUPS].set(b_rg).at[SUBLANES:].set(b_re.reshape(-1))
    return wr.astype(BF16), jnp.broadcast_to(br[:, None], (ROUTER_ROWS, LANES))


def _strict_upper(t):
    i = np.arange(t)
    return jnp.asarray(i[:, None] < i[None, :], BF16)


def _pad_rows(a, rows):
    return jnp.zeros((rows,) + a.shape[1:], a.dtype).at[:a.shape[0]].set(a)


def kernel(x_prompt, x_sample, state_conv, state_ret, p_prompt, p_sample, g_mix, w_in, conv_w, ret_gn, w_o, g_ffn,
           w_router_group, b_router_group, w_router_expert, b_router_expert, w_gate, w_up, w_down, g_ple,
           w_ple_proj, w_ple_gate, g_final):
    bp, tp, _ = x_prompt.shape
    bs = x_sample.shape[0]
    assert x_sample.shape[1] == 1 and g_mix.shape[0] == 1, "one layer, one new token per sample request"
    assert bs < SAMPLE_ROWS and tp % MIX_TILE == 0
    n_p = bp * tp

    tabs = _rope_tables(np.arange(tp)) + _decay_tables(TOK_TILE)
    wr, br = _router_params(w_router_group[0], b_router_group[0], w_router_expert[0], b_router_expert[0])
    win, wo, convw = w_in[0].astype(BF16), w_o[0].astype(BF16), _pad_rows(conv_w[0], SUBLANES)
    tri, ltri = _strict_upper(TOK_TILE), _strict_upper(N_EXPERTS).T

    xs0 = x_sample[:, 0]
    (yc_s, gate_s, conv_s), sample = _sample_inputs(xs0, state_conv[0], state_ret[0], g_mix, win, convw, ret_gn)
    h_p, xn_p, ri_p, rw_p, n_p_tab, conv_p, ret_p, ret_s, o_s = _mixer_prompt(
        x_prompt, tabs, (g_mix, win, convw, ret_gn, wo, g_ffn, wr, br, tri, ltri), sample)
    h_s, xn_s, ri_s, rw_s, n_s_tab = _sample_out(xs0, yc_s, o_s, gate_s, wo, g_ffn, wr, br, tri, ltri)
    h_p, xn_p = h_p.reshape(n_p, D_MODEL), xn_p.reshape(n_p, D_MODEL)

    n_tab = jnp.concatenate([n_p_tab, n_s_tab], axis=0)[:, :, 0].astype(I32)
    counts, tab = _run_table(n_tab)
    ri = jnp.concatenate([ri_p, ri_s], axis=1)
    n_tiles = n_tab.shape[0]

    xs = _dispatch(tab, xn_p, xn_s, ri)
    ys = _expert_ffn(_work_items(counts, n_tiles * PAIR_ROWS), xs, w_gate[0], w_up[0], w_down[0])

    wpg, wpp = w_ple_gate[0].astype(BF16), w_ple_proj[0].astype(BF16)
    gfin = g_final[None, :]
    p_s = _pad_rows(p_sample[0].reshape(bs, PLE_DIM), SAMPLE_ROWS)
    n_s_tiles = SAMPLE_ROWS // TOK_TILE
    y_p = _combine(tab[:n_tiles - n_s_tiles], h_p, p_prompt[0].reshape(n_p, PLE_DIM), ri_p, rw_p, ys, g_ple, wpg, wpp, gfin)
    y_s = _combine(tab[n_tiles - n_s_tiles:], h_s, p_s, ri_s, rw_s, ys, g_ple, wpg, wpp, gfin)
    return (y_p.reshape(bp, tp, D_MODEL), y_s[:bs].reshape(bs, 1, D_MODEL), conv_p[None], ret_p[None],
            conv_s[None], ret_s[None])
```
